```python
import math
import jax
import jax.numpy as jnp
from jax import lax
import numpy as np

D_MODEL = 1024
BATCH = 8
SEQ = 16384
DEPTH = 2

CHUNK = 64
CONV_WIDTH = 4
EPS = 1e-6

RET_HEADS = 4
RET_DK = 128
RET_DV = 128
RET_QK_W = RET_HEADS * RET_DK
RET_V_W = RET_HEADS * RET_DV
ROPE_THETA = 10000.0

SSD_HEADS = 8
SSD_HEAD_DIM = 64
SSD_GROUPS = 2
SSD_STATE = 128
SSD_INNER = SSD_HEADS * SSD_HEAD_DIM
SSD_XBC_W = SSD_INNER + 2 * SSD_GROUPS * SSD_STATE

GDN_HEADS = 6
GDN_DK = 128
GDN_DV = 128
GDN_QK_W = GDN_HEADS * GDN_DK
GDN_V_W = GDN_HEADS * GDN_DV
GDN_QKV_W = 2 * GDN_QK_W + GDN_V_W

S5_CH = 256
S5_GROUP = 16
S5_GROUPS = S5_CH // S5_GROUP
S5_STATE = 64

D_FF = 4 * D_MODEL

MIX0_W = RET_V_W + SSD_INNER
MIX1_W = GDN_V_W + S5_CH
IN0_W = 2 * RET_QK_W + 2 * RET_V_W + SSD_INNER + SSD_XBC_W + SSD_HEADS
IN1_W = GDN_QKV_W + GDN_V_W + 2 * GDN_HEADS + S5_CH

kernel_name = "hybrid_retention_ssd_gdn_s5_trunk"


def rmsnorm(x, w):
    xf = x.astype(jnp.float32)
    y = xf * lax.rsqrt(jnp.mean(xf * xf, axis=-1, keepdims=True) + EPS)
    return (y * w.astype(jnp.float32)).astype(x.dtype)


def unit_rms(x):
    return x * lax.rsqrt(jnp.mean(x * x, axis=-1, keepdims=True) + EPS)


def l2norm(x):
    return x * lax.rsqrt(jnp.sum(x * x, axis=-1, keepdims=True) + EPS)


def causal_dwconv(x, w):
    k = w.shape[0]
    return lax.conv_general_dilated(
        x, w[:, None, :].astype(x.dtype), window_strides=(1,), padding=[(k - 1, 0)],
        dimension_numbers=("NWC", "WIO", "NWC"), feature_group_count=x.shape[-1])


def rotary(x, pos):
    half = x.shape[-1] // 2
    inv = ROPE_THETA ** (-jnp.arange(half, dtype=jnp.float32) / half)
    ang = pos[:, None] * inv[None, :]
    cos = jnp.cos(ang)[None, :, None, :]
    sin = jnp.sin(ang)[None, :, None, :]
    x1, x2 = x[..., :half], x[..., half:]
    return jnp.concatenate([x1 * cos - x2 * sin, x1 * sin + x2 * cos], axis=-1)


def retention_chunkwise(q, k, v):
    bsz, seqlen, nh, dk = q.shape
    dv = v.shape[-1]
    nc = seqlen // CHUNK
    log_gamma = jnp.log(1.0 - 2.0 ** (-5.0 - jnp.arange(nh, dtype=jnp.float32)))
    q = q.reshape(bsz, nc, CHUNK, nh, dk) * (dk ** -0.5)
    k = k.reshape(bsz, nc, CHUNK, nh, dk)
    v = v.reshape(bsz, nc, CHUNK, nh, dv)
    idx = jnp.arange(CHUNK, dtype=jnp.float32)
    diff = idx[:, None] - idx[None, :]
    causal = diff >= 0
    dmask = jnp.exp(jnp.where(causal[None], log_gamma[:, None, None] * diff[None], -jnp.inf))
    scores = jnp.einsum("bclhd,bcshd->bchls", q, k) * dmask[None, None]
    y_intra = jnp.einsum("bchls,bcshe->bclhe", scores, v)
    k_w = k * jnp.exp(log_gamma[None, :] * (CHUNK - 1.0 - idx)[:, None])[None, None, :, :, None]
    chunk_kv = jnp.einsum("bclhd,bclhe->bchde", k_w, v)
    chunk_decay = jnp.exp(log_gamma * CHUNK)[None, :, None, None]

    def step(state, kv):
        return state * chunk_decay + kv, state

    init = jnp.zeros((bsz, nh, dk, dv), jnp.float32)
    _, prev = lax.scan(step, init, jnp.moveaxis(chunk_kv, 1, 0))
    prev = jnp.moveaxis(prev, 0, 1)
    q_w = q * jnp.exp(log_gamma[None, :] * (idx + 1.0)[:, None])[None, None, :, :, None]
    y_inter = jnp.einsum("bclhd,bchde->bclhe", q_w, prev)
    return (y_intra + y_inter).reshape(bsz, seqlen, nh, dv)


def ssd_chunked(x, dt, a, bm, cm):
    bsz, seqlen, nh, p = x.shape
    ng, n = bm.shape[-2:]
    nj = nh // ng
    nc = seqlen // CHUNK
    xd = (x * dt[..., None]).reshape(bsz, nc, CHUNK, ng, nj, p)
    la = (dt * a).reshape(bsz, nc, CHUNK, ng, nj)
    la_cum = jnp.cumsum(la, axis=2)
    bc = bm.reshape(bsz, nc, CHUNK, ng, n)
    cc = cm.reshape(bsz, nc, CHUNK, ng, n)
    causal = jnp.tril(jnp.ones((CHUNK, CHUNK), bool))[None, None, :, :, None, None]
    seg = la_cum[:, :, :, None] - la_cum[:, :, None, :]
    lmat = jnp.exp(jnp.where(causal, seg, -jnp.inf))
    cb = jnp.einsum("bclgn,bcsgn->bclsg", cc, bc)
    y_diag = jnp.einsum("bclsgj,bcsgjp->bclgjp", cb[..., None] * lmat, xd)
    decay_to_end = jnp.exp(la_cum[:, :, -1:] - la_cum)
    chunk_states = jnp.einsum("bclgn,bclgj,bclgjp->bcgjpn", bc, decay_to_end, xd)
    chunk_decay = jnp.exp(la_cum[:, :, -1])

    def step(state, inp):
        s, d = inp
        return state * d[..., None, None] + s, state

    init = jnp.zeros((bsz, ng, nj, p, n), jnp.float32)
    _, prev = lax.scan(step, init, (jnp.moveaxis(chunk_states, 1, 0), jnp.moveaxis(chunk_decay, 1, 0)))
    prev = jnp.moveaxis(prev, 0, 1)
    y_off = jnp.einsum("bclgn,bcgjpn,bclgj->bclgjp", cc, prev, jnp.exp(la_cum))
    return (y_diag + y_off).reshape(bsz, seqlen, nh, p)


def gated_delta_chunked(q, k, v, g, beta):
    bsz, seqlen, nh, dk = q.shape
    dv = v.shape[-1]
    nc = seqlen // CHUNK

    def to_chunks(t):
        return t.reshape(bsz, nc, CHUNK, nh, t.shape[-1]).transpose(0, 3, 1, 2, 4)

    q = to_chunks(q * (dk ** -0.5))
    k = to_chunks(k)
    v = to_chunks(v)
    g = g.reshape(bsz, nc, CHUNK, nh).transpose(0, 3, 1, 2)
    beta = beta.reshape(bsz, nc, CHUNK, nh).transpose(0, 3, 1, 2)
    g_cum = jnp.cumsum(g, axis=-1)
    causal = jnp.tril(jnp.ones((CHUNK, CHUNK), bool))
    strict = jnp.tril(jnp.ones((CHUNK, CHUNK), bool), k=-1)
    decay = jnp.exp(jnp.where(causal, g_cum[..., :, None] - g_cum[..., None, :], -jnp.inf))
    k_beta = k * beta[..., None]
    v_beta = v * beta[..., None]
    lower = jnp.where(strict, jnp.einsum("bhcld,bhcsd->bhcls", k_beta, k) * decay, 0.0)
    eye = jnp.eye(CHUNK, dtype=jnp.float32)
    t_inv = lax.linalg.triangular_solve(eye + lower, jnp.broadcast_to(eye, lower.shape),
                                        left_side=True, lower=True)
    u = t_inv @ v_beta
    w = t_inv @ (k_beta * jnp.exp(g_cum)[..., None])
    attn = jnp.where(causal, jnp.einsum("bhcld,bhcsd->bhcls", q, k) * decay, 0.0)
    q_g = q * jnp.exp(g_cum)[..., None]
    k_tail = k * jnp.exp(g_cum[..., -1:] - g_cum)[..., None]
    chunk_decay = jnp.exp(g_cum[..., -1])

    def step(s, inp):
        q_i, w_i, u_i, a_i, kt_i, d_i = inp
        v_new = u_i - w_i @ s
        o = q_i @ s + a_i @ v_new
        s = s * d_i[..., None, None] + jnp.einsum("bhcd,bhce->bhde", kt_i, v_new)
        return s, o

    xs = tuple(jnp.moveaxis(t, 2, 0) for t in (q_g, w, u, attn, k_tail, chunk_decay))
    init = jnp.zeros((bsz, nh, dk, dv), jnp.float32)
    _, o = lax.scan(step, init, xs)
    return o.transpose(1, 0, 3, 2, 4).reshape(bsz, seqlen, nh, dv)


def s5_group_ssm(u, a_re, a_im, log_step, b_re, b_im, c_re, c_im, d_skip, w_glu, b_glu):
    f32 = jnp.float32
    bsz, seqlen, _ = u.shape
    ug = u.reshape(bsz, seqlen, S5_GROUPS, S5_GROUP)
    lam = lax.complex(a_re.astype(f32), a_im.astype(f32))
    step = jnp.exp(log_step.astype(f32))[:, None]
    lam_bar = jnp.exp(lam * step)
    b_mat = lax.complex(b_re.astype(f32), b_im.astype(f32))
    b_bar = ((lam_bar - 1.0) / lam)[..., None] * b_mat
    bu = jnp.einsum("blgc,gnc->blgn", ug.astype(jnp.complex64), b_bar)
    a_seq = jnp.broadcast_to(lam_bar, bu.shape)

    def combine(left, right):
        a_l, b_l = left
        a_r, b_r = right
        return a_r * a_l, a_r * b_l + b_r

    _, h = lax.associative_scan(combine, (a_seq, bu), axis=1)
    c_mat = lax.complex(c_re.astype(f32), c_im.astype(f32))
    y = jnp.einsum("blgn,gcn->blgc", h, c_mat).real + d_skip.astype(f32).reshape(S5_GROUPS, S5_GROUP) * ug
    y = jax.nn.gelu(y.reshape(bsz, seqlen, S5_CH))
    return y * jax.nn.sigmoid(y @ w_glu.astype(f32) + b_glu.astype(f32))


def retention_ssd_mixer(h, w_in, ssd_conv_w, ssd_conv_b, ssd_dt_bias, ssd_A_log, ssd_D, ssd_norm_w, w_out):
    f32 = jnp.float32
    bsz, seqlen, _ = h.shape
    proj = h @ w_in
    offs = np.cumsum([RET_QK_W, RET_QK_W, RET_V_W, RET_V_W, SSD_INNER, SSD_XBC_W]).tolist()
    q, k, v, gate, z, xbc, dt = jnp.split(proj, offs, axis=-1)
    pos = jnp.arange(seqlen, dtype=f32)
    q = rotary(q.astype(f32).reshape(bsz, seqlen, RET_HEADS, RET_DK), pos)
    k = rotary(k.astype(f32).reshape(bsz, seqlen, RET_HEADS, RET_DK), pos)
    v = v.astype(f32).reshape(bsz, seqlen, RET_HEADS, RET_DV)
    r = unit_rms(retention_chunkwise(q, k, v)).reshape(bsz, seqlen, RET_V_W)
    ret_out = jax.nn.silu(gate.astype(f32)) * r
    xbc = jax.nn.silu(causal_dwconv(xbc, ssd_conv_w) + ssd_conv_b.astype(xbc.dtype)).astype(f32)
    xs, bm, cm = jnp.split(xbc, [SSD_INNER, SSD_INNER + SSD_GROUPS * SSD_STATE], axis=-1)
    dt = jax.nn.softplus(dt.astype(f32) + ssd_dt_bias.astype(f32))
    a = -jnp.exp(ssd_A_log.astype(f32))
    xs = xs.reshape(bsz, seqlen, SSD_HEADS, SSD_HEAD_DIM)
    y = ssd_chunked(xs, dt, a, bm.reshape(bsz, seqlen, SSD_GROUPS, SSD_STATE),
                    cm.reshape(bsz, seqlen, SSD_GROUPS, SSD_STATE))
    y = (y + ssd_D.astype(f32)[:, None] * xs).reshape(bsz, seqlen, SSD_INNER)
    yg = (y * jax.nn.silu(z.astype(f32))).reshape(bsz, seqlen, SSD_GROUPS, SSD_INNER // SSD_GROUPS)
    ssd_out = unit_rms(yg).reshape(bsz, seqlen, SSD_INNER) * ssd_norm_w.astype(f32)
    mixed = jnp.concatenate([ret_out, ssd_out], axis=-1).astype(h.dtype)
    return mixed @ w_out


def deltanet_s5_mixer(h, w_in, gdn_conv_w, gdn_A_log, gdn_dt_bias, gdn_norm_w,
                      s5_A_re, s5_A_im, s5_log_step, s5_B_re, s5_B_im, s5_C_re, s5_C_im,
                      s5_D, s5_w_glu, s5_b_glu, w_out):
    f32 = jnp.float32
    bsz, seqlen, _ = h.shape
    proj = h @ w_in
    offs = np.cumsum([GDN_QKV_W, GDN_V_W, GDN_HEADS, GDN_HEADS]).tolist()
    qkv, z, b_raw, a_raw, u = jnp.split(proj, offs, axis=-1)
    qkv = jax.nn.silu(causal_dwconv(qkv, gdn_conv_w)).astype(f32)
    q, k, v = jnp.split(qkv, [GDN_QK_W, 2 * GDN_QK_W], axis=-1)
    q = l2norm(q.reshape(bsz, seqlen, GDN_HEADS, GDN_DK))
    k = l2norm(k.reshape(bsz, seqlen, GDN_HEADS, GDN_DK))
    v = v.reshape(bsz, seqlen, GDN_HEADS, GDN_DV)
    beta = jax.nn.sigmoid(b_raw.astype(f32))
    g = -jnp.exp(gdn_A_log.astype(f32)) * jax.nn.softplus(a_raw.astype(f32) + gdn_dt_bias.astype(f32))
    o = gated_delta_chunked(q, k, v, g, beta)
    o = unit_rms(o) * gdn_norm_w.astype(f32) * jax.nn.silu(z.astype(f32).reshape(bsz, seqlen, GDN_HEADS, GDN_DV))
    gdn_out = o.reshape(bsz, seqlen, GDN_V_W)
    s5_out = s5_group_ssm(u.astype(f32), s5_A_re, s5_A_im, s5_log_step, s5_B_re, s5_B_im,
                          s5_C_re, s5_C_im, s5_D, s5_w_glu, s5_b_glu)
    mixed = jnp.concatenate([gdn_out, s5_out], axis=-1).astype(h.dtype)
    return mixed @ w_out


def sqrelu_mlp(h, w_up, w_down):
    a = jax.nn.relu(h @ w_up)
    return (a * a) @ w_down


def _fwd_setup_inputs(seed: int = 0) -> dict:
    key = jax.random.key(seed)
    ks = jax.random.split(key, 34)
    f32 = jnp.float32

    def nrm(i, shape, scale):
        return scale * jax.random.normal(ks[i], shape, f32)

    def gain(i, n):
        return 1.0 + nrm(i, (n,), 0.02)

    def dt_bias(i, n):
        dt = jnp.exp(jax.random.uniform(ks[i], (n,), f32, math.log(1e-3), math.log(1e-1)))
        return dt + jnp.log(-jnp.expm1(-dt))

    def a_log(i, n):
        return jnp.log(jax.random.uniform(ks[i], (n,), f32, 1.0, 16.0))

    dm = D_MODEL ** -0.5
    return {
        "x": nrm(0, (BATCH, SEQ, D_MODEL), 1.0),
        "l0_norm_mix": gain(1, D_MODEL),
        "l0_w_in": nrm(2, (D_MODEL, IN0_W), dm),
        "ssd_conv_w": nrm(3, (CONV_WIDTH, SSD_XBC_W), CONV_WIDTH ** -0.5),
        "ssd_conv_b": nrm(4, (SSD_XBC_W,), 0.02),
        "ssd_dt_bias": dt_bias(5, SSD_HEADS),
        "ssd_A_log": a_log(6, SSD_HEADS),
        "ssd_D": gain(7, SSD_HEADS),
        "ssd_norm_w": gain(8, SSD_INNER),
        "l0_w_out": nrm(9, (MIX0_W, D_MODEL), MIX0_W ** -0.5),
        "l0_norm_mlp": gain(10, D_MODEL),
        "l0_w_up": nrm(11, (D_MODEL, D_FF), dm),
        "l0_w_down": nrm(12, (D_FF, D_MODEL), D_FF ** -0.5),
        "l1_norm_mix": gain(13, D_MODEL),
        "l1_w_in": nrm(14, (D_MODEL, IN1_W), dm),
        "gdn_conv_w": nrm(15, (CONV_WIDTH, GDN_QKV_W), CONV_WIDTH ** -0.5),
        "gdn_A_log": a_log(16, GDN_HEADS),
        "gdn_dt_bias": dt_bias(17, GDN_HEADS),
        "gdn_norm_w": gain(18, GDN_DV),
        "s5_A_re": -0.5 + nrm(19, (S5_GROUPS, S5_STATE), 0.01),
        "s5_A_im": math.pi * jnp.arange(S5_STATE, dtype=f32)[None, :] + nrm(20, (S5_GROUPS, S5_STATE), 0.01),
        "s5_log_step": jax.random.uniform(ks[21], (S5_GROUPS,), f32, math.log(1e-3), math.log(1e-1)),
        "s5_B_re": nrm(22, (S5_GROUPS, S5_STATE, S5_GROUP), (2 * S5_GROUP) ** -0.5),
        "s5_B_im": nrm(23, (S5_GROUPS, S5_STATE, S5_GROUP), (2 * S5_GROUP) ** -0.5),
        "s5_C_re": nrm(24, (S5_GROUPS, S5_GROUP, S5_STATE), (2 * S5_STATE) ** -0.5),
        "s5_C_im": nrm(25, (S5_GROUPS, S5_GROUP, S5_STATE), (2 * S5_STATE) ** -0.5),
        "s5_D": nrm(26, (S5_CH,), 1.0),
        "s5_w_glu": nrm(27, (S5_CH, S5_CH), S5_CH ** -0.5),
        "s5_b_glu": nrm(28, (S5_CH,), 0.02),
        "l1_w_out": nrm(29, (MIX1_W, D_MODEL), MIX1_W ** -0.5),
        "l1_norm_mlp": gain(30, D_MODEL),
        "l1_w_up": nrm(31, (D_MODEL, D_FF), dm),
        "l1_w_down": nrm(32, (D_FF, D_MODEL), D_FF ** -0.5),
        "final_norm": gain(33, D_MODEL),
    }


def _fwd_reference(x, l0_norm_mix, l0_w_in, ssd_conv_w, ssd_conv_b, ssd_dt_bias, ssd_A_log, ssd_D,
              ssd_norm_w, l0_w_out, l0_norm_mlp, l0_w_up, l0_w_down, l1_norm_mix, l1_w_in,
              gdn_conv_w, gdn_A_log, gdn_dt_bias, gdn_norm_w, s5_A_re, s5_A_im, s5_log_step,
              s5_B_re, s5_B_im, s5_C_re, s5_C_im, s5_D, s5_w_glu, s5_b_glu, l1_w_out,
              l1_norm_mlp, l1_w_up, l1_w_down, final_norm):
    for layer in range(DEPTH):
        if layer % 2 == 0:
            x = x + retention_ssd_mixer(rmsnorm(x, l0_norm_mix), l0_w_in, ssd_conv_w, ssd_conv_b,
                                        ssd_dt_bias, ssd_A_log, ssd_D, ssd_norm_w, l0_w_out)
            x = x + sqrelu_mlp(rmsnorm(x, l0_norm_mlp), l0_w_up, l0_w_down)
        else:
            x = x + deltanet_s5_mixer(rmsnorm(x, l1_norm_mix), l1_w_in, gdn_conv_w, gdn_A_log,
                                      gdn_dt_bias, gdn_norm_w, s5_A_re, s5_A_im, s5_log_step,
                                      s5_B_re, s5_B_im, s5_C_re, s5_C_im, s5_D, s5_w_glu,
                                      s5_b_glu, l1_w_out)
            x = x + sqrelu_mlp(rmsnorm(x, l1_norm_mlp), l1_w_up, l1_w_down)
    return rmsnorm(x, final_norm)


import jax as _jax
import jax.numpy as _jnp

TWIN_FORMAT = 'train_step'
FWD_PARAMS = ['x', 'l0_norm_mix', 'l0_w_in', 'ssd_conv_w', 'ssd_conv_b', 'ssd_dt_bias', 'ssd_A_log', 'ssd_D', 'ssd_norm_w', 'l0_w_out', 'l0_norm_mlp', 'l0_w_up', 'l0_w_down', 'l1_norm_mix', 'l1_w_in', 'gdn_conv_w', 'gdn_A_log', 'gdn_dt_bias', 'gdn_norm_w', 's5_A_re', 's5_A_im', 's5_log_step', 's5_B_re', 's5_B_im', 's5_C_re', 's5_C_im', 's5_D', 's5_w_glu', 's5_b_glu', 'l1_w_out', 'l1_norm_mlp', 'l1_w_up', 'l1_w_down', 'final_norm']
TWIN_WEIGHTS = ['l0_norm_mix', 'l0_w_in', 'ssd_conv_w', 'ssd_conv_b', 'ssd_dt_bias', 'ssd_A_log', 'ssd_D', 'ssd_norm_w', 'l0_w_out', 'l0_norm_mlp', 'l0_w_up', 'l0_w_down', 'l1_norm_mix', 'l1_w_in', 'gdn_conv_w', 'gdn_A_log', 'gdn_dt_bias', 'gdn_norm_w', 's5_A_re', 's5_A_im', 's5_log_step', 's5_B_re', 's5_B_im', 's5_C_re', 's5_C_im', 's5_D', 's5_w_glu', 's5_b_glu', 'l1_w_out', 'l1_norm_mlp', 'l1_w_up', 'l1_w_down', 'final_norm']
TWIN_DIFF_INPUT = 'x'
TWIN_INPUTS = ['x', 'l0_norm_mix', 'l0_w_in', 'ssd_conv_w', 'ssd_conv_b', 'ssd_dt_bias', 'ssd_A_log', 'ssd_D', 'ssd_norm_w', 'l0_w_out', 'l0_norm_mlp', 'l0_w_up', 'l0_w_down', 'l1_norm_mix', 'l1_w_in', 'gdn_conv_w', 'gdn_A_log', 'gdn_dt_bias', 'gdn_norm_w', 's5_A_re', 's5_A_im', 's5_log_step', 's5_B_re', 's5_B_im', 's5_C_re', 's5_C_im', 's5_D', 's5_w_glu', 's5_b_glu', 'l1_w_out', 'l1_norm_mlp', 'l1_w_up', 'l1_w_down', 'final_norm', 'loss_target', 'm_l0_norm_mix', 'm_l0_w_in', 'm_ssd_conv_w', 'm_ssd_conv_b', 'm_ssd_dt_bias', 'm_ssd_A_log', 'm_ssd_D', 'm_ssd_norm_w', 'm_l0_w_out', 'm_l0_norm_mlp', 'm_l0_w_up', 'm_l0_w_down', 'm_l1_norm_mix', 'm_l1_w_in', 'm_gdn_conv_w', 'm_gdn_A_log', 'm_gdn_dt_bias', 'm_gdn_norm_w', 'm_s5_A_re', 'm_s5_A_im', 'm_s5_log_step', 'm_s5_B_re', 'm_s5_B_im', 'm_s5_C_re', 'm_s5_C_im', 'm_s5_D', 'm_s5_w_glu', 'm_s5_b_glu', 'm_l1_w_out', 'm_l1_norm_mlp', 'm_l1_w_up', 'm_l1_w_down', 'm_final_norm', 'v_l0_norm_mix', 'v_l0_w_in', 'v_ssd_conv_w', 'v_ssd_conv_b', 'v_ssd_dt_bias', 'v_ssd_A_log', 'v_ssd_D', 'v_ssd_norm_w', 'v_l0_w_out', 'v_l0_norm_mlp', 'v_l0_w_up', 'v_l0_w_down', 'v_l1_norm_mix', 'v_l1_w_in', 'v_gdn_conv_w', 'v_gdn_A_log', 'v_gdn_dt_bias', 'v_gdn_norm_w', 'v_s5_A_re', 'v_s5_A_im', 'v_s5_log_step', 'v_s5_B_re', 'v_s5_B_im', 'v_s5_C_re', 'v_s5_C_im', 'v_s5_D', 'v_s5_w_glu', 'v_s5_b_glu', 'v_l1_w_out', 'v_l1_norm_mlp', 'v_l1_w_up', 'v_l1_w_down', 'v_final_norm']
TWIN_OUTPUTS = ['loss', 'grad_x', 'grad_l0_norm_mix', 'grad_l0_w_in', 'grad_ssd_conv_w', 'grad_ssd_conv_b', 'grad_ssd_dt_bias', 'grad_ssd_A_log', 'grad_ssd_D', 'grad_ssd_norm_w', 'grad_l0_w_out', 'grad_l0_norm_mlp', 'grad_l0_w_up', 'grad_l0_w_down', 'grad_l1_norm_mix', 'grad_l1_w_in', 'grad_gdn_conv_w', 'grad_gdn_A_log', 'grad_gdn_dt_bias', 'grad_gdn_norm_w', 'grad_s5_A_re', 'grad_s5_A_im', 'grad_s5_log_step', 'grad_s5_B_re', 'grad_s5_B_im', 'grad_s5_C_re', 'grad_s5_C_im', 'grad_s5_D', 'grad_s5_w_glu', 'grad_s5_b_glu', 'grad_l1_w_out', 'grad_l1_norm_mlp', 'grad_l1_w_up', 'grad_l1_w_down', 'grad_final_norm', 'delta_l0_norm_mix', 'delta_l0_w_in', 'delta_ssd_conv_w', 'delta_ssd_conv_b', 'delta_ssd_dt_bias', 'delta_ssd_A_log', 'delta_ssd_D', 'delta_ssd_norm_w', 'delta_l0_w_out', 'delta_l0_norm_mlp', 'delta_l0_w_up', 'delta_l0_w_down', 'delta_l1_norm_mix', 'delta_l1_w_in', 'delta_gdn_conv_w', 'delta_gdn_A_log', 'delta_gdn_dt_bias', 'delta_gdn_norm_w', 'delta_s5_A_re', 'delta_s5_A_im', 'delta_s5_log_step', 'delta_s5_B_re', 'delta_s5_B_im', 'delta_s5_C_re', 'delta_s5_C_im', 'delta_s5_D', 'delta_s5_w_glu', 'delta_s5_b_glu', 'delta_l1_w_out', 'delta_l1_norm_mlp', 'delta_l1_w_up', 'delta_l1_w_down', 'delta_final_norm', 'new_m_l0_norm_mix', 'new_m_l0_w_in', 'new_m_ssd_conv_w', 'new_m_ssd_conv_b', 'new_m_ssd_dt_bias', 'new_m_ssd_A_log', 'new_m_ssd_D', 'new_m_ssd_norm_w', 'new_m_l0_w_out', 'new_m_l0_norm_mlp', 'new_m_l0_w_up', 'new_m_l0_w_down', 'new_m_l1_norm_mix', 'new_m_l1_w_in', 'new_m_gdn_conv_w', 'new_m_gdn_A_log', 'new_m_gdn_dt_bias', 'new_m_gdn_norm_w', 'new_m_s5_A_re', 'new_m_s5_A_im', 'new_m_s5_log_step', 'new_m_s5_B_re', 'new_m_s5_B_im', 'new_m_s5_C_re', 'new_m_s5_C_im', 'new_m_s5_D', 'new_m_s5_w_glu', 'new_m_s5_b_glu', 'new_m_l1_w_out', 'new_m_l1_norm_mlp', 'new_m_l1_w_up', 'new_m_l1_w_down', 'new_m_final_norm', 'new_v_l0_norm_mix', 'new_v_l0_w_in', 'new_v_ssd_conv_w', 'new_v_ssd_conv_b', 'new_v_ssd_dt_bias', 'new_v_ssd_A_log', 'new_v_ssd_D', 'new_v_ssd_norm_w', 'new_v_l0_w_out', 'new_v_l0_norm_mlp', 'new_v_l0_w_up', 'new_v_l0_w_down', 'new_v_l1_norm_mix', 'new_v_l1_w_in', 'new_v_gdn_conv_w', 'new_v_gdn_A_log', 'new_v_gdn_dt_bias', 'new_v_gdn_norm_w', 'new_v_s5_A_re', 'new_v_s5_A_im', 'new_v_s5_log_step', 'new_v_s5_B_re', 'new_v_s5_B_im', 'new_v_s5_C_re', 'new_v_s5_C_im', 'new_v_s5_D', 'new_v_s5_w_glu', 'new_v_s5_b_glu', 'new_v_l1_w_out', 'new_v_l1_norm_mlp', 'new_v_l1_w_up', 'new_v_l1_w_down', 'new_v_final_norm']
TWIN_LEAF_KINDS = {'loss': 'loss', 'grad_x': 'grad_x', 'grad_l0_norm_mix': 'grad_w', 'grad_l0_w_in': 'grad_w', 'grad_ssd_conv_w': 'grad_w', 'grad_ssd_conv_b': 'grad_w', 'grad_ssd_dt_bias': 'grad_w', 'grad_ssd_A_log': 'grad_w', 'grad_ssd_D': 'grad_w', 'grad_ssd_norm_w': 'grad_w', 'grad_l0_w_out': 'grad_w', 'grad_l0_norm_mlp': 'grad_w', 'grad_l0_w_up': 'grad_w', 'grad_l0_w_down': 'grad_w', 'grad_l1_norm_mix': 'grad_w', 'grad_l1_w_in': 'grad_w', 'grad_gdn_conv_w': 'grad_w', 'grad_gdn_A_log': 'grad_w', 'grad_gdn_dt_bias': 'grad_w', 'grad_gdn_norm_w': 'grad_w', 'grad_s5_A_re': 'grad_w', 'grad_s5_A_im': 'grad_w', 'grad_s5_log_step': 'grad_w', 'grad_s5_B_re': 'grad_w', 'grad_s5_B_im': 'grad_w', 'grad_s5_C_re': 'grad_w', 'grad_s5_C_im': 'grad_w', 'grad_s5_D': 'grad_w', 'grad_s5_w_glu': 'grad_w', 'grad_s5_b_glu': 'grad_w', 'grad_l1_w_out': 'grad_w', 'grad_l1_norm_mlp': 'grad_w', 'grad_l1_w_up': 'grad_w', 'grad_l1_w_down': 'grad_w', 'grad_final_norm': 'grad_w', 'delta_l0_norm_mix': 'delta_w', 'delta_l0_w_in': 'delta_w', 'delta_ssd_conv_w': 'delta_w', 'delta_ssd_conv_b': 'delta_w', 'delta_ssd_dt_bias': 'delta_w', 'delta_ssd_A_log': 'delta_w', 'delta_ssd_D': 'delta_w', 'delta_ssd_norm_w': 'delta_w', 'delta_l0_w_out': 'delta_w', 'delta_l0_norm_mlp': 'delta_w', 'delta_l0_w_up': 'delta_w', 'delta_l0_w_down': 'delta_w', 'delta_l1_norm_mix': 'delta_w', 'delta_l1_w_in': 'delta_w', 'delta_gdn_conv_w': 'delta_w', 'delta_gdn_A_log': 'delta_w', 'delta_gdn_dt_bias': 'delta_w', 'delta_gdn_norm_w': 'delta_w', 'delta_s5_A_re': 'delta_w', 'delta_s5_A_im': 'delta_w', 'delta_s5_log_step': 'delta_w', 'delta_s5_B_re': 'delta_w', 'delta_s5_B_im': 'delta_w', 'delta_s5_C_re': 'delta_w', 'delta_s5_C_im': 'delta_w', 'delta_s5_D': 'delta_w', 'delta_s5_w_glu': 'delta_w', 'delta_s5_b_glu': 'delta_w', 'delta_l1_w_out': 'delta_w', 'delta_l1_norm_mlp': 'delta_w', 'delta_l1_w_up': 'delta_w', 'delta_l1_w_down': 'delta_w', 'delta_final_norm': 'delta_w', 'new_m_l0_norm_mix': 'new_m', 'new_m_l0_w_in': 'new_m', 'new_m_ssd_conv_w': 'new_m', 'new_m_ssd_conv_b': 'new_m', 'new_m_ssd_dt_bias': 'new_m', 'new_m_ssd_A_log': 'new_m', 'new_m_ssd_D': 'new_m', 'new_m_ssd_norm_w': 'new_m', 'new_m_l0_w_out': 'new_m', 'new_m_l0_norm_mlp': 'new_m', 'new_m_l0_w_up': 'new_m', 'new_m_l0_w_down': 'new_m', 'new_m_l1_norm_mix': 'new_m', 'new_m_l1_w_in': 'new_m', 'new_m_gdn_conv_w': 'new_m', 'new_m_gdn_A_log': 'new_m', 'new_m_gdn_dt_bias': 'new_m', 'new_m_gdn_norm_w': 'new_m', 'new_m_s5_A_re': 'new_m', 'new_m_s5_A_im': 'new_m', 'new_m_s5_log_step': 'new_m', 'new_m_s5_B_re': 'new_m', 'new_m_s5_B_im': 'new_m', 'new_m_s5_C_re': 'new_m', 'new_m_s5_C_im': 'new_m', 'new_m_s5_D': 'new_m', 'new_m_s5_w_glu': 'new_m', 'new_m_s5_b_glu': 'new_m', 'new_m_l1_w_out': 'new_m', 'new_m_l1_norm_mlp': 'new_m', 'new_m_l1_w_up': 'new_m', 'new_m_l1_w_down': 'new_m', 'new_m_final_norm': 'new_m', 'new_v_l0_norm_mix': 'new_v', 'new_v_l0_w_in': 'new_v', 'new_v_ssd_conv_w': 'new_v', 'new_v_ssd_conv_b': 'new_v', 'new_v_ssd_dt_bias': 'new_v', 'new_v_ssd_A_log': 'new_v', 'new_v_ssd_D': 'new_v', 'new_v_ssd_norm_w': 'new_v', 'new_v_l0_w_out': 'new_v', 'new_v_l0_norm_mlp': 'new_v', 'new_v_l0_w_up': 'new_v', 'new_v_l0_w_down': 'new_v', 'new_v_l1_norm_mix': 'new_v', 'new_v_l1_w_in': 'new_v', 'new_v_gdn_conv_w': 'new_v', 'new_v_gdn_A_log': 'new_v', 'new_v_gdn_dt_bias': 'new_v', 'new_v_gdn_norm_w': 'new_v', 'new_v_s5_A_re': 'new_v', 'new_v_s5_A_im': 'new_v', 'new_v_s5_log_step': 'new_v', 'new_v_s5_B_re': 'new_v', 'new_v_s5_B_im': 'new_v', 'new_v_s5_C_re': 'new_v', 'new_v_s5_C_im': 'new_v', 'new_v_s5_D': 'new_v', 'new_v_s5_w_glu': 'new_v', 'new_v_s5_b_glu': 'new_v', 'new_v_l1_w_out': 'new_v', 'new_v_l1_norm_mlp': 'new_v', 'new_v_l1_w_up': 'new_v', 'new_v_l1_w_down': 'new_v', 'new_v_final_norm': 'new_v'}


def _forward(args):
    return _fwd_reference(*[args[k] for k in FWD_PARAMS])


def _output_shape():
    def fwd():
        inp = _fwd_setup_inputs(0)
        return _fwd_reference(*[inp[k] for k in FWD_PARAMS])
    out = _jax.eval_shape(fwd)
    return out.shape, out.dtype

N_MICROBATCH = 1
ADAM_LR = 0.001
ADAM_B1 = 0.9
ADAM_B2 = 0.999
ADAM_EPS = 1e-08
ADAM_WD = 0.01
ADAM_STEP = 10
PER_EXAMPLE_BATCH_AXIS = {'x': 0, 'loss_target': 0}
SHARED_INPUTS = []
_WEIGHT_DTYPES = {'l0_norm_mix': _jnp.float32, 'l0_w_in': _jnp.float32, 'ssd_conv_w': _jnp.float32, 'ssd_conv_b': _jnp.float32, 'ssd_dt_bias': _jnp.float32, 'ssd_A_log': _jnp.float32, 'ssd_D': _jnp.float32, 'ssd_norm_w': _jnp.float32, 'l0_w_out': _jnp.float32, 'l0_norm_mlp': _jnp.float32, 'l0_w_up': _jnp.float32, 'l0_w_down': _jnp.float32, 'l1_norm_mix': _jnp.float32, 'l1_w_in': _jnp.float32, 'gdn_conv_w': _jnp.float32, 'gdn_A_log': _jnp.float32, 'gdn_dt_bias': _jnp.float32, 'gdn_norm_w': _jnp.float32, 's5_A_re': _jnp.float32, 's5_A_im': _jnp.float32, 's5_log_step': _jnp.float32, 's5_B_re': _jnp.float32, 's5_B_im': _jnp.float32, 's5_C_re': _jnp.float32, 's5_C_im': _jnp.float32, 's5_D': _jnp.float32, 's5_w_glu': _jnp.float32, 's5_b_glu': _jnp.float32, 'l1_w_out': _jnp.float32, 'l1_norm_mlp': _jnp.float32, 'l1_w_up': _jnp.float32, 'l1_w_down': _jnp.float32, 'final_norm': _jnp.float32}
MOMENT_SCALE = {'l0_norm_mix': 4.845913e-01, 'l0_w_in': 2.452022e-01, 'ssd_conv_w': 2.564627e-01, 'ssd_conv_b': 4.207457e-01, 'ssd_dt_bias': 2.469968e+00, 'ssd_A_log': 7.685877e-01, 'ssd_D': 1.079766e+00, 'ssd_norm_w': 3.217485e-01, 'l0_w_out': 2.782924e-01, 'l0_norm_mlp': 3.313411e-01, 'l0_w_up': 1.593261e-01, 'l0_w_down': 3.466267e-01, 'l1_norm_mix': 2.231726e-01, 'l1_w_in': 1.013561e-01, 'gdn_conv_w': 9.800795e-02, 'gdn_A_log': 7.871709e-01, 'gdn_dt_bias': 8.030769e-01, 'gdn_norm_w': 4.344844e-01, 's5_A_re': 1.277830e-02, 's5_A_im': 1.883898e-02, 's5_log_step': 2.134827e+00, 's5_B_re': 6.541581e-03, 's5_B_im': 3.842060e-03, 's5_C_re': 7.540398e-03, 's5_C_im': 8.004746e-03, 's5_D': 7.854661e-02, 's5_w_glu': 2.260597e-02, 's5_b_glu': 4.107941e-02, 'l1_w_out': 1.252758e-01, 'l1_norm_mlp': 2.502434e-01, 'l1_w_up': 1.203889e-01, 'l1_w_down': 2.761400e-01, 'final_norm': 1.296750e+02}


def _to_microbatches(a, axis):
    t = _jnp.moveaxis(a, axis, 0)
    t = t.reshape((N_MICROBATCH, t.shape[0] // N_MICROBATCH) + t.shape[1:])
    return _jnp.moveaxis(t, 1, axis + 1)


def setup_inputs(seed: int = 0) -> dict:
    inp = _fwd_setup_inputs(seed)
    key = _jax.random.fold_in(_jax.random.key(seed), 7919)
    shape, _ = _output_shape()
    out = dict(inp)
    out["loss_target"] = _jax.random.normal(_jax.random.fold_in(key, 0), shape, _jnp.float32)
    for i, name in enumerate(TWIN_WEIGHTS):
        w = inp[name].astype(_jnp.float32)
        if MOMENT_SCALE is None:
            s = _jnp.sqrt(_jnp.mean(_jnp.square(w)) + 1e-30)
        else:
            s = MOMENT_SCALE[name]
        km, kv = _jax.random.split(_jax.random.fold_in(key, i + 1))
        out[name] = w
        out["m_" + name] = s * _jax.random.normal(km, w.shape, _jnp.float32)
        out["v_" + name] = (s * s) * _jax.random.uniform(kv, w.shape, _jnp.float32, 0.5, 1.5)
    if N_MICROBATCH > 1:
        for name, axis in PER_EXAMPLE_BATCH_AXIS.items():
            out[name] = _to_microbatches(out[name], axis)
    return {'x': out['x'], 'l0_norm_mix': out['l0_norm_mix'], 'l0_w_in': out['l0_w_in'], 'ssd_conv_w': out['ssd_conv_w'], 'ssd_conv_b': out['ssd_conv_b'], 'ssd_dt_bias': out['ssd_dt_bias'], 'ssd_A_log': out['ssd_A_log'], 'ssd_D': out['ssd_D'], 'ssd_norm_w': out['ssd_norm_w'], 'l0_w_out': out['l0_w_out'], 'l0_norm_mlp': out['l0_norm_mlp'], 'l0_w_up': out['l0_w_up'], 'l0_w_down': out['l0_w_down'], 'l1_norm_mix': out['l1_norm_mix'], 'l1_w_in': out['l1_w_in'], 'gdn_conv_w': out['gdn_conv_w'], 'gdn_A_log': out['gdn_A_log'], 'gdn_dt_bias': out['gdn_dt_bias'], 'gdn_norm_w': out['gdn_norm_w'], 's5_A_re': out['s5_A_re'], 's5_A_im': out['s5_A_im'], 's5_log_step': out['s5_log_step'], 's5_B_re': out['s5_B_re'], 's5_B_im': out['s5_B_im'], 's5_C_re': out['s5_C_re'], 's5_C_im': out['s5_C_im'], 's5_D': out['s5_D'], 's5_w_glu': out['s5_w_glu'], 's5_b_glu': out['s5_b_glu'], 'l1_w_out': out['l1_w_out'], 'l1_norm_mlp': out['l1_norm_mlp'], 'l1_w_up': out['l1_w_up'], 'l1_w_down': out['l1_w_down'], 'final_norm': out['final_norm'], 'loss_target': out['loss_target'], 'm_l0_norm_mix': out['m_l0_norm_mix'], 'm_l0_w_in': out['m_l0_w_in'], 'm_ssd_conv_w': out['m_ssd_conv_w'], 'm_ssd_conv_b': out['m_ssd_conv_b'], 'm_ssd_dt_bias': out['m_ssd_dt_bias'], 'm_ssd_A_log': out['m_ssd_A_log'], 'm_ssd_D': out['m_ssd_D'], 'm_ssd_norm_w': out['m_ssd_norm_w'], 'm_l0_w_out': out['m_l0_w_out'], 'm_l0_norm_mlp': out['m_l0_norm_mlp'], 'm_l0_w_up': out['m_l0_w_up'], 'm_l0_w_down': out['m_l0_w_down'], 'm_l1_norm_mix': out['m_l1_norm_mix'], 'm_l1_w_in': out['m_l1_w_in'], 'm_gdn_conv_w': out['m_gdn_conv_w'], 'm_gdn_A_log': out['m_gdn_A_log'], 'm_gdn_dt_bias': out['m_gdn_dt_bias'], 'm_gdn_norm_w': out['m_gdn_norm_w'], 'm_s5_A_re': out['m_s5_A_re'], 'm_s5_A_im': out['m_s5_A_im'], 'm_s5_log_step': out['m_s5_log_step'], 'm_s5_B_re': out['m_s5_B_re'], 'm_s5_B_im': out['m_s5_B_im'], 'm_s5_C_re': out['m_s5_C_re'], 'm_s5_C_im': out['m_s5_C_im'], 'm_s5_D': out['m_s5_D'], 'm_s5_w_glu': out['m_s5_w_glu'], 'm_s5_b_glu': out['m_s5_b_glu'], 'm_l1_w_out': out['m_l1_w_out'], 'm_l1_norm_mlp': out['m_l1_norm_mlp'], 'm_l1_w_up': out['m_l1_w_up'], 'm_l1_w_down': out['m_l1_w_down'], 'm_final_norm': out['m_final_norm'], 'v_l0_norm_mix': out['v_l0_norm_mix'], 'v_l0_w_in': out['v_l0_w_in'], 'v_ssd_conv_w': out['v_ssd_conv_w'], 'v_ssd_conv_b': out['v_ssd_conv_b'], 'v_ssd_dt_bias': out['v_ssd_dt_bias'], 'v_ssd_A_log': out['v_ssd_A_log'], 'v_ssd_D': out['v_ssd_D'], 'v_ssd_norm_w': out['v_ssd_norm_w'], 'v_l0_w_out': out['v_l0_w_out'], 'v_l0_norm_mlp': out['v_l0_norm_mlp'], 'v_l0_w_up': out['v_l0_w_up'], 'v_l0_w_down': out['v_l0_w_down'], 'v_l1_norm_mix': out['v_l1_norm_mix'], 'v_l1_w_in': out['v_l1_w_in'], 'v_gdn_conv_w': out['v_gdn_conv_w'], 'v_gdn_A_log': out['v_gdn_A_log'], 'v_gdn_dt_bias': out['v_gdn_dt_bias'], 'v_gdn_norm_w': out['v_gdn_norm_w'], 'v_s5_A_re': out['v_s5_A_re'], 'v_s5_A_im': out['v_s5_A_im'], 'v_s5_log_step': out['v_s5_log_step'], 'v_s5_B_re': out['v_s5_B_re'], 'v_s5_B_im': out['v_s5_B_im'], 'v_s5_C_re': out['v_s5_C_re'], 'v_s5_C_im': out['v_s5_C_im'], 'v_s5_D': out['v_s5_D'], 'v_s5_w_glu': out['v_s5_w_glu'], 'v_s5_b_glu': out['v_s5_b_glu'], 'v_l1_w_out': out['v_l1_w_out'], 'v_l1_norm_mlp': out['v_l1_norm_mlp'], 'v_l1_w_up': out['v_l1_w_up'], 'v_l1_w_down': out['v_l1_w_down'], 'v_final_norm': out['v_final_norm']}


def _loss(weights, diff, rest, loss_target):
    with _jax.named_scope("forward"):
        args = {**rest, TWIN_DIFF_INPUT: diff, **{k: w.astype(_WEIGHT_DTYPES[k]) for k, w in weights.items()}}
        y = _forward(args)
    with _jax.named_scope("loss_head"):
        err = _jnp.square(y.astype(_jnp.float32) - loss_target)
        return 0.5 * _jnp.sum(_jnp.mean(err, axis=-1)) if err.ndim else 0.5 * err


def _adamw(w, g, m, v):
    m = ADAM_B1 * m + (1.0 - ADAM_B1) * g
    v = ADAM_B2 * v + (1.0 - ADAM_B2) * _jnp.square(g)
    m_hat = m / (1.0 - ADAM_B1 ** ADAM_STEP)
    v_hat = v / (1.0 - ADAM_B2 ** ADAM_STEP)
    delta = -ADAM_LR * (m_hat / (_jnp.sqrt(v_hat) + ADAM_EPS) + ADAM_WD * w)
    return delta, m, v


def reference(x, l0_norm_mix, l0_w_in, ssd_conv_w, ssd_conv_b, ssd_dt_bias, ssd_A_log, ssd_D, ssd_norm_w, l0_w_out, l0_norm_mlp, l0_w_up, l0_w_down, l1_norm_mix, l1_w_in, gdn_conv_w, gdn_A_log, gdn_dt_bias, gdn_norm_w, s5_A_re, s5_A_im, s5_log_step, s5_B_re, s5_B_im, s5_C_re, s5_C_im, s5_D, s5_w_glu, s5_b_glu, l1_w_out, l1_norm_mlp, l1_w_up, l1_w_down, final_norm, loss_target, m_l0_norm_mix, m_l0_w_in, m_ssd_conv_w, m_ssd_conv_b, m_ssd_dt_bias, m_ssd_A_log, m_ssd_D, m_ssd_norm_w, m_l0_w_out, m_l0_norm_mlp, m_l0_w_up, m_l0_w_down, m_l1_norm_mix, m_l1_w_in, m_gdn_conv_w, m_gdn_A_log, m_gdn_dt_bias, m_gdn_norm_w, m_s5_A_re, m_s5_A_im, m_s5_log_step, m_s5_B_re, m_s5_B_im, m_s5_C_re, m_s5_C_im, m_s5_D, m_s5_w_glu, m_s5_b_glu, m_l1_w_out, m_l1_norm_mlp, m_l1_w_up, m_l1_w_down, m_final_norm, v_l0_norm_mix, v_l0_w_in, v_ssd_conv_w, v_ssd_conv_b, v_ssd_dt_bias, v_ssd_A_log, v_ssd_D, v_ssd_norm_w, v_l0_w_out, v_l0_norm_mlp, v_l0_w_up, v_l0_w_down, v_l1_norm_mix, v_l1_w_in, v_gdn_conv_w, v_gdn_A_log, v_gdn_dt_bias, v_gdn_norm_w, v_s5_A_re, v_s5_A_im, v_s5_log_step, v_s5_B_re, v_s5_B_im, v_s5_C_re, v_s5_C_im, v_s5_D, v_s5_w_glu, v_s5_b_glu, v_l1_w_out, v_l1_norm_mlp, v_l1_w_up, v_l1_w_down, v_final_norm):
    given = dict(x=x, l0_norm_mix=l0_norm_mix, l0_w_in=l0_w_in, ssd_conv_w=ssd_conv_w, ssd_conv_b=ssd_conv_b, ssd_dt_bias=ssd_dt_bias, ssd_A_log=ssd_A_log, ssd_D=ssd_D, ssd_norm_w=ssd_norm_w, l0_w_out=l0_w_out, l0_norm_mlp=l0_norm_mlp, l0_w_up=l0_w_up, l0_w_down=l0_w_down, l1_norm_mix=l1_norm_mix, l1_w_in=l1_w_in, gdn_conv_w=gdn_conv_w, gdn_A_log=gdn_A_log, gdn_dt_bias=gdn_dt_bias, gdn_norm_w=gdn_norm_w, s5_A_re=s5_A_re, s5_A_im=s5_A_im, s5_log_step=s5_log_step, s5_B_re=s5_B_re, s5_B_im=s5_B_im, s5_C_re=s5_C_re, s5_C_im=s5_C_im, s5_D=s5_D, s5_w_glu=s5_w_glu, s5_b_glu=s5_b_glu, l1_w_out=l1_w_out, l1_norm_mlp=l1_norm_mlp, l1_w_up=l1_w_up, l1_w_down=l1_w_down, final_norm=final_norm, loss_target=loss_target, m_l0_norm_mix=m_l0_norm_mix, m_l0_w_in=m_l0_w_in, m_ssd_conv_w=m_ssd_conv_w, m_ssd_conv_b=m_ssd_conv_b, m_ssd_dt_bias=m_ssd_dt_bias, m_ssd_A_log=m_ssd_A_log, m_ssd_D=m_ssd_D, m_ssd_norm_w=m_ssd_norm_w, m_l0_w_out=m_l0_w_out, m_l0_norm_mlp=m_l0_norm_mlp, m_l0_w_up=m_l0_w_up, m_l0_w_down=m_l0_w_down, m_l1_norm_mix=m_l1_norm_mix, m_l1_w_in=m_l1_w_in, m_gdn_conv_w=m_gdn_conv_w, m_gdn_A_log=m_gdn_A_log, m_gdn_dt_bias=m_gdn_dt_bias, m_gdn_norm_w=m_gdn_norm_w, m_s5_A_re=m_s5_A_re, m_s5_A_im=m_s5_A_im, m_s5_log_step=m_s5_log_step, m_s5_B_re=m_s5_B_re, m_s5_B_im=m_s5_B_im, m_s5_C_re=m_s5_C_re, m_s5_C_im=m_s5_C_im, m_s5_D=m_s5_D, m_s5_w_glu=m_s5_w_glu, m_s5_b_glu=m_s5_b_glu, m_l1_w_out=m_l1_w_out, m_l1_norm_mlp=m_l1_norm_mlp, m_l1_w_up=m_l1_w_up, m_l1_w_down=m_l1_w_down, m_final_norm=m_final_norm, v_l0_norm_mix=v_l0_norm_mix, v_l0_w_in=v_l0_w_in, v_ssd_conv_w=v_ssd_conv_w, v_ssd_conv_b=v_ssd_conv_b, v_ssd_dt_bias=v_ssd_dt_bias, v_ssd_A_log=v_ssd_A_log, v_ssd_D=v_ssd_D, v_ssd_norm_w=v_ssd_norm_w, v_l0_w_out=v_l0_w_out, v_l0_norm_mlp=v_l0_norm_mlp, v_l0_w_up=v_l0_w_up, v_l0_w_down=v_l0_w_down, v_l1_norm_mix=v_l1_norm_mix, v_l1_w_in=v_l1_w_in, v_gdn_conv_w=v_gdn_conv_w, v_gdn_A_log=v_gdn_A_log, v_gdn_dt_bias=v_gdn_dt_bias, v_gdn_norm_w=v_gdn_norm_w, v_s5_A_re=v_s5_A_re, v_s5_A_im=v_s5_A_im, v_s5_log_step=v_s5_log_step, v_s5_B_re=v_s5_B_re, v_s5_B_im=v_s5_B_im, v_s5_C_re=v_s5_C_re, v_s5_C_im=v_s5_C_im, v_s5_D=v_s5_D, v_s5_w_glu=v_s5_w_glu, v_s5_b_glu=v_s5_b_glu, v_l1_w_out=v_l1_w_out, v_l1_norm_mlp=v_l1_norm_mlp, v_l1_w_up=v_l1_w_up, v_l1_w_down=v_l1_w_down, v_final_norm=v_final_norm)
    weights = {n: given[n] for n in TWIN_WEIGHTS}
    shared = {n: given[n] for n in SHARED_INPUTS}
    per_example = {n: given[n] for n in ['x']}
    grad_fn = _jax.value_and_grad(_loss, argnums=(0, 1))

    def one_microbatch(ex, loss_target):
        ex = dict(ex)
        diff = ex.pop(TWIN_DIFF_INPUT)
        return grad_fn(weights, diff, {**shared, **ex}, loss_target)

    if N_MICROBATCH == 1:
        loss, (grad_w, grad_x) = one_microbatch(per_example, given["loss_target"])
    else:
        def body(carry, xs):
            loss_sum, grad_sum = carry
            l_k, (gw_k, gx_k) = one_microbatch(xs[0], xs[1])
            with _jax.named_scope("update"):
                return (loss_sum + l_k, _jax.tree.map(_jnp.add, grad_sum, gw_k)), gx_k

        init = (_jnp.zeros((), _jnp.float32), _jax.tree.map(_jnp.zeros_like, weights))
        (loss, grad_w), grad_x = _jax.lax.scan(body, init, (per_example, given["loss_target"]))
    with _jax.named_scope("update"):
        delta_w, new_m, new_v = {}, {}, {}
        for n in TWIN_WEIGHTS:
            delta_w[n], new_m[n], new_v[n] = _adamw(weights[n], grad_w[n], given["m_" + n], given["v_" + n])
    return (loss, grad_x, *[grad_w[n] for n in TWIN_WEIGHTS], *[delta_w[n] for n in TWIN_WEIGHTS],
            *[new_m[n] for n in TWIN_WEIGHTS], *[new_v[n] for n in TWIN_WEIGHTS])
```

```python
import functools
import math

import numpy as np
import jax
import jax.numpy as jnp
from jax import lax
from jax.experimental import pallas as pl
from jax.experimental.pallas import tpu as pltpu

f32 = jnp.float32
_MXU = jnp.bfloat16
HI = lax.Precision.HIGHEST

D_MODEL = 1024
CHUNK = 64
EPS = 1e-6
N_DEV = 8
LANES = 128
HALO = 8
CONV_WIDTH = 4

RET_HEADS, RET_D = 4, 128
SSD_HEADS, SSD_P, SSD_N, SSD_GROUPS = 8, 64, 128, 2
SSD_INNER = SSD_HEADS * SSD_P
GDN_HEADS, GDN_D = 6, 128
GDN_W = GDN_HEADS * GDN_D
S5_CH, S5_GROUP, S5_GROUPS, S5_STATE = 256, 16, 16, 64
S5_W = S5_GROUPS * S5_STATE
D_FF = 4096
ROPE_THETA = 10000.0

IN0_W = 3592
IN0_PAD = 3712
IN1_W = 3340
IN1_PAD = 3456

ADAM_LR, ADAM_B1, ADAM_B2, ADAM_EPS, ADAM_WD, ADAM_STEP = 0.001, 0.9, 0.999, 1e-08, 0.01, 10

VMEM_LIMIT = 56 * 1024 * 1024


def _dot(a, b, dims):
    return lax.dot_general(a.astype(_MXU), b.astype(_MXU), (dims, ((), ())), preferred_element_type=f32)


@jax.custom_vjp
def mm(a, b):
    return _dot(a, b, ((1,), (0,)))


@jax.custom_vjp
def mm_nt(a, b):
    return _dot(a, b, ((1,), (1,)))


@jax.custom_vjp
def mm_tn(a, b):
    return _dot(a, b, ((0,), (0,)))


mm.defvjp(lambda a, b: (mm(a, b), (a, b)), lambda r, g: (mm_nt(g, r[1]), mm_tn(r[0], g)))
mm_nt.defvjp(lambda a, b: (mm_nt(a, b), (a, b)), lambda r, g: (mm(g, r[1]), mm_tn(g, r[0])))
mm_tn.defvjp(lambda a, b: (mm_tn(a, b), (a, b)), lambda r, g: (mm_nt(r[1], g), mm(r[0], g)))


def mmh(a, b):
    return jnp.dot(a, b, precision=HI, preferred_element_type=f32)


def _roll(x, shift, axis):
    return pltpu.roll(x, shift, axis)


@functools.partial(jax.custom_vjp, nondiff_argnums=(1,))
def roll_rows(x, s):
    return _roll(x, s, 0) if s else x


roll_rows.defvjp(lambda x, s: (roll_rows(x, s), None),
                 lambda s, _, g: ((_roll(g, g.shape[0] - s, 0) if s else g),))


@jax.custom_vjp
def roll_half(x):
    return _roll(x, x.shape[-1] // 2, 1)


roll_half.defvjp(lambda x: (roll_half(x), None), lambda _, g: (roll_half(g),))


def _iota(shape, axis):
    return lax.broadcasted_iota(jnp.int32, shape, axis)


def silu(x):
    return x * jax.nn.sigmoid(x)


def softplus(x):
    return jnp.maximum(x, 0.0) + jnp.log(1.0 + jnp.exp(-jnp.abs(x)))


def rmsnorm_f(x, w):
    return x * lax.rsqrt(jnp.mean(x * x, axis=-1, keepdims=True) + EPS) * w


def unit_rms(x):
    return x * lax.rsqrt(jnp.mean(x * x, axis=-1, keepdims=True) + EPS)


def _causal(n, strict=False):
    r, c = _iota((n, n), 0), _iota((n, n), 1)
    return (r > c) if strict else (r >= c)


def _tril_ones(n):
    return _causal(n).astype(f32)


def _conv_rows(xe, w):
    acc = w[CONV_WIDTH - 1:CONV_WIDTH, :] * xe
    for j in range(CONV_WIDTH - 1):
        acc = acc + w[j:j + 1, :] * roll_rows(xe, CONV_WIDTH - 1 - j)
    return acc[HALO:, :]


_RET_LOG_GAMMA = [float(np.log(np.float32(1.0) - np.float32(2.0) ** np.float32(-5.0 - h))) for h in range(RET_HEADS)]


def ret_chunk(q, k, v, gate, cos, sin, state):
    c = q.shape[0]
    idx = _iota((c, 1), 0).astype(f32)
    diff = (_iota((c, c), 0) - _iota((c, c), 1)).astype(f32)
    causal = _causal(c)
    outs, states = [], []
    for h in range(RET_HEADS):
        lg = _RET_LOG_GAMMA[h]
        sl = slice(h * RET_D, (h + 1) * RET_D)
        qh, kh, vh, sh = q[:, sl], k[:, sl], v[:, sl], state[sl, :]
        qh = (qh * cos + roll_half(qh) * sin) * (RET_D ** -0.5)
        kh = kh * cos + roll_half(kh) * sin
        dmask = jnp.exp(jnp.where(causal, lg * diff, -jnp.inf))
        y = mm(mm_nt(qh, kh) * dmask, vh) + mm(qh * jnp.exp(lg * (idx + 1.0)), sh)
        states.append(sh * math.exp(lg * c) + mm_tn(kh * jnp.exp(lg * (c - 1.0 - idx)), vh))
        outs.append(unit_rms(y) * silu(gate[:, sl]))
    return jnp.concatenate(outs, axis=1), jnp.concatenate(states, axis=0)


def _head_select(n_heads, width):
    r, c = _iota((LANES, n_heads * width), 0), _iota((LANES, n_heads * width), 1)
    return (c // width == r).astype(f32)


def ssd_chunk(z, xe, dtr, state, conv_w, conv_b, dt_bias, a_log, d_skip, norm_w):
    c = z.shape[0]
    xbc = silu(_conv_rows(xe, conv_w) + conv_b)
    xs, bm, cm = xbc[:, :SSD_INNER], xbc[:, SSD_INNER:SSD_INNER + 256], xbc[:, SSD_INNER + 256:]
    sel = _head_select(SSD_HEADS, SSD_P)
    dt = softplus(dtr + dt_bias)
    la = dt * (-jnp.exp(a_log))
    la_cum = mmh(_tril_ones(c), la)
    la_cum_t = la_cum.T
    last = jnp.sum(la, axis=0, keepdims=True)
    xd = xs * mmh(dt, sel)
    la_x = mmh(la_cum, sel)
    last_x = mmh(last, sel)
    to_end = jnp.exp(last_x - la_x)
    from_start = jnp.exp(la_x)
    causal = _causal(c)
    left = (_iota((1, LANES), 1) < SSD_P).astype(f32)
    upper = _iota((LANES, 1), 0) < SSD_P
    ys, states = [], []
    for p in range(SSD_HEADS // 2):
        g = p // 2
        bc, cc = bm[:, g * SSD_N:(g + 1) * SSD_N], cm[:, g * SSD_N:(g + 1) * SSD_N]
        cb = mm_nt(cc, bc)
        sl = slice(p * LANES, (p + 1) * LANES)
        xd_p, sp = xd[:, sl], state[sl, :]
        y = mm_nt(cc, sp) * from_start[:, sl]
        for i, mask in ((0, left), (1, 1.0 - left)):
            h = 2 * p + i
            seg = la_cum[:, h:h + 1] - la_cum_t[h:h + 1, :]
            y = y + mm(cb * jnp.exp(jnp.where(causal, seg, -jnp.inf)), xd_p * mask)
        cd = jnp.where(upper, jnp.exp(last[:, 2 * p:2 * p + 1]), jnp.exp(last[:, 2 * p + 1:2 * p + 2]))
        states.append(sp * cd + mm_tn(xd_p * to_end[:, sl], bc))
        ys.append(y)
    y = jnp.concatenate(ys, axis=1) + mmh(d_skip, sel) * xs
    yg = y * silu(z)
    half = SSD_INNER // SSD_GROUPS
    out = jnp.concatenate([unit_rms(yg[:, i * half:(i + 1) * half]) for i in range(SSD_GROUPS)], axis=1) * norm_w
    return out, jnp.concatenate(states, axis=0)


def _unit_lower_inverse(lower):
    n = lower.shape[0]
    eye = (_iota((n, n), 0) == _iota((n, n), 1)).astype(f32)
    a = -lower
    p = eye + a
    k = 2
    while k < n:
        a = mmh(a, a)
        p = p + mmh(p, a)
        k *= 2
    return p


GDN_GCOL = 6


def gdn_chunk(xe, z, ba, state, conv_w, a_log, dt_bias, norm_w):
    c = z.shape[0]
    qkv = silu(_conv_rows(xe, conv_w))
    beta_all = jax.nn.sigmoid(ba)
    g_all = -jnp.exp(a_log) * softplus(ba + dt_bias)
    gc = mmh(_tril_ones(c), g_all)
    gc_t = gc.T
    last = jnp.sum(g_all, axis=0, keepdims=True)
    causal, strict = _causal(c), _causal(c, strict=True)
    outs, states = [], []
    for h in range(GDN_HEADS):
        sl = slice(h * GDN_D, (h + 1) * GDN_D)
        qh, kh, vh = qkv[:, sl], qkv[:, GDN_W + h * GDN_D:GDN_W + (h + 1) * GDN_D], qkv[:, 2 * GDN_W + h * GDN_D:2 * GDN_W + (h + 1) * GDN_D]
        qh = qh * lax.rsqrt(jnp.sum(qh * qh, axis=-1, keepdims=True) + EPS) * (GDN_D ** -0.5)
        kh = kh * lax.rsqrt(jnp.sum(kh * kh, axis=-1, keepdims=True) + EPS)
        beta = beta_all[:, h:h + 1]
        col, row = gc[:, GDN_GCOL + h:GDN_GCOL + h + 1], gc_t[GDN_GCOL + h:GDN_GCOL + h + 1, :]
        lst = last[:, GDN_GCOL + h:GDN_GCOL + h + 1]
        decay = jnp.exp(jnp.where(causal, col - row, -jnp.inf))
        kb, vb = kh * beta, vh * beta
        t_inv = _unit_lower_inverse(jnp.where(strict, mm_nt(kb, kh) * decay, 0.0))
        u = mm(t_inv, vb)
        w = mm(t_inv, kb * jnp.exp(col))
        attn = jnp.where(causal, mm_nt(qh, kh) * decay, 0.0)
        sh = state[sl, :]
        v_new = u - mm(w, sh)
        o = mm(qh * jnp.exp(col), sh) + mm(attn, v_new)
        states.append(sh * jnp.exp(lst) + mm_tn(kh * jnp.exp(lst - col), v_new))
        outs.append(unit_rms(o) * norm_w * silu(z[:, sl]))
    return jnp.concatenate(outs, axis=1), jnp.concatenate(states, axis=0)


def _s5_group_mask():
    r, c = _iota((S5_CH, S5_W), 0), _iota((S5_CH, S5_W), 1)
    return (r // S5_GROUP == c // S5_STATE).astype(f32)


def s5_prep(a_re, a_im, log_step, b_re, b_im, c_re, c_im):
    r, c = _iota((LANES, S5_W), 0), _iota((LANES, S5_W), 1)
    step = jnp.exp(mmh(log_step, (c // S5_STATE == r).astype(f32)))
    zr, zi = a_re * step, a_im * step
    e = jnp.exp(zr)
    lr, li = e * jnp.cos(zi), e * jnp.sin(zi)
    den = a_re * a_re + a_im * a_im
    xr, xi = lr - 1.0, li
    cr, ci = (xr * a_re + xi * a_im) / den, (xi * a_re - xr * a_im) / den
    bbr, bbi = cr * b_re - ci * b_im, cr * b_im + ci * b_re
    mask = _s5_group_mask()
    tile = lambda t: jnp.tile(t, (S5_GROUPS, 1)) * mask
    return lr, li, tile(bbr), tile(bbi), tile(c_re), tile(c_im)


def s5_out_fn(h_re, h_im, u, cblk_re, cblk_im, d_skip, w_glu, b_glu):
    y = mm_nt(h_re, cblk_re) - mm_nt(h_im, cblk_im) + d_skip * u
    y = jax.nn.gelu(y)
    return y * jax.nn.sigmoid(mm(y, w_glu) + b_glu)


def _params(sem, **kw):
    return pltpu.CompilerParams(dimension_semantics=sem, vmem_limit_bytes=VMEM_LIMIT, **kw)


def _const_spec(shape):
    return pl.BlockSpec(shape, lambda i: (0,) * len(shape))


def _resident_spec(shape):
    return pl.BlockSpec(shape, lambda i: (0,) * len(shape), pipeline_mode=pl.Buffered(1))


def _row_spec(rows, cols, col_block=0):
    return pl.BlockSpec((rows, cols), lambda i: (i, col_block))


class Seq:
    def __init__(self, array, width, col_block, kind="tile"):
        self.array, self.width, self.col_block, self.kind = array, width, col_block, kind


def mixer_fwd(name, fn, seqs, params, out_width, state_rows, seq_len):
    nc = seq_len // CHUNK
    n_refs = sum(2 if s.kind == "halo" else 1 for s in seqs)

    def body(*refs):
        seq_refs, par_refs = refs[:n_refs], refs[n_refs:n_refs + len(params)]
        out_ref, st_ref, state = refs[n_refs + len(params):]
        c = pl.program_id(0)

        @pl.when(c == 0)
        def _():
            state[...] = jnp.zeros_like(state)

        vals, k = [], 0
        for s in seqs:
            if s.kind == "halo":
                prev = jnp.where(c > 0, seq_refs[k][...], 0.0)
                vals.append(jnp.concatenate([prev, seq_refs[k + 1][...]], axis=0))
                k += 2
            else:
                vals.append(seq_refs[k][...])
                k += 1
        s_in = state[...]
        st_ref[...] = s_in
        out, s_new = fn(*vals, s_in, *[p[...] for p in par_refs])
        out_ref[...] = out
        state[...] = s_new

    in_specs, operands = [], []
    for s in seqs:
        if s.kind == "halo":
            rb, w, cb = CHUNK // HALO, s.width, s.col_block
            in_specs.append(pl.BlockSpec((HALO, w), lambda i, rb=rb, cb=cb: (jnp.maximum(i * rb - 1, 0), cb)))
            operands.append(s.array)
        in_specs.append(pl.BlockSpec((CHUNK, s.width), lambda i, cb=s.col_block: (i, cb)))
        operands.append(s.array)
    for p in params:
        in_specs.append(_const_spec(p.shape))
        operands.append(p)
    return pl.pallas_call(
        body, grid=(nc,), in_specs=in_specs,
        out_specs=[pl.BlockSpec((CHUNK, out_width), lambda i: (i, 0)),
                   pl.BlockSpec((None, state_rows, LANES), lambda i: (i, 0, 0))],
        out_shape=[jax.ShapeDtypeStruct((seq_len, out_width), f32),
                   jax.ShapeDtypeStruct((nc, state_rows, LANES), f32)],
        scratch_shapes=[pltpu.VMEM((state_rows, LANES), f32)],
        compiler_params=_params(("arbitrary",)), name=name)(*operands)


def mixer_bwd(name, fn, seqs, params, states, d_out, seq_len):
    nc = seq_len // CHUNK
    state_rows = states.shape[1]
    diff = [s for s in seqs if s.kind != "const"]
    halos = [s for s in diff if s.kind == "halo"]
    n_refs = sum(2 if s.kind == "halo" else 1 for s in seqs)
    n_par = len(params)

    def body(*refs):
        seq_refs, par_refs = refs[:n_refs], refs[n_refs:n_refs + n_par]
        st_ref, dout_ref = refs[n_refs + n_par:n_refs + n_par + 2]
        k0 = n_refs + n_par + 2
        dseq_refs, dpar_refs = refs[k0:k0 + len(diff)], refs[k0 + len(diff):k0 + len(diff) + n_par]
        scratch = refs[k0 + len(diff) + n_par:]
        d_state, carries = scratch[0], scratch[1:]
        i = pl.program_id(0)
        c = nc - 1 - i

        @pl.when(i == 0)
        def _():
            d_state[...] = jnp.zeros_like(d_state)
            for r in list(carries) + list(dpar_refs):
                r[...] = jnp.zeros_like(r)

        dvals, consts, k = [], [], 0
        for s in seqs:
            if s.kind == "halo":
                prev = jnp.where(c > 0, seq_refs[k][...], 0.0)
                dvals.append(jnp.concatenate([prev, seq_refs[k + 1][...]], axis=0))
                k += 2
            elif s.kind == "tile":
                dvals.append(seq_refs[k][...])
                k += 1
            else:
                consts.append(seq_refs[k][...])
                k += 1
        nd = len(dvals)

        def call(*a):
            it_d, it_c = iter(a[:nd]), iter(consts)
            vals = [next(it_c) if s.kind == "const" else next(it_d) for s in seqs]
            return fn(*vals, *a[nd:])

        _, vjp = jax.vjp(call, *dvals, st_ref[...], *[p[...] for p in par_refs])
        cts = vjp((dout_ref[...], d_state[...]))
        hk = 0
        for j, s in enumerate(diff):
            if s.kind == "halo":
                dseq_refs[j][...] = cts[j][HALO:, :]
                dseq_refs[j][CHUNK - HALO:, :] += carries[hk][...]
                carries[hk][...] = cts[j][:HALO, :]
                hk += 1
            else:
                dseq_refs[j][...] = cts[j]
        d_state[...] = cts[nd]
        for j in range(n_par):
            dpar_refs[j][...] += cts[nd + 1 + j]

    in_specs, operands = [], []
    for s in seqs:
        if s.kind == "halo":
            rb, cb = CHUNK // HALO, s.col_block
            in_specs.append(pl.BlockSpec((HALO, s.width), lambda i, rb=rb, cb=cb: (jnp.maximum((nc - 1 - i) * rb - 1, 0), cb)))
            operands.append(s.array)
        in_specs.append(pl.BlockSpec((CHUNK, s.width), lambda i, cb=s.col_block: (nc - 1 - i, cb)))
        operands.append(s.array)
    for p in params:
        in_specs.append(_const_spec(p.shape))
        operands.append(p)
    in_specs.append(pl.BlockSpec((None, state_rows, LANES), lambda i: (nc - 1 - i, 0, 0)))
    in_specs.append(pl.BlockSpec((CHUNK, d_out.shape[1]), lambda i: (nc - 1 - i, 0)))
    operands += [states, d_out]
    outs = pl.pallas_call(
        body, grid=(nc,), in_specs=in_specs,
        out_specs=[pl.BlockSpec((CHUNK, s.width), lambda i: (nc - 1 - i, 0)) for s in diff]
        + [_const_spec(p.shape) for p in params],
        out_shape=[jax.ShapeDtypeStruct((seq_len, s.width), f32) for s in diff]
        + [jax.ShapeDtypeStruct(p.shape, f32) for p in params],
        scratch_shapes=[pltpu.VMEM((state_rows, LANES), f32)] + [pltpu.VMEM((HALO, s.width), f32) for s in halos],
        compiler_params=_params(("arbitrary",)), name=name)(*operands)
    return outs[:len(diff)], outs[len(diff):]


TOK = 256


def inproj_fwd(name, x, nw, w):
    seq_len, n = x.shape[0], w.shape[1]

    def body(x_ref, nw_ref, w_ref, o_ref):
        o_ref[...] = mm(rmsnorm_f(x_ref[...], nw_ref[...]), w_ref[...])

    return pl.pallas_call(
        body, grid=(seq_len // TOK,),
        in_specs=[_row_spec(TOK, D_MODEL), _const_spec(nw.shape), _resident_spec(w.shape)],
        out_specs=_row_spec(TOK, n), out_shape=jax.ShapeDtypeStruct((seq_len, n), f32),
        compiler_params=_params(("arbitrary",)), name=name)(x, nw, w)


def inproj_bwd(name, x, nw, w, pieces, d_res):
    seq_len, n = x.shape[0], w.shape[1]
    widths = [p.shape[1] for p in pieces]
    assert sum(widths) == n
    k = len(pieces)

    def body(*refs):
        x_ref, nw_ref, w_ref = refs[:3]
        p_refs, dres_ref = refs[3:3 + k], refs[3 + k]
        dx_ref, dw_ref, dnw_ref = refs[4 + k:]

        @pl.when(pl.program_id(0) == 0)
        def _():
            dw_ref[...] = jnp.zeros_like(dw_ref)
            dnw_ref[...] = jnp.zeros_like(dnw_ref)

        h, vjp = jax.vjp(rmsnorm_f, x_ref[...], nw_ref[...])
        dh, off = jnp.zeros_like(h), 0
        for p_ref, wd in zip(p_refs, widths):
            g = p_ref[...]
            dh = dh + mm_nt(g, w_ref[:, off:off + wd])
            dw_ref[:, off:off + wd] += mm_tn(h, g)
            off += wd
        dx, dnw = vjp(dh)
        dx_ref[...] = dres_ref[...] + dx
        dnw_ref[...] += dnw

    return pl.pallas_call(
        body, grid=(seq_len // TOK,),
        in_specs=[_row_spec(TOK, D_MODEL), _const_spec(nw.shape), _resident_spec(w.shape)]
        + [_row_spec(TOK, wd) for wd in widths] + [_row_spec(TOK, D_MODEL)],
        out_specs=[_row_spec(TOK, D_MODEL), _resident_spec((D_MODEL, n)), _const_spec(nw.shape)],
        out_shape=[jax.ShapeDtypeStruct((seq_len, D_MODEL), f32), jax.ShapeDtypeStruct((D_MODEL, n), f32),
                   jax.ShapeDtypeStruct(nw.shape, f32)],
        compiler_params=_params(("arbitrary",)), name=name)(x, nw, w, *pieces, d_res)


def outproj_fwd(name, x, a, b, w):
    seq_len, wa, wb = x.shape[0], a.shape[1], b.shape[1]

    def body(x_ref, a_ref, b_ref, w_ref, o_ref):
        o_ref[...] = x_ref[...] + mm(a_ref[...], w_ref[:wa, :]) + mm(b_ref[...], w_ref[wa:, :])

    return pl.pallas_call(
        body, grid=(seq_len // TOK,),
        in_specs=[_row_spec(TOK, D_MODEL), _row_spec(TOK, wa), _row_spec(TOK, wb), _resident_spec(w.shape)],
        out_specs=_row_spec(TOK, D_MODEL), out_shape=jax.ShapeDtypeStruct((seq_len, D_MODEL), f32),
        compiler_params=_params(("arbitrary",)), name=name)(x, a, b, w)


def outproj_bwd(name, dy, a, b, w):
    seq_len, wa, wb = dy.shape[0], a.shape[1], b.shape[1]

    def body(dy_ref, a_ref, b_ref, w_ref, da_ref, db_ref, dw_ref):
        @pl.when(pl.program_id(0) == 0)
        def _():
            dw_ref[...] = jnp.zeros_like(dw_ref)

        g = dy_ref[...]
        da_ref[...] = mm_nt(g, w_ref[:wa, :])
        db_ref[...] = mm_nt(g, w_ref[wa:, :])
        dw_ref[:wa, :] += mm_tn(a_ref[...], g)
        dw_ref[wa:, :] += mm_tn(b_ref[...], g)

    return pl.pallas_call(
        body, grid=(seq_len // TOK,),
        in_specs=[_row_spec(TOK, D_MODEL), _row_spec(TOK, wa), _row_spec(TOK, wb), _resident_spec(w.shape)],
        out_specs=[_row_spec(TOK, wa), _row_spec(TOK, wb), _resident_spec(w.shape)],
        out_shape=[jax.ShapeDtypeStruct((seq_len, wa), f32), jax.ShapeDtypeStruct((seq_len, wb), f32),
                   jax.ShapeDtypeStruct(w.shape, f32)],
        compiler_params=_params(("arbitrary",)), name=name)(dy, a, b, w)


FF_BLOCK = D_FF // N_DEV


def mlp_fwd(name, x, nw, w_up, w_down):
    seq_len = x.shape[0]

    def body(x_ref, nw_ref, up_ref, down_ref, o_ref):
        xv = x_ref[...]
        h = rmsnorm_f(xv, nw_ref[...])
        acc = xv
        for d in range(N_DEV):
            r = jnp.maximum(mm(h, up_ref[d]), 0.0)
            acc = acc + mm(r * r, down_ref[d])
        o_ref[...] = acc

    return pl.pallas_call(
        body, grid=(seq_len // TOK,),
        in_specs=[_row_spec(TOK, D_MODEL), _const_spec(nw.shape), _resident_spec(w_up.shape), _resident_spec(w_down.shape)],
        out_specs=_row_spec(TOK, D_MODEL), out_shape=jax.ShapeDtypeStruct((seq_len, D_MODEL), f32),
        compiler_params=_params(("arbitrary",)), name=name)(x, nw, w_up, w_down)


MLP_SPLIT = 2


def mlp_bwd(name, x, nw, w_up, w_down, dy):
    seq_len = x.shape[0]
    nt = seq_len // TOK
    per = N_DEV // MLP_SPLIT

    def body(x_ref, nw_ref, up_ref, down_ref, dy_ref, dh_ref, dup_ref, ddown_ref):
        @pl.when(pl.program_id(1) == 0)
        def _():
            dup_ref[...] = jnp.zeros_like(dup_ref)
            ddown_ref[...] = jnp.zeros_like(ddown_ref)

        h = rmsnorm_f(x_ref[...], nw_ref[...])
        g = dy_ref[...]
        dh = jnp.zeros_like(h)
        for d in range(per):
            r = jnp.maximum(mm(h, up_ref[d]), 0.0)
            da = mm_nt(g, down_ref[d]) * (2.0 * r)
            ddown_ref[d] += mm_tn(r * r, g)
            dup_ref[d] += mm_tn(h, da)
            dh = dh + mm_nt(da, up_ref[d])
        dh_ref[...] = dh

    dh_parts, d_up, d_down = pl.pallas_call(
        body, grid=(MLP_SPLIT, nt),
        in_specs=[pl.BlockSpec((TOK, D_MODEL), lambda j, i: (i, 0)), pl.BlockSpec(nw.shape, lambda j, i: (0, 0)),
                  pl.BlockSpec((per, D_MODEL, FF_BLOCK), lambda j, i: (j, 0, 0), pipeline_mode=pl.Buffered(1)),
                  pl.BlockSpec((per, FF_BLOCK, D_MODEL), lambda j, i: (j, 0, 0), pipeline_mode=pl.Buffered(1)),
                  pl.BlockSpec((TOK, D_MODEL), lambda j, i: (i, 0))],
        out_specs=[pl.BlockSpec((None, TOK, D_MODEL), lambda j, i: (j, i, 0)),
                   pl.BlockSpec((per, D_MODEL, FF_BLOCK), lambda j, i: (j, 0, 0), pipeline_mode=pl.Buffered(1)),
                   pl.BlockSpec((per, FF_BLOCK, D_MODEL), lambda j, i: (j, 0, 0), pipeline_mode=pl.Buffered(1))],
        out_shape=[jax.ShapeDtypeStruct((MLP_SPLIT, seq_len, D_MODEL), f32),
                   jax.ShapeDtypeStruct(w_up.shape, f32), jax.ShapeDtypeStruct(w_down.shape, f32)],
        compiler_params=_params(("arbitrary", "arbitrary")), name=name)(x, nw, w_up, w_down, dy)

    def norm_body(x_ref, nw_ref, dh_ref, dy_ref, dx_ref, dnw_ref):
        @pl.when(pl.program_id(0) == 0)
        def _():
            dnw_ref[...] = jnp.zeros_like(dnw_ref)

        _, vjp = jax.vjp(rmsnorm_f, x_ref[...], nw_ref[...])
        dh = dh_ref[0]
        for j in range(1, MLP_SPLIT):
            dh = dh + dh_ref[j]
        dx, dnw = vjp(dh)
        dx_ref[...] = dy_ref[...] + dx
        dnw_ref[...] += dnw

    dx, dnw = pl.pallas_call(
        norm_body, grid=(nt,),
        in_specs=[_row_spec(TOK, D_MODEL), _const_spec(nw.shape),
                  pl.BlockSpec((MLP_SPLIT, TOK, D_MODEL), lambda i: (0, i, 0)), _row_spec(TOK, D_MODEL)],
        out_specs=[_row_spec(TOK, D_MODEL), _const_spec(nw.shape)],
        out_shape=[jax.ShapeDtypeStruct((seq_len, D_MODEL), f32), jax.ShapeDtypeStruct(nw.shape, f32)],
        compiler_params=_params(("arbitrary",)), name=name + "_norm")(x, nw, dh_parts, dy)
    return dx, d_up, d_down, dnw


def final_loss(name, x, nw, target):
    seq_len = x.shape[0]

    def body(x_ref, nw_ref, t_ref, loss_ref, dx_ref, dnw_ref):
        @pl.when(pl.program_id(0) == 0)
        def _():
            loss_ref[...] = jnp.zeros_like(loss_ref)
            dnw_ref[...] = jnp.zeros_like(dnw_ref)

        y, vjp = jax.vjp(rmsnorm_f, x_ref[...], nw_ref[...])
        err = y - t_ref[...]
        loss_ref[...] += 0.5 * jnp.sum(jnp.mean(err * err, axis=-1, keepdims=True), axis=0, keepdims=True)
        dx, dnw = vjp(err * (1.0 / D_MODEL))
        dx_ref[...] = dx
        dnw_ref[...] += dnw

    return pl.pallas_call(
        body, grid=(seq_len // TOK,),
        in_specs=[_row_spec(TOK, D_MODEL), _const_spec(nw.shape), _row_spec(TOK, D_MODEL)],
        out_specs=[_const_spec((8, LANES)), _row_spec(TOK, D_MODEL), _const_spec(nw.shape)],
        out_shape=[jax.ShapeDtypeStruct((8, LANES), f32), jax.ShapeDtypeStruct((seq_len, D_MODEL), f32),
                   jax.ShapeDtypeStruct(nw.shape, f32)],
        compiler_params=_params(("arbitrary",)), name=name)(x, nw, target)


def _whole(a):
    return pl.BlockSpec(a.shape, lambda: (0,) * len(a.shape))


def s5_prep_fwd(name, raw):
    def body(*refs):
        outs = s5_prep(*[r[...] for r in refs[:7]])
        for o_ref, o in zip(refs[7:], outs):
            o_ref[...] = o

    shapes = [(1, S5_W)] * 2 + [(S5_CH, S5_W)] * 4
    return pl.pallas_call(
        body, in_specs=[_whole(a) for a in raw], out_specs=[pl.BlockSpec(s, lambda s=s: (0,) * len(s)) for s in shapes],
        out_shape=[jax.ShapeDtypeStruct(s, f32) for s in shapes],
        compiler_params=pltpu.CompilerParams(vmem_limit_bytes=VMEM_LIMIT), name=name)(*raw)


def s5_prep_bwd(name, raw, cts):
    def body(*refs):
        _, vjp = jax.vjp(s5_prep, *[r[...] for r in refs[:7]])
        grads = vjp(tuple(r[...] for r in refs[7:13]))
        for o_ref, g in zip(refs[13:], grads):
            o_ref[...] = g

    return pl.pallas_call(
        body, in_specs=[_whole(a) for a in list(raw) + list(cts)], out_specs=[_whole(a) for a in raw],
        out_shape=[jax.ShapeDtypeStruct(a.shape, f32) for a in raw],
        compiler_params=pltpu.CompilerParams(vmem_limit_bytes=VMEM_LIMIT), name=name)(*raw, *cts)


def s5_fwd(name, proj, u_block, lam_re, lam_im, bblk_re, bblk_im, cblk_re, cblk_im, d_skip, w_glu, b_glu):
    seq_len = proj.shape[0]

    def body(u_ref, lr_ref, li_ref, br_ref, bi_ref, cr_ref, ci_ref, d_ref, wg_ref, bg_ref,
             o_ref, hr_ref, hi_ref, carry_re, carry_im):
        @pl.when(pl.program_id(0) == 0)
        def _():
            carry_re[...] = jnp.zeros_like(carry_re)
            carry_im[...] = jnp.zeros_like(carry_im)

        u = u_ref[...]
        hr_ref[...] = mm(u, br_ref[...])
        hi_ref[...] = mm(u, bi_ref[...])
        lr, li = lr_ref[...], li_ref[...]

        def step(t, h):
            pr, pi = h
            row = pl.ds(t, 1)
            nr = lr * pr - li * pi + hr_ref[row, :]
            ni = lr * pi + li * pr + hi_ref[row, :]
            hr_ref[row, :] = nr
            hi_ref[row, :] = ni
            return nr, ni

        er, ei = lax.fori_loop(0, TOK, step, (carry_re[0:1, :], carry_im[0:1, :]))
        carry_re[0:1, :] = er
        carry_im[0:1, :] = ei
        o_ref[...] = s5_out_fn(hr_ref[...], hi_ref[...], u, cr_ref[...], ci_ref[...], d_ref[...],
                               wg_ref[...].astype(f32), bg_ref[...])

    consts = [lam_re, lam_im, bblk_re, bblk_im, cblk_re, cblk_im, d_skip, w_glu, b_glu]
    return pl.pallas_call(
        body, grid=(seq_len // TOK,),
        in_specs=[_row_spec(TOK, S5_CH, u_block)] + [_const_spec(a.shape) for a in consts],
        out_specs=[_row_spec(TOK, S5_CH), _row_spec(TOK, S5_W), _row_spec(TOK, S5_W)],
        out_shape=[jax.ShapeDtypeStruct((seq_len, S5_CH), f32), jax.ShapeDtypeStruct((seq_len, S5_W), f32),
                   jax.ShapeDtypeStruct((seq_len, S5_W), f32)],
        scratch_shapes=[pltpu.VMEM((HALO, S5_W), f32), pltpu.VMEM((HALO, S5_W), f32)],
        compiler_params=_params(("arbitrary",)), name=name)(proj, *consts)


def s5_bwd(name, proj, u_block, h_re, h_im, d_out, lam_re, lam_im, bblk_re, bblk_im, cblk_re, cblk_im, d_skip, w_glu, b_glu):
    seq_len = proj.shape[0]
    nt = seq_len // TOK
    consts = [lam_re, lam_im, bblk_re, bblk_im, cblk_re, cblk_im, d_skip, w_glu, b_glu]

    def body(u_ref, hr_ref, hi_ref, pr_ref, pi_ref, dout_ref, lr_ref, li_ref, br_ref, bi_ref, cr_ref, ci_ref, d_ref, wg_ref, bg_ref,
             du_ref, dlr_ref, dli_ref, dbr_ref, dbi_ref, dcr_ref, dci_ref, dd_ref, dwg_ref, dbg_ref,
             gr_ref, gi_ref, carry_re, carry_im):
        i = pl.program_id(0)
        tile = nt - 1 - i

        @pl.when(i == 0)
        def _():
            for r in (carry_re, carry_im, dlr_ref, dli_ref, dbr_ref, dbi_ref, dcr_ref, dci_ref, dd_ref, dwg_ref, dbg_ref):
                r[...] = jnp.zeros_like(r)

        u, h_r, h_i = u_ref[...], hr_ref[...], hi_ref[...]
        _, vjp = jax.vjp(s5_out_fn, h_r, h_i, u, cr_ref[...], ci_ref[...], d_ref[...], wg_ref[...].astype(f32), bg_ref[...])
        ghr, ghi, du, dcr, dci, dd, dwg, dbg = vjp(dout_ref[...])
        gr_ref[...] = ghr
        gi_ref[...] = ghi
        lr, li = lr_ref[...], li_ref[...]

        def step(k, g):
            nr, ni = g
            row = pl.ds(TOK - 1 - k, 1)
            cr = gr_ref[row, :] + lr * nr + li * ni
            ci = gi_ref[row, :] - li * nr + lr * ni
            gr_ref[row, :] = cr
            gi_ref[row, :] = ci
            return cr, ci

        er, ei = lax.fori_loop(0, TOK, step, (carry_re[0:1, :], carry_im[0:1, :]))
        carry_re[0:1, :] = er
        carry_im[0:1, :] = ei
        g_r, g_i = gr_ref[...], gi_ref[...]
        first = _iota((TOK, 1), 0) == 0
        keep = jnp.where(tile > 0, 1.0, 0.0)
        p_r = jnp.where(first, pr_ref[HALO - 1:HALO, :] * keep, _roll(h_r, 1, 0))
        p_i = jnp.where(first, pi_ref[HALO - 1:HALO, :] * keep, _roll(h_i, 1, 0))
        dlr_ref[...] += jnp.sum(g_r * p_r + g_i * p_i, axis=0, keepdims=True)
        dli_ref[...] += jnp.sum(g_i * p_r - g_r * p_i, axis=0, keepdims=True)
        du_ref[...] = du + mm_nt(g_r, br_ref[...]) + mm_nt(g_i, bi_ref[...])
        dbr_ref[...] += mm_tn(u, g_r)
        dbi_ref[...] += mm_tn(u, g_i)
        dcr_ref[...] += dcr
        dci_ref[...] += dci
        dd_ref[...] += dd
        dwg_ref[...] += dwg
        dbg_ref[...] += dbg

    rev = lambda cols, cb=0: pl.BlockSpec((TOK, cols), lambda i, cb=cb: (nt - 1 - i, cb))
    prev = pl.BlockSpec((HALO, S5_W), lambda i: (jnp.maximum((nt - 1 - i) * (TOK // HALO) - 1, 0), 0))
    outs = pl.pallas_call(
        body, grid=(nt,),
        in_specs=[rev(S5_CH, u_block), rev(S5_W), rev(S5_W), prev, prev, rev(S5_CH)] + [_const_spec(a.shape) for a in consts],
        out_specs=[rev(S5_CH)] + [_const_spec(a.shape) for a in consts],
        out_shape=[jax.ShapeDtypeStruct((seq_len, S5_CH), f32)] + [jax.ShapeDtypeStruct(a.shape, f32) for a in consts],
        scratch_shapes=[pltpu.VMEM((TOK, S5_W), f32), pltpu.VMEM((TOK, S5_W), f32),
                        pltpu.VMEM((HALO, S5_W), f32), pltpu.VMEM((HALO, S5_W), f32)],
        compiler_params=_params(("arbitrary",)), name=name)(proj, h_re, h_im, h_re, h_im, d_out, *consts)
    return outs[0], outs[1:]


ANY = pl.BlockSpec(memory_space=pl.ANY)


def _mesh_position():
    x, y, c = lax.axis_index("x"), lax.axis_index("y"), lax.axis_index("c")
    return x, y, c, 4 * x + 2 * y + c


def _peer(x, y, c, r):
    px = 1 - x if r & 4 else x
    py = 1 - y if r & 2 else y
    pc = 1 - c if r & 1 else c
    return (px, py, pc), 4 * px + 2 * py + pc


def _exchange(name, arrays, gather):
    n = len(arrays)

    def body(*refs):
        ins, outs = refs[:n], refs[n:2 * n]
        send_sems, recv_sems, local_sems = refs[2 * n:]
        x, y, c, me = _mesh_position()
        started = []
        for i in range(n):
            mine = ins[i] if gather else ins[i].at[me]
            local = pltpu.make_async_copy(mine, outs[i].at[me], local_sems.at[i])
            local.start()
            started.append(local)
            for r in range(1, N_DEV):
                peer, peer_idx = _peer(x, y, c, r)
                cp = pltpu.make_async_remote_copy(
                    src_ref=ins[i] if gather else ins[i].at[peer_idx], dst_ref=outs[i].at[me],
                    send_sem=send_sems.at[i, r - 1], recv_sem=recv_sems.at[i, r - 1],
                    device_id=peer, device_id_type=pl.DeviceIdType.MESH)
                cp.start()
        for i in range(n):
            for r in range(1, N_DEV):
                peer, peer_idx = _peer(x, y, c, r)
                cp = pltpu.make_async_remote_copy(
                    src_ref=ins[i] if gather else ins[i].at[peer_idx], dst_ref=outs[i].at[peer_idx],
                    send_sem=send_sems.at[i, r - 1], recv_sem=recv_sems.at[i, r - 1],
                    device_id=peer, device_id_type=pl.DeviceIdType.MESH)
                cp.wait_recv()
                cp.wait_send()
        for local in started:
            local.wait()

    shapes = [((N_DEV,) + a.shape) if gather else a.shape for a in arrays]
    return pl.pallas_call(
        body, in_specs=[ANY] * n, out_specs=[ANY] * n,
        out_shape=[jax.ShapeDtypeStruct(s, a.dtype) for s, a in zip(shapes, arrays)],
        scratch_shapes=[pltpu.SemaphoreType.DMA((n, N_DEV - 1)), pltpu.SemaphoreType.DMA((n, N_DEV - 1)),
                        pltpu.SemaphoreType.DMA((n,))],
        compiler_params=pltpu.CompilerParams(has_side_effects=True), name=name)(*arrays)


def adamw(name, parts, w, m, v):
    rows, cols = w.shape
    tr = rows
    for cand in (512, 256, 128, 64, 32, 16, 8):
        if rows * cols * 4 > (1 << 20) and rows % cand == 0 and cand * cols * 4 <= (1 << 20):
            tr = cand
            break

    def body(p_ref, w_ref, m_ref, v_ref, g_ref, d_ref, nm_ref, nv_ref):
        g = p_ref[0]
        for s in range(1, N_DEV):
            g = g + p_ref[s]
        nm = ADAM_B1 * m_ref[...] + (1.0 - ADAM_B1) * g
        nv = ADAM_B2 * v_ref[...] + (1.0 - ADAM_B2) * (g * g)
        m_hat = nm / (1.0 - ADAM_B1 ** ADAM_STEP)
        v_hat = nv / (1.0 - ADAM_B2 ** ADAM_STEP)
        g_ref[...] = g
        d_ref[...] = -ADAM_LR * (m_hat / (jnp.sqrt(v_hat) + ADAM_EPS) + ADAM_WD * w_ref[...])
        nm_ref[...] = nm
        nv_ref[...] = nv

    blk = pl.BlockSpec((tr, cols), lambda i: (i, 0))
    return pl.pallas_call(
        body, grid=(rows // tr,),
        in_specs=[pl.BlockSpec((N_DEV, tr, cols), lambda i: (0, i, 0)), blk, blk, blk],
        out_specs=[blk] * 4, out_shape=[jax.ShapeDtypeStruct((rows, cols), f32)] * 4,
        compiler_params=_params(("arbitrary",)), name=name)(parts, w, m, v)


WEIGHTS = ['l0_norm_mix', 'l0_w_in', 'ssd_conv_w', 'ssd_conv_b', 'ssd_dt_bias', 'ssd_A_log', 'ssd_D', 'ssd_norm_w',
           'l0_w_out', 'l0_norm_mlp', 'l0_w_up', 'l0_w_down', 'l1_norm_mix', 'l1_w_in', 'gdn_conv_w', 'gdn_A_log',
           'gdn_dt_bias', 'gdn_norm_w', 's5_A_re', 's5_A_im', 's5_log_step', 's5_B_re', 's5_B_im', 's5_C_re', 's5_C_im',
           's5_D', 's5_w_glu', 's5_b_glu', 'l1_w_out', 'l1_norm_mlp', 'l1_w_up', 'l1_w_down', 'final_norm']
SHARDED = ['l0_w_in', 'l0_w_out', 'l0_w_up', 'l0_w_down', 'l1_w_in', 's5_w_glu', 'l1_w_out', 'l1_w_up', 'l1_w_down',
           'ssd_conv_w', 'gdn_conv_w']
F32_GATHER = ('ssd_conv_w', 'gdn_conv_w')
REPLICATED = [n for n in WEIGHTS if n not in SHARDED]
INPUTS = ['x'] + WEIGHTS + ['loss_target'] + ['m_' + n for n in WEIGHTS] + ['v_' + n for n in WEIGHTS]


def _row(v):
    return v.reshape(1, -1)


def _pad_lanes(v, offset=0):
    return jnp.pad(v, (offset, LANES - offset - v.shape[0])).reshape(1, LANES)


def _cols_to_blocks(g):
    return g.reshape(g.shape[0], N_DEV, -1).transpose(1, 0, 2)


def _blocks_to_cols(g):
    return g.transpose(1, 0, 2).reshape(g.shape[1], -1)


def _pack(arrays):
    parts, slots, at = [], [], 0
    for a in arrays:
        n = a.size
        rows = -(-n // (8 * LANES)) * 8
        parts.append(jnp.pad(a.reshape(-1), (0, rows * LANES - n)).reshape(rows, LANES))
        slots.append((at, rows, n, a.shape))
        at += rows
    return jnp.concatenate(parts, axis=0), slots


def _unpack(buf, slots):
    return [buf[at:at + rows].reshape(-1)[:n].reshape(shape) for at, rows, n, shape in slots]


def kernel(*args):
    a = dict(zip(INPUTS, args, strict=True))
    seq_len = a['x'].shape[1]
    x0 = a['x'].reshape(seq_len, D_MODEL)
    target = a['loss_target'].reshape(seq_len, D_MODEL)

    gathered = _exchange("gather_weights",
                         [a[n] if n in F32_GATHER else a[n].astype(_MXU) for n in SHARDED], gather=True)
    g = dict(zip(SHARDED, gathered))
    w_nat = _blocks_to_cols(g['l0_w_in'])
    win0 = jnp.concatenate([w_nat[:, :2048], w_nat[:, 2560:3584], w_nat[:, 2048:2560], w_nat[:, 3584:3592],
                            jnp.zeros((D_MODEL, IN0_PAD - IN0_W), _MXU)], axis=1)
    w_nat = g['l1_w_in'].reshape(D_MODEL, IN1_W)
    win1 = jnp.concatenate([w_nat[:, :3072], w_nat[:, 3084:3340], w_nat[:, 3072:3084],
                            jnp.zeros((D_MODEL, IN1_PAD - IN1_W), _MXU)], axis=1)
    wout0, wout1 = g['l0_w_out'].reshape(D_MODEL, D_MODEL), g['l1_w_out'].reshape(D_MODEL, D_MODEL)
    wglu = g['s5_w_glu'].reshape(S5_CH, S5_CH)
    ssd_cw, gdn_cw = _blocks_to_cols(g['ssd_conv_w']), _blocks_to_cols(g['gdn_conv_w'])

    half = RET_D // 2
    inv = ROPE_THETA ** (-jnp.arange(half, dtype=f32) / half)
    ang = jnp.arange(seq_len, dtype=f32)[:, None] * inv[None, :]
    cos, sin = jnp.cos(ang), jnp.sin(ang)
    cos, sin = jnp.concatenate([cos, cos], axis=1), jnp.concatenate([-sin, sin], axis=1)
    ssd_params = [ssd_cw, _row(a['ssd_conv_b']), _pad_lanes(a['ssd_dt_bias']), _pad_lanes(a['ssd_A_log']),
                  _pad_lanes(a['ssd_D']), _row(a['ssd_norm_w'])]
    gdn_params = [gdn_cw, _pad_lanes(a['gdn_A_log'], GDN_GCOL), _pad_lanes(a['gdn_dt_bias'], GDN_GCOL), _row(a['gdn_norm_w'])]
    s5_raw = [a['s5_A_re'].reshape(1, S5_W), a['s5_A_im'].reshape(1, S5_W), _pad_lanes(a['s5_log_step']),
              a['s5_B_re'].transpose(2, 0, 1).reshape(S5_GROUP, S5_W), a['s5_B_im'].transpose(2, 0, 1).reshape(S5_GROUP, S5_W),
              a['s5_C_re'].transpose(1, 0, 2).reshape(S5_GROUP, S5_W), a['s5_C_im'].transpose(1, 0, 2).reshape(S5_GROUP, S5_W)]
    s5_d, s5_bg = _row(a['s5_D']), _row(a['s5_b_glu'])
    nw = {n: _row(a[n]) for n in ('l0_norm_mix', 'l0_norm_mlp', 'l1_norm_mix', 'l1_norm_mlp', 'final_norm')}

    proj0 = inproj_fwd("l0_in", x0, nw['l0_norm_mix'], win0)
    ret_seqs = [Seq(proj0, 512, 0), Seq(proj0, 512, 1), Seq(proj0, 512, 2), Seq(proj0, 512, 3),
                Seq(cos, LANES, 0, "const"), Seq(sin, LANES, 0, "const")]
    ret_out, ret_st = mixer_fwd("ret_fwd", ret_chunk, ret_seqs, [], 512, RET_HEADS * RET_D, seq_len)
    ssd_seqs = [Seq(proj0, 512, 6), Seq(proj0, 1024, 2, "halo"), Seq(proj0, LANES, 28)]
    ssd_out, ssd_st = mixer_fwd("ssd_fwd", ssd_chunk, ssd_seqs, ssd_params, SSD_INNER, SSD_INNER, seq_len)
    x1 = outproj_fwd("l0_out", x0, ret_out, ssd_out, wout0)
    x2 = mlp_fwd("l0_mlp", x1, nw['l0_norm_mlp'], g['l0_w_up'], g['l0_w_down'])
    proj1 = inproj_fwd("l1_in", x2, nw['l1_norm_mix'], win1)
    gdn_seqs = [Seq(proj1, 3 * GDN_W, 0, "halo"), Seq(proj1, GDN_W, 3), Seq(proj1, LANES, 26)]
    gdn_out, gdn_st = mixer_fwd("gdn_fwd", gdn_chunk, gdn_seqs, gdn_params, GDN_W, GDN_W, seq_len)
    prep = s5_prep_fwd("s5_prep", s5_raw)
    s5_out, h_re, h_im = s5_fwd("s5_fwd", proj1, 12, *prep, s5_d, wglu, s5_bg)
    x3 = outproj_fwd("l1_out", x2, gdn_out, s5_out, wout1)
    x4 = mlp_fwd("l1_mlp", x3, nw['l1_norm_mlp'], g['l1_w_up'], g['l1_w_down'])
    loss_blk, dx4, d_final = final_loss("final_loss", x4, nw['final_norm'], target)

    dx3, d_up1, d_down1, d_nmlp1 = mlp_bwd("l1_mlp_bwd", x3, nw['l1_norm_mlp'], g['l1_w_up'], g['l1_w_down'], dx4)
    d_gdn, d_s5, d_wout1 = outproj_bwd("l1_out_bwd", dx3, gdn_out, s5_out, wout1)
    d_u, s5_g = s5_bwd("s5_bwd", proj1, 12, h_re, h_im, d_s5, *prep, s5_d, wglu, s5_bg)
    s5_raw_g = s5_prep_bwd("s5_prep_bwd", s5_raw, s5_g[:6])
    (d_qkv, d_z1, d_ba), gdn_pg = mixer_bwd("gdn_bwd", gdn_chunk, gdn_seqs, gdn_params, gdn_st, d_gdn, seq_len)
    dx2, d_win1, d_nmix1 = inproj_bwd("l1_in_bwd", x2, nw['l1_norm_mix'], win1, [d_qkv, d_z1, d_u, d_ba], dx3)
    dx1, d_up0, d_down0, d_nmlp0 = mlp_bwd("l0_mlp_bwd", x1, nw['l0_norm_mlp'], g['l0_w_up'], g['l0_w_down'], dx2)
    d_ret, d_ssd, d_wout0 = outproj_bwd("l0_out_bwd", dx1, ret_out, ssd_out, wout0)
    d_qkvg, _ = mixer_bwd("ret_bwd", ret_chunk, ret_seqs, [], ret_st, d_ret, seq_len)
    (d_z0, d_xbc, d_dt), ssd_pg = mixer_bwd("ssd_bwd", ssd_chunk, ssd_seqs, ssd_params, ssd_st, d_ssd, seq_len)
    dx0, d_win0, d_nmix0 = inproj_bwd("l0_in_bwd", x0, nw['l0_norm_mix'], win0, list(d_qkvg) + [d_xbc, d_z0, d_dt], dx1)

    d_win0 = jnp.concatenate([d_win0[:, :2048], d_win0[:, 3072:3584], d_win0[:, 2048:3072], d_win0[:, 3584:3592]], axis=1)
    d_win1 = jnp.concatenate([d_win1[:, :3072], d_win1[:, 3328:3340], d_win1[:, 3072:3328]], axis=1)
    sharded_g = {
        'l0_w_in': _cols_to_blocks(d_win0), 'l0_w_out': d_wout0.reshape(N_DEV, -1, D_MODEL), 'l0_w_up': d_up0, 'l0_w_down': d_down0,
        'l1_w_in': d_win1.reshape(N_DEV, -1, IN1_W), 's5_w_glu': s5_g[7].reshape(N_DEV, -1, S5_CH),
        'l1_w_out': d_wout1.reshape(N_DEV, -1, D_MODEL), 'l1_w_up': d_up1, 'l1_w_down': d_down1,
        'ssd_conv_w': _cols_to_blocks(ssd_pg[0]), 'gdn_conv_w': _cols_to_blocks(gdn_pg[0])}
    from_b = lambda t: t.reshape(S5_GROUP, S5_GROUPS, S5_STATE).transpose(1, 2, 0)
    from_c = lambda t: t.reshape(S5_GROUP, S5_GROUPS, S5_STATE).transpose(1, 0, 2)
    replicated_g = {
        'l0_norm_mix': d_nmix0, 'ssd_conv_b': ssd_pg[1], 'ssd_dt_bias': ssd_pg[2][0, :SSD_HEADS], 'ssd_A_log': ssd_pg[3][0, :SSD_HEADS],
        'ssd_D': ssd_pg[4][0, :SSD_HEADS], 'ssd_norm_w': ssd_pg[5], 'l0_norm_mlp': d_nmlp0, 'l1_norm_mix': d_nmix1,
        'gdn_A_log': gdn_pg[1][0, GDN_GCOL:GDN_GCOL + GDN_HEADS], 'gdn_dt_bias': gdn_pg[2][0, GDN_GCOL:GDN_GCOL + GDN_HEADS],
        'gdn_norm_w': gdn_pg[3], 's5_A_re': s5_raw_g[0], 's5_A_im': s5_raw_g[1], 's5_log_step': s5_raw_g[2][0, :S5_GROUPS],
        's5_B_re': from_b(s5_raw_g[3]), 's5_B_im': from_b(s5_raw_g[4]), 's5_C_re': from_c(s5_raw_g[5]), 's5_C_im': from_c(s5_raw_g[6]),
        's5_D': s5_g[6], 's5_b_glu': s5_g[8], 'l1_norm_mlp': d_nmlp1, 'final_norm': d_final}
    replicated_g = {n: replicated_g[n].reshape(a[n].shape) for n in REPLICATED}

    parts = _exchange("scatter_grads", [sharded_g[n] for n in SHARDED], gather=False)
    packed_g, slots = _pack([replicated_g[n] for n in REPLICATED])
    (packed_parts,) = _exchange("gather_small_grads", [packed_g], gather=True)
    results = {}
    for n, p in zip(SHARDED, parts):
        results[n] = adamw("adamw_" + n, p, a[n], a['m_' + n], a['v_' + n])
    packed = [_pack([a[pre + n] for n in REPLICATED])[0] for pre in ('', 'm_', 'v_')]
    small = [_unpack(t, slots) for t in adamw("adamw_small", packed_parts, *packed)]
    for i, n in enumerate(REPLICATED):
        results[n] = tuple(small[k][i] for k in range(4))

    loss = lax.psum(loss_blk[0, 0], ("x", "y", "c"))
    grad_x = dx0.reshape(a['x'].shape)
    return (loss, grad_x, *[results[n][0] for n in WEIGHTS], *[results[n][1] for n in WEIGHTS],
            *[results[n][2] for n in WEIGHTS], *[results[n][3] for n in WEIGHTS])
```

```python
import functools
import math

import numpy as np
import jax
import jax.numpy as jnp
from jax import lax
from jax.experimental import pallas as pl
from jax.experimental.pallas import tpu as pltpu

f32 = jnp.float32
_MXU = jnp.bfloat16
HI = lax.Precision.HIGHEST

D_MODEL = 1024
CHUNK = 64
EPS = 1e-6
N_DEV = 8
LANES = 128
HALO = 8
CONV_WIDTH = 4

RET_HEADS, RET_D = 4, 128
SSD_HEADS, SSD_P, SSD_N, SSD_GROUPS = 8, 64, 128, 2
SSD_INNER = SSD_HEADS * SSD_P
GDN_HEADS, GDN_D = 6, 128
GDN_W = GDN_HEADS * GDN_D
S5_CH, S5_GROUP, S5_GROUPS, S5_STATE = 256, 16, 16, 64
S5_W = S5_GROUPS * S5_STATE
D_FF = 4096
ROPE_THETA = 10000.0

IN0_W = 3592
IN0_PAD = 3712
IN1_W = 3340
IN1_PAD = 3456

ADAM_LR, ADAM_B1, ADAM_B2, ADAM_EPS, ADAM_WD, ADAM_STEP = 0.001, 0.9, 0.999, 1e-08, 0.01, 10

VMEM_LIMIT = 56 * 1024 * 1024


def _dot(a, b, dims):
    return lax.dot_general(a.astype(_MXU), b.astype(_MXU), (dims, ((), ())), preferred_element_type=f32)


@jax.custom_vjp
def mm(a, b):
    return _dot(a, b, ((1,), (0,)))


@jax.custom_vjp
def mm_nt(a, b):
    return _dot(a, b, ((1,), (1,)))


@jax.custom_vjp
def mm_tn(a, b):
    return _dot(a, b, ((0,), (0,)))


mm.defvjp(lambda a, b: (mm(a, b), (a, b)), lambda r, g: (mm_nt(g, r[1]), mm_tn(r[0], g)))
mm_nt.defvjp(lambda a, b: (mm_nt(a, b), (a, b)), lambda r, g: (mm(g, r[1]), mm_tn(g, r[0])))
mm_tn.defvjp(lambda a, b: (mm_tn(a, b), (a, b)), lambda r, g: (mm_nt(r[1], g), mm(r[0], g)))


def mmh(a, b):
    return jnp.dot(a, b, precision=HI, preferred_element_type=f32)


def _roll(x, shift, axis):
    return pltpu.roll(x, shift, axis)


@functools.partial(jax.custom_vjp, nondiff_argnums=(1,))
def roll_rows(x, s):
    return _roll(x, s, 0) if s else x


roll_rows.defvjp(lambda x, s: (roll_rows(x, s), None),
                 lambda s, _, g: ((_roll(g, g.shape[0] - s, 0) if s else g),))


@jax.custom_vjp
def roll_half(x):
    return _roll(x, x.shape[-1] // 2, 1)


roll_half.defvjp(lambda x: (roll_half(x), None), lambda _, g: (roll_half(g),))


def _iota(shape, axis):
    return lax.broadcasted_iota(jnp.int32, shape, axis)


def silu(x):
    return x * jax.nn.sigmoid(x)


def softplus(x):
    return jnp.maximum(x, 0.0) + jnp.log(1.0 + jnp.exp(-jnp.abs(x)))


def rmsnorm_f(x, w):
    return x * lax.rsqrt(jnp.mean(x * x, axis=-1, keepdims=True) + EPS) * w


def unit_rms(x):
    return x * lax.rsqrt(jnp.mean(x * x, axis=-1, keepdims=True) + EPS)


def _causal(n, strict=False):
    r, c = _iota((n, n), 0), _iota((n, n), 1)
    return (r > c) if strict else (r >= c)


def _tril_ones(n):
    return _causal(n).astype(f32)


def _conv_rows(xe, w):
    acc = w[CONV_WIDTH - 1:CONV_WIDTH, :] * xe
    for j in range(CONV_WIDTH - 1):
        acc = acc + w[j:j + 1, :] * roll_rows(xe, CONV_WIDTH - 1 - j)
    return acc[HALO:, :]


_RET_LOG_GAMMA = [float(np.log(np.float32(1.0) - np.float32(2.0) ** np.float32(-5.0 - h))) for h in range(RET_HEADS)]


def ret_chunk(q, k, v, gate, cos, sin, state):
    c = q.shape[0]
    idx = _iota((c, 1), 0).astype(f32)
    diff = (_iota((c, c), 0) - _iota((c, c), 1)).astype(f32)
    causal = _causal(c)
    hs = range(RET_HEADS)
    cols = [slice(h * RET_D, (h + 1) * RET_D) for h in hs]
    lg = _RET_LOG_GAMMA
    qh = [(q[:, s] * cos + roll_half(q[:, s]) * sin) * (RET_D ** -0.5) for s in cols]
    kh = [k[:, s] * cos + roll_half(k[:, s]) * sin for s in cols]
    vh = [v[:, s] for s in cols]
    sh = [state[s, :] for s in cols]
    scores = [mm_nt(qh[h], kh[h]) * jnp.exp(jnp.where(causal, lg[h] * diff, -jnp.inf)) for h in hs]
    inter = [mm(qh[h] * jnp.exp(lg[h] * (idx + 1.0)), sh[h]) for h in hs]
    y = [mm(scores[h], vh[h]) + inter[h] for h in hs]
    states = [sh[h] * math.exp(lg[h] * c) + mm_tn(kh[h] * jnp.exp(lg[h] * (c - 1.0 - idx)), vh[h]) for h in hs]
    outs = [unit_rms(y[h]) * silu(gate[:, cols[h]]) for h in hs]
    return jnp.concatenate(outs, axis=1), jnp.concatenate(states, axis=0)


def _head_select(n_heads, width):
    r, c = _iota((LANES, n_heads * width), 0), _iota((LANES, n_heads * width), 1)
    return (c // width == r).astype(f32)


def ssd_chunk(z, xe, dtr, state, conv_w, conv_b, dt_bias, a_log, d_skip, norm_w):
    c = z.shape[0]
    xbc = silu(_conv_rows(xe, conv_w) + conv_b)
    xs, bm, cm = xbc[:, :SSD_INNER], xbc[:, SSD_INNER:SSD_INNER + 256], xbc[:, SSD_INNER + 256:]
    sel = _head_select(SSD_HEADS, SSD_P)
    dt = softplus(dtr + dt_bias)
    la = dt * (-jnp.exp(a_log))
    la_cum = mmh(_tril_ones(c), la)
    la_cum_t = la_cum.T
    last = jnp.sum(la, axis=0, keepdims=True)
    xd = xs * mmh(dt, sel)
    la_x = mmh(la_cum, sel)
    last_x = mmh(last, sel)
    to_end = jnp.exp(last_x - la_x)
    from_start = jnp.exp(la_x)
    causal = _causal(c)
    left = (_iota((1, LANES), 1) < SSD_P).astype(f32)
    upper = _iota((LANES, 1), 0) < SSD_P
    pairs, heads = range(SSD_HEADS // 2), range(SSD_HEADS)
    bc = [bm[:, g * SSD_N:(g + 1) * SSD_N] for g in range(SSD_GROUPS)]
    cc = [cm[:, g * SSD_N:(g + 1) * SSD_N] for g in range(SSD_GROUPS)]
    cb = [mm_nt(cc[g], bc[g]) for g in range(SSD_GROUPS)]
    cols = [slice(p * LANES, (p + 1) * LANES) for p in pairs]
    xd_p = [xd[:, s] for s in cols]
    sp = [state[s, :] for s in cols]
    lmat = [jnp.exp(jnp.where(causal, la_cum[:, h:h + 1] - la_cum_t[h:h + 1, :], -jnp.inf)) for h in heads]
    off = [mm_nt(cc[p // 2], sp[p]) * from_start[:, cols[p]] for p in pairs]
    diag = [mm(cb[h // 4] * lmat[h], xd_p[h // 2] * (left if h % 2 == 0 else 1.0 - left)) for h in heads]
    cd = [jnp.where(upper, jnp.exp(last[:, 2 * p:2 * p + 1]), jnp.exp(last[:, 2 * p + 1:2 * p + 2])) for p in pairs]
    states = [sp[p] * cd[p] + mm_tn(xd_p[p] * to_end[:, cols[p]], bc[p // 2]) for p in pairs]
    ys = [off[p] + diag[2 * p] + diag[2 * p + 1] for p in pairs]
    y = jnp.concatenate(ys, axis=1) + mmh(d_skip, sel) * xs
    yg = y * silu(z)
    half = SSD_INNER // SSD_GROUPS
    out = jnp.concatenate([unit_rms(yg[:, i * half:(i + 1) * half]) for i in range(SSD_GROUPS)], axis=1) * norm_w
    return out, jnp.concatenate(states, axis=0)


def mm3(a, b):
    return jnp.dot(a, b, precision=lax.Precision.HIGH, preferred_element_type=f32)


def _unit_lower_inverses(lowers):
    n = lowers[0].shape[0]
    eye = (_iota((n, n), 0) == _iota((n, n), 1)).astype(f32)
    a = [-l for l in lowers]
    p = [eye + x for x in a]
    k = 2
    while k < n:
        a = [mm3(x, x) for x in a]
        p = [y + mm3(y, x) for y, x in zip(p, a)]
        k *= 2
    return p


GDN_GCOL = 6


def gdn_chunk(xe, z, ba, state, conv_w, a_log, dt_bias, norm_w):
    c = z.shape[0]
    qkv = silu(_conv_rows(xe, conv_w))
    beta_all = jax.nn.sigmoid(ba)
    g_all = -jnp.exp(a_log) * softplus(ba + dt_bias)
    gc = mmh(_tril_ones(c), g_all)
    gc_t = gc.T
    last = jnp.sum(g_all, axis=0, keepdims=True)
    causal, strict = _causal(c), _causal(c, strict=True)
    hs = range(GDN_HEADS)
    cols = [slice(h * GDN_D, (h + 1) * GDN_D) for h in hs]
    qh = [qkv[:, h * GDN_D:(h + 1) * GDN_D] for h in hs]
    kh = [qkv[:, GDN_W + h * GDN_D:GDN_W + (h + 1) * GDN_D] for h in hs]
    vh = [qkv[:, 2 * GDN_W + h * GDN_D:2 * GDN_W + (h + 1) * GDN_D] for h in hs]
    qh = [t * lax.rsqrt(jnp.sum(t * t, axis=-1, keepdims=True) + EPS) * (GDN_D ** -0.5) for t in qh]
    kh = [t * lax.rsqrt(jnp.sum(t * t, axis=-1, keepdims=True) + EPS) for t in kh]
    beta = [beta_all[:, h:h + 1] for h in hs]
    col = [gc[:, GDN_GCOL + h:GDN_GCOL + h + 1] for h in hs]
    row = [gc_t[GDN_GCOL + h:GDN_GCOL + h + 1, :] for h in hs]
    lst = [last[:, GDN_GCOL + h:GDN_GCOL + h + 1] for h in hs]
    decay = [jnp.exp(jnp.where(causal, col[h] - row[h], -jnp.inf)) for h in hs]
    e_col = [jnp.exp(t) for t in col]
    kb = [kh[h] * beta[h] for h in hs]
    vb = [vh[h] * beta[h] for h in hs]
    kk = [mm_nt(kb[h], kh[h]) for h in hs]
    qk = [mm_nt(qh[h], kh[h]) for h in hs]
    t_inv = _unit_lower_inverses([jnp.where(strict, kk[h] * decay[h], 0.0) for h in hs])
    u = [mm(t_inv[h], vb[h]) for h in hs]
    w = [mm(t_inv[h], kb[h] * e_col[h]) for h in hs]
    attn = [jnp.where(causal, qk[h] * decay[h], 0.0) for h in hs]
    sh = [state[s, :] for s in cols]
    ws = [mm(w[h], sh[h]) for h in hs]
    qs = [mm(qh[h] * e_col[h], sh[h]) for h in hs]
    v_new = [u[h] - ws[h] for h in hs]
    o = [qs[h] + mm(attn[h], v_new[h]) for h in hs]
    states = [sh[h] * jnp.exp(lst[h]) + mm_tn(kh[h] * jnp.exp(lst[h] - col[h]), v_new[h]) for h in hs]
    outs = [unit_rms(o[h]) * norm_w * silu(z[:, cols[h]]) for h in hs]
    return jnp.concatenate(outs, axis=1), jnp.concatenate(states, axis=0)


def _s5_group_mask():
    r, c = _iota((S5_CH, S5_W), 0), _iota((S5_CH, S5_W), 1)
    return (r // S5_GROUP == c // S5_STATE).astype(f32)


def s5_prep(a_re, a_im, log_step, b_re, b_im, c_re, c_im):
    r, c = _iota((LANES, S5_W), 0), _iota((LANES, S5_W), 1)
    step = jnp.exp(mmh(log_step, (c // S5_STATE == r).astype(f32)))
    zr, zi = a_re * step, a_im * step
    e = jnp.exp(zr)
    lr, li = e * jnp.cos(zi), e * jnp.sin(zi)
    den = a_re * a_re + a_im * a_im
    xr, xi = lr - 1.0, li
    cr, ci = (xr * a_re + xi * a_im) / den, (xi * a_re - xr * a_im) / den
    bbr, bbi = cr * b_re - ci * b_im, cr * b_im + ci * b_re
    mask = _s5_group_mask()
    tile = lambda t: jnp.tile(t, (S5_GROUPS, 1)) * mask
    return lr, li, tile(bbr), tile(bbi), tile(c_re), tile(c_im)


def s5_out_fn(h_re, h_im, u, cblk_re, cblk_im, d_skip, w_glu, b_glu):
    y = mm_nt(h_re, cblk_re) - mm_nt(h_im, cblk_im) + d_skip * u
    y = jax.nn.gelu(y)
    return y * jax.nn.sigmoid(mm(y, w_glu) + b_glu)


def _params(sem, **kw):
    return pltpu.CompilerParams(dimension_semantics=sem, vmem_limit_bytes=VMEM_LIMIT, **kw)


def _const_spec(shape):
    return pl.BlockSpec(shape, lambda i: (0,) * len(shape))


def _resident_spec(shape):
    return pl.BlockSpec(shape, lambda i: (0,) * len(shape), pipeline_mode=pl.Buffered(1))


def _row_spec(rows, cols, col_block=0):
    return pl.BlockSpec((rows, cols), lambda i: (i, col_block))


class Seq:
    def __init__(self, array, width, col_block, kind="tile"):
        self.array, self.width, self.col_block, self.kind = array, width, col_block, kind


def mixer_fwd(name, fn, seqs, params, out_width, state_rows, seq_len):
    nc = seq_len // CHUNK
    n_refs = sum(2 if s.kind == "halo" else 1 for s in seqs)

    def body(*refs):
        seq_refs, par_refs = refs[:n_refs], refs[n_refs:n_refs + len(params)]
        out_ref, st_ref, state = refs[n_refs + len(params):]
        c = pl.program_id(0)

        @pl.when(c == 0)
        def _():
            state[...] = jnp.zeros_like(state)

        vals, k = [], 0
        for s in seqs:
            if s.kind == "halo":
                prev = jnp.where(c > 0, seq_refs[k][...], 0.0)
                vals.append(jnp.concatenate([prev, seq_refs[k + 1][...]], axis=0))
                k += 2
            else:
                vals.append(seq_refs[k][...])
                k += 1
        s_in = state[...]
        st_ref[...] = s_in
        out, s_new = fn(*vals, s_in, *[p[...] for p in par_refs])
        out_ref[...] = out
        state[...] = s_new

    in_specs, operands = [], []
    for s in seqs:
        if s.kind == "halo":
            rb, w, cb = CHUNK // HALO, s.width, s.col_block
            in_specs.append(pl.BlockSpec((HALO, w), lambda i, rb=rb, cb=cb: (jnp.maximum(i * rb - 1, 0), cb)))
            operands.append(s.array)
        in_specs.append(pl.BlockSpec((CHUNK, s.width), lambda i, cb=s.col_block: (i, cb)))
        operands.append(s.array)
    for p in params:
        in_specs.append(_const_spec(p.shape))
        operands.append(p)
    return pl.pallas_call(
        body, grid=(nc,), in_specs=in_specs,
        out_specs=[pl.BlockSpec((CHUNK, out_width), lambda i: (i, 0)),
                   pl.BlockSpec((None, state_rows, LANES), lambda i: (i, 0, 0))],
        out_shape=[jax.ShapeDtypeStruct((seq_len, out_width), f32),
                   jax.ShapeDtypeStruct((nc, state_rows, LANES), f32)],
        scratch_shapes=[pltpu.VMEM((state_rows, LANES), f32)],
        compiler_params=_params(("arbitrary",)), name=name)(*operands)


def mixer_bwd(name, fn, seqs, params, states, d_out, seq_len):
    nc = seq_len // CHUNK
    state_rows = states.shape[1]
    diff = [s for s in seqs if s.kind != "const"]
    halos = [s for s in diff if s.kind == "halo"]
    n_refs = sum(2 if s.kind == "halo" else 1 for s in seqs)
    n_par = len(params)

    def body(*refs):
        seq_refs, par_refs = refs[:n_refs], refs[n_refs:n_refs + n_par]
        st_ref, dout_ref = refs[n_refs + n_par:n_refs + n_par + 2]
        k0 = n_refs + n_par + 2
        dseq_refs, dpar_refs = refs[k0:k0 + len(diff)], refs[k0 + len(diff):k0 + len(diff) + n_par]
        scratch = refs[k0 + len(diff) + n_par:]
        d_state, carries = scratch[0], scratch[1:]
        i = pl.program_id(0)
        c = nc - 1 - i

        @pl.when(i == 0)
        def _():
            d_state[...] = jnp.zeros_like(d_state)
            for r in list(carries) + list(dpar_refs):
                r[...] = jnp.zeros_like(r)

        dvals, consts, k = [], [], 0
        for s in seqs:
            if s.kind == "halo":
                prev = jnp.where(c > 0, seq_refs[k][...], 0.0)
                dvals.append(jnp.concatenate([prev, seq_refs[k + 1][...]], axis=0))
                k += 2
            elif s.kind == "tile":
                dvals.append(seq_refs[k][...])
                k += 1
            else:
                consts.append(seq_refs[k][...])
                k += 1
        nd = len(dvals)

        def call(*a):
            it_d, it_c = iter(a[:nd]), iter(consts)
            vals = [next(it_c) if s.kind == "const" else next(it_d) for s in seqs]
            return fn(*vals, *a[nd:])

        _, vjp = jax.vjp(call, *dvals, st_ref[...], *[p[...] for p in par_refs])
        cts = vjp((dout_ref[...], d_state[...]))
        hk = 0
        for j, s in enumerate(diff):
            if s.kind == "halo":
                dseq_refs[j][...] = cts[j][HALO:, :]
                dseq_refs[j][CHUNK - HALO:, :] += carries[hk][...]
                carries[hk][...] = cts[j][:HALO, :]
                hk += 1
            else:
                dseq_refs[j][...] = cts[j]
        d_state[...] = cts[nd]
        for j in range(n_par):
            dpar_refs[j][...] += cts[nd + 1 + j]

    in_specs, operands = [], []
    for s in seqs:
        if s.kind == "halo":
            rb, cb = CHUNK // HALO, s.col_block
            in_specs.append(pl.BlockSpec((HALO, s.width), lambda i, rb=rb, cb=cb: (jnp.maximum((nc - 1 - i) * rb - 1, 0), cb)))
            operands.append(s.array)
        in_specs.append(pl.BlockSpec((CHUNK, s.width), lambda i, cb=s.col_block: (nc - 1 - i, cb)))
        operands.append(s.array)
    for p in params:
        in_specs.append(_const_spec(p.shape))
        operands.append(p)
    in_specs.append(pl.BlockSpec((None, state_rows, LANES), lambda i: (nc - 1 - i, 0, 0)))
    in_specs.append(pl.BlockSpec((CHUNK, d_out.shape[1]), lambda i: (nc - 1 - i, 0)))
    operands += [states, d_out]
    outs = pl.pallas_call(
        body, grid=(nc,), in_specs=in_specs,
        out_specs=[pl.BlockSpec((CHUNK, s.width), lambda i: (nc - 1 - i, 0)) for s in diff]
        + [_const_spec(p.shape) for p in params],
        out_shape=[jax.ShapeDtypeStruct((seq_len, s.width), f32) for s in diff]
        + [jax.ShapeDtypeStruct(p.shape, f32) for p in params],
        scratch_shapes=[pltpu.VMEM((state_rows, LANES), f32)] + [pltpu.VMEM((HALO, s.width), f32) for s in halos],
        compiler_params=_params(("arbitrary",)), name=name)(*operands)
    return outs[:len(diff)], outs[len(diff):]


TOK = 256


def inproj_fwd(name, x, nw, w):
    seq_len, n = x.shape[0], w.shape[1]

    def body(x_ref, nw_ref, w_ref, o_ref):
        o_ref[...] = mm(rmsnorm_f(x_ref[...], nw_ref[...]), w_ref[...])

    return pl.pallas_call(
        body, grid=(seq_len // TOK,),
        in_specs=[_row_spec(TOK, D_MODEL), _const_spec(nw.shape), _resident_spec(w.shape)],
        out_specs=_row_spec(TOK, n), out_shape=jax.ShapeDtypeStruct((seq_len, n), f32),
        compiler_params=_params(("arbitrary",)), name=name)(x, nw, w)


def inproj_bwd(name, x, nw, w, pieces, d_res):
    seq_len, n = x.shape[0], w.shape[1]
    widths = [p.shape[1] for p in pieces]
    assert sum(widths) == n
    k = len(pieces)

    def body(*refs):
        x_ref, nw_ref, w_ref = refs[:3]
        p_refs, dres_ref = refs[3:3 + k], refs[3 + k]
        dx_ref, dw_ref, dnw_ref = refs[4 + k:]

        @pl.when(pl.program_id(0) == 0)
        def _():
            dw_ref[...] = jnp.zeros_like(dw_ref)
            dnw_ref[...] = jnp.zeros_like(dnw_ref)

        h, vjp = jax.vjp(rmsnorm_f, x_ref[...], nw_ref[...])
        dh, off = jnp.zeros_like(h), 0
        for p_ref, wd in zip(p_refs, widths):
            g = p_ref[...]
            dh = dh + mm_nt(g, w_ref[:, off:off + wd])
            dw_ref[:, off:off + wd] += mm_tn(h, g)
            off += wd
        dx, dnw = vjp(dh)
        dx_ref[...] = dres_ref[...] + dx
        dnw_ref[...] += dnw

    return pl.pallas_call(
        body, grid=(seq_len // TOK,),
        in_specs=[_row_spec(TOK, D_MODEL), _const_spec(nw.shape), _resident_spec(w.shape)]
        + [_row_spec(TOK, wd) for wd in widths] + [_row_spec(TOK, D_MODEL)],
        out_specs=[_row_spec(TOK, D_MODEL), _resident_spec((D_MODEL, n)), _const_spec(nw.shape)],
        out_shape=[jax.ShapeDtypeStruct((seq_len, D_MODEL), f32), jax.ShapeDtypeStruct((D_MODEL, n), f32),
                   jax.ShapeDtypeStruct(nw.shape, f32)],
        compiler_params=_params(("arbitrary",)), name=name)(x, nw, w, *pieces, d_res)


def outproj_fwd(name, x, a, b, w):
    seq_len, wa, wb = x.shape[0], a.shape[1], b.shape[1]

    def body(x_ref, a_ref, b_ref, w_ref, o_ref):
        o_ref[...] = x_ref[...] + mm(a_ref[...], w_ref[:wa, :]) + mm(b_ref[...], w_ref[wa:, :])

    return pl.pallas_call(
        body, grid=(seq_len // TOK,),
        in_specs=[_row_spec(TOK, D_MODEL), _row_spec(TOK, wa), _row_spec(TOK, wb), _resident_spec(w.shape)],
        out_specs=_row_spec(TOK, D_MODEL), out_shape=jax.ShapeDtypeStruct((seq_len, D_MODEL), f32),
        compiler_params=_params(("arbitrary",)), name=name)(x, a, b, w)


def outproj_bwd(name, dy, a, b, w):
    seq_len, wa, wb = dy.shape[0], a.shape[1], b.shape[1]

    def body(dy_ref, a_ref, b_ref, w_ref, da_ref, db_ref, dw_ref):
        @pl.when(pl.program_id(0) == 0)
        def _():
            dw_ref[...] = jnp.zeros_like(dw_ref)

        g = dy_ref[...]
        da_ref[...] = mm_nt(g, w_ref[:wa, :])
        db_ref[...] = mm_nt(g, w_ref[wa:, :])
        dw_ref[:wa, :] += mm_tn(a_ref[...], g)
        dw_ref[wa:, :] += mm_tn(b_ref[...], g)

    return pl.pallas_call(
        body, grid=(seq_len // TOK,),
        in_specs=[_row_spec(TOK, D_MODEL), _row_spec(TOK, wa), _row_spec(TOK, wb), _resident_spec(w.shape)],
        out_specs=[_row_spec(TOK, wa), _row_spec(TOK, wb), _resident_spec(w.shape)],
        out_shape=[jax.ShapeDtypeStruct((seq_len, wa), f32), jax.ShapeDtypeStruct((seq_len, wb), f32),
                   jax.ShapeDtypeStruct(w.shape, f32)],
        compiler_params=_params(("arbitrary",)), name=name)(dy, a, b, w)


FF_BLOCK = D_FF // N_DEV


def mlp_fwd(name, x, nw, w_up, w_down):
    seq_len = x.shape[0]

    def body(x_ref, nw_ref, up_ref, down_ref, o_ref):
        xv = x_ref[...]
        h = rmsnorm_f(xv, nw_ref[...])
        acc = xv
        for d in range(N_DEV):
            r = jnp.maximum(mm(h, up_ref[d]), 0.0)
            acc = acc + mm(r * r, down_ref[d])
        o_ref[...] = acc

    return pl.pallas_call(
        body, grid=(seq_len // TOK,),
        in_specs=[_row_spec(TOK, D_MODEL), _const_spec(nw.shape), _resident_spec(w_up.shape), _resident_spec(w_down.shape)],
        out_specs=_row_spec(TOK, D_MODEL), out_shape=jax.ShapeDtypeStruct((seq_len, D_MODEL), f32),
        compiler_params=_params(("arbitrary",)), name=name)(x, nw, w_up, w_down)


MLP_SPLIT = 2


def mlp_bwd(name, x, nw, w_up, w_down, dy):
    seq_len = x.shape[0]
    nt = seq_len // TOK
    per = N_DEV // MLP_SPLIT

    def body(x_ref, nw_ref, up_ref, down_ref, dy_ref, dh_ref, dup_ref, ddown_ref):
        @pl.when(pl.program_id(1) == 0)
        def _():
            dup_ref[...] = jnp.zeros_like(dup_ref)
            ddown_ref[...] = jnp.zeros_like(ddown_ref)

        h = rmsnorm_f(x_ref[...], nw_ref[...])
        g = dy_ref[...]
        dh = jnp.zeros_like(h)
        for d in range(per):
            r = jnp.maximum(mm(h, up_ref[d]), 0.0)
            da = mm_nt(g, down_ref[d]) * (2.0 * r)
            ddown_ref[d] += mm_tn(r * r, g)
            dup_ref[d] += mm_tn(h, da)
            dh = dh + mm_nt(da, up_ref[d])
        dh_ref[...] = dh

    dh_parts, d_up, d_down = pl.pallas_call(
        body, grid=(MLP_SPLIT, nt),
        in_specs=[pl.BlockSpec((TOK, D_MODEL), lambda j, i: (i, 0)), pl.BlockSpec(nw.shape, lambda j, i: (0, 0)),
                  pl.BlockSpec((per, D_MODEL, FF_BLOCK), lambda j, i: (j, 0, 0), pipeline_mode=pl.Buffered(1)),
                  pl.BlockSpec((per, FF_BLOCK, D_MODEL), lambda j, i: (j, 0, 0), pipeline_mode=pl.Buffered(1)),
                  pl.BlockSpec((TOK, D_MODEL), lambda j, i: (i, 0))],
        out_specs=[pl.BlockSpec((None, TOK, D_MODEL), lambda j, i: (j, i, 0)),
                   pl.BlockSpec((per, D_MODEL, FF_BLOCK), lambda j, i: (j, 0, 0), pipeline_mode=pl.Buffered(1)),
                   pl.BlockSpec((per, FF_BLOCK, D_MODEL), lambda j, i: (j, 0, 0), pipeline_mode=pl.Buffered(1))],
        out_shape=[jax.ShapeDtypeStruct((MLP_SPLIT, seq_len, D_MODEL), f32),
                   jax.ShapeDtypeStruct(w_up.shape, f32), jax.ShapeDtypeStruct(w_down.shape, f32)],
        compiler_params=_params(("arbitrary", "arbitrary")), name=name)(x, nw, w_up, w_down, dy)

    def norm_body(x_ref, nw_ref, dh_ref, dy_ref, dx_ref, dnw_ref):
        @pl.when(pl.program_id(0) == 0)
        def _():
            dnw_ref[...] = jnp.zeros_like(dnw_ref)

        _, vjp = jax.vjp(rmsnorm_f, x_ref[...], nw_ref[...])
        dh = dh_ref[0]
        for j in range(1, MLP_SPLIT):
            dh = dh + dh_ref[j]
        dx, dnw = vjp(dh)
        dx_ref[...] = dy_ref[...] + dx
        dnw_ref[...] += dnw

    dx, dnw = pl.pallas_call(
        norm_body, grid=(nt,),
        in_specs=[_row_spec(TOK, D_MODEL), _const_spec(nw.shape),
                  pl.BlockSpec((MLP_SPLIT, TOK, D_MODEL), lambda i: (0, i, 0)), _row_spec(TOK, D_MODEL)],
        out_specs=[_row_spec(TOK, D_MODEL), _const_spec(nw.shape)],
        out_shape=[jax.ShapeDtypeStruct((seq_len, D_MODEL), f32), jax.ShapeDtypeStruct(nw.shape, f32)],
        compiler_params=_params(("arbitrary",)), name=name + "_norm")(x, nw, dh_parts, dy)
    return dx, d_up, d_down, dnw


def final_loss(name, x, nw, target):
    seq_len = x.shape[0]

    def body(x_ref, nw_ref, t_ref, loss_ref, dx_ref, dnw_ref):
        @pl.when(pl.program_id(0) == 0)
        def _():
            loss_ref[...] = jnp.zeros_like(loss_ref)
            dnw_ref[...] = jnp.zeros_like(dnw_ref)

        y, vjp = jax.vjp(rmsnorm_f, x_ref[...], nw_ref[...])
        err = y - t_ref[...]
        loss_ref[...] += 0.5 * jnp.sum(jnp.mean(err * err, axis=-1, keepdims=True), axis=0, keepdims=True)
        dx, dnw = vjp(err * (1.0 / D_MODEL))
        dx_ref[...] = dx
        dnw_ref[...] += dnw

    return pl.pallas_call(
        body, grid=(seq_len // TOK,),
        in_specs=[_row_spec(TOK, D_MODEL), _const_spec(nw.shape), _row_spec(TOK, D_MODEL)],
        out_specs=[_const_spec((8, LANES)), _row_spec(TOK, D_MODEL), _const_spec(nw.shape)],
        out_shape=[jax.ShapeDtypeStruct((8, LANES), f32), jax.ShapeDtypeStruct((seq_len, D_MODEL), f32),
                   jax.ShapeDtypeStruct(nw.shape, f32)],
        compiler_params=_params(("arbitrary",)), name=name)(x, nw, target)


def _whole(a):
    return pl.BlockSpec(a.shape, lambda: (0,) * len(a.shape))


def s5_prep_fwd(name, raw):
    def body(*refs):
        outs = s5_prep(*[r[...] for r in refs[:7]])
        for o_ref, o in zip(refs[7:], outs):
            o_ref[...] = o

    shapes = [(1, S5_W)] * 2 + [(S5_CH, S5_W)] * 4
    return pl.pallas_call(
        body, in_specs=[_whole(a) for a in raw], out_specs=[pl.BlockSpec(s, lambda s=s: (0,) * len(s)) for s in shapes],
        out_shape=[jax.ShapeDtypeStruct(s, f32) for s in shapes],
        compiler_params=pltpu.CompilerParams(vmem_limit_bytes=VMEM_LIMIT), name=name)(*raw)


def s5_prep_bwd(name, raw, cts):
    def body(*refs):
        _, vjp = jax.vjp(s5_prep, *[r[...] for r in refs[:7]])
        grads = vjp(tuple(r[...] for r in refs[7:13]))
        for o_ref, g in zip(refs[13:], grads):
            o_ref[...] = g

    return pl.pallas_call(
        body, in_specs=[_whole(a) for a in list(raw) + list(cts)], out_specs=[_whole(a) for a in raw],
        out_shape=[jax.ShapeDtypeStruct(a.shape, f32) for a in raw],
        compiler_params=pltpu.CompilerParams(vmem_limit_bytes=VMEM_LIMIT), name=name)(*raw, *cts)


def s5_fwd(name, proj, u_block, lam_re, lam_im, bblk_re, bblk_im, cblk_re, cblk_im, d_skip, w_glu, b_glu):
    seq_len = proj.shape[0]

    def body(u_ref, lr_ref, li_ref, br_ref, bi_ref, cr_ref, ci_ref, d_ref, wg_ref, bg_ref,
             o_ref, hr_ref, hi_ref, carry_re, carry_im):
        @pl.when(pl.program_id(0) == 0)
        def _():
            carry_re[...] = jnp.zeros_like(carry_re)
            carry_im[...] = jnp.zeros_like(carry_im)

        u = u_ref[...]
        hr_ref[...] = mm(u, br_ref[...])
        hi_ref[...] = mm(u, bi_ref[...])
        lr, li = lr_ref[...], li_ref[...]

        def step(t, h):
            pr, pi = h
            row = pl.ds(t, 1)
            nr = lr * pr - li * pi + hr_ref[row, :]
            ni = lr * pi + li * pr + hi_ref[row, :]
            hr_ref[row, :] = nr
            hi_ref[row, :] = ni
            return nr, ni

        er, ei = lax.fori_loop(0, TOK, step, (carry_re[0:1, :], carry_im[0:1, :]))
        carry_re[0:1, :] = er
        carry_im[0:1, :] = ei
        o_ref[...] = s5_out_fn(hr_ref[...], hi_ref[...], u, cr_ref[...], ci_ref[...], d_ref[...],
                               wg_ref[...].astype(f32), bg_ref[...])

    consts = [lam_re, lam_im, bblk_re, bblk_im, cblk_re, cblk_im, d_skip, w_glu, b_glu]
    return pl.pallas_call(
        body, grid=(seq_len // TOK,),
        in_specs=[_row_spec(TOK, S5_CH, u_block)] + [_const_spec(a.shape) for a in consts],
        out_specs=[_row_spec(TOK, S5_CH), _row_spec(TOK, S5_W), _row_spec(TOK, S5_W)],
        out_shape=[jax.ShapeDtypeStruct((seq_len, S5_CH), f32), jax.ShapeDtypeStruct((seq_len, S5_W), f32),
                   jax.ShapeDtypeStruct((seq_len, S5_W), f32)],
        scratch_shapes=[pltpu.VMEM((HALO, S5_W), f32), pltpu.VMEM((HALO, S5_W), f32)],
        compiler_params=_params(("arbitrary",)), name=name)(proj, *consts)


def s5_bwd(name, proj, u_block, h_re, h_im, d_out, lam_re, lam_im, bblk_re, bblk_im, cblk_re, cblk_im, d_skip, w_glu, b_glu):
    seq_len = proj.shape[0]
    nt = seq_len // TOK
    consts = [lam_re, lam_im, bblk_re, bblk_im, cblk_re, cblk_im, d_skip, w_glu, b_glu]

    def body(u_ref, hr_ref, hi_ref, pr_ref, pi_ref, dout_ref, lr_ref, li_ref, br_ref, bi_ref, cr_ref, ci_ref, d_ref, wg_ref, bg_ref,
             du_ref, dlr_ref, dli_ref, dbr_ref, dbi_ref, dcr_ref, dci_ref, dd_ref, dwg_ref, dbg_ref,
             gr_ref, gi_ref, carry_re, carry_im):
        i = pl.program_id(0)
        tile = nt - 1 - i

        @pl.when(i == 0)
        def _():
            for r in (carry_re, carry_im, dlr_ref, dli_ref, dbr_ref, dbi_ref, dcr_ref, dci_ref, dd_ref, dwg_ref, dbg_ref):
                r[...] = jnp.zeros_like(r)

        u, h_r, h_i = u_ref[...], hr_ref[...], hi_ref[...]
        _, vjp = jax.vjp(s5_out_fn, h_r, h_i, u, cr_ref[...], ci_ref[...], d_ref[...], wg_ref[...].astype(f32), bg_ref[...])
        ghr, ghi, du, dcr, dci, dd, dwg, dbg = vjp(dout_ref[...])
        gr_ref[...] = ghr
        gi_ref[...] = ghi
        lr, li = lr_ref[...], li_ref[...]

        def step(k, g):
            nr, ni = g
            row = pl.ds(TOK - 1 - k, 1)
            cr = gr_ref[row, :] + lr * nr + li * ni
            ci = gi_ref[row, :] - li * nr + lr * ni
            gr_ref[row, :] = cr
            gi_ref[row, :] = ci
            return cr, ci

        er, ei = lax.fori_loop(0, TOK, step, (carry_re[0:1, :], carry_im[0:1, :]))
        carry_re[0:1, :] = er
        carry_im[0:1, :] = ei
        g_r, g_i = gr_ref[...], gi_ref[...]
        first = _iota((TOK, 1), 0) == 0
        keep = jnp.where(tile > 0, 1.0, 0.0)
        p_r = jnp.where(first, pr_ref[HALO - 1:HALO, :] * keep, _roll(h_r, 1, 0))
        p_i = jnp.where(first, pi_ref[HALO - 1:HALO, :] * keep, _roll(h_i, 1, 0))
        dlr_ref[...] += jnp.sum(g_r * p_r + g_i * p_i, axis=0, keepdims=True)
        dli_ref[...] += jnp.sum(g_i * p_r - g_r * p_i, axis=0, keepdims=True)
        du_ref[...] = du + mm_nt(g_r, br_ref[...]) + mm_nt(g_i, bi_ref[...])
        dbr_ref[...] += mm_tn(u, g_r)
        dbi_ref[...] += mm_tn(u, g_i)
        dcr_ref[...] += dcr
        dci_ref[...] += dci
        dd_ref[...] += dd
        dwg_ref[...] += dwg
        dbg_ref[...] += dbg

    rev = lambda cols, cb=0: pl.BlockSpec((TOK, cols), lambda i, cb=cb: (nt - 1 - i, cb))
    prev = pl.BlockSpec((HALO, S5_W), lambda i: (jnp.maximum((nt - 1 - i) * (TOK // HALO) - 1, 0), 0))
    outs = pl.pallas_call(
        body, grid=(nt,),
        in_specs=[rev(S5_CH, u_block), rev(S5_W), rev(S5_W), prev, prev, rev(S5_CH)] + [_const_spec(a.shape) for a in consts],
        out_specs=[rev(S5_CH)] + [_const_spec(a.shape) for a in consts],
        out_shape=[jax.ShapeDtypeStruct((seq_len, S5_CH), f32)] + [jax.ShapeDtypeStruct(a.shape, f32) for a in consts],
        scratch_shapes=[pltpu.VMEM((TOK, S5_W), f32), pltpu.VMEM((TOK, S5_W), f32),
                        pltpu.VMEM((HALO, S5_W), f32), pltpu.VMEM((HALO, S5_W), f32)],
        compiler_params=_params(("arbitrary",)), name=name)(proj, h_re, h_im, h_re, h_im, d_out, *consts)
    return outs[0], outs[1:]


ANY = pl.BlockSpec(memory_space=pl.ANY)


def _mesh_position():
    x, y, c = lax.axis_index("x"), lax.axis_index("y"), lax.axis_index("c")
    return x, y, c, 4 * x + 2 * y + c


def _peer(x, y, c, r):
    px = 1 - x if r & 4 else x
    py = 1 - y if r & 2 else y
    pc = 1 - c if r & 1 else c
    return (px, py, pc), 4 * px + 2 * py + pc


def _exchange(name, arrays, gather):
    n = len(arrays)

    def body(*refs):
        ins, outs = refs[:n], refs[n:2 * n]
        send_sems, recv_sems, local_sems = refs[2 * n:]
        x, y, c, me = _mesh_position()
        started = []
        for i in range(n):
            mine = ins[i] if gather else ins[i].at[me]
            local = pltpu.make_async_copy(mine, outs[i].at[me], local_sems.at[i])
            local.start()
            started.append(local)
            for r in range(1, N_DEV):
                peer, peer_idx = _peer(x, y, c, r)
                cp = pltpu.make_async_remote_copy(
                    src_ref=ins[i] if gather else ins[i].at[peer_idx], dst_ref=outs[i].at[me],
                    send_sem=send_sems.at[i, r - 1], recv_sem=recv_sems.at[i, r - 1],
                    device_id=peer, device_id_type=pl.DeviceIdType.MESH)
                cp.start()
        for i in range(n):
            for r in range(1, N_DEV):
                peer, peer_idx = _peer(x, y, c, r)
                cp = pltpu.make_async_remote_copy(
                    src_ref=ins[i] if gather else ins[i].at[peer_idx], dst_ref=outs[i].at[peer_idx],
                    send_sem=send_sems.at[i, r - 1], recv_sem=recv_sems.at[i, r - 1],
                    device_id=peer, device_id_type=pl.DeviceIdType.MESH)
                cp.wait_recv()
                cp.wait_send()
        for local in started:
            local.wait()

    shapes = [((N_DEV,) + a.shape) if gather else a.shape for a in arrays]
    return pl.pallas_call(
        body, in_specs=[ANY] * n, out_specs=[ANY] * n,
        out_shape=[jax.ShapeDtypeStruct(s, a.dtype) for s, a in zip(shapes, arrays)],
        scratch_shapes=[pltpu.SemaphoreType.DMA((n, N_DEV - 1)), pltpu.SemaphoreType.DMA((n, N_DEV - 1)),
                        pltpu.SemaphoreType.DMA((n,))],
        compiler_params=pltpu.CompilerParams(has_side_effects=True), name=name)(*arrays)


def adamw(name, parts, w, m, v):
    rows, cols = w.shape
    tr = rows
    for cand in (512, 256, 128, 64, 32, 16, 8):
        if rows * cols * 4 > (1 << 20) and rows % cand == 0 and cand * cols * 4 <= (1 << 20):
            tr = cand
            break

    def body(p_ref, w_ref, m_ref, v_ref, g_ref, d_ref, nm_ref, nv_ref):
        g = p_ref[0]
        for s in range(1, N_DEV):
            g = g + p_ref[s]
        nm = ADAM_B1 * m_ref[...] + (1.0 - ADAM_B1) * g
        nv = ADAM_B2 * v_ref[...] + (1.0 - ADAM_B2) * (g * g)
        m_hat = nm / (1.0 - ADAM_B1 ** ADAM_STEP)
        v_hat = nv / (1.0 - ADAM_B2 ** ADAM_STEP)
        g_ref[...] = g
        d_ref[...] = -ADAM_LR * (m_hat / (jnp.sqrt(v_hat) + ADAM_EPS) + ADAM_WD * w_ref[...])
        nm_ref[...] = nm
        nv_ref[...] = nv

    blk = pl.BlockSpec((tr, cols), lambda i: (i, 0))
    return pl.pallas_call(
        body, grid=(rows // tr,),
        in_specs=[pl.BlockSpec((N_DEV, tr, cols), lambda i: (0, i, 0)), blk, blk, blk],
        out_specs=[blk] * 4, out_shape=[jax.ShapeDtypeStruct((rows, cols), f32)] * 4,
        compiler_params=_params(("arbitrary",)), name=name)(parts, w, m, v)


WEIGHTS = ['l0_norm_mix', 'l0_w_in', 'ssd_conv_w', 'ssd_conv_b', 'ssd_dt_bias', 'ssd_A_log', 'ssd_D', 'ssd_norm_w',
           'l0_w_out', 'l0_norm_mlp', 'l0_w_up', 'l0_w_down', 'l1_norm_mix', 'l1_w_in', 'gdn_conv_w', 'gdn_A_log',
           'gdn_dt_bias', 'gdn_norm_w', 's5_A_re', 's5_A_im', 's5_log_step', 's5_B_re', 's5_B_im', 's5_C_re', 's5_C_im',
           's5_D', 's5_w_glu', 's5_b_glu', 'l1_w_out', 'l1_norm_mlp', 'l1_w_up', 'l1_w_down', 'final_norm']
SHARDED = ['l0_w_in', 'l0_w_out', 'l0_w_up', 'l0_w_down', 'l1_w_in', 's5_w_glu', 'l1_w_out', 'l1_w_up', 'l1_w_down',
           'ssd_conv_w', 'gdn_conv_w']
F32_GATHER = ('ssd_conv_w', 'gdn_conv_w')
REPLICATED = [n for n in WEIGHTS if n not in SHARDED]
INPUTS = ['x'] + WEIGHTS + ['loss_target'] + ['m_' + n for n in WEIGHTS] + ['v_' + n for n in WEIGHTS]


def _row(v):
    return v.reshape(1, -1)


def _pad_lanes(v, offset=0):
    return jnp.pad(v, (offset, LANES - offset - v.shape[0])).reshape(1, LANES)


def _cols_to_blocks(g):
    return g.reshape(g.shape[0], N_DEV, -1).transpose(1, 0, 2)


def _blocks_to_cols(g):
    return g.transpose(1, 0, 2).reshape(g.shape[1], -1)


def _pack(arrays):
    parts, slots, at = [], [], 0
    for a in arrays:
        n = a.size
        rows = -(-n // (8 * LANES)) * 8
        parts.append(jnp.pad(a.reshape(-1), (0, rows * LANES - n)).reshape(rows, LANES))
        slots.append((at, rows, n, a.shape))
        at += rows
    return jnp.concatenate(parts, axis=0), slots


def _unpack(buf, slots):
    return [buf[at:at + rows].reshape(-1)[:n].reshape(shape) for at, rows, n, shape in slots]


def kernel(*args):
    a = dict(zip(INPUTS, args, strict=True))
    seq_len = a['x'].shape[1]
    x0 = a['x'].reshape(seq_len, D_MODEL)
    target = a['loss_target'].reshape(seq_len, D_MODEL)

    gathered = _exchange("gather_weights",
                         [a[n] if n in F32_GATHER else a[n].astype(_MXU) for n in SHARDED], gather=True)
    g = dict(zip(SHARDED, gathered))
    w_nat = _blocks_to_cols(g['l0_w_in'])
    win0 = jnp.concatenate([w_nat[:, :2048], w_nat[:, 2560:3584], w_nat[:, 2048:2560], w_nat[:, 3584:3592],
                            jnp.zeros((D_MODEL, IN0_PAD - IN0_W), _MXU)], axis=1)
    w_nat = g['l1_w_in'].reshape(D_MODEL, IN1_W)
    win1 = jnp.concatenate([w_nat[:, :3072], w_nat[:, 3084:3340], w_nat[:, 3072:3084],
                            jnp.zeros((D_MODEL, IN1_PAD - IN1_W), _MXU)], axis=1)
    wout0, wout1 = g['l0_w_out'].reshape(D_MODEL, D_MODEL), g['l1_w_out'].reshape(D_MODEL, D_MODEL)
    wglu = g['s5_w_glu'].reshape(S5_CH, S5_CH)
    ssd_cw, gdn_cw = _blocks_to_cols(g['ssd_conv_w']), _blocks_to_cols(g['gdn_conv_w'])

    half = RET_D // 2
    inv = ROPE_THETA ** (-jnp.arange(half, dtype=f32) / half)
    ang = jnp.arange(seq_len, dtype=f32)[:, None] * inv[None, :]
    cos, sin = jnp.cos(ang), jnp.sin(ang)
    cos, sin = jnp.concatenate([cos, cos], axis=1), jnp.concatenate([-sin, sin], axis=1)
    ssd_params = [ssd_cw, _row(a['ssd_conv_b']), _pad_lanes(a['ssd_dt_bias']), _pad_lanes(a['ssd_A_log']),
                  _pad_lanes(a['ssd_D']), _row(a['ssd_norm_w'])]
    gdn_params = [gdn_cw, _pad_lanes(a['gdn_A_log'], GDN_GCOL), _pad_lanes(a['gdn_dt_bias'], GDN_GCOL), _row(a['gdn_norm_w'])]
    s5_raw = [a['s5_A_re'].reshape(1, S5_W), a['s5_A_im'].reshape(1, S5_W), _pad_lanes(a['s5_log_step']),
              a['s5_B_re'].transpose(2, 0, 1).reshape(S5_GROUP, S5_W), a['s5_B_im'].transpose(2, 0, 1).reshape(S5_GROUP, S5_W),
              a['s5_C_re'].transpose(1, 0, 2).reshape(S5_GROUP, S5_W), a['s5_C_im'].transpose(1, 0, 2).reshape(S5_GROUP, S5_W)]
    s5_d, s5_bg = _row(a['s5_D']), _row(a['s5_b_glu'])
    nw = {n: _row(a[n]) for n in ('l0_norm_mix', 'l0_norm_mlp', 'l1_norm_mix', 'l1_norm_mlp', 'final_norm')}

    proj0 = inproj_fwd("l0_in", x0, nw['l0_norm_mix'], win0)
    ret_seqs = [Seq(proj0, 512, 0), Seq(proj0, 512, 1), Seq(proj0, 512, 2), Seq(proj0, 512, 3),
                Seq(cos, LANES, 0, "const"), Seq(sin, LANES, 0, "const")]
    ret_out, ret_st = mixer_fwd("ret_fwd", ret_chunk, ret_seqs, [], 512, RET_HEADS * RET_D, seq_len)
    ssd_seqs = [Seq(proj0, 512, 6), Seq(proj0, 1024, 2, "halo"), Seq(proj0, LANES, 28)]
    ssd_out, ssd_st = mixer_fwd("ssd_fwd", ssd_chunk, ssd_seqs, ssd_params, SSD_INNER, SSD_INNER, seq_len)
    x1 = outproj_fwd("l0_out", x0, ret_out, ssd_out, wout0)
    x2 = mlp_fwd("l0_mlp", x1, nw['l0_norm_mlp'], g['l0_w_up'], g['l0_w_down'])
    proj1 = inproj_fwd("l1_in", x2, nw['l1_norm_mix'], win1)
    gdn_seqs = [Seq(proj1, 3 * GDN_W, 0, "halo"), Seq(proj1, GDN_W, 3), Seq(proj1, LANES, 26)]
    gdn_out, gdn_st = mixer_fwd("gdn_fwd", gdn_chunk, gdn_seqs, gdn_params, GDN_W, GDN_W, seq_len)
    prep = s5_prep_fwd("s5_prep", s5_raw)
    s5_out, h_re, h_im = s5_fwd("s5_fwd", proj1, 12, *prep, s5_d, wglu, s5_bg)
    x3 = outproj_fwd("l1_out", x2, gdn_out, s5_out, wout1)
    x4 = mlp_fwd("l1_mlp", x3, nw['l1_norm_mlp'], g['l1_w_up'], g['l1_w_down'])
    loss_blk, dx4, d_final = final_loss("final_loss", x4, nw['final_norm'], target)

    dx3, d_up1, d_down1, d_nmlp1 = mlp_bwd("l1_mlp_bwd", x3, nw['l1_norm_mlp'], g['l1_w_up'], g['l1_w_down'], dx4)
    d_gdn, d_s5, d_wout1 = outproj_bwd("l1_out_bwd", dx3, gdn_out, s5_out, wout1)
    d_u, s5_g = s5_bwd("s5_bwd", proj1, 12, h_re, h_im, d_s5, *prep, s5_d, wglu, s5_bg)
    s5_raw_g = s5_prep_bwd("s5_prep_bwd", s5_raw, s5_g[:6])
    (d_qkv, d_z1, d_ba), gdn_pg = mixer_bwd("gdn_bwd", gdn_chunk, gdn_seqs, gdn_params, gdn_st, d_gdn, seq_len)
    dx2, d_win1, d_nmix1 = inproj_bwd("l1_in_bwd", x2, nw['l1_norm_mix'], win1, [d_qkv, d_z1, d_u, d_ba], dx3)
    dx1, d_up0, d_down0, d_nmlp0 = mlp_bwd("l0_mlp_bwd", x1, nw['l0_norm_mlp'], g['l0_w_up'], g['l0_w_down'], dx2)
    d_ret, d_ssd, d_wout0 = outproj_bwd("l0_out_bwd", dx1, ret_out, ssd_out, wout0)
    d_qkvg, _ = mixer_bwd("ret_bwd", ret_chunk, ret_seqs, [], ret_st, d_ret, seq_len)
    (d_z0, d_xbc, d_dt), ssd_pg = mixer_bwd("ssd_bwd", ssd_chunk, ssd_seqs, ssd_params, ssd_st, d_ssd, seq_len)
    dx0, d_win0, d_nmix0 = inproj_bwd("l0_in_bwd", x0, nw['l0_norm_mix'], win0, list(d_qkvg) + [d_xbc, d_z0, d_dt], dx1)

    d_win0 = jnp.concatenate([d_win0[:, :2048], d_win0[:, 3072:3584], d_win0[:, 2048:3072], d_win0[:, 3584:3592]], axis=1)
    d_win1 = jnp.concatenate([d_win1[:, :3072], d_win1[:, 3328:3340], d_win1[:, 3072:3328]], axis=1)
    sharded_g = {
        'l0_w_in': _cols_to_blocks(d_win0), 'l0_w_out': d_wout0.reshape(N_DEV, -1, D_MODEL), 'l0_w_up': d_up0, 'l0_w_down': d_down0,
        'l1_w_in': d_win1.reshape(N_DEV, -1, IN1_W), 's5_w_glu': s5_g[7].reshape(N_DEV, -1, S5_CH),
        'l1_w_out': d_wout1.reshape(N_DEV, -1, D_MODEL), 'l1_w_up': d_up1, 'l1_w_down': d_down1,
        'ssd_conv_w': _cols_to_blocks(ssd_pg[0]), 'gdn_conv_w': _cols_to_blocks(gdn_pg[0])}
    from_b = lambda t: t.reshape(S5_GROUP, S5_GROUPS, S5_STATE).transpose(1, 2, 0)
    from_c = lambda t: t.reshape(S5_GROUP, S5_GROUPS, S5_STATE).transpose(1, 0, 2)
    replicated_g = {
        'l0_norm_mix': d_nmix0, 'ssd_conv_b': ssd_pg[1], 'ssd_dt_bias': ssd_pg[2][0, :SSD_HEADS], 'ssd_A_log': ssd_pg[3][0, :SSD_HEADS],
        'ssd_D': ssd_pg[4][0, :SSD_HEADS], 'ssd_norm_w': ssd_pg[5], 'l0_norm_mlp': d_nmlp0, 'l1_norm_mix': d_nmix1,
        'gdn_A_log': gdn_pg[1][0, GDN_GCOL:GDN_GCOL + GDN_HEADS], 'gdn_dt_bias': gdn_pg[2][0, GDN_GCOL:GDN_GCOL + GDN_HEADS],
        'gdn_norm_w': gdn_pg[3], 's5_A_re': s5_raw_g[0], 's5_A_im': s5_raw_g[1], 's5_log_step': s5_raw_g[2][0, :S5_GROUPS],
        's5_B_re': from_b(s5_raw_g[3]), 's5_B_im': from_b(s5_raw_g[4]), 's5_C_re': from_c(s5_raw_g[5]), 's5_C_im': from_c(s5_raw_g[6]),
        's5_D': s5_g[6], 's5_b_glu': s5_g[8], 'l1_norm_mlp': d_nmlp1, 'final_norm': d_final}
    replicated_g = {n: replicated_g[n].reshape(a[n].shape) for n in REPLICATED}

    parts = _exchange("scatter_grads", [sharded_g[n] for n in SHARDED], gather=False)
    packed_g, slots = _pack([replicated_g[n] for n in REPLICATED])
    (packed_parts,) = _exchange("gather_small_grads", [packed_g], gather=True)
    results = {}
    for n, p in zip(SHARDED, parts):
        results[n] = adamw("adamw_" + n, p, a[n], a['m_' + n], a['v_' + n])
    packed = [_pack([a[pre + n] for n in REPLICATED])[0] for pre in ('', 'm_', 'v_')]
    small = [_unpack(t, slots) for t in adamw("adamw_small", packed_parts, *packed)]
    for i, n in enumerate(REPLICATED):
        results[n] = tuple(small[k][i] for k in range(4))

    loss = lax.psum(loss_blk[0, 0], ("x", "y", "c"))
    grad_x = dx0.reshape(a['x'].shape)
    return (loss, grad_x, *[results[n][0] for n in WEIGHTS], *[results[n][1] for n in WEIGHTS],
            *[results[n][2] for n in WEIGHTS], *[results[n][3] for n in WEIGHTS])
```

```python
import functools
import math

import numpy as np
import jax
import jax.numpy as jnp
from jax import lax
from jax.experimental import pallas as pl
from jax.experimental.pallas import tpu as pltpu

f32 = jnp.float32
_MXU = jnp.bfloat16
HI = lax.Precision.HIGHEST

D_MODEL = 1024
CHUNK = 64
EPS = 1e-6
N_DEV = 8
LANES = 128
HALO = 8
CONV_WIDTH = 4

RET_HEADS, RET_D = 4, 128
SSD_HEADS, SSD_P, SSD_N, SSD_GROUPS = 8, 64, 128, 2
SSD_INNER = SSD_HEADS * SSD_P
GDN_HEADS, GDN_D = 6, 128
GDN_W = GDN_HEADS * GDN_D
S5_CH, S5_GROUP, S5_GROUPS, S5_STATE = 256, 16, 16, 64
S5_W = S5_GROUPS * S5_STATE
D_FF = 4096
ROPE_THETA = 10000.0

IN0_W = 3592
IN0_PAD = 3712
IN1_W = 3340
IN1_PAD = 3456

ADAM_LR, ADAM_B1, ADAM_B2, ADAM_EPS, ADAM_WD, ADAM_STEP = 0.001, 0.9, 0.999, 1e-08, 0.01, 10

VMEM_LIMIT = 56 * 1024 * 1024


def _dot(a, b, dims):
    return lax.dot_general(a.astype(_MXU), b.astype(_MXU), (dims, ((), ())), preferred_element_type=f32)


@jax.custom_vjp
def mm(a, b):
    return _dot(a, b, ((1,), (0,)))


@jax.custom_vjp
def mm_nt(a, b):
    return _dot(a, b, ((1,), (1,)))


@jax.custom_vjp
def mm_tn(a, b):
    return _dot(a, b, ((0,), (0,)))


mm.defvjp(lambda a, b: (mm(a, b), (a, b)), lambda r, g: (mm_nt(g, r[1]), mm_tn(r[0], g)))
mm_nt.defvjp(lambda a, b: (mm_nt(a, b), (a, b)), lambda r, g: (mm(g, r[1]), mm_tn(g, r[0])))
mm_tn.defvjp(lambda a, b: (mm_tn(a, b), (a, b)), lambda r, g: (mm_nt(r[1], g), mm(r[0], g)))


def mmh(a, b):
    return jnp.dot(a, b, precision=HI, preferred_element_type=f32)


def _roll(x, shift, axis):
    return pltpu.roll(x, shift, axis)


@functools.partial(jax.custom_vjp, nondiff_argnums=(1,))
def roll_rows(x, s):
    return _roll(x, s, 0) if s else x


roll_rows.defvjp(lambda x, s: (roll_rows(x, s), None),
                 lambda s, _, g: ((_roll(g, g.shape[0] - s, 0) if s else g),))


@jax.custom_vjp
def roll_half(x):
    return _roll(x, x.shape[-1] // 2, 1)


roll_half.defvjp(lambda x: (roll_half(x), None), lambda _, g: (roll_half(g),))


def _iota(shape, axis):
    return lax.broadcasted_iota(jnp.int32, shape, axis)


def silu(x):
    return x * jax.nn.sigmoid(x)


def softplus(x):
    return jnp.maximum(x, 0.0) + jnp.log(1.0 + jnp.exp(-jnp.abs(x)))


def rmsnorm_f(x, w):
    return x * lax.rsqrt(jnp.mean(x * x, axis=-1, keepdims=True) + EPS) * w


def unit_rms(x):
    return x * lax.rsqrt(jnp.mean(x * x, axis=-1, keepdims=True) + EPS)


def _causal(n, strict=False):
    r, c = _iota((n, n), 0), _iota((n, n), 1)
    return (r > c) if strict else (r >= c)


def _tril_ones(n):
    return _causal(n).astype(f32)


def _conv_rows(xe, w):
    acc = w[CONV_WIDTH - 1:CONV_WIDTH, :] * xe
    for j in range(CONV_WIDTH - 1):
        acc = acc + w[j:j + 1, :] * roll_rows(xe, CONV_WIDTH - 1 - j)
    return acc[HALO:, :]


_RET_LOG_GAMMA = [float(np.log(np.float32(1.0) - np.float32(2.0) ** np.float32(-5.0 - h))) for h in range(RET_HEADS)]


def ret_chunk(q, k, v, gate, cos, sin, state):
    c = q.shape[0]
    idx = _iota((c, 1), 0).astype(f32)
    diff = (_iota((c, c), 0) - _iota((c, c), 1)).astype(f32)
    causal = _causal(c)
    hs = range(RET_HEADS)
    cols = [slice(h * RET_D, (h + 1) * RET_D) for h in hs]
    lg = _RET_LOG_GAMMA
    qh = [(q[:, s] * cos + roll_half(q[:, s]) * sin) * (RET_D ** -0.5) for s in cols]
    kh = [k[:, s] * cos + roll_half(k[:, s]) * sin for s in cols]
    vh = [v[:, s] for s in cols]
    sh = [state[s, :] for s in cols]
    scores = [mm_nt(qh[h], kh[h]) * jnp.exp(jnp.where(causal, lg[h] * diff, -jnp.inf)) for h in hs]
    inter = [mm(qh[h] * jnp.exp(lg[h] * (idx + 1.0)), sh[h]) for h in hs]
    y = [mm(scores[h], vh[h]) + inter[h] for h in hs]
    states = [sh[h] * math.exp(lg[h] * c) + mm_tn(kh[h] * jnp.exp(lg[h] * (c - 1.0 - idx)), vh[h]) for h in hs]
    outs = [unit_rms(y[h]) * silu(gate[:, cols[h]]) for h in hs]
    return jnp.concatenate(outs, axis=1), jnp.concatenate(states, axis=0)


def _head_select(n_heads, width):
    r, c = _iota((LANES, n_heads * width), 0), _iota((LANES, n_heads * width), 1)
    return (c // width == r).astype(f32)


def ssd_chunk(z, xe, dtr, state, conv_w, conv_b, dt_bias, a_log, d_skip, norm_w):
    c = z.shape[0]
    xbc = silu(_conv_rows(xe, conv_w) + conv_b)
    xs, bm, cm = xbc[:, :SSD_INNER], xbc[:, SSD_INNER:SSD_INNER + 256], xbc[:, SSD_INNER + 256:]
    sel = _head_select(SSD_HEADS, SSD_P)
    dt = softplus(dtr + dt_bias)
    la = dt * (-jnp.exp(a_log))
    la_cum = mmh(_tril_ones(c), la)
    la_cum_t = la_cum.T
    last = jnp.sum(la, axis=0, keepdims=True)
    xd = xs * mmh(dt, sel)
    la_x = mmh(la_cum, sel)
    last_x = mmh(last, sel)
    to_end = jnp.exp(last_x - la_x)
    from_start = jnp.exp(la_x)
    causal = _causal(c)
    left = (_iota((1, LANES), 1) < SSD_P).astype(f32)
    upper = _iota((LANES, 1), 0) < SSD_P
    pairs, heads = range(SSD_HEADS // 2), range(SSD_HEADS)
    bc = [bm[:, g * SSD_N:(g + 1) * SSD_N] for g in range(SSD_GROUPS)]
    cc = [cm[:, g * SSD_N:(g + 1) * SSD_N] for g in range(SSD_GROUPS)]
    cb = [mm_nt(cc[g], bc[g]) for g in range(SSD_GROUPS)]
    cols = [slice(p * LANES, (p + 1) * LANES) for p in pairs]
    xd_p = [xd[:, s] for s in cols]
    sp = [state[s, :] for s in cols]
    lmat = [jnp.exp(jnp.where(causal, la_cum[:, h:h + 1] - la_cum_t[h:h + 1, :], -jnp.inf)) for h in heads]
    off = [mm_nt(cc[p // 2], sp[p]) * from_start[:, cols[p]] for p in pairs]
    diag = [mm(cb[h // 4] * lmat[h], xd_p[h // 2] * (left if h % 2 == 0 else 1.0 - left)) for h in heads]
    cd = [jnp.where(upper, jnp.exp(last[:, 2 * p:2 * p + 1]), jnp.exp(last[:, 2 * p + 1:2 * p + 2])) for p in pairs]
    states = [sp[p] * cd[p] + mm_tn(xd_p[p] * to_end[:, cols[p]], bc[p // 2]) for p in pairs]
    ys = [off[p] + diag[2 * p] + diag[2 * p + 1] for p in pairs]
    y = jnp.concatenate(ys, axis=1) + mmh(d_skip, sel) * xs
    yg = y * silu(z)
    half = SSD_INNER // SSD_GROUPS
    out = jnp.concatenate([unit_rms(yg[:, i * half:(i + 1) * half]) for i in range(SSD_GROUPS)], axis=1) * norm_w
    return out, jnp.concatenate(states, axis=0)


def mm3(a, b):
    return jnp.dot(a, b, precision=lax.Precision.HIGH, preferred_element_type=f32)


def _unit_lower_inverses(lowers):
    n = lowers[0].shape[0]
    eye = (_iota((n, n), 0) == _iota((n, n), 1)).astype(f32)
    a = [-l for l in lowers]
    p = [eye + x for x in a]
    k = 2
    while k < n:
        a = [mm3(x, x) for x in a]
        p = [y + mm3(y, x) for y, x in zip(p, a)]
        k *= 2
    return p


GDN_GCOL = 6


def gdn_chunk(xe, z, ba, state, conv_w, a_log, dt_bias, norm_w):
    c = z.shape[0]
    qkv = silu(_conv_rows(xe, conv_w))
    beta_all = jax.nn.sigmoid(ba)
    g_all = -jnp.exp(a_log) * softplus(ba + dt_bias)
    gc = mmh(_tril_ones(c), g_all)
    gc_t = gc.T
    last = jnp.sum(g_all, axis=0, keepdims=True)
    causal, strict = _causal(c), _causal(c, strict=True)
    hs = range(GDN_HEADS)
    cols = [slice(h * GDN_D, (h + 1) * GDN_D) for h in hs]
    qh = [qkv[:, h * GDN_D:(h + 1) * GDN_D] for h in hs]
    kh = [qkv[:, GDN_W + h * GDN_D:GDN_W + (h + 1) * GDN_D] for h in hs]
    vh = [qkv[:, 2 * GDN_W + h * GDN_D:2 * GDN_W + (h + 1) * GDN_D] for h in hs]
    qh = [t * lax.rsqrt(jnp.sum(t * t, axis=-1, keepdims=True) + EPS) * (GDN_D ** -0.5) for t in qh]
    kh = [t * lax.rsqrt(jnp.sum(t * t, axis=-1, keepdims=True) + EPS) for t in kh]
    beta = [beta_all[:, h:h + 1] for h in hs]
    col = [gc[:, GDN_GCOL + h:GDN_GCOL + h + 1] for h in hs]
    row = [gc_t[GDN_GCOL + h:GDN_GCOL + h + 1, :] for h in hs]
    lst = [last[:, GDN_GCOL + h:GDN_GCOL + h + 1] for h in hs]
    decay = [jnp.exp(jnp.where(causal, col[h] - row[h], -jnp.inf)) for h in hs]
    e_col = [jnp.exp(t) for t in col]
    kb = [kh[h] * beta[h] for h in hs]
    vb = [vh[h] * beta[h] for h in hs]
    kk = [mm_nt(kb[h], kh[h]) for h in hs]
    qk = [mm_nt(qh[h], kh[h]) for h in hs]
    t_inv = _unit_lower_inverses([jnp.where(strict, kk[h] * decay[h], 0.0) for h in hs])
    u = [mm(t_inv[h], vb[h]) for h in hs]
    w = [mm(t_inv[h], kb[h] * e_col[h]) for h in hs]
    attn = [jnp.where(causal, qk[h] * decay[h], 0.0) for h in hs]
    sh = [state[s, :] for s in cols]
    ws = [mm(w[h], sh[h]) for h in hs]
    qs = [mm(qh[h] * e_col[h], sh[h]) for h in hs]
    v_new = [u[h] - ws[h] for h in hs]
    o = [qs[h] + mm(attn[h], v_new[h]) for h in hs]
    states = [sh[h] * jnp.exp(lst[h]) + mm_tn(kh[h] * jnp.exp(lst[h] - col[h]), v_new[h]) for h in hs]
    outs = [unit_rms(o[h]) * norm_w * silu(z[:, cols[h]]) for h in hs]
    return jnp.concatenate(outs, axis=1), jnp.concatenate(states, axis=0)


def _s5_group_mask():
    r, c = _iota((S5_CH, S5_W), 0), _iota((S5_CH, S5_W), 1)
    return (r // S5_GROUP == c // S5_STATE).astype(f32)


def s5_prep(a_re, a_im, log_step, b_re, b_im, c_re, c_im):
    r, c = _iota((LANES, S5_W), 0), _iota((LANES, S5_W), 1)
    step = jnp.exp(mmh(log_step, (c // S5_STATE == r).astype(f32)))
    zr, zi = a_re * step, a_im * step
    e = jnp.exp(zr)
    lr, li = e * jnp.cos(zi), e * jnp.sin(zi)
    den = a_re * a_re + a_im * a_im
    xr, xi = lr - 1.0, li
    cr, ci = (xr * a_re + xi * a_im) / den, (xi * a_re - xr * a_im) / den
    bbr, bbi = cr * b_re - ci * b_im, cr * b_im + ci * b_re
    mask = _s5_group_mask()
    tile = lambda t: jnp.tile(t, (S5_GROUPS, 1)) * mask
    return lr, li, tile(bbr), tile(bbi), tile(c_re), tile(c_im)


def s5_out_fn(h_re, h_im, u, cblk_re, cblk_im, d_skip, w_glu, b_glu):
    y = mm_nt(h_re, cblk_re) - mm_nt(h_im, cblk_im) + d_skip * u
    y = jax.nn.gelu(y)
    return y * jax.nn.sigmoid(mm(y, w_glu) + b_glu)


def _params(sem, **kw):
    return pltpu.CompilerParams(dimension_semantics=sem, vmem_limit_bytes=VMEM_LIMIT, **kw)


def _const_spec(shape):
    return pl.BlockSpec(shape, lambda i: (0,) * len(shape))


def _resident_spec(shape):
    return pl.BlockSpec(shape, lambda i: (0,) * len(shape), pipeline_mode=pl.Buffered(1))


def _row_spec(rows, cols, col_block=0):
    return pl.BlockSpec((rows, cols), lambda i: (i, col_block))


class Seq:
    def __init__(self, array, width, col_block, kind="tile"):
        self.array, self.width, self.col_block, self.kind = array, width, col_block, kind


def mixer_fwd(name, fn, seqs, params, out_width, state_rows, seq_len, exchange=None):
    nc = seq_len // CHUNK
    n_refs = sum(2 if s.kind == "halo" else 1 for s in seqs)
    n_par = len(params)
    car = Carried(exchange)

    def body(*refs):
        seq_refs, par_refs = refs[:n_refs], refs[n_refs:n_refs + n_par]
        k0 = n_refs + n_par
        ex_ins, (out_ref, st_ref) = refs[k0:k0 + car.n], refs[k0 + car.n:k0 + car.n + 2]
        ex_outs, state, ex_sems = refs[k0 + car.n + 2:k0 + 2 * car.n + 2], refs[k0 + 2 * car.n + 2], refs[k0 + 2 * car.n + 3:]
        c = pl.program_id(0)
        car.start_at(c == 0, ex_ins, ex_outs, ex_sems)

        @pl.when(c == 0)
        def _():
            state[...] = jnp.zeros_like(state)

        vals, k = [], 0
        for s in seqs:
            if s.kind == "halo":
                prev = jnp.where(c > 0, seq_refs[k][...], 0.0)
                vals.append(jnp.concatenate([prev, seq_refs[k + 1][...]], axis=0))
                k += 2
            else:
                vals.append(seq_refs[k][...])
                k += 1
        s_in = state[...]
        st_ref[...] = s_in
        out, s_new = fn(*vals, s_in, *[p[...] for p in par_refs])
        out_ref[...] = out
        state[...] = s_new
        car.wait_at(c == nc - 1, ex_ins, ex_outs, ex_sems)

    in_specs, operands = [], []
    for s in seqs:
        if s.kind == "halo":
            rb, w, cb = CHUNK // HALO, s.width, s.col_block
            in_specs.append(pl.BlockSpec((HALO, w), lambda i, rb=rb, cb=cb: (jnp.maximum(i * rb - 1, 0), cb)))
            operands.append(s.array)
        in_specs.append(pl.BlockSpec((CHUNK, s.width), lambda i, cb=s.col_block: (i, cb)))
        operands.append(s.array)
    for p in params:
        in_specs.append(_const_spec(p.shape))
        operands.append(p)
    outs = pl.pallas_call(
        body, grid=(nc,), in_specs=in_specs + car.in_specs,
        out_specs=[pl.BlockSpec((CHUNK, out_width), lambda i: (i, 0)),
                   pl.BlockSpec((None, state_rows, LANES), lambda i: (i, 0, 0))] + car.out_specs,
        out_shape=[jax.ShapeDtypeStruct((seq_len, out_width), f32),
                   jax.ShapeDtypeStruct((nc, state_rows, LANES), f32)] + car.out_shape,
        scratch_shapes=[pltpu.VMEM((state_rows, LANES), f32)] + car.scratch,
        compiler_params=_params(("arbitrary",)), name=name)(*operands, *car.operands)
    return outs[0], outs[1], outs[2:]


def mixer_bwd(name, fn, seqs, params, states, d_out, seq_len, exchange=None):
    nc = seq_len // CHUNK
    state_rows = states.shape[1]
    diff = [s for s in seqs if s.kind != "const"]
    halos = [s for s in diff if s.kind == "halo"]
    n_refs = sum(2 if s.kind == "halo" else 1 for s in seqs)
    n_par = len(params)
    car = Carried(exchange)

    def body(*refs):
        seq_refs, par_refs = refs[:n_refs], refs[n_refs:n_refs + n_par]
        st_ref, dout_ref = refs[n_refs + n_par:n_refs + n_par + 2]
        k0 = n_refs + n_par + 2
        ex_ins = refs[k0:k0 + car.n]
        k0 += car.n
        dseq_refs, dpar_refs = refs[k0:k0 + len(diff)], refs[k0 + len(diff):k0 + len(diff) + n_par]
        k0 += len(diff) + n_par
        ex_outs = refs[k0:k0 + car.n]
        scratch = refs[k0 + car.n:]
        d_state, carries, ex_sems = scratch[0], scratch[1:1 + len(halos)], scratch[1 + len(halos):]
        i = pl.program_id(0)
        c = nc - 1 - i
        car.start_at(i == 0, ex_ins, ex_outs, ex_sems)

        @pl.when(i == 0)
        def _():
            d_state[...] = jnp.zeros_like(d_state)
            for r in list(carries) + list(dpar_refs):
                r[...] = jnp.zeros_like(r)

        dvals, consts, k = [], [], 0
        for s in seqs:
            if s.kind == "halo":
                prev = jnp.where(c > 0, seq_refs[k][...], 0.0)
                dvals.append(jnp.concatenate([prev, seq_refs[k + 1][...]], axis=0))
                k += 2
            elif s.kind == "tile":
                dvals.append(seq_refs[k][...])
                k += 1
            else:
                consts.append(seq_refs[k][...])
                k += 1
        nd = len(dvals)

        def call(*a):
            it_d, it_c = iter(a[:nd]), iter(consts)
            vals = [next(it_c) if s.kind == "const" else next(it_d) for s in seqs]
            return fn(*vals, *a[nd:])

        _, vjp = jax.vjp(call, *dvals, st_ref[...], *[p[...] for p in par_refs])
        cts = vjp((dout_ref[...], d_state[...]))
        hk = 0
        for j, s in enumerate(diff):
            if s.kind == "halo":
                dseq_refs[j][...] = cts[j][HALO:, :]
                dseq_refs[j][CHUNK - HALO:, :] += carries[hk][...]
                carries[hk][...] = cts[j][:HALO, :]
                hk += 1
            else:
                dseq_refs[j][...] = cts[j]
        d_state[...] = cts[nd]
        for j in range(n_par):
            dpar_refs[j][...] += cts[nd + 1 + j]
        car.wait_at(i == nc - 1, ex_ins, ex_outs, ex_sems)

    in_specs, operands = [], []
    for s in seqs:
        if s.kind == "halo":
            rb, cb = CHUNK // HALO, s.col_block
            in_specs.append(pl.BlockSpec((HALO, s.width), lambda i, rb=rb, cb=cb: (jnp.maximum((nc - 1 - i) * rb - 1, 0), cb)))
            operands.append(s.array)
        in_specs.append(pl.BlockSpec((CHUNK, s.width), lambda i, cb=s.col_block: (nc - 1 - i, cb)))
        operands.append(s.array)
    for p in params:
        in_specs.append(_const_spec(p.shape))
        operands.append(p)
    in_specs.append(pl.BlockSpec((None, state_rows, LANES), lambda i: (nc - 1 - i, 0, 0)))
    in_specs.append(pl.BlockSpec((CHUNK, d_out.shape[1]), lambda i: (nc - 1 - i, 0)))
    operands += [states, d_out]
    outs = pl.pallas_call(
        body, grid=(nc,), in_specs=in_specs + car.in_specs,
        out_specs=[pl.BlockSpec((CHUNK, s.width), lambda i: (nc - 1 - i, 0)) for s in diff]
        + [_const_spec(p.shape) for p in params] + car.out_specs,
        out_shape=[jax.ShapeDtypeStruct((seq_len, s.width), f32) for s in diff]
        + [jax.ShapeDtypeStruct(p.shape, f32) for p in params] + car.out_shape,
        scratch_shapes=[pltpu.VMEM((state_rows, LANES), f32)] + [pltpu.VMEM((HALO, s.width), f32) for s in halos]
        + car.scratch,
        compiler_params=_params(("arbitrary",)), name=name)(*operands, *car.operands)
    nd = len(diff)
    return outs[:nd], outs[nd:nd + n_par], outs[nd + n_par:]


TOK = 256


def inproj_fwd(name, x, nw, w):
    seq_len, n = x.shape[0], w.shape[1]

    def body(x_ref, nw_ref, w_ref, o_ref):
        o_ref[...] = mm(rmsnorm_f(x_ref[...], nw_ref[...]), w_ref[...])

    return pl.pallas_call(
        body, grid=(seq_len // TOK,),
        in_specs=[_row_spec(TOK, D_MODEL), _const_spec(nw.shape), _resident_spec(w.shape)],
        out_specs=_row_spec(TOK, n), out_shape=jax.ShapeDtypeStruct((seq_len, n), f32),
        compiler_params=_params(("arbitrary",)), name=name)(x, nw, w)


def inproj_bwd(name, x, nw, w, pieces, d_res):
    seq_len, n = x.shape[0], w.shape[1]
    widths = [p.shape[1] for p in pieces]
    assert sum(widths) == n
    k = len(pieces)

    def body(*refs):
        x_ref, nw_ref, w_ref = refs[:3]
        p_refs, dres_ref = refs[3:3 + k], refs[3 + k]
        dx_ref, dw_ref, dnw_ref = refs[4 + k:]

        @pl.when(pl.program_id(0) == 0)
        def _():
            dw_ref[...] = jnp.zeros_like(dw_ref)
            dnw_ref[...] = jnp.zeros_like(dnw_ref)

        h, vjp = jax.vjp(rmsnorm_f, x_ref[...], nw_ref[...])
        dh, off = jnp.zeros_like(h), 0
        for p_ref, wd in zip(p_refs, widths):
            g = p_ref[...]
            dh = dh + mm_nt(g, w_ref[:, off:off + wd])
            dw_ref[:, off:off + wd] += mm_tn(h, g)
            off += wd
        dx, dnw = vjp(dh)
        dx_ref[...] = dres_ref[...] + dx
        dnw_ref[...] += dnw

    return pl.pallas_call(
        body, grid=(seq_len // TOK,),
        in_specs=[_row_spec(TOK, D_MODEL), _const_spec(nw.shape), _resident_spec(w.shape)]
        + [_row_spec(TOK, wd) for wd in widths] + [_row_spec(TOK, D_MODEL)],
        out_specs=[_row_spec(TOK, D_MODEL), _resident_spec((D_MODEL, n)), _const_spec(nw.shape)],
        out_shape=[jax.ShapeDtypeStruct((seq_len, D_MODEL), f32), jax.ShapeDtypeStruct((D_MODEL, n), f32),
                   jax.ShapeDtypeStruct(nw.shape, f32)],
        compiler_params=_params(("arbitrary",)), name=name)(x, nw, w, *pieces, d_res)


def outproj_fwd(name, x, a, b, w):
    seq_len, wa, wb = x.shape[0], a.shape[1], b.shape[1]

    def body(x_ref, a_ref, b_ref, w_ref, o_ref):
        o_ref[...] = x_ref[...] + mm(a_ref[...], w_ref[:wa, :]) + mm(b_ref[...], w_ref[wa:, :])

    return pl.pallas_call(
        body, grid=(seq_len // TOK,),
        in_specs=[_row_spec(TOK, D_MODEL), _row_spec(TOK, wa), _row_spec(TOK, wb), _resident_spec(w.shape)],
        out_specs=_row_spec(TOK, D_MODEL), out_shape=jax.ShapeDtypeStruct((seq_len, D_MODEL), f32),
        compiler_params=_params(("arbitrary",)), name=name)(x, a, b, w)


def outproj_bwd(name, dy, a, b, w):
    seq_len, wa, wb = dy.shape[0], a.shape[1], b.shape[1]

    def body(dy_ref, a_ref, b_ref, w_ref, da_ref, db_ref, dw_ref):
        @pl.when(pl.program_id(0) == 0)
        def _():
            dw_ref[...] = jnp.zeros_like(dw_ref)

        g = dy_ref[...]
        da_ref[...] = mm_nt(g, w_ref[:wa, :])
        db_ref[...] = mm_nt(g, w_ref[wa:, :])
        dw_ref[:wa, :] += mm_tn(a_ref[...], g)
        dw_ref[wa:, :] += mm_tn(b_ref[...], g)

    return pl.pallas_call(
        body, grid=(seq_len // TOK,),
        in_specs=[_row_spec(TOK, D_MODEL), _row_spec(TOK, wa), _row_spec(TOK, wb), _resident_spec(w.shape)],
        out_specs=[_row_spec(TOK, wa), _row_spec(TOK, wb), _resident_spec(w.shape)],
        out_shape=[jax.ShapeDtypeStruct((seq_len, wa), f32), jax.ShapeDtypeStruct((seq_len, wb), f32),
                   jax.ShapeDtypeStruct(w.shape, f32)],
        compiler_params=_params(("arbitrary",)), name=name)(dy, a, b, w)


FF_BLOCK = D_FF // N_DEV


def mlp_fwd(name, x, nw, w_up, w_down, exchange=None):
    seq_len = x.shape[0]
    nt = seq_len // TOK
    car = Carried(exchange)

    def body(*refs):
        x_ref, nw_ref, up_ref, down_ref = refs[:4]
        ex_ins, o_ref = refs[4:4 + car.n], refs[4 + car.n]
        ex_outs, ex_sems = refs[5 + car.n:5 + 2 * car.n], refs[5 + 2 * car.n:]
        i = pl.program_id(0)
        car.start_at(i == 0, ex_ins, ex_outs, ex_sems)
        xv = x_ref[...]
        h = rmsnorm_f(xv, nw_ref[...])
        acc = xv
        for d in range(N_DEV):
            r = jnp.maximum(mm(h, up_ref[d]), 0.0)
            acc = acc + mm(r * r, down_ref[d])
        o_ref[...] = acc
        car.wait_at(i == nt - 1, ex_ins, ex_outs, ex_sems)

    outs = pl.pallas_call(
        body, grid=(nt,),
        in_specs=[_row_spec(TOK, D_MODEL), _const_spec(nw.shape), _resident_spec(w_up.shape), _resident_spec(w_down.shape)]
        + car.in_specs,
        out_specs=[_row_spec(TOK, D_MODEL)] + car.out_specs,
        out_shape=[jax.ShapeDtypeStruct((seq_len, D_MODEL), f32)] + car.out_shape, scratch_shapes=car.scratch,
        compiler_params=_params(("arbitrary",)), name=name)(x, nw, w_up, w_down, *car.operands)
    return outs[0], outs[1:]


MLP_SPLIT = 2


def mlp_bwd(name, x, nw, w_up, w_down, dy, exchange=None):
    seq_len = x.shape[0]
    nt = seq_len // TOK
    per = N_DEV // MLP_SPLIT
    car = Carried(exchange)

    def body(*refs):
        x_ref, nw_ref, up_ref, down_ref, dy_ref = refs[:5]
        ex_ins, (dh_ref, dup_ref, ddown_ref) = refs[5:5 + car.n], refs[5 + car.n:8 + car.n]
        ex_outs, ex_sems = refs[8 + car.n:8 + 2 * car.n], refs[8 + 2 * car.n:]
        j, i = pl.program_id(0), pl.program_id(1)
        car.start_at(jnp.logical_and(j == 0, i == 0), ex_ins, ex_outs, ex_sems)

        @pl.when(i == 0)
        def _():
            dup_ref[...] = jnp.zeros_like(dup_ref)
            ddown_ref[...] = jnp.zeros_like(ddown_ref)

        h = rmsnorm_f(x_ref[...], nw_ref[...])
        g = dy_ref[...]
        dh = jnp.zeros_like(h)
        for d in range(per):
            r = jnp.maximum(mm(h, up_ref[d]), 0.0)
            da = mm_nt(g, down_ref[d]) * (2.0 * r)
            ddown_ref[d] += mm_tn(r * r, g)
            dup_ref[d] += mm_tn(h, da)
            dh = dh + mm_nt(da, up_ref[d])
        dh_ref[...] = dh
        car.wait_at(jnp.logical_and(j == MLP_SPLIT - 1, i == nt - 1), ex_ins, ex_outs, ex_sems)

    outs = pl.pallas_call(
        body, grid=(MLP_SPLIT, nt),
        in_specs=[pl.BlockSpec((TOK, D_MODEL), lambda j, i: (i, 0)), pl.BlockSpec(nw.shape, lambda j, i: (0, 0)),
                  pl.BlockSpec((per, D_MODEL, FF_BLOCK), lambda j, i: (j, 0, 0), pipeline_mode=pl.Buffered(1)),
                  pl.BlockSpec((per, FF_BLOCK, D_MODEL), lambda j, i: (j, 0, 0), pipeline_mode=pl.Buffered(1)),
                  pl.BlockSpec((TOK, D_MODEL), lambda j, i: (i, 0))] + car.in_specs,
        out_specs=[pl.BlockSpec((None, TOK, D_MODEL), lambda j, i: (j, i, 0)),
                   pl.BlockSpec((per, D_MODEL, FF_BLOCK), lambda j, i: (j, 0, 0), pipeline_mode=pl.Buffered(1)),
                   pl.BlockSpec((per, FF_BLOCK, D_MODEL), lambda j, i: (j, 0, 0), pipeline_mode=pl.Buffered(1))]
        + car.out_specs,
        out_shape=[jax.ShapeDtypeStruct((MLP_SPLIT, seq_len, D_MODEL), f32),
                   jax.ShapeDtypeStruct(w_up.shape, f32), jax.ShapeDtypeStruct(w_down.shape, f32)] + car.out_shape,
        scratch_shapes=car.scratch,
        compiler_params=_params(("arbitrary", "arbitrary")), name=name)(x, nw, w_up, w_down, dy, *car.operands)
    dh_parts, d_up, d_down, ex_results = outs[0], outs[1], outs[2], outs[3:]

    def norm_body(x_ref, nw_ref, dh_ref, dy_ref, dx_ref, dnw_ref):
        @pl.when(pl.program_id(0) == 0)
        def _():
            dnw_ref[...] = jnp.zeros_like(dnw_ref)

        _, vjp = jax.vjp(rmsnorm_f, x_ref[...], nw_ref[...])
        dh = dh_ref[0]
        for j in range(1, MLP_SPLIT):
            dh = dh + dh_ref[j]
        dx, dnw = vjp(dh)
        dx_ref[...] = dy_ref[...] + dx
        dnw_ref[...] += dnw

    dx, dnw = pl.pallas_call(
        norm_body, grid=(nt,),
        in_specs=[_row_spec(TOK, D_MODEL), _const_spec(nw.shape),
                  pl.BlockSpec((MLP_SPLIT, TOK, D_MODEL), lambda i: (0, i, 0)), _row_spec(TOK, D_MODEL)],
        out_specs=[_row_spec(TOK, D_MODEL), _const_spec(nw.shape)],
        out_shape=[jax.ShapeDtypeStruct((seq_len, D_MODEL), f32), jax.ShapeDtypeStruct(nw.shape, f32)],
        compiler_params=_params(("arbitrary",)), name=name + "_norm")(x, nw, dh_parts, dy)
    return dx, d_up, d_down, dnw, ex_results


def final_loss(name, x, nw, target):
    seq_len = x.shape[0]

    def body(x_ref, nw_ref, t_ref, loss_ref, dx_ref, dnw_ref):
        @pl.when(pl.program_id(0) == 0)
        def _():
            loss_ref[...] = jnp.zeros_like(loss_ref)
            dnw_ref[...] = jnp.zeros_like(dnw_ref)

        y, vjp = jax.vjp(rmsnorm_f, x_ref[...], nw_ref[...])
        err = y - t_ref[...]
        loss_ref[...] += 0.5 * jnp.sum(jnp.mean(err * err, axis=-1, keepdims=True), axis=0, keepdims=True)
        dx, dnw = vjp(err * (1.0 / D_MODEL))
        dx_ref[...] = dx
        dnw_ref[...] += dnw

    return pl.pallas_call(
        body, grid=(seq_len // TOK,),
        in_specs=[_row_spec(TOK, D_MODEL), _const_spec(nw.shape), _row_spec(TOK, D_MODEL)],
        out_specs=[_const_spec((8, LANES)), _row_spec(TOK, D_MODEL), _const_spec(nw.shape)],
        out_shape=[jax.ShapeDtypeStruct((8, LANES), f32), jax.ShapeDtypeStruct((seq_len, D_MODEL), f32),
                   jax.ShapeDtypeStruct(nw.shape, f32)],
        compiler_params=_params(("arbitrary",)), name=name)(x, nw, target)


def _whole(a):
    return pl.BlockSpec(a.shape, lambda: (0,) * len(a.shape))


def s5_prep_fwd(name, raw):
    def body(*refs):
        outs = s5_prep(*[r[...] for r in refs[:7]])
        for o_ref, o in zip(refs[7:], outs):
            o_ref[...] = o

    shapes = [(1, S5_W)] * 2 + [(S5_CH, S5_W)] * 4
    return pl.pallas_call(
        body, in_specs=[_whole(a) for a in raw], out_specs=[pl.BlockSpec(s, lambda s=s: (0,) * len(s)) for s in shapes],
        out_shape=[jax.ShapeDtypeStruct(s, f32) for s in shapes],
        compiler_params=pltpu.CompilerParams(vmem_limit_bytes=VMEM_LIMIT), name=name)(*raw)


def s5_prep_bwd(name, raw, cts):
    def body(*refs):
        _, vjp = jax.vjp(s5_prep, *[r[...] for r in refs[:7]])
        grads = vjp(tuple(r[...] for r in refs[7:13]))
        for o_ref, g in zip(refs[13:], grads):
            o_ref[...] = g

    return pl.pallas_call(
        body, in_specs=[_whole(a) for a in list(raw) + list(cts)], out_specs=[_whole(a) for a in raw],
        out_shape=[jax.ShapeDtypeStruct(a.shape, f32) for a in raw],
        compiler_params=pltpu.CompilerParams(vmem_limit_bytes=VMEM_LIMIT), name=name)(*raw, *cts)


def s5_fwd(name, proj, u_block, lam_re, lam_im, bblk_re, bblk_im, cblk_re, cblk_im, d_skip, w_glu, b_glu):
    seq_len = proj.shape[0]

    def body(u_ref, lr_ref, li_ref, br_ref, bi_ref, cr_ref, ci_ref, d_ref, wg_ref, bg_ref,
             o_ref, hr_ref, hi_ref, carry_re, carry_im):
        @pl.when(pl.program_id(0) == 0)
        def _():
            carry_re[...] = jnp.zeros_like(carry_re)
            carry_im[...] = jnp.zeros_like(carry_im)

        u = u_ref[...]
        hr_ref[...] = mm(u, br_ref[...])
        hi_ref[...] = mm(u, bi_ref[...])
        lr, li = lr_ref[...], li_ref[...]

        def step(t, h):
            pr, pi = h
            row = pl.ds(t, 1)
            nr = lr * pr - li * pi + hr_ref[row, :]
            ni = lr * pi + li * pr + hi_ref[row, :]
            hr_ref[row, :] = nr
            hi_ref[row, :] = ni
            return nr, ni

        er, ei = lax.fori_loop(0, TOK, step, (carry_re[0:1, :], carry_im[0:1, :]))
        carry_re[0:1, :] = er
        carry_im[0:1, :] = ei
        o_ref[...] = s5_out_fn(hr_ref[...], hi_ref[...], u, cr_ref[...], ci_ref[...], d_ref[...],
                               wg_ref[...].astype(f32), bg_ref[...])

    consts = [lam_re, lam_im, bblk_re, bblk_im, cblk_re, cblk_im, d_skip, w_glu, b_glu]
    return pl.pallas_call(
        body, grid=(seq_len // TOK,),
        in_specs=[_row_spec(TOK, S5_CH, u_block)] + [_const_spec(a.shape) for a in consts],
        out_specs=[_row_spec(TOK, S5_CH), _row_spec(TOK, S5_W), _row_spec(TOK, S5_W)],
        out_shape=[jax.ShapeDtypeStruct((seq_len, S5_CH), f32), jax.ShapeDtypeStruct((seq_len, S5_W), f32),
                   jax.ShapeDtypeStruct((seq_len, S5_W), f32)],
        scratch_shapes=[pltpu.VMEM((HALO, S5_W), f32), pltpu.VMEM((HALO, S5_W), f32)],
        compiler_params=_params(("arbitrary",)), name=name)(proj, *consts)


def s5_bwd(name, proj, u_block, h_re, h_im, d_out, lam_re, lam_im, bblk_re, bblk_im, cblk_re, cblk_im, d_skip, w_glu, b_glu):
    seq_len = proj.shape[0]
    nt = seq_len // TOK
    consts = [lam_re, lam_im, bblk_re, bblk_im, cblk_re, cblk_im, d_skip, w_glu, b_glu]

    def body(u_ref, hr_ref, hi_ref, pr_ref, pi_ref, dout_ref, lr_ref, li_ref, br_ref, bi_ref, cr_ref, ci_ref, d_ref, wg_ref, bg_ref,
             du_ref, dlr_ref, dli_ref, dbr_ref, dbi_ref, dcr_ref, dci_ref, dd_ref, dwg_ref, dbg_ref,
             gr_ref, gi_ref, carry_re, carry_im):
        i = pl.program_id(0)
        tile = nt - 1 - i

        @pl.when(i == 0)
        def _():
            for r in (carry_re, carry_im, dlr_ref, dli_ref, dbr_ref, dbi_ref, dcr_ref, dci_ref, dd_ref, dwg_ref, dbg_ref):
                r[...] = jnp.zeros_like(r)

        u, h_r, h_i = u_ref[...], hr_ref[...], hi_ref[...]
        _, vjp = jax.vjp(s5_out_fn, h_r, h_i, u, cr_ref[...], ci_ref[...], d_ref[...], wg_ref[...].astype(f32), bg_ref[...])
        ghr, ghi, du, dcr, dci, dd, dwg, dbg = vjp(dout_ref[...])
        gr_ref[...] = ghr
        gi_ref[...] = ghi
        lr, li = lr_ref[...], li_ref[...]

        def step(k, g):
            nr, ni = g
            row = pl.ds(TOK - 1 - k, 1)
            cr = gr_ref[row, :] + lr * nr + li * ni
            ci = gi_ref[row, :] - li * nr + lr * ni
            gr_ref[row, :] = cr
            gi_ref[row, :] = ci
            return cr, ci

        er, ei = lax.fori_loop(0, TOK, step, (carry_re[0:1, :], carry_im[0:1, :]))
        carry_re[0:1, :] = er
        carry_im[0:1, :] = ei
        g_r, g_i = gr_ref[...], gi_ref[...]
        first = _iota((TOK, 1), 0) == 0
        keep = jnp.where(tile > 0, 1.0, 0.0)
        p_r = jnp.where(first, pr_ref[HALO - 1:HALO, :] * keep, _roll(h_r, 1, 0))
        p_i = jnp.where(first, pi_ref[HALO - 1:HALO, :] * keep, _roll(h_i, 1, 0))
        dlr_ref[...] += jnp.sum(g_r * p_r + g_i * p_i, axis=0, keepdims=True)
        dli_ref[...] += jnp.sum(g_i * p_r - g_r * p_i, axis=0, keepdims=True)
        du_ref[...] = du + mm_nt(g_r, br_ref[...]) + mm_nt(g_i, bi_ref[...])
        dbr_ref[...] += mm_tn(u, g_r)
        dbi_ref[...] += mm_tn(u, g_i)
        dcr_ref[...] += dcr
        dci_ref[...] += dci
        dd_ref[...] += dd
        dwg_ref[...] += dwg
        dbg_ref[...] += dbg

    rev = lambda cols, cb=0: pl.BlockSpec((TOK, cols), lambda i, cb=cb: (nt - 1 - i, cb))
    prev = pl.BlockSpec((HALO, S5_W), lambda i: (jnp.maximum((nt - 1 - i) * (TOK // HALO) - 1, 0), 0))
    outs = pl.pallas_call(
        body, grid=(nt,),
        in_specs=[rev(S5_CH, u_block), rev(S5_W), rev(S5_W), prev, prev, rev(S5_CH)] + [_const_spec(a.shape) for a in consts],
        out_specs=[rev(S5_CH)] + [_const_spec(a.shape) for a in consts],
        out_shape=[jax.ShapeDtypeStruct((seq_len, S5_CH), f32)] + [jax.ShapeDtypeStruct(a.shape, f32) for a in consts],
        scratch_shapes=[pltpu.VMEM((TOK, S5_W), f32), pltpu.VMEM((TOK, S5_W), f32),
                        pltpu.VMEM((HALO, S5_W), f32), pltpu.VMEM((HALO, S5_W), f32)],
        compiler_params=_params(("arbitrary",)), name=name)(proj, h_re, h_im, h_re, h_im, d_out, *consts)
    return outs[0], outs[1:]


ANY = pl.BlockSpec(memory_space=pl.ANY)


def _mesh_position():
    x, y, c = lax.axis_index("x"), lax.axis_index("y"), lax.axis_index("c")
    return x, y, c, 4 * x + 2 * y + c


def _peer(x, y, c, r):
    px = 1 - x if r & 4 else x
    py = 1 - y if r & 2 else y
    pc = 1 - c if r & 1 else c
    return (px, py, pc), 4 * px + 2 * py + pc


class Exchange:
    def __init__(self, arrays, gather):
        self.arrays, self.gather, self.n = list(arrays), gather, len(arrays)
        self.in_specs = [ANY] * self.n
        self.out_specs = [ANY] * self.n
        shapes = [((N_DEV,) + a.shape) if gather else a.shape for a in self.arrays]
        self.out_shape = [jax.ShapeDtypeStruct(s, a.dtype) for s, a in zip(shapes, self.arrays)]
        self.scratch = [pltpu.SemaphoreType.DMA((self.n, N_DEV - 1)), pltpu.SemaphoreType.DMA((self.n, N_DEV - 1)),
                        pltpu.SemaphoreType.DMA((self.n,))]

    def _copies(self, ins, outs, sems, landed):
        send_sems, recv_sems, local_sems = sems
        x, y, c, me = _mesh_position()
        local, remote = [], []
        for i in range(self.n):
            mine = ins[i] if self.gather else ins[i].at[me]
            local.append(pltpu.make_async_copy(mine, outs[i].at[me], local_sems.at[i]))
            for r in range(1, N_DEV):
                peer, peer_idx = _peer(x, y, c, r)
                remote.append(pltpu.make_async_remote_copy(
                    src_ref=ins[i] if self.gather else ins[i].at[peer_idx],
                    dst_ref=outs[i].at[peer_idx if landed else me],
                    send_sem=send_sems.at[i, r - 1], recv_sem=recv_sems.at[i, r - 1],
                    device_id=peer, device_id_type=pl.DeviceIdType.MESH))
        return local, remote

    def start(self, ins, outs, sems):
        local, remote = self._copies(ins, outs, sems, landed=False)
        for cp in local + remote:
            cp.start()

    def wait(self, ins, outs, sems):
        local, remote = self._copies(ins, outs, sems, landed=True)
        for cp in remote:
            cp.wait_recv()
            cp.wait_send()
        for cp in local:
            cp.wait()

    def run(self, name):
        n = self.n

        def body(*refs):
            ins, outs, sems = refs[:n], refs[n:2 * n], refs[2 * n:]
            self.start(ins, outs, sems)
            self.wait(ins, outs, sems)

        return pl.pallas_call(body, in_specs=self.in_specs, out_specs=self.out_specs, out_shape=self.out_shape,
                              scratch_shapes=self.scratch, name=name)(*self.arrays)


class Carried:
    def __init__(self, exchange):
        self.ex = exchange
        self.n = exchange.n if exchange else 0
        self.in_specs = exchange.in_specs if exchange else []
        self.out_specs = exchange.out_specs if exchange else []
        self.out_shape = exchange.out_shape if exchange else []
        self.scratch = exchange.scratch if exchange else []
        self.operands = exchange.arrays if exchange else []

    def start_at(self, first, ins, outs, sems):
        if self.ex is not None:
            @pl.when(first)
            def _():
                self.ex.start(ins, outs, sems)

    def wait_at(self, last, ins, outs, sems):
        if self.ex is not None:
            @pl.when(last)
            def _():
                self.ex.wait(ins, outs, sems)


def adamw(name, parts, w, m, v):
    rows, cols = w.shape
    tr = rows
    for cand in (512, 256, 128, 64, 32, 16, 8):
        if rows * cols * 4 > (1 << 20) and rows % cand == 0 and cand * cols * 4 <= (1 << 20):
            tr = cand
            break

    def body(p_ref, w_ref, m_ref, v_ref, g_ref, d_ref, nm_ref, nv_ref):
        g = p_ref[0]
        for s in range(1, N_DEV):
            g = g + p_ref[s]
        nm = ADAM_B1 * m_ref[...] + (1.0 - ADAM_B1) * g
        nv = ADAM_B2 * v_ref[...] + (1.0 - ADAM_B2) * (g * g)
        m_hat = nm / (1.0 - ADAM_B1 ** ADAM_STEP)
        v_hat = nv / (1.0 - ADAM_B2 ** ADAM_STEP)
        g_ref[...] = g
        d_ref[...] = -ADAM_LR * (m_hat / (jnp.sqrt(v_hat) + ADAM_EPS) + ADAM_WD * w_ref[...])
        nm_ref[...] = nm
        nv_ref[...] = nv

    blk = pl.BlockSpec((tr, cols), lambda i: (i, 0))
    return pl.pallas_call(
        body, grid=(rows // tr,),
        in_specs=[pl.BlockSpec((N_DEV, tr, cols), lambda i: (0, i, 0)), blk, blk, blk],
        out_specs=[blk] * 4, out_shape=[jax.ShapeDtypeStruct((rows, cols), f32)] * 4,
        compiler_params=_params(("arbitrary",)), name=name)(parts, w, m, v)


WEIGHTS = ['l0_norm_mix', 'l0_w_in', 'ssd_conv_w', 'ssd_conv_b', 'ssd_dt_bias', 'ssd_A_log', 'ssd_D', 'ssd_norm_w',
           'l0_w_out', 'l0_norm_mlp', 'l0_w_up', 'l0_w_down', 'l1_norm_mix', 'l1_w_in', 'gdn_conv_w', 'gdn_A_log',
           'gdn_dt_bias', 'gdn_norm_w', 's5_A_re', 's5_A_im', 's5_log_step', 's5_B_re', 's5_B_im', 's5_C_re', 's5_C_im',
           's5_D', 's5_w_glu', 's5_b_glu', 'l1_w_out', 'l1_norm_mlp', 'l1_w_up', 'l1_w_down', 'final_norm']
SHARDED = ['l0_w_in', 'l0_w_out', 'l0_w_up', 'l0_w_down', 'l1_w_in', 's5_w_glu', 'l1_w_out', 'l1_w_up', 'l1_w_down',
           'ssd_conv_w', 'gdn_conv_w']
F32_GATHER = ('ssd_conv_w', 'gdn_conv_w')
REPLICATED = [n for n in WEIGHTS if n not in SHARDED]
INPUTS = ['x'] + WEIGHTS + ['loss_target'] + ['m_' + n for n in WEIGHTS] + ['v_' + n for n in WEIGHTS]


def _row(v):
    return v.reshape(1, -1)


def _pad_lanes(v, offset=0):
    return jnp.pad(v, (offset, LANES - offset - v.shape[0])).reshape(1, LANES)


def _cols_to_blocks(g):
    return g.reshape(g.shape[0], N_DEV, -1).transpose(1, 0, 2)


def _blocks_to_cols(g):
    return g.transpose(1, 0, 2).reshape(g.shape[1], -1)


def _pack(arrays):
    parts, slots, at = [], [], 0
    for a in arrays:
        n = a.size
        rows = -(-n // (8 * LANES)) * 8
        parts.append(jnp.pad(a.reshape(-1), (0, rows * LANES - n)).reshape(rows, LANES))
        slots.append((at, rows, n, a.shape))
        at += rows
    return jnp.concatenate(parts, axis=0), slots


def _unpack(buf, slots):
    return [buf[at:at + rows].reshape(-1)[:n].reshape(shape) for at, rows, n, shape in slots]


def kernel(*args):
    a = dict(zip(INPUTS, args, strict=True))
    seq_len = a['x'].shape[1]
    x0 = a['x'].reshape(seq_len, D_MODEL)
    target = a['loss_target'].reshape(seq_len, D_MODEL)

    shard = {n: a[n] if n in F32_GATHER else a[n].astype(_MXU) for n in SHARDED}
    first = ['l0_w_in', 'l0_w_out', 's5_w_glu', 'ssd_conv_w', 'gdn_conv_w']
    g = dict(zip(first, Exchange([shard[n] for n in first], gather=True).run("gather_first")))
    w_nat = _blocks_to_cols(g['l0_w_in'])
    win0 = jnp.concatenate([w_nat[:, :2048], w_nat[:, 2560:3584], w_nat[:, 2048:2560], w_nat[:, 3584:3592],
                            jnp.zeros((D_MODEL, IN0_PAD - IN0_W), _MXU)], axis=1)
    wout0 = g['l0_w_out'].reshape(D_MODEL, D_MODEL)
    wglu = g['s5_w_glu'].reshape(S5_CH, S5_CH)
    ssd_cw, gdn_cw = _blocks_to_cols(g['ssd_conv_w']), _blocks_to_cols(g['gdn_conv_w'])

    half = RET_D // 2
    inv = ROPE_THETA ** (-jnp.arange(half, dtype=f32) / half)
    ang = jnp.arange(seq_len, dtype=f32)[:, None] * inv[None, :]
    cos, sin = jnp.cos(ang), jnp.sin(ang)
    cos, sin = jnp.concatenate([cos, cos], axis=1), jnp.concatenate([-sin, sin], axis=1)
    ssd_params = [ssd_cw, _row(a['ssd_conv_b']), _pad_lanes(a['ssd_dt_bias']), _pad_lanes(a['ssd_A_log']),
                  _pad_lanes(a['ssd_D']), _row(a['ssd_norm_w'])]
    gdn_params = [gdn_cw, _pad_lanes(a['gdn_A_log'], GDN_GCOL), _pad_lanes(a['gdn_dt_bias'], GDN_GCOL), _row(a['gdn_norm_w'])]
    s5_raw = [a['s5_A_re'].reshape(1, S5_W), a['s5_A_im'].reshape(1, S5_W), _pad_lanes(a['s5_log_step']),
              a['s5_B_re'].transpose(2, 0, 1).reshape(S5_GROUP, S5_W), a['s5_B_im'].transpose(2, 0, 1).reshape(S5_GROUP, S5_W),
              a['s5_C_re'].transpose(1, 0, 2).reshape(S5_GROUP, S5_W), a['s5_C_im'].transpose(1, 0, 2).reshape(S5_GROUP, S5_W)]
    s5_d, s5_bg = _row(a['s5_D']), _row(a['s5_b_glu'])
    nw = {n: _row(a[n]) for n in ('l0_norm_mix', 'l0_norm_mlp', 'l1_norm_mix', 'l1_norm_mlp', 'final_norm')}

    proj0 = inproj_fwd("l0_in", x0, nw['l0_norm_mix'], win0)
    ret_seqs = [Seq(proj0, 512, 0), Seq(proj0, 512, 1), Seq(proj0, 512, 2), Seq(proj0, 512, 3),
                Seq(cos, LANES, 0, "const"), Seq(sin, LANES, 0, "const")]
    later = ['l0_w_up', 'l0_w_down']
    ret_out, ret_st, got = mixer_fwd("ret_fwd", ret_chunk, ret_seqs, [], 512, RET_HEADS * RET_D, seq_len,
                                     exchange=Exchange([shard[n] for n in later], gather=True))
    g.update(zip(later, got))
    ssd_seqs = [Seq(proj0, 512, 6), Seq(proj0, 1024, 2, "halo"), Seq(proj0, LANES, 28)]
    ssd_out, ssd_st, _ = mixer_fwd("ssd_fwd", ssd_chunk, ssd_seqs, ssd_params, SSD_INNER, SSD_INNER, seq_len)
    x1 = outproj_fwd("l0_out", x0, ret_out, ssd_out, wout0)
    later = ['l1_w_in', 'l1_w_out', 'l1_w_up', 'l1_w_down']
    x2, got = mlp_fwd("l0_mlp", x1, nw['l0_norm_mlp'], g['l0_w_up'], g['l0_w_down'],
                      exchange=Exchange([shard[n] for n in later], gather=True))
    g.update(zip(later, got))
    w_nat = g['l1_w_in'].reshape(D_MODEL, IN1_W)
    win1 = jnp.concatenate([w_nat[:, :3072], w_nat[:, 3084:3340], w_nat[:, 3072:3084],
                            jnp.zeros((D_MODEL, IN1_PAD - IN1_W), _MXU)], axis=1)
    wout1 = g['l1_w_out'].reshape(D_MODEL, D_MODEL)
    proj1 = inproj_fwd("l1_in", x2, nw['l1_norm_mix'], win1)
    gdn_seqs = [Seq(proj1, 3 * GDN_W, 0, "halo"), Seq(proj1, GDN_W, 3), Seq(proj1, LANES, 26)]
    gdn_out, gdn_st, _ = mixer_fwd("gdn_fwd", gdn_chunk, gdn_seqs, gdn_params, GDN_W, GDN_W, seq_len)
    prep = s5_prep_fwd("s5_prep", s5_raw)
    s5_out, h_re, h_im = s5_fwd("s5_fwd", proj1, 12, *prep, s5_d, wglu, s5_bg)
    x3 = outproj_fwd("l1_out", x2, gdn_out, s5_out, wout1)
    x4, _ = mlp_fwd("l1_mlp", x3, nw['l1_norm_mlp'], g['l1_w_up'], g['l1_w_down'])
    loss_blk, dx4, d_final = final_loss("final_loss", x4, nw['final_norm'], target)

    parts = {}
    dx3, d_up1, d_down1, d_nmlp1, _ = mlp_bwd("l1_mlp_bwd", x3, nw['l1_norm_mlp'], g['l1_w_up'], g['l1_w_down'], dx4)
    d_gdn, d_s5, d_wout1 = outproj_bwd("l1_out_bwd", dx3, gdn_out, s5_out, wout1)
    d_u, s5_g = s5_bwd("s5_bwd", proj1, 12, h_re, h_im, d_s5, *prep, s5_d, wglu, s5_bg)
    s5_raw_g = s5_prep_bwd("s5_prep_bwd", s5_raw, s5_g[:6])
    ready = {'l1_w_up': d_up1, 'l1_w_down': d_down1, 'l1_w_out': d_wout1.reshape(N_DEV, -1, D_MODEL),
             's5_w_glu': s5_g[7].reshape(N_DEV, -1, S5_CH)}
    (d_qkv, d_z1, d_ba), gdn_pg, got = mixer_bwd("gdn_bwd", gdn_chunk, gdn_seqs, gdn_params, gdn_st, d_gdn, seq_len,
                                                 exchange=Exchange(list(ready.values()), gather=False))
    parts.update(zip(ready, got))
    dx2, d_win1, d_nmix1 = inproj_bwd("l1_in_bwd", x2, nw['l1_norm_mix'], win1, [d_qkv, d_z1, d_u, d_ba], dx3)
    d_win1 = jnp.concatenate([d_win1[:, :3072], d_win1[:, 3328:3340], d_win1[:, 3072:3328]], axis=1)
    ready = {'l1_w_in': d_win1.reshape(N_DEV, -1, IN1_W), 'gdn_conv_w': _cols_to_blocks(gdn_pg[0])}
    dx1, d_up0, d_down0, d_nmlp0, got = mlp_bwd("l0_mlp_bwd", x1, nw['l0_norm_mlp'], g['l0_w_up'], g['l0_w_down'], dx2,
                                                exchange=Exchange(list(ready.values()), gather=False))
    parts.update(zip(ready, got))
    d_ret, d_ssd, d_wout0 = outproj_bwd("l0_out_bwd", dx1, ret_out, ssd_out, wout0)
    d_qkvg, _, _ = mixer_bwd("ret_bwd", ret_chunk, ret_seqs, [], ret_st, d_ret, seq_len)
    ready = {'l0_w_up': d_up0, 'l0_w_down': d_down0, 'l0_w_out': d_wout0.reshape(N_DEV, -1, D_MODEL)}
    (d_z0, d_xbc, d_dt), ssd_pg, got = mixer_bwd("ssd_bwd", ssd_chunk, ssd_seqs, ssd_params, ssd_st, d_ssd, seq_len,
                                                 exchange=Exchange(list(ready.values()), gather=False))
    parts.update(zip(ready, got))
    dx0, d_win0, d_nmix0 = inproj_bwd("l0_in_bwd", x0, nw['l0_norm_mix'], win0, list(d_qkvg) + [d_xbc, d_z0, d_dt], dx1)

    d_win0 = jnp.concatenate([d_win0[:, :2048], d_win0[:, 3072:3584], d_win0[:, 2048:3072], d_win0[:, 3584:3592]], axis=1)
    ready = {'l0_w_in': _cols_to_blocks(d_win0), 'ssd_conv_w': _cols_to_blocks(ssd_pg[0])}
    from_b = lambda t: t.reshape(S5_GROUP, S5_GROUPS, S5_STATE).transpose(1, 2, 0)
    from_c = lambda t: t.reshape(S5_GROUP, S5_GROUPS, S5_STATE).transpose(1, 0, 2)
    replicated_g = {
        'l0_norm_mix': d_nmix0, 'ssd_conv_b': ssd_pg[1], 'ssd_dt_bias': ssd_pg[2][0, :SSD_HEADS], 'ssd_A_log': ssd_pg[3][0, :SSD_HEADS],
        'ssd_D': ssd_pg[4][0, :SSD_HEADS], 'ssd_norm_w': ssd_pg[5], 'l0_norm_mlp': d_nmlp0, 'l1_norm_mix': d_nmix1,
        'gdn_A_log': gdn_pg[1][0, GDN_GCOL:GDN_GCOL + GDN_HEADS], 'gdn_dt_bias': gdn_pg[2][0, GDN_GCOL:GDN_GCOL + GDN_HEADS],
        'gdn_norm_w': gdn_pg[3], 's5_A_re': s5_raw_g[0], 's5_A_im': s5_raw_g[1], 's5_log_step': s5_raw_g[2][0, :S5_GROUPS],
        's5_B_re': from_b(s5_raw_g[3]), 's5_B_im': from_b(s5_raw_g[4]), 's5_C_re': from_c(s5_raw_g[5]), 's5_C_im': from_c(s5_raw_g[6]),
        's5_D': s5_g[6], 's5_b_glu': s5_g[8], 'l1_norm_mlp': d_nmlp1, 'final_norm': d_final}
    replicated_g = {n: replicated_g[n].reshape(a[n].shape) for n in REPLICATED}

    parts.update(zip(ready, Exchange(list(ready.values()), gather=False).run("scatter_last")))
    packed_g, slots = _pack([replicated_g[n] for n in REPLICATED])
    (packed_parts,) = Exchange([packed_g], gather=True).run("gather_small_grads")
    results = {}
    for n in SHARDED:
        results[n] = adamw("adamw_" + n, parts[n], a[n], a['m_' + n], a['v_' + n])
    packed = [_pack([a[pre + n] for n in REPLICATED])[0] for pre in ('', 'm_', 'v_')]
    small = [_unpack(t, slots) for t in adamw("adamw_small", packed_parts, *packed)]
    for i, n in enumerate(REPLICATED):
        results[n] = tuple(small[k][i] for k in range(4))

    loss = lax.psum(loss_blk[0, 0], ("x", "y", "c"))
    grad_x = dx0.reshape(a['x'].shape)
    return (loss, grad_x, *[results[n][0] for n in WEIGHTS], *[results[n][1] for n in WEIGHTS],
            *[results[n][2] for n in WEIGHTS], *[results[n][3] for n in WEIGHTS])
```

```python
import functools
import math

import numpy as np
import jax
import jax.numpy as jnp
from jax import lax
from jax.experimental import pallas as pl
from jax.experimental.pallas import tpu as pltpu

f32 = jnp.float32
_MXU = jnp.bfloat16
HI = lax.Precision.HIGHEST

D_MODEL = 1024
CHUNK = 64
EPS = 1e-6
N_DEV = 8
LANES = 128
HALO = 8
CONV_WIDTH = 4

RET_HEADS, RET_D = 4, 128
SSD_HEADS, SSD_P, SSD_N, SSD_GROUPS = 8, 64, 128, 2
SSD_INNER = SSD_HEADS * SSD_P
GDN_HEADS, GDN_D = 6, 128
GDN_W = GDN_HEADS * GDN_D
S5_CH, S5_GROUP, S5_GROUPS, S5_STATE = 256, 16, 16, 64
S5_W = S5_GROUPS * S5_STATE
D_FF = 4096
ROPE_THETA = 10000.0

IN0_W = 3592
IN0_PAD = 3712
IN1_W = 3340
IN1_PAD = 3456

ADAM_LR, ADAM_B1, ADAM_B2, ADAM_EPS, ADAM_WD, ADAM_STEP = 0.001, 0.9, 0.999, 1e-08, 0.01, 10

VMEM_LIMIT = 56 * 1024 * 1024


def _dot(a, b, dims):
    return lax.dot_general(a.astype(_MXU), b.astype(_MXU), (dims, ((), ())), preferred_element_type=f32)


@jax.custom_vjp
def mm(a, b):
    return _dot(a, b, ((1,), (0,)))


@jax.custom_vjp
def mm_nt(a, b):
    return _dot(a, b, ((1,), (1,)))


@jax.custom_vjp
def mm_tn(a, b):
    return _dot(a, b, ((0,), (0,)))


mm.defvjp(lambda a, b: (mm(a, b), (a, b)), lambda r, g: (mm_nt(g, r[1]), mm_tn(r[0], g)))
mm_nt.defvjp(lambda a, b: (mm_nt(a, b), (a, b)), lambda r, g: (mm(g, r[1]), mm_tn(g, r[0])))
mm_tn.defvjp(lambda a, b: (mm_tn(a, b), (a, b)), lambda r, g: (mm_nt(r[1], g), mm(r[0], g)))


def mmh(a, b):
    return jnp.dot(a, b, precision=HI, preferred_element_type=f32)


def _exact01(x, m01, dims, m_first):
    hi = x.astype(jnp.bfloat16)
    r = x - hi.astype(f32)
    mid = r.astype(jnp.bfloat16)
    lo = (r - mid.astype(f32)).astype(jnp.bfloat16)
    m = m01.astype(jnp.bfloat16)
    dot = lambda p: lax.dot_general(m, p, (dims, ((), ())), preferred_element_type=f32) if m_first else \
        lax.dot_general(p, m, (dims, ((), ())), preferred_element_type=f32)
    return dot(hi) + dot(mid) + dot(lo)


@jax.custom_vjp
def spread01(x, sel):
    return _exact01(x, sel, ((1,), (0,)), False)


spread01.defvjp(lambda x, sel: (spread01(x, sel), sel),
                lambda sel, g: (_exact01(g, sel, ((1,), (1,)), False), jnp.zeros_like(sel)))


@jax.custom_vjp
def cumsum01(tril, x):
    return _exact01(x, tril, ((1,), (0,)), True)


cumsum01.defvjp(lambda tril, x: (cumsum01(tril, x), tril),
                lambda tril, g: (jnp.zeros_like(tril), _exact01(g, tril, ((0,), (0,)), True)))


def _roll(x, shift, axis):
    return pltpu.roll(x, shift, axis)


@functools.partial(jax.custom_vjp, nondiff_argnums=(1,))
def roll_rows(x, s):
    return _roll(x, s, 0) if s else x


roll_rows.defvjp(lambda x, s: (roll_rows(x, s), None),
                 lambda s, _, g: ((_roll(g, g.shape[0] - s, 0) if s else g),))


@jax.custom_vjp
def roll_half(x):
    return _roll(x, x.shape[-1] // 2, 1)


roll_half.defvjp(lambda x: (roll_half(x), None), lambda _, g: (roll_half(g),))


def _iota(shape, axis):
    return lax.broadcasted_iota(jnp.int32, shape, axis)


def silu(x):
    return x * jax.nn.sigmoid(x)


def softplus(x):
    return jnp.maximum(x, 0.0) + jnp.log(1.0 + jnp.exp(-jnp.abs(x)))


def rmsnorm_f(x, w):
    return x * lax.rsqrt(jnp.mean(x * x, axis=-1, keepdims=True) + EPS) * w


def unit_rms(x):
    return x * lax.rsqrt(jnp.mean(x * x, axis=-1, keepdims=True) + EPS)


def _causal(n, strict=False):
    r, c = _iota((n, n), 0), _iota((n, n), 1)
    return (r > c) if strict else (r >= c)


def _tril_ones(n):
    return _causal(n).astype(f32)


def _conv_rows(xe, w):
    acc = w[CONV_WIDTH - 1:CONV_WIDTH, :] * xe
    for j in range(CONV_WIDTH - 1):
        acc = acc + w[j:j + 1, :] * roll_rows(xe, CONV_WIDTH - 1 - j)
    return acc[HALO:, :]


_RET_LOG_GAMMA = [float(np.log(np.float32(1.0) - np.float32(2.0) ** np.float32(-5.0 - h))) for h in range(RET_HEADS)]


def ret_chunk(q, k, v, gate, cos, sin, state):
    c = q.shape[0]
    idx = _iota((c, 1), 0).astype(f32)
    diff = (_iota((c, c), 0) - _iota((c, c), 1)).astype(f32)
    causal = _causal(c)
    hs = range(RET_HEADS)
    cols = [slice(h * RET_D, (h + 1) * RET_D) for h in hs]
    lg = _RET_LOG_GAMMA
    qh = [(q[:, s] * cos + roll_half(q[:, s]) * sin) * (RET_D ** -0.5) for s in cols]
    kh = [k[:, s] * cos + roll_half(k[:, s]) * sin for s in cols]
    vh = [v[:, s] for s in cols]
    sh = [state[s, :] for s in cols]
    scores = [mm_nt(qh[h], kh[h]) * jnp.exp(jnp.where(causal, lg[h] * diff, -jnp.inf)) for h in hs]
    inter = [mm(qh[h] * jnp.exp(lg[h] * (idx + 1.0)), sh[h]) for h in hs]
    y = [mm(scores[h], vh[h]) + inter[h] for h in hs]
    states = [sh[h] * math.exp(lg[h] * c) + mm_tn(kh[h] * jnp.exp(lg[h] * (c - 1.0 - idx)), vh[h]) for h in hs]
    outs = [unit_rms(y[h]) * silu(gate[:, cols[h]]) for h in hs]
    return jnp.concatenate(outs, axis=1), jnp.concatenate(states, axis=0)


def _head_select(n_heads, width):
    r, c = _iota((LANES, n_heads * width), 0), _iota((LANES, n_heads * width), 1)
    return (c // width == r).astype(f32)


def ssd_chunk(z, xe, dtr, state, conv_w, conv_b, dt_bias, a_log, d_skip, norm_w):
    c = z.shape[0]
    xbc = silu(_conv_rows(xe, conv_w) + conv_b)
    xs, bm, cm = xbc[:, :SSD_INNER], xbc[:, SSD_INNER:SSD_INNER + 256], xbc[:, SSD_INNER + 256:]
    sel = _head_select(SSD_HEADS, SSD_P)
    dt = softplus(dtr + dt_bias)
    la = dt * (-jnp.exp(a_log))
    la_cum = cumsum01(_tril_ones(c), la)
    la_cum_t = la_cum.T
    last = jnp.sum(la, axis=0, keepdims=True)
    xd = xs * spread01(dt, sel)
    la_x = spread01(la_cum, sel)
    last_x = spread01(last, sel)
    to_end = jnp.exp(last_x - la_x)
    from_start = jnp.exp(la_x)
    causal = _causal(c)
    left = (_iota((1, LANES), 1) < SSD_P).astype(f32)
    upper = _iota((LANES, 1), 0) < SSD_P
    pairs, heads = range(SSD_HEADS // 2), range(SSD_HEADS)
    bc = [bm[:, g * SSD_N:(g + 1) * SSD_N] for g in range(SSD_GROUPS)]
    cc = [cm[:, g * SSD_N:(g + 1) * SSD_N] for g in range(SSD_GROUPS)]
    cb = [mm_nt(cc[g], bc[g]) for g in range(SSD_GROUPS)]
    cols = [slice(p * LANES, (p + 1) * LANES) for p in pairs]
    xd_p = [xd[:, s] for s in cols]
    sp = [state[s, :] for s in cols]
    lmat = [jnp.exp(jnp.where(causal, la_cum[:, h:h + 1] - la_cum_t[h:h + 1, :], -jnp.inf)) for h in heads]
    off = [mm_nt(cc[p // 2], sp[p]) * from_start[:, cols[p]] for p in pairs]
    diag = [mm(cb[h // 4] * lmat[h], xd_p[h // 2] * (left if h % 2 == 0 else 1.0 - left)) for h in heads]
    cd = [jnp.where(upper, jnp.exp(last[:, 2 * p:2 * p + 1]), jnp.exp(last[:, 2 * p + 1:2 * p + 2])) for p in pairs]
    states = [sp[p] * cd[p] + mm_tn(xd_p[p] * to_end[:, cols[p]], bc[p // 2]) for p in pairs]
    ys = [off[p] + diag[2 * p] + diag[2 * p + 1] for p in pairs]
    y = jnp.concatenate(ys, axis=1) + spread01(d_skip, sel) * xs
    yg = y * silu(z)
    half = SSD_INNER // SSD_GROUPS
    out = jnp.concatenate([unit_rms(yg[:, i * half:(i + 1) * half]) for i in range(SSD_GROUPS)], axis=1) * norm_w
    return out, jnp.concatenate(states, axis=0)


def mm3(a, b):
    return jnp.dot(a, b, precision=lax.Precision.HIGH, preferred_element_type=f32)


@jax.custom_vjp
def _unit_lower_inverses(lowers):
    n = lowers[0].shape[0]
    eye = (_iota((n, n), 0) == _iota((n, n), 1)).astype(f32)
    a = [-l for l in lowers]
    p = [eye + x for x in a]
    k = 2
    while k < n:
        a = [mm3(x, x) for x in a]
        p = [y + mm3(y, x) for y, x in zip(p, a)]
        k *= 2
    return p


def _unit_lower_inverses_bwd(t_inv, g):
    dims_tn, dims_nt = (((0,), (0,)), ((), ())), (((1,), (1,)), ((), ()))
    x = [lax.dot_general(t, gi, dims_tn, precision=lax.Precision.HIGH, preferred_element_type=f32) for t, gi in zip(t_inv, g)]
    return ([-lax.dot_general(xi, t, dims_nt, precision=lax.Precision.HIGH, preferred_element_type=f32) for xi, t in zip(x, t_inv)],)


def _unit_lower_inverses_fwd(lowers):
    t_inv = _unit_lower_inverses(lowers)
    return t_inv, t_inv


_unit_lower_inverses.defvjp(_unit_lower_inverses_fwd, _unit_lower_inverses_bwd)


GDN_GCOL = 6


def gdn_chunk(xe, z, ba, state, conv_w, a_log, dt_bias, norm_w):
    c = z.shape[0]
    qkv = silu(_conv_rows(xe, conv_w))
    beta_all = jax.nn.sigmoid(ba)
    g_all = -jnp.exp(a_log) * softplus(ba + dt_bias)
    gc = cumsum01(_tril_ones(c), g_all)
    gc_t = gc.T
    last = jnp.sum(g_all, axis=0, keepdims=True)
    causal, strict = _causal(c), _causal(c, strict=True)
    hs = range(GDN_HEADS)
    cols = [slice(h * GDN_D, (h + 1) * GDN_D) for h in hs]
    qh = [qkv[:, h * GDN_D:(h + 1) * GDN_D] for h in hs]
    kh = [qkv[:, GDN_W + h * GDN_D:GDN_W + (h + 1) * GDN_D] for h in hs]
    vh = [qkv[:, 2 * GDN_W + h * GDN_D:2 * GDN_W + (h + 1) * GDN_D] for h in hs]
    qh = [t * lax.rsqrt(jnp.sum(t * t, axis=-1, keepdims=True) + EPS) * (GDN_D ** -0.5) for t in qh]
    kh = [t * lax.rsqrt(jnp.sum(t * t, axis=-1, keepdims=True) + EPS) for t in kh]
    beta = [beta_all[:, h:h + 1] for h in hs]
    col = [gc[:, GDN_GCOL + h:GDN_GCOL + h + 1] for h in hs]
    row = [gc_t[GDN_GCOL + h:GDN_GCOL + h + 1, :] for h in hs]
    lst = [last[:, GDN_GCOL + h:GDN_GCOL + h + 1] for h in hs]
    decay = [jnp.exp(jnp.where(causal, col[h] - row[h], -jnp.inf)) for h in hs]
    e_col = [jnp.exp(t) for t in col]
    kb = [kh[h] * beta[h] for h in hs]
    vb = [vh[h] * beta[h] for h in hs]
    kk = [mm_nt(kb[h], kh[h]) for h in hs]
    qk = [mm_nt(qh[h], kh[h]) for h in hs]
    t_inv = _unit_lower_inverses([jnp.where(strict, kk[h] * decay[h], 0.0) for h in hs])
    u = [mm(t_inv[h], vb[h]) for h in hs]
    w = [mm(t_inv[h], kb[h] * e_col[h]) for h in hs]
    attn = [jnp.where(causal, qk[h] * decay[h], 0.0) for h in hs]
    sh = [state[s, :] for s in cols]
    ws = [mm(w[h], sh[h]) for h in hs]
    qs = [mm(qh[h] * e_col[h], sh[h]) for h in hs]
    v_new = [u[h] - ws[h] for h in hs]
    o = [qs[h] + mm(attn[h], v_new[h]) for h in hs]
    states = [sh[h] * jnp.exp(lst[h]) + mm_tn(kh[h] * jnp.exp(lst[h] - col[h]), v_new[h]) for h in hs]
    outs = [unit_rms(o[h]) * norm_w * silu(z[:, cols[h]]) for h in hs]
    return jnp.concatenate(outs, axis=1), jnp.concatenate(states, axis=0)


def _s5_group_mask():
    r, c = _iota((S5_CH, S5_W), 0), _iota((S5_CH, S5_W), 1)
    return (r // S5_GROUP == c // S5_STATE).astype(f32)


def s5_prep(a_re, a_im, log_step, b_re, b_im, c_re, c_im):
    r, c = _iota((LANES, S5_W), 0), _iota((LANES, S5_W), 1)
    step = jnp.exp(mmh(log_step, (c // S5_STATE == r).astype(f32)))
    zr, zi = a_re * step, a_im * step
    e = jnp.exp(zr)
    lr, li = e * jnp.cos(zi), e * jnp.sin(zi)
    den = a_re * a_re + a_im * a_im
    xr, xi = lr - 1.0, li
    cr, ci = (xr * a_re + xi * a_im) / den, (xi * a_re - xr * a_im) / den
    bbr, bbi = cr * b_re - ci * b_im, cr * b_im + ci * b_re
    mask = _s5_group_mask()
    tile = lambda t: jnp.tile(t, (S5_GROUPS, 1)) * mask
    return lr, li, tile(bbr), tile(bbi), tile(c_re), tile(c_im)


def s5_out_fn(h_re, h_im, u, cblk_re, cblk_im, d_skip, w_glu, b_glu):
    y = mm_nt(h_re, cblk_re) - mm_nt(h_im, cblk_im) + d_skip * u
    y = jax.nn.gelu(y)
    return y * jax.nn.sigmoid(mm(y, w_glu) + b_glu)


def _params(sem, **kw):
    return pltpu.CompilerParams(dimension_semantics=sem, vmem_limit_bytes=VMEM_LIMIT, **kw)


def _const_spec(shape):
    return pl.BlockSpec(shape, lambda i: (0,) * len(shape))


def _resident_spec(shape):
    return pl.BlockSpec(shape, lambda i: (0,) * len(shape), pipeline_mode=pl.Buffered(1))


def _row_spec(rows, cols, col_block=0):
    return pl.BlockSpec((rows, cols), lambda i: (i, col_block))


class Seq:
    def __init__(self, array, width, col_block, kind="tile"):
        self.array, self.width, self.col_block, self.kind = array, width, col_block, kind


def mixer_fwd(name, fn, seqs, params, out_width, state_rows, seq_len, exchange=None):
    nc = seq_len // CHUNK
    n_refs = sum(2 if s.kind == "halo" else 1 for s in seqs)
    n_par = len(params)
    car = Carried(exchange)

    def body(*refs):
        seq_refs, par_refs = refs[:n_refs], refs[n_refs:n_refs + n_par]
        k0 = n_refs + n_par
        ex_ins, (out_ref, st_ref) = refs[k0:k0 + car.n], refs[k0 + car.n:k0 + car.n + 2]
        ex_outs, state, ex_sems = refs[k0 + car.n + 2:k0 + 2 * car.n + 2], refs[k0 + 2 * car.n + 2], refs[k0 + 2 * car.n + 3:]
        c = pl.program_id(0)
        car.start_at(c == 0, ex_ins, ex_outs, ex_sems)

        @pl.when(c == 0)
        def _():
            state[...] = jnp.zeros_like(state)

        vals, k = [], 0
        for s in seqs:
            if s.kind == "halo":
                prev = jnp.where(c > 0, seq_refs[k][...], 0.0)
                vals.append(jnp.concatenate([prev, seq_refs[k + 1][...]], axis=0))
                k += 2
            else:
                vals.append(seq_refs[k][...])
                k += 1
        s_in = state[...]
        st_ref[...] = s_in
        out, s_new = fn(*vals, s_in, *[p[...] for p in par_refs])
        out_ref[...] = out
        state[...] = s_new
        car.wait_at(c == nc - 1, ex_ins, ex_outs, ex_sems)

    in_specs, operands = [], []
    for s in seqs:
        if s.kind == "halo":
            rb, w, cb = CHUNK // HALO, s.width, s.col_block
            in_specs.append(pl.BlockSpec((HALO, w), lambda i, rb=rb, cb=cb: (jnp.maximum(i * rb - 1, 0), cb)))
            operands.append(s.array)
        in_specs.append(pl.BlockSpec((CHUNK, s.width), lambda i, cb=s.col_block: (i, cb)))
        operands.append(s.array)
    for p in params:
        in_specs.append(_const_spec(p.shape))
        operands.append(p)
    outs = pl.pallas_call(
        body, grid=(nc,), in_specs=in_specs + car.in_specs,
        out_specs=[pl.BlockSpec((CHUNK, out_width), lambda i: (i, 0)),
                   pl.BlockSpec((None, state_rows, LANES), lambda i: (i, 0, 0))] + car.out_specs,
        out_shape=[jax.ShapeDtypeStruct((seq_len, out_width), f32),
                   jax.ShapeDtypeStruct((nc, state_rows, LANES), f32)] + car.out_shape,
        scratch_shapes=[pltpu.VMEM((state_rows, LANES), f32)] + car.scratch,
        compiler_params=_params(("arbitrary",)), name=name)(*operands, *car.operands)
    return outs[0], outs[1], outs[2:]


def mixer_bwd(name, fn, seqs, params, states, d_out, seq_len, exchange=None):
    nc = seq_len // CHUNK
    state_rows = states.shape[1]
    diff = [s for s in seqs if s.kind != "const"]
    halos = [s for s in diff if s.kind == "halo"]
    n_refs = sum(2 if s.kind == "halo" else 1 for s in seqs)
    n_par = len(params)
    car = Carried(exchange)

    def body(*refs):
        seq_refs, par_refs = refs[:n_refs], refs[n_refs:n_refs + n_par]
        st_ref, dout_ref = refs[n_refs + n_par:n_refs + n_par + 2]
        k0 = n_refs + n_par + 2
        ex_ins = refs[k0:k0 + car.n]
        k0 += car.n
        dseq_refs, dpar_refs = refs[k0:k0 + len(diff)], refs[k0 + len(diff):k0 + len(diff) + n_par]
        k0 += len(diff) + n_par
        ex_outs = refs[k0:k0 + car.n]
        scratch = refs[k0 + car.n:]
        d_state, carries, ex_sems = scratch[0], scratch[1:1 + len(halos)], scratch[1 + len(halos):]
        i = pl.program_id(0)
        c = nc - 1 - i
        car.start_at(i == 0, ex_ins, ex_outs, ex_sems)

        @pl.when(i == 0)
        def _():
            d_state[...] = jnp.zeros_like(d_state)
            for r in list(carries) + list(dpar_refs):
                r[...] = jnp.zeros_like(r)

        dvals, consts, k = [], [], 0
        for s in seqs:
            if s.kind == "halo":
                prev = jnp.where(c > 0, seq_refs[k][...], 0.0)
                dvals.append(jnp.concatenate([prev, seq_refs[k + 1][...]], axis=0))
                k += 2
            elif s.kind == "tile":
                dvals.append(seq_refs[k][...])
                k += 1
            else:
                consts.append(seq_refs[k][...])
                k += 1
        nd = len(dvals)

        def call(*a):
            it_d, it_c = iter(a[:nd]), iter(consts)
            vals = [next(it_c) if s.kind == "const" else next(it_d) for s in seqs]
            return fn(*vals, *a[nd:])

        _, vjp = jax.vjp(call, *dvals, st_ref[...], *[p[...] for p in par_refs])
        cts = vjp((dout_ref[...], d_state[...]))
        hk = 0
        for j, s in enumerate(diff):
            if s.kind == "halo":
                dseq_refs[j][...] = cts[j][HALO:, :]
                dseq_refs[j][CHUNK - HALO:, :] += carries[hk][...]
                carries[hk][...] = cts[j][:HALO, :]
                hk += 1
            else:
                dseq_refs[j][...] = cts[j]
        d_state[...] = cts[nd]
        for j in range(n_par):
            dpar_refs[j][...] += cts[nd + 1 + j]
        car.wait_at(i == nc - 1, ex_ins, ex_outs, ex_sems)

    in_specs, operands = [], []
    for s in seqs:
        if s.kind == "halo":
            rb, cb = CHUNK // HALO, s.col_block
            in_specs.append(pl.BlockSpec((HALO, s.width), lambda i, rb=rb, cb=cb: (jnp.maximum((nc - 1 - i) * rb - 1, 0), cb)))
            operands.append(s.array)
        in_specs.append(pl.BlockSpec((CHUNK, s.width), lambda i, cb=s.col_block: (nc - 1 - i, cb)))
        operands.append(s.array)
    for p in params:
        in_specs.append(_const_spec(p.shape))
        operands.append(p)
    in_specs.append(pl.BlockSpec((None, state_rows, LANES), lambda i: (nc - 1 - i, 0, 0)))
    in_specs.append(pl.BlockSpec((CHUNK, d_out.shape[1]), lambda i: (nc - 1 - i, 0)))
    operands += [states, d_out]
    outs = pl.pallas_call(
        body, grid=(nc,), in_specs=in_specs + car.in_specs,
        out_specs=[pl.BlockSpec((CHUNK, s.width), lambda i: (nc - 1 - i, 0)) for s in diff]
        + [_const_spec(p.shape) for p in params] + car.out_specs,
        out_shape=[jax.ShapeDtypeStruct((seq_len, s.width), f32) for s in diff]
        + [jax.ShapeDtypeStruct(p.shape, f32) for p in params] + car.out_shape,
        scratch_shapes=[pltpu.VMEM((state_rows, LANES), f32)] + [pltpu.VMEM((HALO, s.width), f32) for s in halos]
        + car.scratch,
        compiler_params=_params(("arbitrary",)), name=name)(*operands, *car.operands)
    nd = len(diff)
    return outs[:nd], outs[nd:nd + n_par], outs[nd + n_par:]


TOK = 256


def inproj_fwd(name, x, nw, w):
    seq_len, n = x.shape[0], w.shape[1]

    def body(x_ref, nw_ref, w_ref, o_ref):
        o_ref[...] = mm(rmsnorm_f(x_ref[...], nw_ref[...]), w_ref[...])

    return pl.pallas_call(
        body, grid=(seq_len // TOK,),
        in_specs=[_row_spec(TOK, D_MODEL), _const_spec(nw.shape), _resident_spec(w.shape)],
        out_specs=_row_spec(TOK, n), out_shape=jax.ShapeDtypeStruct((seq_len, n), f32),
        compiler_params=_params(("arbitrary",)), name=name)(x, nw, w)


def inproj_bwd(name, x, nw, w, pieces, d_res):
    seq_len, n = x.shape[0], w.shape[1]
    widths = [p.shape[1] for p in pieces]
    assert sum(widths) == n
    k = len(pieces)

    def body(*refs):
        x_ref, nw_ref, w_ref = refs[:3]
        p_refs, dres_ref = refs[3:3 + k], refs[3 + k]
        dx_ref, dw_ref, dnw_ref = refs[4 + k:]

        @pl.when(pl.program_id(0) == 0)
        def _():
            dw_ref[...] = jnp.zeros_like(dw_ref)
            dnw_ref[...] = jnp.zeros_like(dnw_ref)

        h, vjp = jax.vjp(rmsnorm_f, x_ref[...], nw_ref[...])
        dh, off = jnp.zeros_like(h), 0
        for p_ref, wd in zip(p_refs, widths):
            g = p_ref[...]
            dh = dh + mm_nt(g, w_ref[:, off:off + wd])
            dw_ref[:, off:off + wd] += mm_tn(h, g)
            off += wd
        dx, dnw = vjp(dh)
        dx_ref[...] = dres_ref[...] + dx
        dnw_ref[...] += dnw

    return pl.pallas_call(
        body, grid=(seq_len // TOK,),
        in_specs=[_row_spec(TOK, D_MODEL), _const_spec(nw.shape), _resident_spec(w.shape)]
        + [_row_spec(TOK, wd) for wd in widths] + [_row_spec(TOK, D_MODEL)],
        out_specs=[_row_spec(TOK, D_MODEL), _resident_spec((D_MODEL, n)), _const_spec(nw.shape)],
        out_shape=[jax.ShapeDtypeStruct((seq_len, D_MODEL), f32), jax.ShapeDtypeStruct((D_MODEL, n), f32),
                   jax.ShapeDtypeStruct(nw.shape, f32)],
        compiler_params=_params(("arbitrary",)), name=name)(x, nw, w, *pieces, d_res)


def outproj_fwd(name, x, a, b, w):
    seq_len, wa, wb = x.shape[0], a.shape[1], b.shape[1]

    def body(x_ref, a_ref, b_ref, w_ref, o_ref):
        o_ref[...] = x_ref[...] + mm(a_ref[...], w_ref[:wa, :]) + mm(b_ref[...], w_ref[wa:, :])

    return pl.pallas_call(
        body, grid=(seq_len // TOK,),
        in_specs=[_row_spec(TOK, D_MODEL), _row_spec(TOK, wa), _row_spec(TOK, wb), _resident_spec(w.shape)],
        out_specs=_row_spec(TOK, D_MODEL), out_shape=jax.ShapeDtypeStruct((seq_len, D_MODEL), f32),
        compiler_params=_params(("arbitrary",)), name=name)(x, a, b, w)


def outproj_bwd(name, dy, a, b, w):
    seq_len, wa, wb = dy.shape[0], a.shape[1], b.shape[1]

    def body(dy_ref, a_ref, b_ref, w_ref, da_ref, db_ref, dw_ref):
        @pl.when(pl.program_id(0) == 0)
        def _():
            dw_ref[...] = jnp.zeros_like(dw_ref)

        g = dy_ref[...]
        da_ref[...] = mm_nt(g, w_ref[:wa, :])
        db_ref[...] = mm_nt(g, w_ref[wa:, :])
        dw_ref[:wa, :] += mm_tn(a_ref[...], g)
        dw_ref[wa:, :] += mm_tn(b_ref[...], g)

    return pl.pallas_call(
        body, grid=(seq_len // TOK,),
        in_specs=[_row_spec(TOK, D_MODEL), _row_spec(TOK, wa), _row_spec(TOK, wb), _resident_spec(w.shape)],
        out_specs=[_row_spec(TOK, wa), _row_spec(TOK, wb), _resident_spec(w.shape)],
        out_shape=[jax.ShapeDtypeStruct((seq_len, wa), f32), jax.ShapeDtypeStruct((seq_len, wb), f32),
                   jax.ShapeDtypeStruct(w.shape, f32)],
        compiler_params=_params(("arbitrary",)), name=name)(dy, a, b, w)


FF_BLOCK = D_FF // N_DEV


def mlp_fwd(name, x, nw, w_up, w_down, exchange=None):
    seq_len = x.shape[0]
    nt = seq_len // TOK
    car = Carried(exchange)

    def body(*refs):
        x_ref, nw_ref, up_ref, down_ref = refs[:4]
        ex_ins, (o_ref, relu_ref) = refs[4:4 + car.n], refs[4 + car.n:6 + car.n]
        ex_outs, ex_sems = refs[6 + car.n:6 + 2 * car.n], refs[6 + 2 * car.n:]
        i = pl.program_id(0)
        car.start_at(i == 0, ex_ins, ex_outs, ex_sems)
        xv = x_ref[...]
        h = rmsnorm_f(xv, nw_ref[...])
        acc = xv
        for d in range(N_DEV):
            r = jnp.maximum(mm(h, up_ref[d]), 0.0)
            relu_ref[d] = r.astype(_MXU)
            acc = acc + mm(r * r, down_ref[d])
        o_ref[...] = acc
        car.wait_at(i == nt - 1, ex_ins, ex_outs, ex_sems)

    outs = pl.pallas_call(
        body, grid=(nt,),
        in_specs=[_row_spec(TOK, D_MODEL), _const_spec(nw.shape), _resident_spec(w_up.shape), _resident_spec(w_down.shape)]
        + car.in_specs,
        out_specs=[_row_spec(TOK, D_MODEL), pl.BlockSpec((N_DEV, TOK, FF_BLOCK), lambda i: (0, i, 0))] + car.out_specs,
        out_shape=[jax.ShapeDtypeStruct((seq_len, D_MODEL), f32),
                   jax.ShapeDtypeStruct((N_DEV, seq_len, FF_BLOCK), _MXU)] + car.out_shape, scratch_shapes=car.scratch,
        compiler_params=_params(("arbitrary",)), name=name)(x, nw, w_up, w_down, *car.operands)
    return outs[0], outs[1], outs[2:]


MLP_SPLIT = 2


def mlp_bwd(name, x, nw, w_up, w_down, relu, dy, exchange=None):
    seq_len = x.shape[0]
    nt = seq_len // TOK
    per = N_DEV // MLP_SPLIT
    car = Carried(exchange)

    def body(*refs):
        x_ref, nw_ref, up_ref, down_ref, relu_ref, dy_ref = refs[:6]
        ex_ins, (dh_ref, dup_ref, ddown_ref) = refs[6:6 + car.n], refs[6 + car.n:9 + car.n]
        ex_outs, ex_sems = refs[9 + car.n:9 + 2 * car.n], refs[9 + 2 * car.n:]
        j, i = pl.program_id(0), pl.program_id(1)
        car.start_at(jnp.logical_and(j == 0, i == 0), ex_ins, ex_outs, ex_sems)

        @pl.when(i == 0)
        def _():
            dup_ref[...] = jnp.zeros_like(dup_ref)
            ddown_ref[...] = jnp.zeros_like(ddown_ref)

        h = rmsnorm_f(x_ref[...], nw_ref[...])
        g = dy_ref[...]
        dh = jnp.zeros_like(h)
        for d in range(per):
            r = relu_ref[d].astype(f32)
            da = mm_nt(g, down_ref[d]) * (2.0 * r)
            ddown_ref[d] += mm_tn(r * r, g)
            dup_ref[d] += mm_tn(h, da)
            dh = dh + mm_nt(da, up_ref[d])
        dh_ref[...] = dh
        car.wait_at(jnp.logical_and(j == MLP_SPLIT - 1, i == nt - 1), ex_ins, ex_outs, ex_sems)

    outs = pl.pallas_call(
        body, grid=(MLP_SPLIT, nt),
        in_specs=[pl.BlockSpec((TOK, D_MODEL), lambda j, i: (i, 0)), pl.BlockSpec(nw.shape, lambda j, i: (0, 0)),
                  pl.BlockSpec((per, D_MODEL, FF_BLOCK), lambda j, i: (j, 0, 0), pipeline_mode=pl.Buffered(1)),
                  pl.BlockSpec((per, FF_BLOCK, D_MODEL), lambda j, i: (j, 0, 0), pipeline_mode=pl.Buffered(1)),
                  pl.BlockSpec((per, TOK, FF_BLOCK), lambda j, i: (j, i, 0)),
                  pl.BlockSpec((TOK, D_MODEL), lambda j, i: (i, 0))] + car.in_specs,
        out_specs=[pl.BlockSpec((None, TOK, D_MODEL), lambda j, i: (j, i, 0)),
                   pl.BlockSpec((per, D_MODEL, FF_BLOCK), lambda j, i: (j, 0, 0), pipeline_mode=pl.Buffered(1)),
                   pl.BlockSpec((per, FF_BLOCK, D_MODEL), lambda j, i: (j, 0, 0), pipeline_mode=pl.Buffered(1))]
        + car.out_specs,
        out_shape=[jax.ShapeDtypeStruct((MLP_SPLIT, seq_len, D_MODEL), f32),
                   jax.ShapeDtypeStruct(w_up.shape, f32), jax.ShapeDtypeStruct(w_down.shape, f32)] + car.out_shape,
        scratch_shapes=car.scratch,
        compiler_params=_params(("arbitrary", "arbitrary")), name=name)(x, nw, w_up, w_down, relu, dy, *car.operands)
    dh_parts, d_up, d_down, ex_results = outs[0], outs[1], outs[2], outs[3:]

    def norm_body(x_ref, nw_ref, dh_ref, dy_ref, dx_ref, dnw_ref):
        @pl.when(pl.program_id(0) == 0)
        def _():
            dnw_ref[...] = jnp.zeros_like(dnw_ref)

        _, vjp = jax.vjp(rmsnorm_f, x_ref[...], nw_ref[...])
        dh = dh_ref[0]
        for j in range(1, MLP_SPLIT):
            dh = dh + dh_ref[j]
        dx, dnw = vjp(dh)
        dx_ref[...] = dy_ref[...] + dx
        dnw_ref[...] += dnw

    dx, dnw = pl.pallas_call(
        norm_body, grid=(nt,),
        in_specs=[_row_spec(TOK, D_MODEL), _const_spec(nw.shape),
                  pl.BlockSpec((MLP_SPLIT, TOK, D_MODEL), lambda i: (0, i, 0)), _row_spec(TOK, D_MODEL)],
        out_specs=[_row_spec(TOK, D_MODEL), _const_spec(nw.shape)],
        out_shape=[jax.ShapeDtypeStruct((seq_len, D_MODEL), f32), jax.ShapeDtypeStruct(nw.shape, f32)],
        compiler_params=_params(("arbitrary",)), name=name + "_norm")(x, nw, dh_parts, dy)
    return dx, d_up, d_down, dnw, ex_results


def final_loss(name, x, nw, target):
    seq_len = x.shape[0]

    def body(x_ref, nw_ref, t_ref, loss_ref, dx_ref, dnw_ref):
        @pl.when(pl.program_id(0) == 0)
        def _():
            loss_ref[...] = jnp.zeros_like(loss_ref)
            dnw_ref[...] = jnp.zeros_like(dnw_ref)

        y, vjp = jax.vjp(rmsnorm_f, x_ref[...], nw_ref[...])
        err = y - t_ref[...]
        loss_ref[...] += 0.5 * jnp.sum(jnp.mean(err * err, axis=-1, keepdims=True), axis=0, keepdims=True)
        dx, dnw = vjp(err * (1.0 / D_MODEL))
        dx_ref[...] = dx
        dnw_ref[...] += dnw

    return pl.pallas_call(
        body, grid=(seq_len // TOK,),
        in_specs=[_row_spec(TOK, D_MODEL), _const_spec(nw.shape), _row_spec(TOK, D_MODEL)],
        out_specs=[_const_spec((8, LANES)), _row_spec(TOK, D_MODEL), _const_spec(nw.shape)],
        out_shape=[jax.ShapeDtypeStruct((8, LANES), f32), jax.ShapeDtypeStruct((seq_len, D_MODEL), f32),
                   jax.ShapeDtypeStruct(nw.shape, f32)],
        compiler_params=_params(("arbitrary",)), name=name)(x, nw, target)


def _whole(a):
    return pl.BlockSpec(a.shape, lambda: (0,) * len(a.shape))


def s5_prep_fwd(name, raw):
    def body(*refs):
        outs = s5_prep(*[r[...] for r in refs[:7]])
        for o_ref, o in zip(refs[7:], outs):
            o_ref[...] = o

    shapes = [(1, S5_W)] * 2 + [(S5_CH, S5_W)] * 4
    return pl.pallas_call(
        body, in_specs=[_whole(a) for a in raw], out_specs=[pl.BlockSpec(s, lambda s=s: (0,) * len(s)) for s in shapes],
        out_shape=[jax.ShapeDtypeStruct(s, f32) for s in shapes],
        compiler_params=pltpu.CompilerParams(vmem_limit_bytes=VMEM_LIMIT), name=name)(*raw)


def s5_prep_bwd(name, raw, cts):
    def body(*refs):
        _, vjp = jax.vjp(s5_prep, *[r[...] for r in refs[:7]])
        grads = vjp(tuple(r[...] for r in refs[7:13]))
        for o_ref, g in zip(refs[13:], grads):
            o_ref[...] = g

    return pl.pallas_call(
        body, in_specs=[_whole(a) for a in list(raw) + list(cts)], out_specs=[_whole(a) for a in raw],
        out_shape=[jax.ShapeDtypeStruct(a.shape, f32) for a in raw],
        compiler_params=pltpu.CompilerParams(vmem_limit_bytes=VMEM_LIMIT), name=name)(*raw, *cts)


def s5_fwd(name, proj, u_block, lam_re, lam_im, bblk_re, bblk_im, cblk_re, cblk_im, d_skip, w_glu, b_glu):
    seq_len = proj.shape[0]

    def body(u_ref, lr_ref, li_ref, br_ref, bi_ref, cr_ref, ci_ref, d_ref, wg_ref, bg_ref,
             o_ref, hr_ref, hi_ref, carry_re, carry_im):
        @pl.when(pl.program_id(0) == 0)
        def _():
            carry_re[...] = jnp.zeros_like(carry_re)
            carry_im[...] = jnp.zeros_like(carry_im)

        u = u_ref[...]
        hr_ref[...] = mm(u, br_ref[...])
        hi_ref[...] = mm(u, bi_ref[...])
        lr, li = lr_ref[...], li_ref[...]

        def step(t, h):
            pr, pi = h
            row = pl.ds(t, 1)
            nr = lr * pr - li * pi + hr_ref[row, :]
            ni = lr * pi + li * pr + hi_ref[row, :]
            hr_ref[row, :] = nr
            hi_ref[row, :] = ni
            return nr, ni

        er, ei = lax.fori_loop(0, TOK, step, (carry_re[0:1, :], carry_im[0:1, :]))
        carry_re[0:1, :] = er
        carry_im[0:1, :] = ei
        o_ref[...] = s5_out_fn(hr_ref[...], hi_ref[...], u, cr_ref[...], ci_ref[...], d_ref[...],
                               wg_ref[...].astype(f32), bg_ref[...])

    consts = [lam_re, lam_im, bblk_re, bblk_im, cblk_re, cblk_im, d_skip, w_glu, b_glu]
    return pl.pallas_call(
        body, grid=(seq_len // TOK,),
        in_specs=[_row_spec(TOK, S5_CH, u_block)] + [_const_spec(a.shape) for a in consts],
        out_specs=[_row_spec(TOK, S5_CH), _row_spec(TOK, S5_W), _row_spec(TOK, S5_W)],
        out_shape=[jax.ShapeDtypeStruct((seq_len, S5_CH), f32), jax.ShapeDtypeStruct((seq_len, S5_W), f32),
                   jax.ShapeDtypeStruct((seq_len, S5_W), f32)],
        scratch_shapes=[pltpu.VMEM((HALO, S5_W), f32), pltpu.VMEM((HALO, S5_W), f32)],
        compiler_params=_params(("arbitrary",)), name=name)(proj, *consts)


def s5_bwd(name, proj, u_block, h_re, h_im, d_out, lam_re, lam_im, bblk_re, bblk_im, cblk_re, cblk_im, d_skip, w_glu, b_glu):
    seq_len = proj.shape[0]
    nt = seq_len // TOK
    consts = [lam_re, lam_im, bblk_re, bblk_im, cblk_re, cblk_im, d_skip, w_glu, b_glu]

    def body(u_ref, hr_ref, hi_ref, pr_ref, pi_ref, dout_ref, lr_ref, li_ref, br_ref, bi_ref, cr_ref, ci_ref, d_ref, wg_ref, bg_ref,
             du_ref, dlr_ref, dli_ref, dbr_ref, dbi_ref, dcr_ref, dci_ref, dd_ref, dwg_ref, dbg_ref,
             gr_ref, gi_ref, carry_re, carry_im):
        i = pl.program_id(0)
        tile = nt - 1 - i

        @pl.when(i == 0)
        def _():
            for r in (carry_re, carry_im, dlr_ref, dli_ref, dbr_ref, dbi_ref, dcr_ref, dci_ref, dd_ref, dwg_ref, dbg_ref):
                r[...] = jnp.zeros_like(r)

        u, h_r, h_i = u_ref[...], hr_ref[...], hi_ref[...]
        _, vjp = jax.vjp(s5_out_fn, h_r, h_i, u, cr_ref[...], ci_ref[...], d_ref[...], wg_ref[...].astype(f32), bg_ref[...])
        ghr, ghi, du, dcr, dci, dd, dwg, dbg = vjp(dout_ref[...])
        gr_ref[...] = ghr
        gi_ref[...] = ghi
        lr, li = lr_ref[...], li_ref[...]

        def step(k, g):
            nr, ni = g
            row = pl.ds(TOK - 1 - k, 1)
            cr = gr_ref[row, :] + lr * nr + li * ni
            ci = gi_ref[row, :] - li * nr + lr * ni
            gr_ref[row, :] = cr
            gi_ref[row, :] = ci
            return cr, ci

        er, ei = lax.fori_loop(0, TOK, step, (carry_re[0:1, :], carry_im[0:1, :]))
        carry_re[0:1, :] = er
        carry_im[0:1, :] = ei
        g_r, g_i = gr_ref[...], gi_ref[...]
        first = _iota((TOK, 1), 0) == 0
        keep = jnp.where(tile > 0, 1.0, 0.0)
        p_r = jnp.where(first, pr_ref[HALO - 1:HALO, :] * keep, _roll(h_r, 1, 0))
        p_i = jnp.where(first, pi_ref[HALO - 1:HALO, :] * keep, _roll(h_i, 1, 0))
        dlr_ref[...] += jnp.sum(g_r * p_r + g_i * p_i, axis=0, keepdims=True)
        dli_ref[...] += jnp.sum(g_i * p_r - g_r * p_i, axis=0, keepdims=True)
        du_ref[...] = du + mm_nt(g_r, br_ref[...]) + mm_nt(g_i, bi_ref[...])
        dbr_ref[...] += mm_tn(u, g_r)
        dbi_ref[...] += mm_tn(u, g_i)
        dcr_ref[...] += dcr
        dci_ref[...] += dci
        dd_ref[...] += dd
        dwg_ref[...] += dwg
        dbg_ref[...] += dbg

    rev = lambda cols, cb=0: pl.BlockSpec((TOK, cols), lambda i, cb=cb: (nt - 1 - i, cb))
    prev = pl.BlockSpec((HALO, S5_W), lambda i: (jnp.maximum((nt - 1 - i) * (TOK // HALO) - 1, 0), 0))
    outs = pl.pallas_call(
        body, grid=(nt,),
        in_specs=[rev(S5_CH, u_block), rev(S5_W), rev(S5_W), prev, prev, rev(S5_CH)] + [_const_spec(a.shape) for a in consts],
        out_specs=[rev(S5_CH)] + [_const_spec(a.shape) for a in consts],
        out_shape=[jax.ShapeDtypeStruct((seq_len, S5_CH), f32)] + [jax.ShapeDtypeStruct(a.shape, f32) for a in consts],
        scratch_shapes=[pltpu.VMEM((TOK, S5_W), f32), pltpu.VMEM((TOK, S5_W), f32),
                        pltpu.VMEM((HALO, S5_W), f32), pltpu.VMEM((HALO, S5_W), f32)],
        compiler_params=_params(("arbitrary",)), name=name)(proj, h_re, h_im, h_re, h_im, d_out, *consts)
    return outs[0], outs[1:]


ANY = pl.BlockSpec(memory_space=pl.ANY)


def _mesh_position():
    x, y, c = lax.axis_index("x"), lax.axis_index("y"), lax.axis_index("c")
    return x, y, c, 4 * x + 2 * y + c


def _peer(x, y, c, r):
    px = 1 - x if r & 4 else x
    py = 1 - y if r & 2 else y
    pc = 1 - c if r & 1 else c
    return (px, py, pc), 4 * px + 2 * py + pc


class Exchange:
    def __init__(self, arrays, gather):
        self.arrays, self.gather, self.n = list(arrays), gather, len(arrays)
        self.in_specs = [ANY] * self.n
        self.out_specs = [ANY] * self.n
        shapes = [((N_DEV,) + a.shape) if gather else a.shape for a in self.arrays]
        self.out_shape = [jax.ShapeDtypeStruct(s, a.dtype) for s, a in zip(shapes, self.arrays)]
        self.scratch = [pltpu.SemaphoreType.DMA((self.n, N_DEV - 1)), pltpu.SemaphoreType.DMA((self.n, N_DEV - 1)),
                        pltpu.SemaphoreType.DMA((self.n,))]

    def _copies(self, ins, outs, sems, landed):
        send_sems, recv_sems, local_sems = sems
        x, y, c, me = _mesh_position()
        local, remote = [], []
        for i in range(self.n):
            mine = ins[i] if self.gather else ins[i].at[me]
            local.append(pltpu.make_async_copy(mine, outs[i].at[me], local_sems.at[i]))
            for r in range(1, N_DEV):
                peer, peer_idx = _peer(x, y, c, r)
                remote.append(pltpu.make_async_remote_copy(
                    src_ref=ins[i] if self.gather else ins[i].at[peer_idx],
                    dst_ref=outs[i].at[peer_idx if landed else me],
                    send_sem=send_sems.at[i, r - 1], recv_sem=recv_sems.at[i, r - 1],
                    device_id=peer, device_id_type=pl.DeviceIdType.MESH))
        return local, remote

    def start(self, ins, outs, sems):
        local, remote = self._copies(ins, outs, sems, landed=False)
        for cp in local + remote:
            cp.start()

    def wait(self, ins, outs, sems):
        local, remote = self._copies(ins, outs, sems, landed=True)
        for cp in remote:
            cp.wait_recv()
            cp.wait_send()
        for cp in local:
            cp.wait()

    def run(self, name):
        n = self.n

        def body(*refs):
            ins, outs, sems = refs[:n], refs[n:2 * n], refs[2 * n:]
            self.start(ins, outs, sems)
            self.wait(ins, outs, sems)

        return pl.pallas_call(body, in_specs=self.in_specs, out_specs=self.out_specs, out_shape=self.out_shape,
                              scratch_shapes=self.scratch, name=name)(*self.arrays)


class Carried:
    def __init__(self, exchange):
        self.ex = exchange
        self.n = exchange.n if exchange else 0
        self.in_specs = exchange.in_specs if exchange else []
        self.out_specs = exchange.out_specs if exchange else []
        self.out_shape = exchange.out_shape if exchange else []
        self.scratch = exchange.scratch if exchange else []
        self.operands = exchange.arrays if exchange else []

    def start_at(self, first, ins, outs, sems):
        if self.ex is not None:
            @pl.when(first)
            def _():
                self.ex.start(ins, outs, sems)

    def wait_at(self, last, ins, outs, sems):
        if self.ex is not None:
            @pl.when(last)
            def _():
                self.ex.wait(ins, outs, sems)


def adamw(name, parts, w, m, v):
    rows, cols = w.shape
    tr = rows
    for cand in (512, 256, 128, 64, 32, 16, 8):
        if rows * cols * 4 > (1 << 20) and rows % cand == 0 and cand * cols * 4 <= (1 << 20):
            tr = cand
            break

    def body(p_ref, w_ref, m_ref, v_ref, g_ref, d_ref, nm_ref, nv_ref):
        g = p_ref[0]
        for s in range(1, N_DEV):
            g = g + p_ref[s]
        nm = ADAM_B1 * m_ref[...] + (1.0 - ADAM_B1) * g
        nv = ADAM_B2 * v_ref[...] + (1.0 - ADAM_B2) * (g * g)
        m_hat = nm / (1.0 - ADAM_B1 ** ADAM_STEP)
        v_hat = nv / (1.0 - ADAM_B2 ** ADAM_STEP)
        g_ref[...] = g
        d_ref[...] = -ADAM_LR * (m_hat / (jnp.sqrt(v_hat) + ADAM_EPS) + ADAM_WD * w_ref[...])
        nm_ref[...] = nm
        nv_ref[...] = nv

    blk = pl.BlockSpec((tr, cols), lambda i: (i, 0))
    return pl.pallas_call(
        body, grid=(rows // tr,),
        in_specs=[pl.BlockSpec((N_DEV, tr, cols), lambda i: (0, i, 0)), blk, blk, blk],
        out_specs=[blk] * 4, out_shape=[jax.ShapeDtypeStruct((rows, cols), f32)] * 4,
        compiler_params=_params(("arbitrary",)), name=name)(parts, w, m, v)


WEIGHTS = ['l0_norm_mix', 'l0_w_in', 'ssd_conv_w', 'ssd_conv_b', 'ssd_dt_bias', 'ssd_A_log', 'ssd_D', 'ssd_norm_w',
           'l0_w_out', 'l0_norm_mlp', 'l0_w_up', 'l0_w_down', 'l1_norm_mix', 'l1_w_in', 'gdn_conv_w', 'gdn_A_log',
           'gdn_dt_bias', 'gdn_norm_w', 's5_A_re', 's5_A_im', 's5_log_step', 's5_B_re', 's5_B_im', 's5_C_re', 's5_C_im',
           's5_D', 's5_w_glu', 's5_b_glu', 'l1_w_out', 'l1_norm_mlp', 'l1_w_up', 'l1_w_down', 'final_norm']
SHARDED = ['l0_w_in', 'l0_w_out', 'l0_w_up', 'l0_w_down', 'l1_w_in', 's5_w_glu', 'l1_w_out', 'l1_w_up', 'l1_w_down',
           'ssd_conv_w', 'gdn_conv_w']
F32_GATHER = ('ssd_conv_w', 'gdn_conv_w')
REPLICATED = [n for n in WEIGHTS if n not in SHARDED]
INPUTS = ['x'] + WEIGHTS + ['loss_target'] + ['m_' + n for n in WEIGHTS] + ['v_' + n for n in WEIGHTS]


def _row(v):
    return v.reshape(1, -1)


def _pad_lanes(v, offset=0):
    return jnp.pad(v, (offset, LANES - offset - v.shape[0])).reshape(1, LANES)


def _cols_to_blocks(g):
    return g.reshape(g.shape[0], N_DEV, -1).transpose(1, 0, 2)


def _blocks_to_cols(g):
    return g.transpose(1, 0, 2).reshape(g.shape[1], -1)


def _pack(arrays):
    parts, slots, at = [], [], 0
    for a in arrays:
        n = a.size
        rows = -(-n // (8 * LANES)) * 8
        parts.append(jnp.pad(a.reshape(-1), (0, rows * LANES - n)).reshape(rows, LANES))
        slots.append((at, rows, n, a.shape))
        at += rows
    return jnp.concatenate(parts, axis=0), slots


def _unpack(buf, slots):
    return [buf[at:at + rows].reshape(-1)[:n].reshape(shape) for at, rows, n, shape in slots]


def kernel(*args):
    a = dict(zip(INPUTS, args, strict=True))
    seq_len = a['x'].shape[1]
    x0 = a['x'].reshape(seq_len, D_MODEL)
    target = a['loss_target'].reshape(seq_len, D_MODEL)

    shard = {n: a[n] if n in F32_GATHER else a[n].astype(_MXU) for n in SHARDED}
    first = ['l0_w_in', 'l0_w_out', 's5_w_glu', 'ssd_conv_w', 'gdn_conv_w']
    g = dict(zip(first, Exchange([shard[n] for n in first], gather=True).run("gather_first")))
    w_nat = _blocks_to_cols(g['l0_w_in'])
    win0 = jnp.concatenate([w_nat[:, :2048], w_nat[:, 2560:3584], w_nat[:, 2048:2560], w_nat[:, 3584:3592],
                            jnp.zeros((D_MODEL, IN0_PAD - IN0_W), _MXU)], axis=1)
    wout0 = g['l0_w_out'].reshape(D_MODEL, D_MODEL)
    wglu = g['s5_w_glu'].reshape(S5_CH, S5_CH)
    ssd_cw, gdn_cw = _blocks_to_cols(g['ssd_conv_w']), _blocks_to_cols(g['gdn_conv_w'])

    half = RET_D // 2
    inv = ROPE_THETA ** (-jnp.arange(half, dtype=f32) / half)
    ang = jnp.arange(seq_len, dtype=f32)[:, None] * inv[None, :]
    cos, sin = jnp.cos(ang), jnp.sin(ang)
    cos, sin = jnp.concatenate([cos, cos], axis=1), jnp.concatenate([-sin, sin], axis=1)
    ssd_params = [ssd_cw, _row(a['ssd_conv_b']), _pad_lanes(a['ssd_dt_bias']), _pad_lanes(a['ssd_A_log']),
                  _pad_lanes(a['ssd_D']), _row(a['ssd_norm_w'])]
    gdn_params = [gdn_cw, _pad_lanes(a['gdn_A_log'], GDN_GCOL), _pad_lanes(a['gdn_dt_bias'], GDN_GCOL), _row(a['gdn_norm_w'])]
    s5_raw = [a['s5_A_re'].reshape(1, S5_W), a['s5_A_im'].reshape(1, S5_W), _pad_lanes(a['s5_log_step']),
              a['s5_B_re'].transpose(2, 0, 1).reshape(S5_GROUP, S5_W), a['s5_B_im'].transpose(2, 0, 1).reshape(S5_GROUP, S5_W),
              a['s5_C_re'].transpose(1, 0, 2).reshape(S5_GROUP, S5_W), a['s5_C_im'].transpose(1, 0, 2).reshape(S5_GROUP, S5_W)]
    s5_d, s5_bg = _row(a['s5_D']), _row(a['s5_b_glu'])
    nw = {n: _row(a[n]) for n in ('l0_norm_mix', 'l0_norm_mlp', 'l1_norm_mix', 'l1_norm_mlp', 'final_norm')}

    proj0 = inproj_fwd("l0_in", x0, nw['l0_norm_mix'], win0)
    ret_seqs = [Seq(proj0, 512, 0), Seq(proj0, 512, 1), Seq(proj0, 512, 2), Seq(proj0, 512, 3),
                Seq(cos, LANES, 0, "const"), Seq(sin, LANES, 0, "const")]
    later = ['l0_w_up', 'l0_w_down']
    ret_out, ret_st, got = mixer_fwd("ret_fwd", ret_chunk, ret_seqs, [], 512, RET_HEADS * RET_D, seq_len,
                                     exchange=Exchange([shard[n] for n in later], gather=True))
    g.update(zip(later, got))
    ssd_seqs = [Seq(proj0, 512, 6), Seq(proj0, 1024, 2, "halo"), Seq(proj0, LANES, 28)]
    ssd_out, ssd_st, _ = mixer_fwd("ssd_fwd", ssd_chunk, ssd_seqs, ssd_params, SSD_INNER, SSD_INNER, seq_len)
    x1 = outproj_fwd("l0_out", x0, ret_out, ssd_out, wout0)
    later = ['l1_w_in', 'l1_w_out', 'l1_w_up', 'l1_w_down']
    x2, relu0, got = mlp_fwd("l0_mlp", x1, nw['l0_norm_mlp'], g['l0_w_up'], g['l0_w_down'],
                      exchange=Exchange([shard[n] for n in later], gather=True))
    g.update(zip(later, got))
    w_nat = g['l1_w_in'].reshape(D_MODEL, IN1_W)
    win1 = jnp.concatenate([w_nat[:, :3072], w_nat[:, 3084:3340], w_nat[:, 3072:3084],
                            jnp.zeros((D_MODEL, IN1_PAD - IN1_W), _MXU)], axis=1)
    wout1 = g['l1_w_out'].reshape(D_MODEL, D_MODEL)
    proj1 = inproj_fwd("l1_in", x2, nw['l1_norm_mix'], win1)
    gdn_seqs = [Seq(proj1, 3 * GDN_W, 0, "halo"), Seq(proj1, GDN_W, 3), Seq(proj1, LANES, 26)]
    gdn_out, gdn_st, _ = mixer_fwd("gdn_fwd", gdn_chunk, gdn_seqs, gdn_params, GDN_W, GDN_W, seq_len)
    prep = s5_prep_fwd("s5_prep", s5_raw)
    s5_out, h_re, h_im = s5_fwd("s5_fwd", proj1, 12, *prep, s5_d, wglu, s5_bg)
    x3 = outproj_fwd("l1_out", x2, gdn_out, s5_out, wout1)
    x4, relu1, _ = mlp_fwd("l1_mlp", x3, nw['l1_norm_mlp'], g['l1_w_up'], g['l1_w_down'])
    loss_blk, dx4, d_final = final_loss("final_loss", x4, nw['final_norm'], target)

    parts = {}
    dx3, d_up1, d_down1, d_nmlp1, _ = mlp_bwd("l1_mlp_bwd", x3, nw['l1_norm_mlp'], g['l1_w_up'], g['l1_w_down'], relu1, dx4)
    d_gdn, d_s5, d_wout1 = outproj_bwd("l1_out_bwd", dx3, gdn_out, s5_out, wout1)
    d_u, s5_g = s5_bwd("s5_bwd", proj1, 12, h_re, h_im, d_s5, *prep, s5_d, wglu, s5_bg)
    s5_raw_g = s5_prep_bwd("s5_prep_bwd", s5_raw, s5_g[:6])
    ready = {'l1_w_up': d_up1, 'l1_w_down': d_down1, 'l1_w_out': d_wout1.reshape(N_DEV, -1, D_MODEL),
             's5_w_glu': s5_g[7].reshape(N_DEV, -1, S5_CH)}
    (d_qkv, d_z1, d_ba), gdn_pg, got = mixer_bwd("gdn_bwd", gdn_chunk, gdn_seqs, gdn_params, gdn_st, d_gdn, seq_len,
                                                 exchange=Exchange(list(ready.values()), gather=False))
    parts.update(zip(ready, got))
    dx2, d_win1, d_nmix1 = inproj_bwd("l1_in_bwd", x2, nw['l1_norm_mix'], win1, [d_qkv, d_z1, d_u, d_ba], dx3)
    d_win1 = jnp.concatenate([d_win1[:, :3072], d_win1[:, 3328:3340], d_win1[:, 3072:3328]], axis=1)
    ready = {'l1_w_in': d_win1.reshape(N_DEV, -1, IN1_W), 'gdn_conv_w': _cols_to_blocks(gdn_pg[0])}
    dx1, d_up0, d_down0, d_nmlp0, got = mlp_bwd("l0_mlp_bwd", x1, nw['l0_norm_mlp'], g['l0_w_up'], g['l0_w_down'], relu0, dx2,
                                                exchange=Exchange(list(ready.values()), gather=False))
    parts.update(zip(ready, got))
    d_ret, d_ssd, d_wout0 = outproj_bwd("l0_out_bwd", dx1, ret_out, ssd_out, wout0)
    d_qkvg, _, _ = mixer_bwd("ret_bwd", ret_chunk, ret_seqs, [], ret_st, d_ret, seq_len)
    ready = {'l0_w_up': d_up0, 'l0_w_down': d_down0, 'l0_w_out': d_wout0.reshape(N_DEV, -1, D_MODEL)}
    (d_z0, d_xbc, d_dt), ssd_pg, got = mixer_bwd("ssd_bwd", ssd_chunk, ssd_seqs, ssd_params, ssd_st, d_ssd, seq_len,
                                                 exchange=Exchange(list(ready.values()), gather=False))
    parts.update(zip(ready, got))
    dx0, d_win0, d_nmix0 = inproj_bwd("l0_in_bwd", x0, nw['l0_norm_mix'], win0, list(d_qkvg) + [d_xbc, d_z0, d_dt], dx1)

    d_win0 = jnp.concatenate([d_win0[:, :2048], d_win0[:, 3072:3584], d_win0[:, 2048:3072], d_win0[:, 3584:3592]], axis=1)
    ready = {'l0_w_in': _cols_to_blocks(d_win0), 'ssd_conv_w': _cols_to_blocks(ssd_pg[0])}
    from_b = lambda t: t.reshape(S5_GROUP, S5_GROUPS, S5_STATE).transpose(1, 2, 0)
    from_c = lambda t: t.reshape(S5_GROUP, S5_GROUPS, S5_STATE).transpose(1, 0, 2)
    replicated_g = {
        'l0_norm_mix': d_nmix0, 'ssd_conv_b': ssd_pg[1], 'ssd_dt_bias': ssd_pg[2][0, :SSD_HEADS], 'ssd_A_log': ssd_pg[3][0, :SSD_HEADS],
        'ssd_D': ssd_pg[4][0, :SSD_HEADS], 'ssd_norm_w': ssd_pg[5], 'l0_norm_mlp': d_nmlp0, 'l1_norm_mix': d_nmix1,
        'gdn_A_log': gdn_pg[1][0, GDN_GCOL:GDN_GCOL + GDN_HEADS], 'gdn_dt_bias': gdn_pg[2][0, GDN_GCOL:GDN_GCOL + GDN_HEADS],
        'gdn_norm_w': gdn_pg[3], 's5_A_re': s5_raw_g[0], 's5_A_im': s5_raw_g[1], 's5_log_step': s5_raw_g[2][0, :S5_GROUPS],
        's5_B_re': from_b(s5_raw_g[3]), 's5_B_im': from_b(s5_raw_g[4]), 's5_C_re': from_c(s5_raw_g[5]), 's5_C_im': from_c(s5_raw_g[6]),
        's5_D': s5_g[6], 's5_b_glu': s5_g[8], 'l1_norm_mlp': d_nmlp1, 'final_norm': d_final}
    replicated_g = {n: replicated_g[n].reshape(a[n].shape) for n in REPLICATED}

    parts.update(zip(ready, Exchange(list(ready.values()), gather=False).run("scatter_last")))
    packed_g, slots = _pack([replicated_g[n] for n in REPLICATED])
    (packed_parts,) = Exchange([packed_g], gather=True).run("gather_small_grads")
    results = {}
    for n in SHARDED:
        results[n] = adamw("adamw_" + n, parts[n], a[n], a['m_' + n], a['v_' + n])
    packed = [_pack([a[pre + n] for n in REPLICATED])[0] for pre in ('', 'm_', 'v_')]
    small = [_unpack(t, slots) for t in adamw("adamw_small", packed_parts, *packed)]
    for i, n in enumerate(REPLICATED):
        results[n] = tuple(small[k][i] for k in range(4))

    loss = lax.psum(loss_blk[0, 0], ("x", "y", "c"))
    grad_x = dx0.reshape(a['x'].shape)
    return (loss, grad_x, *[results[n][0] for n in WEIGHTS], *[results[n][1] for n in WEIGHTS],
            *[results[n][2] for n in WEIGHTS], *[results[n][3] for n in WEIGHTS])
```

```python
import functools
import math

import numpy as np
import jax
import jax.numpy as jnp
from jax import lax
from jax.experimental import pallas as pl
from jax.experimental.pallas import tpu as pltpu

f32 = jnp.float32
_MXU = jnp.bfloat16
HI = lax.Precision.HIGHEST

D_MODEL = 1024
CHUNK = 64
EPS = 1e-6
N_DEV = 8
LANES = 128
HALO = 8
CONV_WIDTH = 4

RET_HEADS, RET_D = 4, 128
SSD_HEADS, SSD_P, SSD_N, SSD_GROUPS = 8, 64, 128, 2
SSD_INNER = SSD_HEADS * SSD_P
GDN_HEADS, GDN_D = 6, 128
GDN_W = GDN_HEADS * GDN_D
S5_CH, S5_GROUP, S5_GROUPS, S5_STATE = 256, 16, 16, 64
S5_W = S5_GROUPS * S5_STATE
D_FF = 4096
ROPE_THETA = 10000.0

IN0_W = 3592
IN0_PAD = 3712
IN1_W = 3340
IN1_PAD = 3456

ADAM_LR, ADAM_B1, ADAM_B2, ADAM_EPS, ADAM_WD, ADAM_STEP = 0.001, 0.9, 0.999, 1e-08, 0.01, 10

VMEM_LIMIT = 56 * 1024 * 1024


def _dot(a, b, dims):
    return lax.dot_general(a.astype(_MXU), b.astype(_MXU), (dims, ((), ())), preferred_element_type=f32)


@jax.custom_vjp
def mm(a, b):
    return _dot(a, b, ((1,), (0,)))


@jax.custom_vjp
def mm_nt(a, b):
    return _dot(a, b, ((1,), (1,)))


@jax.custom_vjp
def mm_tn(a, b):
    return _dot(a, b, ((0,), (0,)))


mm.defvjp(lambda a, b: (mm(a, b), (a, b)), lambda r, g: (mm_nt(g, r[1]), mm_tn(r[0], g)))
mm_nt.defvjp(lambda a, b: (mm_nt(a, b), (a, b)), lambda r, g: (mm(g, r[1]), mm_tn(g, r[0])))
mm_tn.defvjp(lambda a, b: (mm_tn(a, b), (a, b)), lambda r, g: (mm_nt(r[1], g), mm(r[0], g)))


def mmh(a, b):
    return jnp.dot(a, b, precision=HI, preferred_element_type=f32)


def _exact01(x, m01, dims, m_first):
    hi = x.astype(jnp.bfloat16)
    r = x - hi.astype(f32)
    mid = r.astype(jnp.bfloat16)
    lo = (r - mid.astype(f32)).astype(jnp.bfloat16)
    m = m01.astype(jnp.bfloat16)
    dot = lambda p: lax.dot_general(m, p, (dims, ((), ())), preferred_element_type=f32) if m_first else \
        lax.dot_general(p, m, (dims, ((), ())), preferred_element_type=f32)
    return dot(hi) + dot(mid) + dot(lo)


@jax.custom_vjp
def spread01(x, sel):
    return _exact01(x, sel, ((1,), (0,)), False)


spread01.defvjp(lambda x, sel: (spread01(x, sel), sel),
                lambda sel, g: (_exact01(g, sel, ((1,), (1,)), False), jnp.zeros_like(sel)))


@jax.custom_vjp
def cumsum01(tril, x):
    return _exact01(x, tril, ((1,), (0,)), True)


cumsum01.defvjp(lambda tril, x: (cumsum01(tril, x), tril),
                lambda tril, g: (jnp.zeros_like(tril), _exact01(g, tril, ((0,), (0,)), True)))


def _roll(x, shift, axis):
    return pltpu.roll(x, shift, axis)


@functools.partial(jax.custom_vjp, nondiff_argnums=(1,))
def roll_rows(x, s):
    return _roll(x, s, 0) if s else x


roll_rows.defvjp(lambda x, s: (roll_rows(x, s), None),
                 lambda s, _, g: ((_roll(g, g.shape[0] - s, 0) if s else g),))


@jax.custom_vjp
def roll_half(x):
    return _roll(x, x.shape[-1] // 2, 1)


roll_half.defvjp(lambda x: (roll_half(x), None), lambda _, g: (roll_half(g),))


def _iota(shape, axis):
    return lax.broadcasted_iota(jnp.int32, shape, axis)


def silu(x):
    return x * jax.nn.sigmoid(x)


def softplus(x):
    return jnp.maximum(x, 0.0) + jnp.log(1.0 + jnp.exp(-jnp.abs(x)))


def rmsnorm_f(x, w):
    return x * lax.rsqrt(jnp.mean(x * x, axis=-1, keepdims=True) + EPS) * w


def unit_rms(x):
    return x * lax.rsqrt(jnp.mean(x * x, axis=-1, keepdims=True) + EPS)


def _causal(n, strict=False):
    r, c = _iota((n, n), 0), _iota((n, n), 1)
    return (r > c) if strict else (r >= c)


def _tril_ones(n):
    return _causal(n).astype(f32)


def _conv_rows(xe, w):
    acc = w[CONV_WIDTH - 1:CONV_WIDTH, :] * xe
    for j in range(CONV_WIDTH - 1):
        acc = acc + w[j:j + 1, :] * roll_rows(xe, CONV_WIDTH - 1 - j)
    return acc[HALO:, :]


_RET_LOG_GAMMA = [float(np.log(np.float32(1.0) - np.float32(2.0) ** np.float32(-5.0 - h))) for h in range(RET_HEADS)]


def ret_chunk(q, k, v, gate, cos, sin, state):
    c = q.shape[0]
    idx = _iota((c, 1), 0).astype(f32)
    diff = (_iota((c, c), 0) - _iota((c, c), 1)).astype(f32)
    causal = _causal(c)
    hs = range(RET_HEADS)
    cols = [slice(h * RET_D, (h + 1) * RET_D) for h in hs]
    lg = _RET_LOG_GAMMA
    qh = [(q[:, s] * cos + roll_half(q[:, s]) * sin) * (RET_D ** -0.5) for s in cols]
    kh = [k[:, s] * cos + roll_half(k[:, s]) * sin for s in cols]
    vh = [v[:, s] for s in cols]
    sh = [state[s, :] for s in cols]
    scores = [mm_nt(qh[h], kh[h]) * jnp.exp(jnp.where(causal, lg[h] * diff, -jnp.inf)) for h in hs]
    inter = [mm(qh[h] * jnp.exp(lg[h] * (idx + 1.0)), sh[h]) for h in hs]
    y = [mm(scores[h], vh[h]) + inter[h] for h in hs]
    states = [sh[h] * math.exp(lg[h] * c) + mm_tn(kh[h] * jnp.exp(lg[h] * (c - 1.0 - idx)), vh[h]) for h in hs]
    outs = [unit_rms(y[h]) * silu(gate[:, cols[h]]) for h in hs]
    return jnp.concatenate(outs, axis=1), jnp.concatenate(states, axis=0)


def _head_select(n_heads, width):
    r, c = _iota((LANES, n_heads * width), 0), _iota((LANES, n_heads * width), 1)
    return (c // width == r).astype(f32)


def ssd_chunk(z, xe, dtr, state, conv_w, conv_b, dt_bias, a_log, d_skip, norm_w):
    c = z.shape[0]
    xbc = silu(_conv_rows(xe, conv_w) + conv_b)
    xs, bm, cm = xbc[:, :SSD_INNER], xbc[:, SSD_INNER:SSD_INNER + 256], xbc[:, SSD_INNER + 256:]
    sel = _head_select(SSD_HEADS, SSD_P)
    dt = softplus(dtr + dt_bias)
    la = dt * (-jnp.exp(a_log))
    la_cum = cumsum01(_tril_ones(c), la)
    la_cum_t = la_cum.T
    last = jnp.sum(la, axis=0, keepdims=True)
    xd = xs * spread01(dt, sel)
    la_x = spread01(la_cum, sel)
    last_x = spread01(last, sel)
    to_end = jnp.exp(last_x - la_x)
    from_start = jnp.exp(la_x)
    causal = _causal(c)
    left = (_iota((1, LANES), 1) < SSD_P).astype(f32)
    upper = _iota((LANES, 1), 0) < SSD_P
    pairs, heads = range(SSD_HEADS // 2), range(SSD_HEADS)
    bc = [bm[:, g * SSD_N:(g + 1) * SSD_N] for g in range(SSD_GROUPS)]
    cc = [cm[:, g * SSD_N:(g + 1) * SSD_N] for g in range(SSD_GROUPS)]
    cb = [mm_nt(cc[g], bc[g]) for g in range(SSD_GROUPS)]
    cols = [slice(p * LANES, (p + 1) * LANES) for p in pairs]
    xd_p = [xd[:, s] for s in cols]
    sp = [state[s, :] for s in cols]
    lmat = [jnp.exp(jnp.where(causal, la_cum[:, h:h + 1] - la_cum_t[h:h + 1, :], -jnp.inf)) for h in heads]
    off = [mm_nt(cc[p // 2], sp[p]) * from_start[:, cols[p]] for p in pairs]
    diag = [mm(cb[h // 4] * lmat[h], xd_p[h // 2] * (left if h % 2 == 0 else 1.0 - left)) for h in heads]
    cd = [jnp.where(upper, jnp.exp(last[:, 2 * p:2 * p + 1]), jnp.exp(last[:, 2 * p + 1:2 * p + 2])) for p in pairs]
    states = [sp[p] * cd[p] + mm_tn(xd_p[p] * to_end[:, cols[p]], bc[p // 2]) for p in pairs]
    ys = [off[p] + diag[2 * p] + diag[2 * p + 1] for p in pairs]
    y = jnp.concatenate(ys, axis=1) + spread01(d_skip, sel) * xs
    yg = y * silu(z)
    half = SSD_INNER // SSD_GROUPS
    out = jnp.concatenate([unit_rms(yg[:, i * half:(i + 1) * half]) for i in range(SSD_GROUPS)], axis=1) * norm_w
    return out, jnp.concatenate(states, axis=0)


def mm3(a, b):
    return jnp.dot(a, b, precision=lax.Precision.HIGH, preferred_element_type=f32)


@jax.custom_vjp
def _unit_lower_inverses(lowers):
    n = lowers[0].shape[0]
    eye = (_iota((n, n), 0) == _iota((n, n), 1)).astype(f32)
    a = [-l for l in lowers]
    p = [eye + x for x in a]
    k = 2
    while k < n:
        a = [mm3(x, x) for x in a]
        p = [y + mm3(y, x) for y, x in zip(p, a)]
        k *= 2
    return p


def _unit_lower_inverses_bwd(t_inv, g):
    dims_tn, dims_nt = (((0,), (0,)), ((), ())), (((1,), (1,)), ((), ()))
    x = [lax.dot_general(t, gi, dims_tn, precision=lax.Precision.HIGH, preferred_element_type=f32) for t, gi in zip(t_inv, g)]
    return ([-lax.dot_general(xi, t, dims_nt, precision=lax.Precision.HIGH, preferred_element_type=f32) for xi, t in zip(x, t_inv)],)


def _unit_lower_inverses_fwd(lowers):
    t_inv = _unit_lower_inverses(lowers)
    return t_inv, t_inv


_unit_lower_inverses.defvjp(_unit_lower_inverses_fwd, _unit_lower_inverses_bwd)


GDN_GCOL = 6


def gdn_chunk(xe, z, ba, state, conv_w, a_log, dt_bias, norm_w):
    c = z.shape[0]
    qkv = silu(_conv_rows(xe, conv_w))
    beta_all = jax.nn.sigmoid(ba)
    g_all = -jnp.exp(a_log) * softplus(ba + dt_bias)
    gc = cumsum01(_tril_ones(c), g_all)
    gc_t = gc.T
    last = jnp.sum(g_all, axis=0, keepdims=True)
    causal, strict = _causal(c), _causal(c, strict=True)
    hs = range(GDN_HEADS)
    cols = [slice(h * GDN_D, (h + 1) * GDN_D) for h in hs]
    qh = [qkv[:, h * GDN_D:(h + 1) * GDN_D] for h in hs]
    kh = [qkv[:, GDN_W + h * GDN_D:GDN_W + (h + 1) * GDN_D] for h in hs]
    vh = [qkv[:, 2 * GDN_W + h * GDN_D:2 * GDN_W + (h + 1) * GDN_D] for h in hs]
    qh = [t * lax.rsqrt(jnp.sum(t * t, axis=-1, keepdims=True) + EPS) * (GDN_D ** -0.5) for t in qh]
    kh = [t * lax.rsqrt(jnp.sum(t * t, axis=-1, keepdims=True) + EPS) for t in kh]
    beta = [beta_all[:, h:h + 1] for h in hs]
    col = [gc[:, GDN_GCOL + h:GDN_GCOL + h + 1] for h in hs]
    row = [gc_t[GDN_GCOL + h:GDN_GCOL + h + 1, :] for h in hs]
    lst = [last[:, GDN_GCOL + h:GDN_GCOL + h + 1] for h in hs]
    decay = [jnp.exp(jnp.where(causal, col[h] - row[h], -jnp.inf)) for h in hs]
    e_col = [jnp.exp(t) for t in col]
    kb = [kh[h] * beta[h] for h in hs]
    vb = [vh[h] * beta[h] for h in hs]
    kk = [mm_nt(kb[h], kh[h]) for h in hs]
    qk = [mm_nt(qh[h], kh[h]) for h in hs]
    t_inv = _unit_lower_inverses([jnp.where(strict, kk[h] * decay[h], 0.0) for h in hs])
    u = [mm(t_inv[h], vb[h]) for h in hs]
    w = [mm(t_inv[h], kb[h] * e_col[h]) for h in hs]
    attn = [jnp.where(causal, qk[h] * decay[h], 0.0) for h in hs]
    sh = [state[s, :] for s in cols]
    ws = [mm(w[h], sh[h]) for h in hs]
    qs = [mm(qh[h] * e_col[h], sh[h]) for h in hs]
    v_new = [u[h] - ws[h] for h in hs]
    o = [qs[h] + mm(attn[h], v_new[h]) for h in hs]
    states = [sh[h] * jnp.exp(lst[h]) + mm_tn(kh[h] * jnp.exp(lst[h] - col[h]), v_new[h]) for h in hs]
    outs = [unit_rms(o[h]) * norm_w * silu(z[:, cols[h]]) for h in hs]
    return jnp.concatenate(outs, axis=1), jnp.concatenate(states, axis=0)


def _s5_group_mask():
    r, c = _iota((S5_CH, S5_W), 0), _iota((S5_CH, S5_W), 1)
    return (r // S5_GROUP == c // S5_STATE).astype(f32)


def s5_prep(a_re, a_im, log_step, b_re, b_im, c_re, c_im):
    r, c = _iota((LANES, S5_W), 0), _iota((LANES, S5_W), 1)
    step = jnp.exp(mmh(log_step, (c // S5_STATE == r).astype(f32)))
    zr, zi = a_re * step, a_im * step
    e = jnp.exp(zr)
    lr, li = e * jnp.cos(zi), e * jnp.sin(zi)
    den = a_re * a_re + a_im * a_im
    xr, xi = lr - 1.0, li
    cr, ci = (xr * a_re + xi * a_im) / den, (xi * a_re - xr * a_im) / den
    bbr, bbi = cr * b_re - ci * b_im, cr * b_im + ci * b_re
    mask = _s5_group_mask()
    tile = lambda t: jnp.tile(t, (S5_GROUPS, 1)) * mask
    return lr, li, tile(bbr), tile(bbi), tile(c_re), tile(c_im)


def s5_out_fn(h_re, h_im, u, cblk_re, cblk_im, d_skip, w_glu, b_glu):
    y = mm_nt(h_re, cblk_re) - mm_nt(h_im, cblk_im) + d_skip * u
    y = jax.nn.gelu(y)
    return y * jax.nn.sigmoid(mm(y, w_glu) + b_glu)


def _params(sem, **kw):
    return pltpu.CompilerParams(dimension_semantics=sem, vmem_limit_bytes=VMEM_LIMIT, **kw)


def _const_spec(shape):
    return pl.BlockSpec(shape, lambda i: (0,) * len(shape))


def _resident_spec(shape):
    return pl.BlockSpec(shape, lambda i: (0,) * len(shape), pipeline_mode=pl.Buffered(1))


def _row_spec(rows, cols, col_block=0):
    return pl.BlockSpec((rows, cols), lambda i: (i, col_block))


class Seq:
    def __init__(self, array, width, col_block, kind="tile"):
        self.array, self.width, self.col_block, self.kind = array, width, col_block, kind


CHUNKS_PER_STEP = 4


def mixer_fwd(name, fn, seqs, params, out_width, state_rows, seq_len, per_step=CHUNKS_PER_STEP, exchange=None):
    nc = seq_len // CHUNK
    rows, steps = per_step * CHUNK, nc // per_step
    n_refs = sum(2 if s.kind == "halo" else 1 for s in seqs)
    n_par = len(params)
    car = Carried(exchange)

    def body(*refs):
        seq_refs, par_refs = refs[:n_refs], refs[n_refs:n_refs + n_par]
        k0 = n_refs + n_par
        ex_ins, (out_ref, st_ref) = refs[k0:k0 + car.n], refs[k0 + car.n:k0 + car.n + 2]
        ex_outs, state, ex_sems = refs[k0 + car.n + 2:k0 + 2 * car.n + 2], refs[k0 + 2 * car.n + 2], refs[k0 + 2 * car.n + 3:]
        c = pl.program_id(0)
        car.start_at(c == 0, ex_ins, ex_outs, ex_sems)

        @pl.when(c == 0)
        def _():
            state[...] = jnp.zeros_like(state)

        par_vals = [p[...] for p in par_refs]
        s_cur = state[...]
        for kk in range(per_step):
            lo = kk * CHUNK
            vals, k = [], 0
            for s in seqs:
                if s.kind == "halo":
                    prev = jnp.where(c > 0, seq_refs[k][...], 0.0) if kk == 0 else seq_refs[k + 1][lo - HALO:lo, :]
                    vals.append(jnp.concatenate([prev, seq_refs[k + 1][lo:lo + CHUNK, :]], axis=0))
                    k += 2
                else:
                    vals.append(seq_refs[k][lo:lo + CHUNK, :])
                    k += 1
            st_ref[kk] = s_cur
            out, s_cur = fn(*vals, s_cur, *par_vals)
            out_ref[lo:lo + CHUNK, :] = out
        state[...] = s_cur
        car.wait_at(c == steps - 1, ex_ins, ex_outs, ex_sems)

    in_specs, operands = [], []
    for s in seqs:
        if s.kind == "halo":
            rb, w, cb = rows // HALO, s.width, s.col_block
            in_specs.append(pl.BlockSpec((HALO, w), lambda i, rb=rb, cb=cb: (jnp.maximum(i * rb - 1, 0), cb)))
            operands.append(s.array)
        in_specs.append(pl.BlockSpec((rows, s.width), lambda i, cb=s.col_block: (i, cb)))
        operands.append(s.array)
    for p in params:
        in_specs.append(_const_spec(p.shape))
        operands.append(p)
    outs = pl.pallas_call(
        body, grid=(steps,), in_specs=in_specs + car.in_specs,
        out_specs=[pl.BlockSpec((rows, out_width), lambda i: (i, 0)),
                   pl.BlockSpec((per_step, state_rows, LANES), lambda i: (i, 0, 0))] + car.out_specs,
        out_shape=[jax.ShapeDtypeStruct((seq_len, out_width), f32),
                   jax.ShapeDtypeStruct((nc, state_rows, LANES), f32)] + car.out_shape,
        scratch_shapes=[pltpu.VMEM((state_rows, LANES), f32)] + car.scratch,
        compiler_params=_params(("arbitrary",)), name=name)(*operands, *car.operands)
    return outs[0], outs[1], outs[2:]


def mixer_bwd(name, fn, seqs, params, states, d_out, seq_len, per_step=CHUNKS_PER_STEP, exchange=None):
    nc = seq_len // CHUNK
    rows, steps = per_step * CHUNK, nc // per_step
    state_rows = states.shape[1]
    diff = [s for s in seqs if s.kind != "const"]
    halos = [s for s in diff if s.kind == "halo"]
    n_refs = sum(2 if s.kind == "halo" else 1 for s in seqs)
    n_par = len(params)
    car = Carried(exchange)

    def body(*refs):
        seq_refs, par_refs = refs[:n_refs], refs[n_refs:n_refs + n_par]
        st_ref, dout_ref = refs[n_refs + n_par:n_refs + n_par + 2]
        k0 = n_refs + n_par + 2
        ex_ins = refs[k0:k0 + car.n]
        k0 += car.n
        dseq_refs, dpar_refs = refs[k0:k0 + len(diff)], refs[k0 + len(diff):k0 + len(diff) + n_par]
        k0 += len(diff) + n_par
        ex_outs = refs[k0:k0 + car.n]
        scratch = refs[k0 + car.n:]
        d_state, carries, ex_sems = scratch[0], scratch[1:1 + len(halos)], scratch[1 + len(halos):]
        i = pl.program_id(0)
        step = steps - 1 - i
        car.start_at(i == 0, ex_ins, ex_outs, ex_sems)

        @pl.when(i == 0)
        def _():
            d_state[...] = jnp.zeros_like(d_state)
            for r in list(carries) + list(dpar_refs):
                r[...] = jnp.zeros_like(r)

        par_vals = [p[...] for p in par_refs]
        d_s = d_state[...]
        d_par = [None] * n_par
        halo_ct = [r[...] for r in carries]
        for kk in range(per_step - 1, -1, -1):
            lo = kk * CHUNK
            dvals, consts, k = [], [], 0
            for s in seqs:
                if s.kind == "halo":
                    prev = jnp.where(step > 0, seq_refs[k][...], 0.0) if kk == 0 else seq_refs[k + 1][lo - HALO:lo, :]
                    dvals.append(jnp.concatenate([prev, seq_refs[k + 1][lo:lo + CHUNK, :]], axis=0))
                    k += 2
                elif s.kind == "tile":
                    dvals.append(seq_refs[k][lo:lo + CHUNK, :])
                    k += 1
                else:
                    consts.append(seq_refs[k][lo:lo + CHUNK, :])
                    k += 1
            nd = len(dvals)

            def call(*a, consts=consts, nd=nd):
                it_d, it_c = iter(a[:nd]), iter(consts)
                vals = [next(it_c) if s.kind == "const" else next(it_d) for s in seqs]
                return fn(*vals, *a[nd:])

            _, vjp = jax.vjp(call, *dvals, st_ref[kk], *par_vals)
            cts = vjp((dout_ref[lo:lo + CHUNK, :], d_s))
            hk = 0
            for j, s in enumerate(diff):
                if s.kind == "halo":
                    dseq_refs[j][lo:lo + CHUNK, :] = cts[j][HALO:, :]
                    dseq_refs[j][lo + CHUNK - HALO:lo + CHUNK, :] += halo_ct[hk]
                    halo_ct[hk] = cts[j][:HALO, :]
                    hk += 1
                else:
                    dseq_refs[j][lo:lo + CHUNK, :] = cts[j]
            d_s = cts[nd]
            for j in range(n_par):
                d_par[j] = cts[nd + 1 + j] if d_par[j] is None else d_par[j] + cts[nd + 1 + j]
        d_state[...] = d_s
        for r, v in zip(carries, halo_ct):
            r[...] = v
        for j in range(n_par):
            dpar_refs[j][...] += d_par[j]
        car.wait_at(i == steps - 1, ex_ins, ex_outs, ex_sems)

    in_specs, operands = [], []
    for s in seqs:
        if s.kind == "halo":
            rb, cb = rows // HALO, s.col_block
            in_specs.append(pl.BlockSpec((HALO, s.width), lambda i, rb=rb, cb=cb: (jnp.maximum((steps - 1 - i) * rb - 1, 0), cb)))
            operands.append(s.array)
        in_specs.append(pl.BlockSpec((rows, s.width), lambda i, cb=s.col_block: (steps - 1 - i, cb)))
        operands.append(s.array)
    for p in params:
        in_specs.append(_const_spec(p.shape))
        operands.append(p)
    in_specs.append(pl.BlockSpec((per_step, state_rows, LANES), lambda i: (steps - 1 - i, 0, 0)))
    in_specs.append(pl.BlockSpec((rows, d_out.shape[1]), lambda i: (steps - 1 - i, 0)))
    operands += [states, d_out]
    outs = pl.pallas_call(
        body, grid=(steps,), in_specs=in_specs + car.in_specs,
        out_specs=[pl.BlockSpec((rows, s.width), lambda i: (steps - 1 - i, 0)) for s in diff]
        + [_const_spec(p.shape) for p in params] + car.out_specs,
        out_shape=[jax.ShapeDtypeStruct((seq_len, s.width), f32) for s in diff]
        + [jax.ShapeDtypeStruct(p.shape, f32) for p in params] + car.out_shape,
        scratch_shapes=[pltpu.VMEM((state_rows, LANES), f32)] + [pltpu.VMEM((HALO, s.width), f32) for s in halos]
        + car.scratch,
        compiler_params=_params(("arbitrary",)), name=name)(*operands, *car.operands)
    nd = len(diff)
    return outs[:nd], outs[nd:nd + n_par], outs[nd + n_par:]


TOK = 256


def inproj_fwd(name, x, nw, w):
    seq_len, n = x.shape[0], w.shape[1]

    def body(x_ref, nw_ref, w_ref, o_ref):
        o_ref[...] = mm(rmsnorm_f(x_ref[...], nw_ref[...]), w_ref[...])

    return pl.pallas_call(
        body, grid=(seq_len // TOK,),
        in_specs=[_row_spec(TOK, D_MODEL), _const_spec(nw.shape), _resident_spec(w.shape)],
        out_specs=_row_spec(TOK, n), out_shape=jax.ShapeDtypeStruct((seq_len, n), f32),
        compiler_params=_params(("arbitrary",)), name=name)(x, nw, w)


def inproj_bwd(name, x, nw, w, pieces, d_res):
    seq_len, n = x.shape[0], w.shape[1]
    widths = [p.shape[1] for p in pieces]
    assert sum(widths) == n
    k = len(pieces)

    def body(*refs):
        x_ref, nw_ref, w_ref = refs[:3]
        p_refs, dres_ref = refs[3:3 + k], refs[3 + k]
        dx_ref, dw_ref, dnw_ref = refs[4 + k:]

        @pl.when(pl.program_id(0) == 0)
        def _():
            dw_ref[...] = jnp.zeros_like(dw_ref)
            dnw_ref[...] = jnp.zeros_like(dnw_ref)

        h, vjp = jax.vjp(rmsnorm_f, x_ref[...], nw_ref[...])
        dh, off = jnp.zeros_like(h), 0
        for p_ref, wd in zip(p_refs, widths):
            g = p_ref[...]
            dh = dh + mm_nt(g, w_ref[:, off:off + wd])
            dw_ref[:, off:off + wd] += mm_tn(h, g)
            off += wd
        dx, dnw = vjp(dh)
        dx_ref[...] = dres_ref[...] + dx
        dnw_ref[...] += dnw

    return pl.pallas_call(
        body, grid=(seq_len // TOK,),
        in_specs=[_row_spec(TOK, D_MODEL), _const_spec(nw.shape), _resident_spec(w.shape)]
        + [_row_spec(TOK, wd) for wd in widths] + [_row_spec(TOK, D_MODEL)],
        out_specs=[_row_spec(TOK, D_MODEL), _resident_spec((D_MODEL, n)), _const_spec(nw.shape)],
        out_shape=[jax.ShapeDtypeStruct((seq_len, D_MODEL), f32), jax.ShapeDtypeStruct((D_MODEL, n), f32),
                   jax.ShapeDtypeStruct(nw.shape, f32)],
        compiler_params=_params(("arbitrary",)), name=name)(x, nw, w, *pieces, d_res)


def outproj_fwd(name, x, a, b, w):
    seq_len, wa, wb = x.shape[0], a.shape[1], b.shape[1]

    def body(x_ref, a_ref, b_ref, w_ref, o_ref):
        o_ref[...] = x_ref[...] + mm(a_ref[...], w_ref[:wa, :]) + mm(b_ref[...], w_ref[wa:, :])

    return pl.pallas_call(
        body, grid=(seq_len // TOK,),
        in_specs=[_row_spec(TOK, D_MODEL), _row_spec(TOK, wa), _row_spec(TOK, wb), _resident_spec(w.shape)],
        out_specs=_row_spec(TOK, D_MODEL), out_shape=jax.ShapeDtypeStruct((seq_len, D_MODEL), f32),
        compiler_params=_params(("arbitrary",)), name=name)(x, a, b, w)


def outproj_bwd(name, dy, a, b, w):
    seq_len, wa, wb = dy.shape[0], a.shape[1], b.shape[1]

    def body(dy_ref, a_ref, b_ref, w_ref, da_ref, db_ref, dw_ref):
        @pl.when(pl.program_id(0) == 0)
        def _():
            dw_ref[...] = jnp.zeros_like(dw_ref)

        g = dy_ref[...]
        da_ref[...] = mm_nt(g, w_ref[:wa, :])
        db_ref[...] = mm_nt(g, w_ref[wa:, :])
        dw_ref[:wa, :] += mm_tn(a_ref[...], g)
        dw_ref[wa:, :] += mm_tn(b_ref[...], g)

    return pl.pallas_call(
        body, grid=(seq_len // TOK,),
        in_specs=[_row_spec(TOK, D_MODEL), _row_spec(TOK, wa), _row_spec(TOK, wb), _resident_spec(w.shape)],
        out_specs=[_row_spec(TOK, wa), _row_spec(TOK, wb), _resident_spec(w.shape)],
        out_shape=[jax.ShapeDtypeStruct((seq_len, wa), f32), jax.ShapeDtypeStruct((seq_len, wb), f32),
                   jax.ShapeDtypeStruct(w.shape, f32)],
        compiler_params=_params(("arbitrary",)), name=name)(dy, a, b, w)


FF_BLOCK = D_FF // N_DEV


def mlp_fwd(name, x, nw, w_up, w_down, exchange=None):
    seq_len = x.shape[0]
    nt = seq_len // TOK
    car = Carried(exchange)

    def body(*refs):
        x_ref, nw_ref, up_ref, down_ref = refs[:4]
        ex_ins, (o_ref, relu_ref) = refs[4:4 + car.n], refs[4 + car.n:6 + car.n]
        ex_outs, ex_sems = refs[6 + car.n:6 + 2 * car.n], refs[6 + 2 * car.n:]
        i = pl.program_id(0)
        car.start_at(i == 0, ex_ins, ex_outs, ex_sems)
        xv = x_ref[...]
        h = rmsnorm_f(xv, nw_ref[...])
        acc = xv
        for d in range(N_DEV):
            r = jnp.maximum(mm(h, up_ref[d]), 0.0)
            relu_ref[d] = r.astype(_MXU)
            acc = acc + mm(r * r, down_ref[d])
        o_ref[...] = acc
        car.wait_at(i == nt - 1, ex_ins, ex_outs, ex_sems)

    outs = pl.pallas_call(
        body, grid=(nt,),
        in_specs=[_row_spec(TOK, D_MODEL), _const_spec(nw.shape), _resident_spec(w_up.shape), _resident_spec(w_down.shape)]
        + car.in_specs,
        out_specs=[_row_spec(TOK, D_MODEL), pl.BlockSpec((N_DEV, TOK, FF_BLOCK), lambda i: (0, i, 0))] + car.out_specs,
        out_shape=[jax.ShapeDtypeStruct((seq_len, D_MODEL), f32),
                   jax.ShapeDtypeStruct((N_DEV, seq_len, FF_BLOCK), _MXU)] + car.out_shape, scratch_shapes=car.scratch,
        compiler_params=_params(("arbitrary",)), name=name)(x, nw, w_up, w_down, *car.operands)
    return outs[0], outs[1], outs[2:]


MLP_SPLIT = 2


def mlp_bwd(name, x, nw, w_up, w_down, relu, dy, exchange=None):
    seq_len = x.shape[0]
    nt = seq_len // TOK
    per = N_DEV // MLP_SPLIT
    car = Carried(exchange)

    def body(*refs):
        x_ref, nw_ref, up_ref, down_ref, relu_ref, dy_ref = refs[:6]
        ex_ins, (dh_ref, dup_ref, ddown_ref) = refs[6:6 + car.n], refs[6 + car.n:9 + car.n]
        ex_outs, ex_sems = refs[9 + car.n:9 + 2 * car.n], refs[9 + 2 * car.n:]
        j, i = pl.program_id(0), pl.program_id(1)
        car.start_at(jnp.logical_and(j == 0, i == 0), ex_ins, ex_outs, ex_sems)

        @pl.when(i == 0)
        def _():
            dup_ref[...] = jnp.zeros_like(dup_ref)
            ddown_ref[...] = jnp.zeros_like(ddown_ref)

        h = rmsnorm_f(x_ref[...], nw_ref[...])
        g = dy_ref[...]
        dh = jnp.zeros_like(h)
        for d in range(per):
            r = relu_ref[d].astype(f32)
            da = mm_nt(g, down_ref[d]) * (2.0 * r)
            ddown_ref[d] += mm_tn(r * r, g)
            dup_ref[d] += mm_tn(h, da)
            dh = dh + mm_nt(da, up_ref[d])
        dh_ref[...] = dh
        car.wait_at(jnp.logical_and(j == MLP_SPLIT - 1, i == nt - 1), ex_ins, ex_outs, ex_sems)

    outs = pl.pallas_call(
        body, grid=(MLP_SPLIT, nt),
        in_specs=[pl.BlockSpec((TOK, D_MODEL), lambda j, i: (i, 0)), pl.BlockSpec(nw.shape, lambda j, i: (0, 0)),
                  pl.BlockSpec((per, D_MODEL, FF_BLOCK), lambda j, i: (j, 0, 0), pipeline_mode=pl.Buffered(1)),
                  pl.BlockSpec((per, FF_BLOCK, D_MODEL), lambda j, i: (j, 0, 0), pipeline_mode=pl.Buffered(1)),
                  pl.BlockSpec((per, TOK, FF_BLOCK), lambda j, i: (j, i, 0)),
                  pl.BlockSpec((TOK, D_MODEL), lambda j, i: (i, 0))] + car.in_specs,
        out_specs=[pl.BlockSpec((None, TOK, D_MODEL), lambda j, i: (j, i, 0)),
                   pl.BlockSpec((per, D_MODEL, FF_BLOCK), lambda j, i: (j, 0, 0), pipeline_mode=pl.Buffered(1)),
                   pl.BlockSpec((per, FF_BLOCK, D_MODEL), lambda j, i: (j, 0, 0), pipeline_mode=pl.Buffered(1))]
        + car.out_specs,
        out_shape=[jax.ShapeDtypeStruct((MLP_SPLIT, seq_len, D_MODEL), f32),
                   jax.ShapeDtypeStruct(w_up.shape, f32), jax.ShapeDtypeStruct(w_down.shape, f32)] + car.out_shape,
        scratch_shapes=car.scratch,
        compiler_params=_params(("arbitrary", "arbitrary")), name=name)(x, nw, w_up, w_down, relu, dy, *car.operands)
    dh_parts, d_up, d_down, ex_results = outs[0], outs[1], outs[2], outs[3:]

    def norm_body(x_ref, nw_ref, dh_ref, dy_ref, dx_ref, dnw_ref):
        @pl.when(pl.program_id(0) == 0)
        def _():
            dnw_ref[...] = jnp.zeros_like(dnw_ref)

        _, vjp = jax.vjp(rmsnorm_f, x_ref[...], nw_ref[...])
        dh = dh_ref[0]
        for j in range(1, MLP_SPLIT):
            dh = dh + dh_ref[j]
        dx, dnw = vjp(dh)
        dx_ref[...] = dy_ref[...] + dx
        dnw_ref[...] += dnw

    dx, dnw = pl.pallas_call(
        norm_body, grid=(nt,),
        in_specs=[_row_spec(TOK, D_MODEL), _const_spec(nw.shape),
                  pl.BlockSpec((MLP_SPLIT, TOK, D_MODEL), lambda i: (0, i, 0)), _row_spec(TOK, D_MODEL)],
        out_specs=[_row_spec(TOK, D_MODEL), _const_spec(nw.shape)],
        out_shape=[jax.ShapeDtypeStruct((seq_len, D_MODEL), f32), jax.ShapeDtypeStruct(nw.shape, f32)],
        compiler_params=_params(("arbitrary",)), name=name + "_norm")(x, nw, dh_parts, dy)
    return dx, d_up, d_down, dnw, ex_results


def final_loss(name, x, nw, target):
    seq_len = x.shape[0]

    def body(x_ref, nw_ref, t_ref, loss_ref, dx_ref, dnw_ref):
        @pl.when(pl.program_id(0) == 0)
        def _():
            loss_ref[...] = jnp.zeros_like(loss_ref)
            dnw_ref[...] = jnp.zeros_like(dnw_ref)

        y, vjp = jax.vjp(rmsnorm_f, x_ref[...], nw_ref[...])
        err = y - t_ref[...]
        loss_ref[...] += 0.5 * jnp.sum(jnp.mean(err * err, axis=-1, keepdims=True), axis=0, keepdims=True)
        dx, dnw = vjp(err * (1.0 / D_MODEL))
        dx_ref[...] = dx
        dnw_ref[...] += dnw

    return pl.pallas_call(
        body, grid=(seq_len // TOK,),
        in_specs=[_row_spec(TOK, D_MODEL), _const_spec(nw.shape), _row_spec(TOK, D_MODEL)],
        out_specs=[_const_spec((8, LANES)), _row_spec(TOK, D_MODEL), _const_spec(nw.shape)],
        out_shape=[jax.ShapeDtypeStruct((8, LANES), f32), jax.ShapeDtypeStruct((seq_len, D_MODEL), f32),
                   jax.ShapeDtypeStruct(nw.shape, f32)],
        compiler_params=_params(("arbitrary",)), name=name)(x, nw, target)


def _whole(a):
    return pl.BlockSpec(a.shape, lambda: (0,) * len(a.shape))


def s5_prep_fwd(name, raw):
    def body(*refs):
        outs = s5_prep(*[r[...] for r in refs[:7]])
        for o_ref, o in zip(refs[7:], outs):
            o_ref[...] = o

    shapes = [(1, S5_W)] * 2 + [(S5_CH, S5_W)] * 4
    return pl.pallas_call(
        body, in_specs=[_whole(a) for a in raw], out_specs=[pl.BlockSpec(s, lambda s=s: (0,) * len(s)) for s in shapes],
        out_shape=[jax.ShapeDtypeStruct(s, f32) for s in shapes],
        compiler_params=pltpu.CompilerParams(vmem_limit_bytes=VMEM_LIMIT), name=name)(*raw)


def s5_prep_bwd(name, raw, cts):
    def body(*refs):
        _, vjp = jax.vjp(s5_prep, *[r[...] for r in refs[:7]])
        grads = vjp(tuple(r[...] for r in refs[7:13]))
        for o_ref, g in zip(refs[13:], grads):
            o_ref[...] = g

    return pl.pallas_call(
        body, in_specs=[_whole(a) for a in list(raw) + list(cts)], out_specs=[_whole(a) for a in raw],
        out_shape=[jax.ShapeDtypeStruct(a.shape, f32) for a in raw],
        compiler_params=pltpu.CompilerParams(vmem_limit_bytes=VMEM_LIMIT), name=name)(*raw, *cts)


def s5_fwd(name, proj, u_block, lam_re, lam_im, bblk_re, bblk_im, cblk_re, cblk_im, d_skip, w_glu, b_glu):
    seq_len = proj.shape[0]

    def body(u_ref, lr_ref, li_ref, br_ref, bi_ref, cr_ref, ci_ref, d_ref, wg_ref, bg_ref,
             o_ref, hr_ref, hi_ref, carry_re, carry_im):
        @pl.when(pl.program_id(0) == 0)
        def _():
            carry_re[...] = jnp.zeros_like(carry_re)
            carry_im[...] = jnp.zeros_like(carry_im)

        u = u_ref[...]
        hr_ref[...] = mm(u, br_ref[...])
        hi_ref[...] = mm(u, bi_ref[...])
        lr, li = lr_ref[...], li_ref[...]

        def step(t, h):
            pr, pi = h
            row = pl.ds(t, 1)
            nr = lr * pr - li * pi + hr_ref[row, :]
            ni = lr * pi + li * pr + hi_ref[row, :]
            hr_ref[row, :] = nr
            hi_ref[row, :] = ni
            return nr, ni

        er, ei = lax.fori_loop(0, TOK, step, (carry_re[0:1, :], carry_im[0:1, :]))
        carry_re[0:1, :] = er
        carry_im[0:1, :] = ei
        o_ref[...] = s5_out_fn(hr_ref[...], hi_ref[...], u, cr_ref[...], ci_ref[...], d_ref[...],
                               wg_ref[...].astype(f32), bg_ref[...])

    consts = [lam_re, lam_im, bblk_re, bblk_im, cblk_re, cblk_im, d_skip, w_glu, b_glu]
    return pl.pallas_call(
        body, grid=(seq_len // TOK,),
        in_specs=[_row_spec(TOK, S5_CH, u_block)] + [_const_spec(a.shape) for a in consts],
        out_specs=[_row_spec(TOK, S5_CH), _row_spec(TOK, S5_W), _row_spec(TOK, S5_W)],
        out_shape=[jax.ShapeDtypeStruct((seq_len, S5_CH), f32), jax.ShapeDtypeStruct((seq_len, S5_W), f32),
                   jax.ShapeDtypeStruct((seq_len, S5_W), f32)],
        scratch_shapes=[pltpu.VMEM((HALO, S5_W), f32), pltpu.VMEM((HALO, S5_W), f32)],
        compiler_params=_params(("arbitrary",)), name=name)(proj, *consts)


def s5_bwd(name, proj, u_block, h_re, h_im, d_out, lam_re, lam_im, bblk_re, bblk_im, cblk_re, cblk_im, d_skip, w_glu, b_glu):
    seq_len = proj.shape[0]
    nt = seq_len // TOK
    consts = [lam_re, lam_im, bblk_re, bblk_im, cblk_re, cblk_im, d_skip, w_glu, b_glu]

    def body(u_ref, hr_ref, hi_ref, pr_ref, pi_ref, dout_ref, lr_ref, li_ref, br_ref, bi_ref, cr_ref, ci_ref, d_ref, wg_ref, bg_ref,
             du_ref, dlr_ref, dli_ref, dbr_ref, dbi_ref, dcr_ref, dci_ref, dd_ref, dwg_ref, dbg_ref,
             gr_ref, gi_ref, carry_re, carry_im):
        i = pl.program_id(0)
        tile = nt - 1 - i

        @pl.when(i == 0)
        def _():
            for r in (carry_re, carry_im, dlr_ref, dli_ref, dbr_ref, dbi_ref, dcr_ref, dci_ref, dd_ref, dwg_ref, dbg_ref):
                r[...] = jnp.zeros_like(r)

        u, h_r, h_i = u_ref[...], hr_ref[...], hi_ref[...]
        _, vjp = jax.vjp(s5_out_fn, h_r, h_i, u, cr_ref[...], ci_ref[...], d_ref[...], wg_ref[...].astype(f32), bg_ref[...])
        ghr, ghi, du, dcr, dci, dd, dwg, dbg = vjp(dout_ref[...])
        gr_ref[...] = ghr
        gi_ref[...] = ghi
        lr, li = lr_ref[...], li_ref[...]

        def step(k, g):
            nr, ni = g
            row = pl.ds(TOK - 1 - k, 1)
            cr = gr_ref[row, :] + lr * nr + li * ni
            ci = gi_ref[row, :] - li * nr + lr * ni
            gr_ref[row, :] = cr
            gi_ref[row, :] = ci
            return cr, ci

        er, ei = lax.fori_loop(0, TOK, step, (carry_re[0:1, :], carry_im[0:1, :]))
        carry_re[0:1, :] = er
        carry_im[0:1, :] = ei
        g_r, g_i = gr_ref[...], gi_ref[...]
        first = _iota((TOK, 1), 0) == 0
        keep = jnp.where(tile > 0, 1.0, 0.0)
        p_r = jnp.where(first, pr_ref[HALO - 1:HALO, :] * keep, _roll(h_r, 1, 0))
        p_i = jnp.where(first, pi_ref[HALO - 1:HALO, :] * keep, _roll(h_i, 1, 0))
        dlr_ref[...] += jnp.sum(g_r * p_r + g_i * p_i, axis=0, keepdims=True)
        dli_ref[...] += jnp.sum(g_i * p_r - g_r * p_i, axis=0, keepdims=True)
        du_ref[...] = du + mm_nt(g_r, br_ref[...]) + mm_nt(g_i, bi_ref[...])
        dbr_ref[...] += mm_tn(u, g_r)
        dbi_ref[...] += mm_tn(u, g_i)
        dcr_ref[...] += dcr
        dci_ref[...] += dci
        dd_ref[...] += dd
        dwg_ref[...] += dwg
        dbg_ref[...] += dbg

    rev = lambda cols, cb=0: pl.BlockSpec((TOK, cols), lambda i, cb=cb: (nt - 1 - i, cb))
    prev = pl.BlockSpec((HALO, S5_W), lambda i: (jnp.maximum((nt - 1 - i) * (TOK // HALO) - 1, 0), 0))
    outs = pl.pallas_call(
        body, grid=(nt,),
        in_specs=[rev(S5_CH, u_block), rev(S5_W), rev(S5_W), prev, prev, rev(S5_CH)] + [_const_spec(a.shape) for a in consts],
        out_specs=[rev(S5_CH)] + [_const_spec(a.shape) for a in consts],
        out_shape=[jax.ShapeDtypeStruct((seq_len, S5_CH), f32)] + [jax.ShapeDtypeStruct(a.shape, f32) for a in consts],
        scratch_shapes=[pltpu.VMEM((TOK, S5_W), f32), pltpu.VMEM((TOK, S5_W), f32),
                        pltpu.VMEM((HALO, S5_W), f32), pltpu.VMEM((HALO, S5_W), f32)],
        compiler_params=_params(("arbitrary",)), name=name)(proj, h_re, h_im, h_re, h_im, d_out, *consts)
    return outs[0], outs[1:]


ANY = pl.BlockSpec(memory_space=pl.ANY)


def _mesh_position():
    x, y, c = lax.axis_index("x"), lax.axis_index("y"), lax.axis_index("c")
    return x, y, c, 4 * x + 2 * y + c


def _peer(x, y, c, r):
    px = 1 - x if r & 4 else x
    py = 1 - y if r & 2 else y
    pc = 1 - c if r & 1 else c
    return (px, py, pc), 4 * px + 2 * py + pc


class Exchange:
    def __init__(self, arrays, gather):
        self.arrays, self.gather, self.n = list(arrays), gather, len(arrays)
        self.in_specs = [ANY] * self.n
        self.out_specs = [ANY] * self.n
        shapes = [((N_DEV,) + a.shape) if gather else a.shape for a in self.arrays]
        self.out_shape = [jax.ShapeDtypeStruct(s, a.dtype) for s, a in zip(shapes, self.arrays)]
        self.scratch = [pltpu.SemaphoreType.DMA((self.n, N_DEV - 1)), pltpu.SemaphoreType.DMA((self.n, N_DEV - 1)),
                        pltpu.SemaphoreType.DMA((self.n,))]

    def _copies(self, ins, outs, sems, landed):
        send_sems, recv_sems, local_sems = sems
        x, y, c, me = _mesh_position()
        local, remote = [], []
        for i in range(self.n):
            mine = ins[i] if self.gather else ins[i].at[me]
            local.append(pltpu.make_async_copy(mine, outs[i].at[me], local_sems.at[i]))
            for r in range(1, N_DEV):
                peer, peer_idx = _peer(x, y, c, r)
                remote.append(pltpu.make_async_remote_copy(
                    src_ref=ins[i] if self.gather else ins[i].at[peer_idx],
                    dst_ref=outs[i].at[peer_idx if landed else me],
                    send_sem=send_sems.at[i, r - 1], recv_sem=recv_sems.at[i, r - 1],
                    device_id=peer, device_id_type=pl.DeviceIdType.MESH))
        return local, remote

    def start(self, ins, outs, sems):
        local, remote = self._copies(ins, outs, sems, landed=False)
        for cp in local + remote:
            cp.start()

    def wait(self, ins, outs, sems):
        local, remote = self._copies(ins, outs, sems, landed=True)
        for cp in remote:
            cp.wait_recv()
            cp.wait_send()
        for cp in local:
            cp.wait()

    def run(self, name):
        n = self.n

        def body(*refs):
            ins, outs, sems = refs[:n], refs[n:2 * n], refs[2 * n:]
            self.start(ins, outs, sems)
            self.wait(ins, outs, sems)

        return pl.pallas_call(body, in_specs=self.in_specs, out_specs=self.out_specs, out_shape=self.out_shape,
                              scratch_shapes=self.scratch, name=name)(*self.arrays)


class Carried:
    def __init__(self, exchange):
        self.ex = exchange
        self.n = exchange.n if exchange else 0
        self.in_specs = exchange.in_specs if exchange else []
        self.out_specs = exchange.out_specs if exchange else []
        self.out_shape = exchange.out_shape if exchange else []
        self.scratch = exchange.scratch if exchange else []
        self.operands = exchange.arrays if exchange else []

    def start_at(self, first, ins, outs, sems):
        if self.ex is not None:
            @pl.when(first)
            def _():
                self.ex.start(ins, outs, sems)

    def wait_at(self, last, ins, outs, sems):
        if self.ex is not None:
            @pl.when(last)
            def _():
                self.ex.wait(ins, outs, sems)


def adamw(name, parts, w, m, v):
    rows, cols = w.shape
    tr = rows
    for cand in (512, 256, 128, 64, 32, 16, 8):
        if rows * cols * 4 > (1 << 20) and rows % cand == 0 and cand * cols * 4 <= (1 << 20):
            tr = cand
            break

    def body(p_ref, w_ref, m_ref, v_ref, g_ref, d_ref, nm_ref, nv_ref):
        g = p_ref[0].astype(f32)
        for s in range(1, N_DEV):
            g = g + p_ref[s].astype(f32)
        nm = ADAM_B1 * m_ref[...] + (1.0 - ADAM_B1) * g
        nv = ADAM_B2 * v_ref[...] + (1.0 - ADAM_B2) * (g * g)
        m_hat = nm / (1.0 - ADAM_B1 ** ADAM_STEP)
        v_hat = nv / (1.0 - ADAM_B2 ** ADAM_STEP)
        g_ref[...] = g
        d_ref[...] = -ADAM_LR * (m_hat / (jnp.sqrt(v_hat) + ADAM_EPS) + ADAM_WD * w_ref[...])
        nm_ref[...] = nm
        nv_ref[...] = nv

    blk = pl.BlockSpec((tr, cols), lambda i: (i, 0))
    return pl.pallas_call(
        body, grid=(rows // tr,),
        in_specs=[pl.BlockSpec((N_DEV, tr, cols), lambda i: (0, i, 0)), blk, blk, blk],
        out_specs=[blk] * 4, out_shape=[jax.ShapeDtypeStruct((rows, cols), f32)] * 4,
        compiler_params=_params(("arbitrary",)), name=name)(parts, w, m, v)


WEIGHTS = ['l0_norm_mix', 'l0_w_in', 'ssd_conv_w', 'ssd_conv_b', 'ssd_dt_bias', 'ssd_A_log', 'ssd_D', 'ssd_norm_w',
           'l0_w_out', 'l0_norm_mlp', 'l0_w_up', 'l0_w_down', 'l1_norm_mix', 'l1_w_in', 'gdn_conv_w', 'gdn_A_log',
           'gdn_dt_bias', 'gdn_norm_w', 's5_A_re', 's5_A_im', 's5_log_step', 's5_B_re', 's5_B_im', 's5_C_re', 's5_C_im',
           's5_D', 's5_w_glu', 's5_b_glu', 'l1_w_out', 'l1_norm_mlp', 'l1_w_up', 'l1_w_down', 'final_norm']
SHARDED = ['l0_w_in', 'l0_w_out', 'l0_w_up', 'l0_w_down', 'l1_w_in', 's5_w_glu', 'l1_w_out', 'l1_w_up', 'l1_w_down',
           'ssd_conv_w', 'gdn_conv_w']
F32_GATHER = ('ssd_conv_w', 'gdn_conv_w')
REPLICATED = [n for n in WEIGHTS if n not in SHARDED]
INPUTS = ['x'] + WEIGHTS + ['loss_target'] + ['m_' + n for n in WEIGHTS] + ['v_' + n for n in WEIGHTS]


def _row(v):
    return v.reshape(1, -1)


def _pad_lanes(v, offset=0):
    return jnp.pad(v, (offset, LANES - offset - v.shape[0])).reshape(1, LANES)


def _cols_to_blocks(g):
    return g.reshape(g.shape[0], N_DEV, -1).transpose(1, 0, 2)


def _blocks_to_cols(g):
    return g.transpose(1, 0, 2).reshape(g.shape[1], -1)


def _pack(arrays):
    parts, slots, at = [], [], 0
    for a in arrays:
        n = a.size
        rows = -(-n // (8 * LANES)) * 8
        parts.append(jnp.pad(a.reshape(-1), (0, rows * LANES - n)).reshape(rows, LANES))
        slots.append((at, rows, n, a.shape))
        at += rows
    return jnp.concatenate(parts, axis=0), slots


def _unpack(buf, slots):
    return [buf[at:at + rows].reshape(-1)[:n].reshape(shape) for at, rows, n, shape in slots]


def kernel(*args):
    a = dict(zip(INPUTS, args, strict=True))
    seq_len = a['x'].shape[1]
    x0 = a['x'].reshape(seq_len, D_MODEL)
    target = a['loss_target'].reshape(seq_len, D_MODEL)

    shard = {n: a[n] if n in F32_GATHER else a[n].astype(_MXU) for n in SHARDED}
    first = ['l0_w_in', 'ssd_conv_w', 'gdn_conv_w']
    g = dict(zip(first, Exchange([shard[n] for n in first], gather=True).run("gather_first")))
    w_nat = _blocks_to_cols(g['l0_w_in'])
    win0 = jnp.concatenate([w_nat[:, :2048], w_nat[:, 2560:3584], w_nat[:, 2048:2560], w_nat[:, 3584:3592],
                            jnp.zeros((D_MODEL, IN0_PAD - IN0_W), _MXU)], axis=1)
    ssd_cw, gdn_cw = _blocks_to_cols(g['ssd_conv_w']), _blocks_to_cols(g['gdn_conv_w'])

    half = RET_D // 2
    inv = ROPE_THETA ** (-jnp.arange(half, dtype=f32) / half)
    ang = jnp.arange(seq_len, dtype=f32)[:, None] * inv[None, :]
    cos, sin = jnp.cos(ang), jnp.sin(ang)
    cos, sin = jnp.concatenate([cos, cos], axis=1), jnp.concatenate([-sin, sin], axis=1)
    ssd_params = [ssd_cw, _row(a['ssd_conv_b']), _pad_lanes(a['ssd_dt_bias']), _pad_lanes(a['ssd_A_log']),
                  _pad_lanes(a['ssd_D']), _row(a['ssd_norm_w'])]
    gdn_params = [gdn_cw, _pad_lanes(a['gdn_A_log'], GDN_GCOL), _pad_lanes(a['gdn_dt_bias'], GDN_GCOL), _row(a['gdn_norm_w'])]
    s5_raw = [a['s5_A_re'].reshape(1, S5_W), a['s5_A_im'].reshape(1, S5_W), _pad_lanes(a['s5_log_step']),
              a['s5_B_re'].transpose(2, 0, 1).reshape(S5_GROUP, S5_W), a['s5_B_im'].transpose(2, 0, 1).reshape(S5_GROUP, S5_W),
              a['s5_C_re'].transpose(1, 0, 2).reshape(S5_GROUP, S5_W), a['s5_C_im'].transpose(1, 0, 2).reshape(S5_GROUP, S5_W)]
    s5_d, s5_bg = _row(a['s5_D']), _row(a['s5_b_glu'])
    nw = {n: _row(a[n]) for n in ('l0_norm_mix', 'l0_norm_mlp', 'l1_norm_mix', 'l1_norm_mlp', 'final_norm')}

    proj0 = inproj_fwd("l0_in", x0, nw['l0_norm_mix'], win0)
    ret_seqs = [Seq(proj0, 512, 0), Seq(proj0, 512, 1), Seq(proj0, 512, 2), Seq(proj0, 512, 3),
                Seq(cos, LANES, 0, "const"), Seq(sin, LANES, 0, "const")]
    later = ['l0_w_out', 'l0_w_up', 'l0_w_down', 's5_w_glu']
    ret_out, ret_st, got = mixer_fwd("ret_fwd", ret_chunk, ret_seqs, [], 512, RET_HEADS * RET_D, seq_len,
                                     exchange=Exchange([shard[n] for n in later], gather=True))
    g.update(zip(later, got))
    wout0 = g['l0_w_out'].reshape(D_MODEL, D_MODEL)
    wglu = g['s5_w_glu'].reshape(S5_CH, S5_CH)
    ssd_seqs = [Seq(proj0, 512, 6), Seq(proj0, 1024, 2, "halo"), Seq(proj0, LANES, 28)]
    ssd_out, ssd_st, _ = mixer_fwd("ssd_fwd", ssd_chunk, ssd_seqs, ssd_params, SSD_INNER, SSD_INNER, seq_len)
    x1 = outproj_fwd("l0_out", x0, ret_out, ssd_out, wout0)
    later = ['l1_w_in', 'l1_w_out', 'l1_w_up', 'l1_w_down']
    x2, relu0, got = mlp_fwd("l0_mlp", x1, nw['l0_norm_mlp'], g['l0_w_up'], g['l0_w_down'],
                      exchange=Exchange([shard[n] for n in later], gather=True))
    g.update(zip(later, got))
    w_nat = g['l1_w_in'].reshape(D_MODEL, IN1_W)
    win1 = jnp.concatenate([w_nat[:, :3072], w_nat[:, 3084:3340], w_nat[:, 3072:3084],
                            jnp.zeros((D_MODEL, IN1_PAD - IN1_W), _MXU)], axis=1)
    wout1 = g['l1_w_out'].reshape(D_MODEL, D_MODEL)
    proj1 = inproj_fwd("l1_in", x2, nw['l1_norm_mix'], win1)
    gdn_seqs = [Seq(proj1, 3 * GDN_W, 0, "halo"), Seq(proj1, GDN_W, 3), Seq(proj1, LANES, 26)]
    gdn_out, gdn_st, _ = mixer_fwd("gdn_fwd", gdn_chunk, gdn_seqs, gdn_params, GDN_W, GDN_W, seq_len)
    prep = s5_prep_fwd("s5_prep", s5_raw)
    s5_out, h_re, h_im = s5_fwd("s5_fwd", proj1, 12, *prep, s5_d, wglu, s5_bg)
    x3 = outproj_fwd("l1_out", x2, gdn_out, s5_out, wout1)
    x4, relu1, _ = mlp_fwd("l1_mlp", x3, nw['l1_norm_mlp'], g['l1_w_up'], g['l1_w_down'])
    loss_blk, dx4, d_final = final_loss("final_loss", x4, nw['final_norm'], target)

    parts = {}
    dx3, d_up1, d_down1, d_nmlp1, _ = mlp_bwd("l1_mlp_bwd", x3, nw['l1_norm_mlp'], g['l1_w_up'], g['l1_w_down'], relu1, dx4)
    d_gdn, d_s5, d_wout1 = outproj_bwd("l1_out_bwd", dx3, gdn_out, s5_out, wout1)
    d_u, s5_g = s5_bwd("s5_bwd", proj1, 12, h_re, h_im, d_s5, *prep, s5_d, wglu, s5_bg)
    s5_raw_g = s5_prep_bwd("s5_prep_bwd", s5_raw, s5_g[:6])
    ready = {'l1_w_up': d_up1, 'l1_w_down': d_down1, 'l1_w_out': d_wout1.reshape(N_DEV, -1, D_MODEL),
             's5_w_glu': s5_g[7].reshape(N_DEV, -1, S5_CH)}
    (d_qkv, d_z1, d_ba), gdn_pg, got = mixer_bwd("gdn_bwd", gdn_chunk, gdn_seqs, gdn_params, gdn_st, d_gdn, seq_len,
                                                 exchange=Exchange(list(ready.values()), gather=False))
    parts.update(zip(ready, got))
    dx2, d_win1, d_nmix1 = inproj_bwd("l1_in_bwd", x2, nw['l1_norm_mix'], win1, [d_qkv, d_z1, d_u, d_ba], dx3)
    d_win1 = jnp.concatenate([d_win1[:, :3072], d_win1[:, 3328:3340], d_win1[:, 3072:3328]], axis=1)
    ready = {'l1_w_in': d_win1.reshape(N_DEV, -1, IN1_W), 'gdn_conv_w': _cols_to_blocks(gdn_pg[0])}
    dx1, d_up0, d_down0, d_nmlp0, got = mlp_bwd("l0_mlp_bwd", x1, nw['l0_norm_mlp'], g['l0_w_up'], g['l0_w_down'], relu0, dx2,
                                                exchange=Exchange(list(ready.values()), gather=False))
    parts.update(zip(ready, got))
    d_ret, d_ssd, d_wout0 = outproj_bwd("l0_out_bwd", dx1, ret_out, ssd_out, wout0)
    d_qkvg, _, _ = mixer_bwd("ret_bwd", ret_chunk, ret_seqs, [], ret_st, d_ret, seq_len)
    ready = {'l0_w_up': d_up0, 'l0_w_down': d_down0, 'l0_w_out': d_wout0.reshape(N_DEV, -1, D_MODEL)}
    (d_z0, d_xbc, d_dt), ssd_pg, got = mixer_bwd("ssd_bwd", ssd_chunk, ssd_seqs, ssd_params, ssd_st, d_ssd, seq_len,
                                                 exchange=Exchange(list(ready.values()), gather=False))
    parts.update(zip(ready, got))
    dx0, d_win0, d_nmix0 = inproj_bwd("l0_in_bwd", x0, nw['l0_norm_mix'], win0, list(d_qkvg) + [d_xbc, d_z0, d_dt], dx1)

    d_win0 = jnp.concatenate([d_win0[:, :2048], d_win0[:, 3072:3584], d_win0[:, 2048:3072], d_win0[:, 3584:3592]], axis=1)
    ready = {'l0_w_in': _cols_to_blocks(d_win0).astype(jnp.bfloat16), 'ssd_conv_w': _cols_to_blocks(ssd_pg[0])}
    from_b = lambda t: t.reshape(S5_GROUP, S5_GROUPS, S5_STATE).transpose(1, 2, 0)
    from_c = lambda t: t.reshape(S5_GROUP, S5_GROUPS, S5_STATE).transpose(1, 0, 2)
    replicated_g = {
        'l0_norm_mix': d_nmix0, 'ssd_conv_b': ssd_pg[1], 'ssd_dt_bias': ssd_pg[2][0, :SSD_HEADS], 'ssd_A_log': ssd_pg[3][0, :SSD_HEADS],
        'ssd_D': ssd_pg[4][0, :SSD_HEADS], 'ssd_norm_w': ssd_pg[5], 'l0_norm_mlp': d_nmlp0, 'l1_norm_mix': d_nmix1,
        'gdn_A_log': gdn_pg[1][0, GDN_GCOL:GDN_GCOL + GDN_HEADS], 'gdn_dt_bias': gdn_pg[2][0, GDN_GCOL:GDN_GCOL + GDN_HEADS],
        'gdn_norm_w': gdn_pg[3], 's5_A_re': s5_raw_g[0], 's5_A_im': s5_raw_g[1], 's5_log_step': s5_raw_g[2][0, :S5_GROUPS],
        's5_B_re': from_b(s5_raw_g[3]), 's5_B_im': from_b(s5_raw_g[4]), 's5_C_re': from_c(s5_raw_g[5]), 's5_C_im': from_c(s5_raw_g[6]),
        's5_D': s5_g[6], 's5_b_glu': s5_g[8], 'l1_norm_mlp': d_nmlp1, 'final_norm': d_final}
    replicated_g = {n: replicated_g[n].reshape(a[n].shape) for n in REPLICATED}

    parts.update(zip(ready, Exchange(list(ready.values()), gather=False).run("scatter_last")))
    packed_g, slots = _pack([replicated_g[n] for n in REPLICATED])
    (packed_parts,) = Exchange([packed_g], gather=True).run("gather_small_grads")
    results = {}
    for n in SHARDED:
        results[n] = adamw("adamw_" + n, parts[n], a[n], a['m_' + n], a['v_' + n])
    packed = [_pack([a[pre + n] for n in REPLICATED])[0] for pre in ('', 'm_', 'v_')]
    small = [_unpack(t, slots) for t in adamw("adamw_small", packed_parts, *packed)]
    for i, n in enumerate(REPLICATED):
        results[n] = tuple(small[k][i] for k in range(4))

    loss = lax.psum(loss_blk[0, 0], ("x", "y", "c"))
    grad_x = dx0.reshape(a['x'].shape)
    return (loss, grad_x, *[results[n][0] for n in WEIGHTS], *[results[n][1] for n in WEIGHTS],
            *[results[n][2] for n in WEIGHTS], *[results[n][3] for n in WEIGHTS])
```

```python
import functools
import math

import numpy as np
import jax
import jax.numpy as jnp
from jax import lax
from jax.experimental import pallas as pl
from jax.experimental.pallas import tpu as pltpu

f32 = jnp.float32
_MXU = jnp.bfloat16
HI = lax.Precision.HIGHEST

D_MODEL = 1024
CHUNK = 64
EPS = 1e-6
N_DEV = 8
LANES = 128
HALO = 8
CONV_WIDTH = 4

RET_HEADS, RET_D = 4, 128
SSD_HEADS, SSD_P, SSD_N, SSD_GROUPS = 8, 64, 128, 2
SSD_INNER = SSD_HEADS * SSD_P
GDN_HEADS, GDN_D = 6, 128
GDN_W = GDN_HEADS * GDN_D
S5_CH, S5_GROUP, S5_GROUPS, S5_STATE = 256, 16, 16, 64
S5_W = S5_GROUPS * S5_STATE
D_FF = 4096
ROPE_THETA = 10000.0

IN0_W = 3592
IN0_PAD = 3712
IN1_W = 3340
IN1_PAD = 3456

ADAM_LR, ADAM_B1, ADAM_B2, ADAM_EPS, ADAM_WD, ADAM_STEP = 0.001, 0.9, 0.999, 1e-08, 0.01, 10

VMEM_LIMIT = 56 * 1024 * 1024


def _dot(a, b, dims):
    return lax.dot_general(a.astype(_MXU), b.astype(_MXU), (dims, ((), ())), preferred_element_type=f32)


@jax.custom_vjp
def mm(a, b):
    return _dot(a, b, ((1,), (0,)))


@jax.custom_vjp
def mm_nt(a, b):
    return _dot(a, b, ((1,), (1,)))


@jax.custom_vjp
def mm_tn(a, b):
    return _dot(a, b, ((0,), (0,)))


mm.defvjp(lambda a, b: (mm(a, b), (a, b)), lambda r, g: (mm_nt(g, r[1]), mm_tn(r[0], g)))
mm_nt.defvjp(lambda a, b: (mm_nt(a, b), (a, b)), lambda r, g: (mm(g, r[1]), mm_tn(g, r[0])))
mm_tn.defvjp(lambda a, b: (mm_tn(a, b), (a, b)), lambda r, g: (mm_nt(r[1], g), mm(r[0], g)))


def mmh(a, b):
    return jnp.dot(a, b, precision=HI, preferred_element_type=f32)


def _exact01(x, m01, dims, m_first):
    hi = x.astype(jnp.bfloat16)
    r = x - hi.astype(f32)
    mid = r.astype(jnp.bfloat16)
    lo = (r - mid.astype(f32)).astype(jnp.bfloat16)
    m = m01.astype(jnp.bfloat16)
    dot = lambda p: lax.dot_general(m, p, (dims, ((), ())), preferred_element_type=f32) if m_first else \
        lax.dot_general(p, m, (dims, ((), ())), preferred_element_type=f32)
    return dot(hi) + dot(mid) + dot(lo)


@jax.custom_vjp
def spread01(x, sel):
    return _exact01(x, sel, ((1,), (0,)), False)


spread01.defvjp(lambda x, sel: (spread01(x, sel), sel),
                lambda sel, g: (_exact01(g, sel, ((1,), (1,)), False), jnp.zeros_like(sel)))


@jax.custom_vjp
def cumsum01(tril, x):
    return _exact01(x, tril, ((1,), (0,)), True)


cumsum01.defvjp(lambda tril, x: (cumsum01(tril, x), tril),
                lambda tril, g: (jnp.zeros_like(tril), _exact01(g, tril, ((0,), (0,)), True)))


def _roll(x, shift, axis):
    return pltpu.roll(x, shift, axis)


@functools.partial(jax.custom_vjp, nondiff_argnums=(1,))
def roll_rows(x, s):
    return _roll(x, s, 0) if s else x


roll_rows.defvjp(lambda x, s: (roll_rows(x, s), None),
                 lambda s, _, g: ((_roll(g, g.shape[0] - s, 0) if s else g),))


@jax.custom_vjp
def roll_half(x):
    return _roll(x, x.shape[-1] // 2, 1)


roll_half.defvjp(lambda x: (roll_half(x), None), lambda _, g: (roll_half(g),))


def _iota(shape, axis):
    return lax.broadcasted_iota(jnp.int32, shape, axis)


def silu(x):
    return x * jax.nn.sigmoid(x)


def softplus(x):
    return jnp.maximum(x, 0.0) + jnp.log(1.0 + jnp.exp(-jnp.abs(x)))


def rmsnorm_f(x, w):
    return x * lax.rsqrt(jnp.mean(x * x, axis=-1, keepdims=True) + EPS) * w


def unit_rms(x):
    return x * lax.rsqrt(jnp.mean(x * x, axis=-1, keepdims=True) + EPS)


def _causal(n, strict=False):
    r, c = _iota((n, n), 0), _iota((n, n), 1)
    return (r > c) if strict else (r >= c)


def _tril_ones(n):
    return _causal(n).astype(f32)


def _conv_rows(xe, w):
    acc = w[CONV_WIDTH - 1:CONV_WIDTH, :] * xe
    for j in range(CONV_WIDTH - 1):
        acc = acc + w[j:j + 1, :] * roll_rows(xe, CONV_WIDTH - 1 - j)
    return acc[HALO:, :]


_RET_LOG_GAMMA = [float(np.log(np.float32(1.0) - np.float32(2.0) ** np.float32(-5.0 - h))) for h in range(RET_HEADS)]


def ret_chunk(q, k, v, gate, cos, sin, state):
    c = q.shape[0]
    idx = _iota((c, 1), 0).astype(f32)
    diff = (_iota((c, c), 0) - _iota((c, c), 1)).astype(f32)
    causal = _causal(c)
    hs = range(RET_HEADS)
    cols = [slice(h * RET_D, (h + 1) * RET_D) for h in hs]
    lg = _RET_LOG_GAMMA
    qh = [(q[:, s] * cos + roll_half(q[:, s]) * sin) * (RET_D ** -0.5) for s in cols]
    kh = [k[:, s] * cos + roll_half(k[:, s]) * sin for s in cols]
    vh = [v[:, s] for s in cols]
    sh = [state[s, :] for s in cols]
    scores = [mm_nt(qh[h], kh[h]) * jnp.exp(jnp.where(causal, lg[h] * diff, -jnp.inf)) for h in hs]
    inter = [mm(qh[h] * jnp.exp(lg[h] * (idx + 1.0)), sh[h]) for h in hs]
    y = [mm(scores[h], vh[h]) + inter[h] for h in hs]
    states = [sh[h] * math.exp(lg[h] * c) + mm_tn(kh[h] * jnp.exp(lg[h] * (c - 1.0 - idx)), vh[h]) for h in hs]
    outs = [unit_rms(y[h]) * silu(gate[:, cols[h]]) for h in hs]
    return jnp.concatenate(outs, axis=1), jnp.concatenate(states, axis=0)


def _head_select(n_heads, width):
    r, c = _iota((LANES, n_heads * width), 0), _iota((LANES, n_heads * width), 1)
    return (c // width == r).astype(f32)


def ssd_chunk(z, xe, dtr, state, conv_w, conv_b, dt_bias, a_log, d_skip, norm_w):
    c = z.shape[0]
    xbc = silu(_conv_rows(xe, conv_w) + conv_b)
    xs, bm, cm = xbc[:, :SSD_INNER], xbc[:, SSD_INNER:SSD_INNER + 256], xbc[:, SSD_INNER + 256:]
    sel = _head_select(SSD_HEADS, SSD_P)
    dt = softplus(dtr + dt_bias)
    la = dt * (-jnp.exp(a_log))
    la_cum = cumsum01(_tril_ones(c), la)
    la_cum_t = la_cum.T
    last = jnp.sum(la, axis=0, keepdims=True)
    xd = xs * spread01(dt, sel)
    la_x = spread01(la_cum, sel)
    last_x = spread01(last, sel)
    to_end = jnp.exp(last_x - la_x)
    from_start = jnp.exp(la_x)
    causal = _causal(c)
    left = (_iota((1, LANES), 1) < SSD_P).astype(f32)
    upper = _iota((LANES, 1), 0) < SSD_P
    pairs, heads = range(SSD_HEADS // 2), range(SSD_HEADS)
    bc = [bm[:, g * SSD_N:(g + 1) * SSD_N] for g in range(SSD_GROUPS)]
    cc = [cm[:, g * SSD_N:(g + 1) * SSD_N] for g in range(SSD_GROUPS)]
    cb = [mm_nt(cc[g], bc[g]) for g in range(SSD_GROUPS)]
    cols = [slice(p * LANES, (p + 1) * LANES) for p in pairs]
    xd_p = [xd[:, s] for s in cols]
    sp = [state[s, :] for s in cols]
    lmat = [jnp.exp(jnp.where(causal, la_cum[:, h:h + 1] - la_cum_t[h:h + 1, :], -jnp.inf)) for h in heads]
    off = [mm_nt(cc[p // 2], sp[p]) * from_start[:, cols[p]] for p in pairs]
    diag = [mm(cb[h // 4] * lmat[h], xd_p[h // 2] * (left if h % 2 == 0 else 1.0 - left)) for h in heads]
    cd = [jnp.where(upper, jnp.exp(last[:, 2 * p:2 * p + 1]), jnp.exp(last[:, 2 * p + 1:2 * p + 2])) for p in pairs]
    states = [sp[p] * cd[p] + mm_tn(xd_p[p] * to_end[:, cols[p]], bc[p // 2]) for p in pairs]
    ys = [off[p] + diag[2 * p] + diag[2 * p + 1] for p in pairs]
    y = jnp.concatenate(ys, axis=1) + spread01(d_skip, sel) * xs
    yg = y * silu(z)
    half = SSD_INNER // SSD_GROUPS
    out = jnp.concatenate([unit_rms(yg[:, i * half:(i + 1) * half]) for i in range(SSD_GROUPS)], axis=1) * norm_w
    return out, jnp.concatenate(states, axis=0)


def mm3(a, b):
    return jnp.dot(a, b, precision=lax.Precision.HIGH, preferred_element_type=f32)


@jax.custom_vjp
def _unit_lower_inverses(lowers):
    n = lowers[0].shape[0]
    eye = (_iota((n, n), 0) == _iota((n, n), 1)).astype(f32)
    a = [-l for l in lowers]
    p = [eye + x for x in a]
    k = 2
    while k < n:
        a = [mm3(x, x) for x in a]
        p = [y + mm3(y, x) for y, x in zip(p, a)]
        k *= 2
    return p


def _unit_lower_inverses_bwd(t_inv, g):
    dims_tn, dims_nt = (((0,), (0,)), ((), ())), (((1,), (1,)), ((), ()))
    x = [lax.dot_general(t, gi, dims_tn, precision=lax.Precision.HIGH, preferred_element_type=f32) for t, gi in zip(t_inv, g)]
    return ([-lax.dot_general(xi, t, dims_nt, precision=lax.Precision.HIGH, preferred_element_type=f32) for xi, t in zip(x, t_inv)],)


def _unit_lower_inverses_fwd(lowers):
    t_inv = _unit_lower_inverses(lowers)
    return t_inv, t_inv


_unit_lower_inverses.defvjp(_unit_lower_inverses_fwd, _unit_lower_inverses_bwd)


GDN_GCOL = 6


def gdn_chunk(xe, z, ba, state, conv_w, a_log, dt_bias, norm_w):
    c = z.shape[0]
    qkv = silu(_conv_rows(xe, conv_w))
    beta_all = jax.nn.sigmoid(ba)
    g_all = -jnp.exp(a_log) * softplus(ba + dt_bias)
    gc = cumsum01(_tril_ones(c), g_all)
    gc_t = gc.T
    last = jnp.sum(g_all, axis=0, keepdims=True)
    causal, strict = _causal(c), _causal(c, strict=True)
    hs = range(GDN_HEADS)
    cols = [slice(h * GDN_D, (h + 1) * GDN_D) for h in hs]
    qh = [qkv[:, h * GDN_D:(h + 1) * GDN_D] for h in hs]
    kh = [qkv[:, GDN_W + h * GDN_D:GDN_W + (h + 1) * GDN_D] for h in hs]
    vh = [qkv[:, 2 * GDN_W + h * GDN_D:2 * GDN_W + (h + 1) * GDN_D] for h in hs]
    qh = [t * lax.rsqrt(jnp.sum(t * t, axis=-1, keepdims=True) + EPS) * (GDN_D ** -0.5) for t in qh]
    kh = [t * lax.rsqrt(jnp.sum(t * t, axis=-1, keepdims=True) + EPS) for t in kh]
    beta = [beta_all[:, h:h + 1] for h in hs]
    col = [gc[:, GDN_GCOL + h:GDN_GCOL + h + 1] for h in hs]
    row = [gc_t[GDN_GCOL + h:GDN_GCOL + h + 1, :] for h in hs]
    lst = [last[:, GDN_GCOL + h:GDN_GCOL + h + 1] for h in hs]
    decay = [jnp.exp(jnp.where(causal, col[h] - row[h], -jnp.inf)) for h in hs]
    e_col = [jnp.exp(t) for t in col]
    kb = [kh[h] * beta[h] for h in hs]
    vb = [vh[h] * beta[h] for h in hs]
    kk = [mm_nt(kb[h], kh[h]) for h in hs]
    qk = [mm_nt(qh[h], kh[h]) for h in hs]
    t_inv = _unit_lower_inverses([jnp.where(strict, kk[h] * decay[h], 0.0) for h in hs])
    u = [mm(t_inv[h], vb[h]) for h in hs]
    w = [mm(t_inv[h], kb[h] * e_col[h]) for h in hs]
    attn = [jnp.where(causal, qk[h] * decay[h], 0.0) for h in hs]
    sh = [state[s, :] for s in cols]
    ws = [mm(w[h], sh[h]) for h in hs]
    qs = [mm(qh[h] * e_col[h], sh[h]) for h in hs]
    v_new = [u[h] - ws[h] for h in hs]
    o = [qs[h] + mm(attn[h], v_new[h]) for h in hs]
    states = [sh[h] * jnp.exp(lst[h]) + mm_tn(kh[h] * jnp.exp(lst[h] - col[h]), v_new[h]) for h in hs]
    outs = [unit_rms(o[h]) * norm_w * silu(z[:, cols[h]]) for h in hs]
    return jnp.concatenate(outs, axis=1), jnp.concatenate(states, axis=0)


def _s5_group_mask():
    r, c = _iota((S5_CH, S5_W), 0), _iota((S5_CH, S5_W), 1)
    return (r // S5_GROUP == c // S5_STATE).astype(f32)


def s5_prep(a_re, a_im, log_step, b_re, b_im, c_re, c_im):
    r, c = _iota((LANES, S5_W), 0), _iota((LANES, S5_W), 1)
    step = jnp.exp(mmh(log_step, (c // S5_STATE == r).astype(f32)))
    zr, zi = a_re * step, a_im * step
    e = jnp.exp(zr)
    lr, li = e * jnp.cos(zi), e * jnp.sin(zi)
    den = a_re * a_re + a_im * a_im
    xr, xi = lr - 1.0, li
    cr, ci = (xr * a_re + xi * a_im) / den, (xi * a_re - xr * a_im) / den
    bbr, bbi = cr * b_re - ci * b_im, cr * b_im + ci * b_re
    mask = _s5_group_mask()
    tile = lambda t: jnp.tile(t, (S5_GROUPS, 1)) * mask
    return lr, li, tile(bbr), tile(bbi), tile(c_re), tile(c_im)


def s5_out_fn(h_re, h_im, u, cblk_re, cblk_im, d_skip, w_glu, b_glu):
    y = mm_nt(h_re, cblk_re) - mm_nt(h_im, cblk_im) + d_skip * u
    y = jax.nn.gelu(y)
    return y * jax.nn.sigmoid(mm(y, w_glu) + b_glu)


def _params(sem, **kw):
    return pltpu.CompilerParams(dimension_semantics=sem, vmem_limit_bytes=VMEM_LIMIT, **kw)


def _const_spec(shape):
    return pl.BlockSpec(shape, lambda i: (0,) * len(shape))


def _resident_spec(shape):
    return pl.BlockSpec(shape, lambda i: (0,) * len(shape), pipeline_mode=pl.Buffered(1))


def _row_spec(rows, cols, col_block=0):
    return pl.BlockSpec((rows, cols), lambda i: (i, col_block))


class Seq:
    def __init__(self, array, width, col_block, kind="tile"):
        self.array, self.width, self.col_block, self.kind = array, width, col_block, kind


CHUNKS_PER_STEP = 4


def mixer_fwd(name, fn, seqs, params, out_width, state_rows, seq_len, per_step=CHUNKS_PER_STEP, exchange=None):
    nc = seq_len // CHUNK
    rows, steps = per_step * CHUNK, nc // per_step
    n_refs = sum(2 if s.kind == "halo" else 1 for s in seqs)
    n_par = len(params)
    car = Carried(exchange)

    def body(*refs):
        seq_refs, par_refs = refs[:n_refs], refs[n_refs:n_refs + n_par]
        k0 = n_refs + n_par
        ex_ins, (out_ref, st_ref) = refs[k0:k0 + car.n], refs[k0 + car.n:k0 + car.n + 2]
        ex_outs, state, ex_sems = refs[k0 + car.n + 2:k0 + 2 * car.n + 2], refs[k0 + 2 * car.n + 2], refs[k0 + 2 * car.n + 3:]
        c = pl.program_id(0)
        car.start_at(c == 0, ex_ins, ex_outs, ex_sems)

        @pl.when(c == 0)
        def _():
            state[...] = jnp.zeros_like(state)

        par_vals = [p[...] for p in par_refs]
        s_cur = state[...]
        for kk in range(per_step):
            lo = kk * CHUNK
            vals, k = [], 0
            for s in seqs:
                if s.kind == "halo":
                    prev = jnp.where(c > 0, seq_refs[k][...], 0.0) if kk == 0 else seq_refs[k + 1][lo - HALO:lo, :]
                    vals.append(jnp.concatenate([prev, seq_refs[k + 1][lo:lo + CHUNK, :]], axis=0))
                    k += 2
                else:
                    vals.append(seq_refs[k][lo:lo + CHUNK, :])
                    k += 1
            st_ref[kk] = s_cur
            out, s_cur = fn(*vals, s_cur, *par_vals)
            out_ref[lo:lo + CHUNK, :] = out
        state[...] = s_cur
        car.wait_at(c == steps - 1, ex_ins, ex_outs, ex_sems)

    in_specs, operands = [], []
    for s in seqs:
        if s.kind == "halo":
            rb, w, cb = rows // HALO, s.width, s.col_block
            in_specs.append(pl.BlockSpec((HALO, w), lambda i, rb=rb, cb=cb: (jnp.maximum(i * rb - 1, 0), cb)))
            operands.append(s.array)
        in_specs.append(pl.BlockSpec((rows, s.width), lambda i, cb=s.col_block: (i, cb)))
        operands.append(s.array)
    for p in params:
        in_specs.append(_const_spec(p.shape))
        operands.append(p)
    outs = pl.pallas_call(
        body, grid=(steps,), in_specs=in_specs + car.in_specs,
        out_specs=[pl.BlockSpec((rows, out_width), lambda i: (i, 0)),
                   pl.BlockSpec((per_step, state_rows, LANES), lambda i: (i, 0, 0))] + car.out_specs,
        out_shape=[jax.ShapeDtypeStruct((seq_len, out_width), f32),
                   jax.ShapeDtypeStruct((nc, state_rows, LANES), f32)] + car.out_shape,
        scratch_shapes=[pltpu.VMEM((state_rows, LANES), f32)] + car.scratch,
        compiler_params=_params(("arbitrary",)), name=name)(*operands, *car.operands)
    return outs[0], outs[1], outs[2:]


def mixer_bwd(name, fn, seqs, params, states, d_out, seq_len, per_step=CHUNKS_PER_STEP, exchange=None):
    nc = seq_len // CHUNK
    rows, steps = per_step * CHUNK, nc // per_step
    state_rows = states.shape[1]
    diff = [s for s in seqs if s.kind != "const"]
    halos = [s for s in diff if s.kind == "halo"]
    n_refs = sum(2 if s.kind == "halo" else 1 for s in seqs)
    n_par = len(params)
    car = Carried(exchange)

    def body(*refs):
        seq_refs, par_refs = refs[:n_refs], refs[n_refs:n_refs + n_par]
        st_ref, dout_ref = refs[n_refs + n_par:n_refs + n_par + 2]
        k0 = n_refs + n_par + 2
        ex_ins = refs[k0:k0 + car.n]
        k0 += car.n
        dseq_refs, dpar_refs = refs[k0:k0 + len(diff)], refs[k0 + len(diff):k0 + len(diff) + n_par]
        k0 += len(diff) + n_par
        ex_outs = refs[k0:k0 + car.n]
        scratch = refs[k0 + car.n:]
        d_state, carries, ex_sems = scratch[0], scratch[1:1 + len(halos)], scratch[1 + len(halos):]
        i = pl.program_id(0)
        step = steps - 1 - i
        car.start_at(i == 0, ex_ins, ex_outs, ex_sems)

        @pl.when(i == 0)
        def _():
            d_state[...] = jnp.zeros_like(d_state)
            for r in list(carries) + list(dpar_refs):
                r[...] = jnp.zeros_like(r)

        par_vals = [p[...] for p in par_refs]
        d_s = d_state[...]
        d_par = [None] * n_par
        halo_ct = [r[...] for r in carries]
        for kk in range(per_step - 1, -1, -1):
            lo = kk * CHUNK
            dvals, consts, k = [], [], 0
            for s in seqs:
                if s.kind == "halo":
                    prev = jnp.where(step > 0, seq_refs[k][...], 0.0) if kk == 0 else seq_refs[k + 1][lo - HALO:lo, :]
                    dvals.append(jnp.concatenate([prev, seq_refs[k + 1][lo:lo + CHUNK, :]], axis=0))
                    k += 2
                elif s.kind == "tile":
                    dvals.append(seq_refs[k][lo:lo + CHUNK, :])
                    k += 1
                else:
                    consts.append(seq_refs[k][lo:lo + CHUNK, :])
                    k += 1
            nd = len(dvals)

            def call(*a, consts=consts, nd=nd):
                it_d, it_c = iter(a[:nd]), iter(consts)
                vals = [next(it_c) if s.kind == "const" else next(it_d) for s in seqs]
                return fn(*vals, *a[nd:])

            _, vjp = jax.vjp(call, *dvals, st_ref[kk], *par_vals)
            cts = vjp((dout_ref[lo:lo + CHUNK, :], d_s))
            hk = 0
            for j, s in enumerate(diff):
                if s.kind == "halo":
                    dseq_refs[j][lo:lo + CHUNK, :] = cts[j][HALO:, :]
                    dseq_refs[j][lo + CHUNK - HALO:lo + CHUNK, :] += halo_ct[hk]
                    halo_ct[hk] = cts[j][:HALO, :]
                    hk += 1
                else:
                    dseq_refs[j][lo:lo + CHUNK, :] = cts[j]
            d_s = cts[nd]
            for j in range(n_par):
                d_par[j] = cts[nd + 1 + j] if d_par[j] is None else d_par[j] + cts[nd + 1 + j]
        d_state[...] = d_s
        for r, v in zip(carries, halo_ct):
            r[...] = v
        for j in range(n_par):
            dpar_refs[j][...] += d_par[j]
        car.wait_at(i == steps - 1, ex_ins, ex_outs, ex_sems)

    in_specs, operands = [], []
    for s in seqs:
        if s.kind == "halo":
            rb, cb = rows // HALO, s.col_block
            in_specs.append(pl.BlockSpec((HALO, s.width), lambda i, rb=rb, cb=cb: (jnp.maximum((steps - 1 - i) * rb - 1, 0), cb)))
            operands.append(s.array)
        in_specs.append(pl.BlockSpec((rows, s.width), lambda i, cb=s.col_block: (steps - 1 - i, cb)))
        operands.append(s.array)
    for p in params:
        in_specs.append(_const_spec(p.shape))
        operands.append(p)
    in_specs.append(pl.BlockSpec((per_step, state_rows, LANES), lambda i: (steps - 1 - i, 0, 0)))
    in_specs.append(pl.BlockSpec((rows, d_out.shape[1]), lambda i: (steps - 1 - i, 0)))
    operands += [states, d_out]
    outs = pl.pallas_call(
        body, grid=(steps,), in_specs=in_specs + car.in_specs,
        out_specs=[pl.BlockSpec((rows, s.width), lambda i: (steps - 1 - i, 0)) for s in diff]
        + [_const_spec(p.shape) for p in params] + car.out_specs,
        out_shape=[jax.ShapeDtypeStruct((seq_len, s.width), f32) for s in diff]
        + [jax.ShapeDtypeStruct(p.shape, f32) for p in params] + car.out_shape,
        scratch_shapes=[pltpu.VMEM((state_rows, LANES), f32)] + [pltpu.VMEM((HALO, s.width), f32) for s in halos]
        + car.scratch,
        compiler_params=_params(("arbitrary",)), name=name)(*operands, *car.operands)
    nd = len(diff)
    return outs[:nd], outs[nd:nd + n_par], outs[nd + n_par:]


TOK = 512


def inproj_fwd(name, x, nw, w):
    seq_len, n = x.shape[0], w.shape[1]

    def body(x_ref, nw_ref, w_ref, o_ref):
        o_ref[...] = mm(rmsnorm_f(x_ref[...], nw_ref[...]), w_ref[...])

    return pl.pallas_call(
        body, grid=(seq_len // TOK,),
        in_specs=[_row_spec(TOK, D_MODEL), _const_spec(nw.shape), _resident_spec(w.shape)],
        out_specs=_row_spec(TOK, n), out_shape=jax.ShapeDtypeStruct((seq_len, n), f32),
        compiler_params=_params(("arbitrary",)), name=name)(x, nw, w)


def inproj_bwd(name, x, nw, w, pieces, d_res):
    seq_len, n = x.shape[0], w.shape[1]
    widths = [p.shape[1] for p in pieces]
    assert sum(widths) == n
    k = len(pieces)

    def body(*refs):
        x_ref, nw_ref, w_ref = refs[:3]
        p_refs, dres_ref = refs[3:3 + k], refs[3 + k]
        dx_ref, dw_ref, dnw_ref = refs[4 + k:]

        @pl.when(pl.program_id(0) == 0)
        def _():
            dw_ref[...] = jnp.zeros_like(dw_ref)
            dnw_ref[...] = jnp.zeros_like(dnw_ref)

        h, vjp = jax.vjp(rmsnorm_f, x_ref[...], nw_ref[...])
        dh, off = jnp.zeros_like(h), 0
        for p_ref, wd in zip(p_refs, widths):
            g = p_ref[...]
            dh = dh + mm_nt(g, w_ref[:, off:off + wd])
            dw_ref[:, off:off + wd] += mm_tn(h, g)
            off += wd
        dx, dnw = vjp(dh)
        dx_ref[...] = dres_ref[...] + dx
        dnw_ref[...] += dnw

    return pl.pallas_call(
        body, grid=(seq_len // TOK,),
        in_specs=[_row_spec(TOK, D_MODEL), _const_spec(nw.shape), _resident_spec(w.shape)]
        + [_row_spec(TOK, wd) for wd in widths] + [_row_spec(TOK, D_MODEL)],
        out_specs=[_row_spec(TOK, D_MODEL), _resident_spec((D_MODEL, n)), _const_spec(nw.shape)],
        out_shape=[jax.ShapeDtypeStruct((seq_len, D_MODEL), f32), jax.ShapeDtypeStruct((D_MODEL, n), f32),
                   jax.ShapeDtypeStruct(nw.shape, f32)],
        compiler_params=_params(("arbitrary",)), name=name)(x, nw, w, *pieces, d_res)


def outproj_fwd(name, x, a, b, w):
    seq_len, wa, wb = x.shape[0], a.shape[1], b.shape[1]

    def body(x_ref, a_ref, b_ref, w_ref, o_ref):
        o_ref[...] = x_ref[...] + mm(a_ref[...], w_ref[:wa, :]) + mm(b_ref[...], w_ref[wa:, :])

    return pl.pallas_call(
        body, grid=(seq_len // TOK,),
        in_specs=[_row_spec(TOK, D_MODEL), _row_spec(TOK, wa), _row_spec(TOK, wb), _resident_spec(w.shape)],
        out_specs=_row_spec(TOK, D_MODEL), out_shape=jax.ShapeDtypeStruct((seq_len, D_MODEL), f32),
        compiler_params=_params(("arbitrary",)), name=name)(x, a, b, w)


def outproj_bwd(name, dy, a, b, w):
    seq_len, wa, wb = dy.shape[0], a.shape[1], b.shape[1]

    def body(dy_ref, a_ref, b_ref, w_ref, da_ref, db_ref, dw_ref):
        @pl.when(pl.program_id(0) == 0)
        def _():
            dw_ref[...] = jnp.zeros_like(dw_ref)

        g = dy_ref[...]
        da_ref[...] = mm_nt(g, w_ref[:wa, :])
        db_ref[...] = mm_nt(g, w_ref[wa:, :])
        dw_ref[:wa, :] += mm_tn(a_ref[...], g)
        dw_ref[wa:, :] += mm_tn(b_ref[...], g)

    return pl.pallas_call(
        body, grid=(seq_len // TOK,),
        in_specs=[_row_spec(TOK, D_MODEL), _row_spec(TOK, wa), _row_spec(TOK, wb), _resident_spec(w.shape)],
        out_specs=[_row_spec(TOK, wa), _row_spec(TOK, wb), _resident_spec(w.shape)],
        out_shape=[jax.ShapeDtypeStruct((seq_len, wa), f32), jax.ShapeDtypeStruct((seq_len, wb), f32),
                   jax.ShapeDtypeStruct(w.shape, f32)],
        compiler_params=_params(("arbitrary",)), name=name)(dy, a, b, w)


FF_BLOCK = D_FF // N_DEV


def mlp_fwd(name, x, nw, w_up, w_down, exchange=None):
    seq_len = x.shape[0]
    nt = seq_len // TOK
    car = Carried(exchange)

    def body(*refs):
        x_ref, nw_ref, up_ref, down_ref = refs[:4]
        ex_ins, (o_ref, relu_ref) = refs[4:4 + car.n], refs[4 + car.n:6 + car.n]
        ex_outs, ex_sems = refs[6 + car.n:6 + 2 * car.n], refs[6 + 2 * car.n:]
        i = pl.program_id(0)
        car.start_at(i == 0, ex_ins, ex_outs, ex_sems)
        xv = x_ref[...]
        h = rmsnorm_f(xv, nw_ref[...])
        acc = xv
        for d in range(N_DEV):
            r = jnp.maximum(mm(h, up_ref[d]), 0.0)
            relu_ref[d] = r.astype(_MXU)
            acc = acc + mm(r * r, down_ref[d])
        o_ref[...] = acc
        car.wait_at(i == nt - 1, ex_ins, ex_outs, ex_sems)

    outs = pl.pallas_call(
        body, grid=(nt,),
        in_specs=[_row_spec(TOK, D_MODEL), _const_spec(nw.shape), _resident_spec(w_up.shape), _resident_spec(w_down.shape)]
        + car.in_specs,
        out_specs=[_row_spec(TOK, D_MODEL), pl.BlockSpec((N_DEV, TOK, FF_BLOCK), lambda i: (0, i, 0))] + car.out_specs,
        out_shape=[jax.ShapeDtypeStruct((seq_len, D_MODEL), f32),
                   jax.ShapeDtypeStruct((N_DEV, seq_len, FF_BLOCK), _MXU)] + car.out_shape, scratch_shapes=car.scratch,
        compiler_params=_params(("arbitrary",)), name=name)(x, nw, w_up, w_down, *car.operands)
    return outs[0], outs[1], outs[2:]


MLP_SPLIT = 2


def mlp_bwd(name, x, nw, w_up, w_down, relu, dy, exchange=None):
    seq_len = x.shape[0]
    nt = seq_len // TOK
    per = N_DEV // MLP_SPLIT
    car = Carried(exchange)

    def body(*refs):
        x_ref, nw_ref, up_ref, down_ref, relu_ref, dy_ref = refs[:6]
        ex_ins, (dh_ref, dup_ref, ddown_ref) = refs[6:6 + car.n], refs[6 + car.n:9 + car.n]
        ex_outs, ex_sems = refs[9 + car.n:9 + 2 * car.n], refs[9 + 2 * car.n:]
        j, i = pl.program_id(0), pl.program_id(1)
        car.start_at(jnp.logical_and(j == 0, i == 0), ex_ins, ex_outs, ex_sems)

        @pl.when(i == 0)
        def _():
            dup_ref[...] = jnp.zeros_like(dup_ref)
            ddown_ref[...] = jnp.zeros_like(ddown_ref)

        h = rmsnorm_f(x_ref[...], nw_ref[...])
        g = dy_ref[...]
        dh = jnp.zeros_like(h)
        for d in range(per):
            r = relu_ref[d].astype(f32)
            da = mm_nt(g, down_ref[d]) * (2.0 * r)
            ddown_ref[d] += mm_tn(r * r, g)
            dup_ref[d] += mm_tn(h, da)
            dh = dh + mm_nt(da, up_ref[d])
        dh_ref[...] = dh
        car.wait_at(jnp.logical_and(j == MLP_SPLIT - 1, i == nt - 1), ex_ins, ex_outs, ex_sems)

    outs = pl.pallas_call(
        body, grid=(MLP_SPLIT, nt),
        in_specs=[pl.BlockSpec((TOK, D_MODEL), lambda j, i: (i, 0)), pl.BlockSpec(nw.shape, lambda j, i: (0, 0)),
                  pl.BlockSpec((per, D_MODEL, FF_BLOCK), lambda j, i: (j, 0, 0), pipeline_mode=pl.Buffered(1)),
                  pl.BlockSpec((per, FF_BLOCK, D_MODEL), lambda j, i: (j, 0, 0), pipeline_mode=pl.Buffered(1)),
                  pl.BlockSpec((per, TOK, FF_BLOCK), lambda j, i: (j, i, 0)),
                  pl.BlockSpec((TOK, D_MODEL), lambda j, i: (i, 0))] + car.in_specs,
        out_specs=[pl.BlockSpec((None, TOK, D_MODEL), lambda j, i: (j, i, 0)),
                   pl.BlockSpec((per, D_MODEL, FF_BLOCK), lambda j, i: (j, 0, 0), pipeline_mode=pl.Buffered(1)),
                   pl.BlockSpec((per, FF_BLOCK, D_MODEL), lambda j, i: (j, 0, 0), pipeline_mode=pl.Buffered(1))]
        + car.out_specs,
        out_shape=[jax.ShapeDtypeStruct((MLP_SPLIT, seq_len, D_MODEL), f32),
                   jax.ShapeDtypeStruct(w_up.shape, f32), jax.ShapeDtypeStruct(w_down.shape, f32)] + car.out_shape,
        scratch_shapes=car.scratch,
        compiler_params=_params(("arbitrary", "arbitrary")), name=name)(x, nw, w_up, w_down, relu, dy, *car.operands)
    dh_parts, d_up, d_down, ex_results = outs[0], outs[1], outs[2], outs[3:]

    def norm_body(x_ref, nw_ref, dh_ref, dy_ref, dx_ref, dnw_ref):
        @pl.when(pl.program_id(0) == 0)
        def _():
            dnw_ref[...] = jnp.zeros_like(dnw_ref)

        _, vjp = jax.vjp(rmsnorm_f, x_ref[...], nw_ref[...])
        dh = dh_ref[0]
        for j in range(1, MLP_SPLIT):
            dh = dh + dh_ref[j]
        dx, dnw = vjp(dh)
        dx_ref[...] = dy_ref[...] + dx
        dnw_ref[...] += dnw

    dx, dnw = pl.pallas_call(
        norm_body, grid=(nt,),
        in_specs=[_row_spec(TOK, D_MODEL), _const_spec(nw.shape),
                  pl.BlockSpec((MLP_SPLIT, TOK, D_MODEL), lambda i: (0, i, 0)), _row_spec(TOK, D_MODEL)],
        out_specs=[_row_spec(TOK, D_MODEL), _const_spec(nw.shape)],
        out_shape=[jax.ShapeDtypeStruct((seq_len, D_MODEL), f32), jax.ShapeDtypeStruct(nw.shape, f32)],
        compiler_params=_params(("arbitrary",)), name=name + "_norm")(x, nw, dh_parts, dy)
    return dx, d_up, d_down, dnw, ex_results


def final_loss(name, x, nw, target):
    seq_len = x.shape[0]

    def body(x_ref, nw_ref, t_ref, loss_ref, dx_ref, dnw_ref):
        @pl.when(pl.program_id(0) == 0)
        def _():
            loss_ref[...] = jnp.zeros_like(loss_ref)
            dnw_ref[...] = jnp.zeros_like(dnw_ref)

        y, vjp = jax.vjp(rmsnorm_f, x_ref[...], nw_ref[...])
        err = y - t_ref[...]
        loss_ref[...] += 0.5 * jnp.sum(jnp.mean(err * err, axis=-1, keepdims=True), axis=0, keepdims=True)
        dx, dnw = vjp(err * (1.0 / D_MODEL))
        dx_ref[...] = dx
        dnw_ref[...] += dnw

    return pl.pallas_call(
        body, grid=(seq_len // TOK,),
        in_specs=[_row_spec(TOK, D_MODEL), _const_spec(nw.shape), _row_spec(TOK, D_MODEL)],
        out_specs=[_const_spec((8, LANES)), _row_spec(TOK, D_MODEL), _const_spec(nw.shape)],
        out_shape=[jax.ShapeDtypeStruct((8, LANES), f32), jax.ShapeDtypeStruct((seq_len, D_MODEL), f32),
                   jax.ShapeDtypeStruct(nw.shape, f32)],
        compiler_params=_params(("arbitrary",)), name=name)(x, nw, target)


def _whole(a):
    return pl.BlockSpec(a.shape, lambda: (0,) * len(a.shape))


def s5_prep_fwd(name, raw):
    def body(*refs):
        outs = s5_prep(*[r[...] for r in refs[:7]])
        for o_ref, o in zip(refs[7:], outs):
            o_ref[...] = o

    shapes = [(1, S5_W)] * 2 + [(S5_CH, S5_W)] * 4
    return pl.pallas_call(
        body, in_specs=[_whole(a) for a in raw], out_specs=[pl.BlockSpec(s, lambda s=s: (0,) * len(s)) for s in shapes],
        out_shape=[jax.ShapeDtypeStruct(s, f32) for s in shapes],
        compiler_params=pltpu.CompilerParams(vmem_limit_bytes=VMEM_LIMIT), name=name)(*raw)


def s5_prep_bwd(name, raw, cts):
    def body(*refs):
        _, vjp = jax.vjp(s5_prep, *[r[...] for r in refs[:7]])
        grads = vjp(tuple(r[...] for r in refs[7:13]))
        for o_ref, g in zip(refs[13:], grads):
            o_ref[...] = g

    return pl.pallas_call(
        body, in_specs=[_whole(a) for a in list(raw) + list(cts)], out_specs=[_whole(a) for a in raw],
        out_shape=[jax.ShapeDtypeStruct(a.shape, f32) for a in raw],
        compiler_params=pltpu.CompilerParams(vmem_limit_bytes=VMEM_LIMIT), name=name)(*raw, *cts)


SCAN_SEG = 8
SCAN_LEN = TOK // SCAN_SEG


def _segment_major():
    r, t = _iota((TOK, TOK), 0), _iota((TOK, TOK), 1)
    return (t == (r % SCAN_SEG) * SCAN_LEN + r // SCAN_SEG).astype(f32)


def _store_powers(lr, li, pr_ref, pi_ref):
    qr, qi = lr, li
    for j in range(SCAN_LEN):
        pr_ref[j:j + 1, :] = qr
        pi_ref[j:j + 1, :] = qi
        qr, qi = lr * qr - li * qi, lr * qi + li * qr


def _tile_scan(xr_ref, xi_ref, lr, li, pr_ref, pi_ref, cr_ref, ci_ref, carry_re, carry_im, reverse):
    sign = -1.0 if reverse else 1.0
    ar = jnp.broadcast_to(lr, (SCAN_SEG, lr.shape[1]))
    ai = jnp.broadcast_to(sign * li, (SCAN_SEG, li.shape[1]))
    rows = lambda j: slice(j * SCAN_SEG, (j + 1) * SCAN_SEG)
    hr = hi = jnp.zeros_like(ar)
    for j in (range(SCAN_LEN - 1, -1, -1) if reverse else range(SCAN_LEN)):
        hr, hi = ar * hr - ai * hi + xr_ref[rows(j), :], ar * hi + ai * hr + xi_ref[rows(j), :]
        xr_ref[rows(j), :] = hr
        xi_ref[rows(j), :] = hi
    wr, wi = pr_ref[SCAN_LEN - 1:SCAN_LEN, :], sign * pi_ref[SCAN_LEN - 1:SCAN_LEN, :]
    er, ei = carry_re[0:1, :], carry_im[0:1, :]
    for s in (range(SCAN_SEG - 1, -1, -1) if reverse else range(SCAN_SEG)):
        cr_ref[s:s + 1, :] = er
        ci_ref[s:s + 1, :] = ei
        er, ei = hr[s:s + 1, :] + wr * er - wi * ei, hi[s:s + 1, :] + wr * ei + wi * er
    carry_re[0:1, :] = er
    carry_im[0:1, :] = ei
    cr, ci = cr_ref[...], ci_ref[...]
    for j in range(SCAN_LEN):
        k = SCAN_LEN - 1 - j if reverse else j
        qr, qi = pr_ref[k:k + 1, :], sign * pi_ref[k:k + 1, :]
        xr_ref[rows(j), :] += qr * cr - qi * ci
        xi_ref[rows(j), :] += qr * ci + qi * cr


_SCAN_SCRATCH = [pltpu.VMEM((HALO, S5_W), f32), pltpu.VMEM((HALO, S5_W), f32),
                 pltpu.VMEM((SCAN_LEN, S5_W), f32), pltpu.VMEM((SCAN_LEN, S5_W), f32),
                 pltpu.VMEM((SCAN_SEG, S5_W), f32), pltpu.VMEM((SCAN_SEG, S5_W), f32)]


def s5_fwd(name, proj, u_block, lam_re, lam_im, bblk_re, bblk_im, cblk_re, cblk_im, d_skip, w_glu, b_glu):
    seq_len = proj.shape[0]

    def body(u_ref, lr_ref, li_ref, br_ref, bi_ref, cr_ref, ci_ref, d_ref, wg_ref, bg_ref,
             o_ref, hr_ref, hi_ref, carry_re, carry_im, pw_re, pw_im, cb_re, cb_im):
        lr, li = lr_ref[...], li_ref[...]

        @pl.when(pl.program_id(0) == 0)
        def _():
            carry_re[...] = jnp.zeros_like(carry_re)
            carry_im[...] = jnp.zeros_like(carry_im)
            _store_powers(lr, li, pw_re, pw_im)

        perm = _segment_major()
        u = _exact01(u_ref[...], perm, ((1,), (0,)), True)
        hr_ref[...] = mm(u, br_ref[...])
        hi_ref[...] = mm(u, bi_ref[...])
        _tile_scan(hr_ref, hi_ref, lr, li, pw_re, pw_im, cb_re, cb_im, carry_re, carry_im, reverse=False)
        out = s5_out_fn(hr_ref[...], hi_ref[...], u, cr_ref[...], ci_ref[...], d_ref[...],
                        wg_ref[...].astype(f32), bg_ref[...])
        o_ref[...] = _exact01(out, perm, ((0,), (0,)), True)

    consts = [lam_re, lam_im, bblk_re, bblk_im, cblk_re, cblk_im, d_skip, w_glu, b_glu]
    return pl.pallas_call(
        body, grid=(seq_len // TOK,),
        in_specs=[_row_spec(TOK, S5_CH, u_block)] + [_const_spec(a.shape) for a in consts],
        out_specs=[_row_spec(TOK, S5_CH), _row_spec(TOK, S5_W), _row_spec(TOK, S5_W)],
        out_shape=[jax.ShapeDtypeStruct((seq_len, S5_CH), f32), jax.ShapeDtypeStruct((seq_len, S5_W), f32),
                   jax.ShapeDtypeStruct((seq_len, S5_W), f32)],
        scratch_shapes=_SCAN_SCRATCH,
        compiler_params=_params(("arbitrary",)), name=name)(proj, *consts)


def s5_bwd(name, proj, u_block, h_re, h_im, d_out, lam_re, lam_im, bblk_re, bblk_im, cblk_re, cblk_im, d_skip, w_glu, b_glu):
    seq_len = proj.shape[0]
    nt = seq_len // TOK
    consts = [lam_re, lam_im, bblk_re, bblk_im, cblk_re, cblk_im, d_skip, w_glu, b_glu]

    def body(u_ref, hr_ref, hi_ref, pr_ref, pi_ref, dout_ref, lr_ref, li_ref, br_ref, bi_ref, cr_ref, ci_ref, d_ref, wg_ref, bg_ref,
             du_ref, dlr_ref, dli_ref, dbr_ref, dbi_ref, dcr_ref, dci_ref, dd_ref, dwg_ref, dbg_ref,
             gr_ref, gi_ref, carry_re, carry_im, pw_re, pw_im, cb_re, cb_im):
        i = pl.program_id(0)
        tile = nt - 1 - i
        lr, li = lr_ref[...], li_ref[...]

        @pl.when(i == 0)
        def _():
            for r in (carry_re, carry_im, dlr_ref, dli_ref, dbr_ref, dbi_ref, dcr_ref, dci_ref, dd_ref, dwg_ref, dbg_ref):
                r[...] = jnp.zeros_like(r)
            _store_powers(lr, li, pw_re, pw_im)

        perm = _segment_major()
        u = _exact01(u_ref[...], perm, ((1,), (0,)), True)
        d_out_p = _exact01(dout_ref[...], perm, ((1,), (0,)), True)
        h_r, h_i = hr_ref[...], hi_ref[...]
        _, vjp = jax.vjp(s5_out_fn, h_r, h_i, u, cr_ref[...], ci_ref[...], d_ref[...], wg_ref[...].astype(f32), bg_ref[...])
        ghr, ghi, du, dcr, dci, dd, dwg, dbg = vjp(d_out_p)
        gr_ref[...] = ghr
        gi_ref[...] = ghi
        _tile_scan(gr_ref, gi_ref, lr, li, pw_re, pw_im, cb_re, cb_im, carry_re, carry_im, reverse=True)
        g_r, g_i = gr_ref[...], gi_ref[...]
        keep = jnp.where(tile > 0, 1.0, 0.0)
        top = _iota((SCAN_SEG, 1), 0) == 0

        def earlier(h, before_ref):
            head = jnp.where(top, before_ref[HALO - 1:HALO, :] * keep, _roll(h[TOK - SCAN_SEG:, :], 1, 0))
            return jnp.concatenate([head, h[:TOK - SCAN_SEG, :]], axis=0)

        p_r, p_i = earlier(h_r, pr_ref), earlier(h_i, pi_ref)
        dlr_ref[...] += jnp.sum(g_r * p_r + g_i * p_i, axis=0, keepdims=True)
        dli_ref[...] += jnp.sum(g_i * p_r - g_r * p_i, axis=0, keepdims=True)
        du_p = du + mm_nt(g_r, br_ref[...]) + mm_nt(g_i, bi_ref[...])
        du_ref[...] = _exact01(du_p, perm, ((0,), (0,)), True)
        dbr_ref[...] += mm_tn(u, g_r)
        dbi_ref[...] += mm_tn(u, g_i)
        dcr_ref[...] += dcr
        dci_ref[...] += dci
        dd_ref[...] += dd
        dwg_ref[...] += dwg
        dbg_ref[...] += dbg

    rev = lambda cols, cb=0: pl.BlockSpec((TOK, cols), lambda i, cb=cb: (nt - 1 - i, cb))
    prev = pl.BlockSpec((HALO, S5_W), lambda i: (jnp.maximum((nt - 1 - i) * (TOK // HALO) - 1, 0), 0))
    outs = pl.pallas_call(
        body, grid=(nt,),
        in_specs=[rev(S5_CH, u_block), rev(S5_W), rev(S5_W), prev, prev, rev(S5_CH)] + [_const_spec(a.shape) for a in consts],
        out_specs=[rev(S5_CH)] + [_const_spec(a.shape) for a in consts],
        out_shape=[jax.ShapeDtypeStruct((seq_len, S5_CH), f32)] + [jax.ShapeDtypeStruct(a.shape, f32) for a in consts],
        scratch_shapes=[pltpu.VMEM((TOK, S5_W), f32), pltpu.VMEM((TOK, S5_W), f32)] + _SCAN_SCRATCH,
        compiler_params=_params(("arbitrary",)), name=name)(proj, h_re, h_im, h_re, h_im, d_out, *consts)
    return outs[0], outs[1:]


ANY = pl.BlockSpec(memory_space=pl.ANY)


def _mesh_position():
    x, y, c = lax.axis_index("x"), lax.axis_index("y"), lax.axis_index("c")
    return x, y, c, 4 * x + 2 * y + c


def _peer(x, y, c, r):
    px = 1 - x if r & 4 else x
    py = 1 - y if r & 2 else y
    pc = 1 - c if r & 1 else c
    return (px, py, pc), 4 * px + 2 * py + pc


class Exchange:
    def __init__(self, arrays, gather):
        self.arrays, self.gather, self.n = list(arrays), gather, len(arrays)
        self.in_specs = [ANY] * self.n
        self.out_specs = [ANY] * self.n
        shapes = [((N_DEV,) + a.shape) if gather else a.shape for a in self.arrays]
        self.out_shape = [jax.ShapeDtypeStruct(s, a.dtype) for s, a in zip(shapes, self.arrays)]
        self.scratch = [pltpu.SemaphoreType.DMA((self.n, N_DEV - 1)), pltpu.SemaphoreType.DMA((self.n, N_DEV - 1)),
                        pltpu.SemaphoreType.DMA((self.n,))]

    def _copies(self, ins, outs, sems, landed):
        send_sems, recv_sems, local_sems = sems
        x, y, c, me = _mesh_position()
        local, remote = [], []
        for i in range(self.n):
            mine = ins[i] if self.gather else ins[i].at[me]
            local.append(pltpu.make_async_copy(mine, outs[i].at[me], local_sems.at[i]))
            for r in range(1, N_DEV):
                peer, peer_idx = _peer(x, y, c, r)
                remote.append(pltpu.make_async_remote_copy(
                    src_ref=ins[i] if self.gather else ins[i].at[peer_idx],
                    dst_ref=outs[i].at[peer_idx if landed else me],
                    send_sem=send_sems.at[i, r - 1], recv_sem=recv_sems.at[i, r - 1],
                    device_id=peer, device_id_type=pl.DeviceIdType.MESH))
        return local, remote

    def start(self, ins, outs, sems):
        local, remote = self._copies(ins, outs, sems, landed=False)
        for cp in local + remote:
            cp.start()

    def wait(self, ins, outs, sems):
        local, remote = self._copies(ins, outs, sems, landed=True)
        for cp in remote:
            cp.wait_recv()
            cp.wait_send()
        for cp in local:
            cp.wait()

    def run(self, name):
        n = self.n

        def body(*refs):
            ins, outs, sems = refs[:n], refs[n:2 * n], refs[2 * n:]
            self.start(ins, outs, sems)
            self.wait(ins, outs, sems)

        return pl.pallas_call(body, in_specs=self.in_specs, out_specs=self.out_specs, out_shape=self.out_shape,
                              scratch_shapes=self.scratch, name=name)(*self.arrays)


class Carried:
    def __init__(self, exchange):
        self.ex = exchange
        self.n = exchange.n if exchange else 0
        self.in_specs = exchange.in_specs if exchange else []
        self.out_specs = exchange.out_specs if exchange else []
        self.out_shape = exchange.out_shape if exchange else []
        self.scratch = exchange.scratch if exchange else []
        self.operands = exchange.arrays if exchange else []

    def start_at(self, first, ins, outs, sems):
        if self.ex is not None:
            @pl.when(first)
            def _():
                self.ex.start(ins, outs, sems)

    def wait_at(self, last, ins, outs, sems):
        if self.ex is not None:
            @pl.when(last)
            def _():
                self.ex.wait(ins, outs, sems)


def adamw(name, parts, w, m, v):
    rows, cols = w.shape
    tr = rows
    for cand in (512, 256, 128, 64, 32, 16, 8):
        if rows * cols * 4 > (1 << 20) and rows % cand == 0 and cand * cols * 4 <= (1 << 20):
            tr = cand
            break

    def body(p_ref, w_ref, m_ref, v_ref, g_ref, d_ref, nm_ref, nv_ref):
        g = p_ref[0].astype(f32)
        for s in range(1, N_DEV):
            g = g + p_ref[s].astype(f32)
        nm = ADAM_B1 * m_ref[...] + (1.0 - ADAM_B1) * g
        nv = ADAM_B2 * v_ref[...] + (1.0 - ADAM_B2) * (g * g)
        m_hat = nm / (1.0 - ADAM_B1 ** ADAM_STEP)
        v_hat = nv / (1.0 - ADAM_B2 ** ADAM_STEP)
        g_ref[...] = g
        d_ref[...] = -ADAM_LR * (m_hat / (jnp.sqrt(v_hat) + ADAM_EPS) + ADAM_WD * w_ref[...])
        nm_ref[...] = nm
        nv_ref[...] = nv

    blk = pl.BlockSpec((tr, cols), lambda i: (i, 0))
    return pl.pallas_call(
        body, grid=(rows // tr,),
        in_specs=[pl.BlockSpec((N_DEV, tr, cols), lambda i: (0, i, 0)), blk, blk, blk],
        out_specs=[blk] * 4, out_shape=[jax.ShapeDtypeStruct((rows, cols), f32)] * 4,
        compiler_params=_params(("arbitrary",)), name=name)(parts, w, m, v)


WEIGHTS = ['l0_norm_mix', 'l0_w_in', 'ssd_conv_w', 'ssd_conv_b', 'ssd_dt_bias', 'ssd_A_log', 'ssd_D', 'ssd_norm_w',
           'l0_w_out', 'l0_norm_mlp', 'l0_w_up', 'l0_w_down', 'l1_norm_mix', 'l1_w_in', 'gdn_conv_w', 'gdn_A_log',
           'gdn_dt_bias', 'gdn_norm_w', 's5_A_re', 's5_A_im', 's5_log_step', 's5_B_re', 's5_B_im', 's5_C_re', 's5_C_im',
           's5_D', 's5_w_glu', 's5_b_glu', 'l1_w_out', 'l1_norm_mlp', 'l1_w_up', 'l1_w_down', 'final_norm']
SHARDED = ['l0_w_in', 'l0_w_out', 'l0_w_up', 'l0_w_down', 'l1_w_in', 's5_w_glu', 'l1_w_out', 'l1_w_up', 'l1_w_down',
           'ssd_conv_w', 'gdn_conv_w']
F32_GATHER = ('ssd_conv_w', 'gdn_conv_w')
REPLICATED = [n for n in WEIGHTS if n not in SHARDED]
INPUTS = ['x'] + WEIGHTS + ['loss_target'] + ['m_' + n for n in WEIGHTS] + ['v_' + n for n in WEIGHTS]


def _row(v):
    return v.reshape(1, -1)


def _pad_lanes(v, offset=0):
    return jnp.pad(v, (offset, LANES - offset - v.shape[0])).reshape(1, LANES)


def _cols_to_blocks(g):
    return g.reshape(g.shape[0], N_DEV, -1).transpose(1, 0, 2)


def _blocks_to_cols(g):
    return g.transpose(1, 0, 2).reshape(g.shape[1], -1)


def _pack(arrays):
    parts, slots, at = [], [], 0
    for a in arrays:
        n = a.size
        rows = -(-n // (8 * LANES)) * 8
        parts.append(jnp.pad(a.reshape(-1), (0, rows * LANES - n)).reshape(rows, LANES))
        slots.append((at, rows, n, a.shape))
        at += rows
    return jnp.concatenate(parts, axis=0), slots


def _unpack(buf, slots):
    return [buf[at:at + rows].reshape(-1)[:n].reshape(shape) for at, rows, n, shape in slots]


def kernel(*args):
    a = dict(zip(INPUTS, args, strict=True))
    seq_len = a['x'].shape[1]
    x0 = a['x'].reshape(seq_len, D_MODEL)
    target = a['loss_target'].reshape(seq_len, D_MODEL)

    shard = {n: a[n] if n in F32_GATHER else a[n].astype(_MXU) for n in SHARDED}
    first = ['l0_w_in', 'ssd_conv_w', 'gdn_conv_w']
    g = dict(zip(first, Exchange([shard[n] for n in first], gather=True).run("gather_first")))
    w_nat = _blocks_to_cols(g['l0_w_in'])
    win0 = jnp.concatenate([w_nat[:, :2048], w_nat[:, 2560:3584], w_nat[:, 2048:2560], w_nat[:, 3584:3592],
                            jnp.zeros((D_MODEL, IN0_PAD - IN0_W), _MXU)], axis=1)
    ssd_cw, gdn_cw = _blocks_to_cols(g['ssd_conv_w']), _blocks_to_cols(g['gdn_conv_w'])

    half = RET_D // 2
    inv = ROPE_THETA ** (-jnp.arange(half, dtype=f32) / half)
    ang = jnp.arange(seq_len, dtype=f32)[:, None] * inv[None, :]
    cos, sin = jnp.cos(ang), jnp.sin(ang)
    cos, sin = jnp.concatenate([cos, cos], axis=1), jnp.concatenate([-sin, sin], axis=1)
    ssd_params = [ssd_cw, _row(a['ssd_conv_b']), _pad_lanes(a['ssd_dt_bias']), _pad_lanes(a['ssd_A_log']),
                  _pad_lanes(a['ssd_D']), _row(a['ssd_norm_w'])]
    gdn_params = [gdn_cw, _pad_lanes(a['gdn_A_log'], GDN_GCOL), _pad_lanes(a['gdn_dt_bias'], GDN_GCOL), _row(a['gdn_norm_w'])]
    s5_raw = [a['s5_A_re'].reshape(1, S5_W), a['s5_A_im'].reshape(1, S5_W), _pad_lanes(a['s5_log_step']),
              a['s5_B_re'].transpose(2, 0, 1).reshape(S5_GROUP, S5_W), a['s5_B_im'].transpose(2, 0, 1).reshape(S5_GROUP, S5_W),
              a['s5_C_re'].transpose(1, 0, 2).reshape(S5_GROUP, S5_W), a['s5_C_im'].transpose(1, 0, 2).reshape(S5_GROUP, S5_W)]
    s5_d, s5_bg = _row(a['s5_D']), _row(a['s5_b_glu'])
    nw = {n: _row(a[n]) for n in ('l0_norm_mix', 'l0_norm_mlp', 'l1_norm_mix', 'l1_norm_mlp', 'final_norm')}

    proj0 = inproj_fwd("l0_in", x0, nw['l0_norm_mix'], win0)
    ret_seqs = [Seq(proj0, 512, 0), Seq(proj0, 512, 1), Seq(proj0, 512, 2), Seq(proj0, 512, 3),
                Seq(cos, LANES, 0, "const"), Seq(sin, LANES, 0, "const")]
    later = ['l0_w_out', 'l0_w_up']
    ret_out, ret_st, got = mixer_fwd("ret_fwd", ret_chunk, ret_seqs, [], 512, RET_HEADS * RET_D, seq_len,
                                     exchange=Exchange([shard[n] for n in later], gather=True))
    g.update(zip(later, got))
    wout0 = g['l0_w_out'].reshape(D_MODEL, D_MODEL)
    ssd_seqs = [Seq(proj0, 512, 6), Seq(proj0, 1024, 2, "halo"), Seq(proj0, LANES, 28)]
    later = ['l0_w_down', 's5_w_glu']
    ssd_out, ssd_st, got = mixer_fwd("ssd_fwd", ssd_chunk, ssd_seqs, ssd_params, SSD_INNER, SSD_INNER, seq_len,
                                     exchange=Exchange([shard[n] for n in later], gather=True))
    g.update(zip(later, got))
    wglu = g['s5_w_glu'].reshape(S5_CH, S5_CH)
    x1 = outproj_fwd("l0_out", x0, ret_out, ssd_out, wout0)
    later = ['l1_w_in', 'l1_w_out', 'l1_w_up', 'l1_w_down']
    x2, relu0, got = mlp_fwd("l0_mlp", x1, nw['l0_norm_mlp'], g['l0_w_up'], g['l0_w_down'],
                      exchange=Exchange([shard[n] for n in later], gather=True))
    g.update(zip(later, got))
    w_nat = g['l1_w_in'].reshape(D_MODEL, IN1_W)
    win1 = jnp.concatenate([w_nat[:, :3072], w_nat[:, 3084:3340], w_nat[:, 3072:3084],
                            jnp.zeros((D_MODEL, IN1_PAD - IN1_W), _MXU)], axis=1)
    wout1 = g['l1_w_out'].reshape(D_MODEL, D_MODEL)
    proj1 = inproj_fwd("l1_in", x2, nw['l1_norm_mix'], win1)
    gdn_seqs = [Seq(proj1, 3 * GDN_W, 0, "halo"), Seq(proj1, GDN_W, 3), Seq(proj1, LANES, 26)]
    gdn_out, gdn_st, _ = mixer_fwd("gdn_fwd", gdn_chunk, gdn_seqs, gdn_params, GDN_W, GDN_W, seq_len)
    prep = s5_prep_fwd("s5_prep", s5_raw)
    s5_out, h_re, h_im = s5_fwd("s5_fwd", proj1, 12, *prep, s5_d, wglu, s5_bg)
    x3 = outproj_fwd("l1_out", x2, gdn_out, s5_out, wout1)
    x4, relu1, _ = mlp_fwd("l1_mlp", x3, nw['l1_norm_mlp'], g['l1_w_up'], g['l1_w_down'])
    loss_blk, dx4, d_final = final_loss("final_loss", x4, nw['final_norm'], target)

    parts = {}
    dx3, d_up1, d_down1, d_nmlp1, _ = mlp_bwd("l1_mlp_bwd", x3, nw['l1_norm_mlp'], g['l1_w_up'], g['l1_w_down'], relu1, dx4)
    d_gdn, d_s5, d_wout1 = outproj_bwd("l1_out_bwd", dx3, gdn_out, s5_out, wout1)
    d_u, s5_g = s5_bwd("s5_bwd", proj1, 12, h_re, h_im, d_s5, *prep, s5_d, wglu, s5_bg)
    s5_raw_g = s5_prep_bwd("s5_prep_bwd", s5_raw, s5_g[:6])
    ready = {'l1_w_up': d_up1, 'l1_w_down': d_down1, 'l1_w_out': d_wout1.reshape(N_DEV, -1, D_MODEL),
             's5_w_glu': s5_g[7].reshape(N_DEV, -1, S5_CH)}
    (d_qkv, d_z1, d_ba), gdn_pg, got = mixer_bwd("gdn_bwd", gdn_chunk, gdn_seqs, gdn_params, gdn_st, d_gdn, seq_len,
                                                 exchange=Exchange(list(ready.values()), gather=False))
    parts.update(zip(ready, got))
    dx2, d_win1, d_nmix1 = inproj_bwd("l1_in_bwd", x2, nw['l1_norm_mix'], win1, [d_qkv, d_z1, d_u, d_ba], dx3)
    d_win1 = jnp.concatenate([d_win1[:, :3072], d_win1[:, 3328:3340], d_win1[:, 3072:3328]], axis=1)
    ready = {'l1_w_in': d_win1.reshape(N_DEV, -1, IN1_W), 'gdn_conv_w': _cols_to_blocks(gdn_pg[0])}
    dx1, d_up0, d_down0, d_nmlp0, got = mlp_bwd("l0_mlp_bwd", x1, nw['l0_norm_mlp'], g['l0_w_up'], g['l0_w_down'], relu0, dx2,
                                                exchange=Exchange(list(ready.values()), gather=False))
    parts.update(zip(ready, got))
    d_ret, d_ssd, d_wout0 = outproj_bwd("l0_out_bwd", dx1, ret_out, ssd_out, wout0)
    ready = {'l0_w_up': d_up0}
    d_qkvg, _, got = mixer_bwd("ret_bwd", ret_chunk, ret_seqs, [], ret_st, d_ret, seq_len,
                               exchange=Exchange(list(ready.values()), gather=False))
    parts.update(zip(ready, got))
    ready = {'l0_w_down': d_down0, 'l0_w_out': d_wout0.reshape(N_DEV, -1, D_MODEL)}
    (d_z0, d_xbc, d_dt), ssd_pg, got = mixer_bwd("ssd_bwd", ssd_chunk, ssd_seqs, ssd_params, ssd_st, d_ssd, seq_len,
                                                 exchange=Exchange(list(ready.values()), gather=False))
    parts.update(zip(ready, got))
    dx0, d_win0, d_nmix0 = inproj_bwd("l0_in_bwd", x0, nw['l0_norm_mix'], win0, list(d_qkvg) + [d_xbc, d_z0, d_dt], dx1)

    d_win0 = jnp.concatenate([d_win0[:, :2048], d_win0[:, 3072:3584], d_win0[:, 2048:3072], d_win0[:, 3584:3592]], axis=1)
    ready = {'l0_w_in': _cols_to_blocks(d_win0).astype(jnp.bfloat16), 'ssd_conv_w': _cols_to_blocks(ssd_pg[0])}
    from_b = lambda t: t.reshape(S5_GROUP, S5_GROUPS, S5_STATE).transpose(1, 2, 0)
    from_c = lambda t: t.reshape(S5_GROUP, S5_GROUPS, S5_STATE).transpose(1, 0, 2)
    replicated_g = {
        'l0_norm_mix': d_nmix0, 'ssd_conv_b': ssd_pg[1], 'ssd_dt_bias': ssd_pg[2][0, :SSD_HEADS], 'ssd_A_log': ssd_pg[3][0, :SSD_HEADS],
        'ssd_D': ssd_pg[4][0, :SSD_HEADS], 'ssd_norm_w': ssd_pg[5], 'l0_norm_mlp': d_nmlp0, 'l1_norm_mix': d_nmix1,
        'gdn_A_log': gdn_pg[1][0, GDN_GCOL:GDN_GCOL + GDN_HEADS], 'gdn_dt_bias': gdn_pg[2][0, GDN_GCOL:GDN_GCOL + GDN_HEADS],
        'gdn_norm_w': gdn_pg[3], 's5_A_re': s5_raw_g[0], 's5_A_im': s5_raw_g[1], 's5_log_step': s5_raw_g[2][0, :S5_GROUPS],
        's5_B_re': from_b(s5_raw_g[3]), 's5_B_im': from_b(s5_raw_g[4]), 's5_C_re': from_c(s5_raw_g[5]), 's5_C_im': from_c(s5_raw_g[6]),
        's5_D': s5_g[6], 's5_b_glu': s5_g[8], 'l1_norm_mlp': d_nmlp1, 'final_norm': d_final}
    replicated_g = {n: replicated_g[n].reshape(a[n].shape) for n in REPLICATED}

    parts.update(zip(ready, Exchange(list(ready.values()), gather=False).run("scatter_last")))
    packed_g, slots = _pack([replicated_g[n] for n in REPLICATED])
    (packed_parts,) = Exchange([packed_g], gather=True).run("gather_small_grads")
    results = {}
    for n in SHARDED:
        results[n] = adamw("adamw_" + n, parts[n], a[n], a['m_' + n], a['v_' + n])
    packed = [_pack([a[pre + n] for n in REPLICATED])[0] for pre in ('', 'm_', 'v_')]
    small = [_unpack(t, slots) for t in adamw("adamw_small", packed_parts, *packed)]
    for i, n in enumerate(REPLICATED):
        results[n] = tuple(small[k][i] for k in range(4))

    loss = lax.psum(loss_blk[0, 0], ("x", "y", "c"))
    grad_x = dx0.reshape(a['x'].shape)
    return (loss, grad_x, *[results[n][0] for n in WEIGHTS], *[results[n][1] for n in WEIGHTS],
            *[results[n][2] for n in WEIGHTS], *[results[n][3] for n in WEIGHTS])
```

```python
import functools
import math

import numpy as np
import jax
import jax.numpy as jnp
from jax import lax
from jax.experimental import pallas as pl
from jax.experimental.pallas import tpu as pltpu

f32 = jnp.float32
_MXU = jnp.bfloat16
HI = lax.Precision.HIGHEST

D_MODEL = 1024
CHUNK = 64
EPS = 1e-6
N_DEV = 8
LANES = 128
HALO = 8
CONV_WIDTH = 4

RET_HEADS, RET_D = 4, 128
SSD_HEADS, SSD_P, SSD_N, SSD_GROUPS = 8, 64, 128, 2
SSD_INNER = SSD_HEADS * SSD_P
GDN_HEADS, GDN_D = 6, 128
GDN_W = GDN_HEADS * GDN_D
S5_CH, S5_GROUP, S5_GROUPS, S5_STATE = 256, 16, 16, 64
S5_W = S5_GROUPS * S5_STATE
D_FF = 4096
ROPE_THETA = 10000.0

IN0_W = 3592
IN0_PAD = 3712
IN1_W = 3340
IN1_PAD = 3456

ADAM_LR, ADAM_B1, ADAM_B2, ADAM_EPS, ADAM_WD, ADAM_STEP = 0.001, 0.9, 0.999, 1e-08, 0.01, 10

VMEM_LIMIT = 56 * 1024 * 1024


def _dot(a, b, dims):
    return lax.dot_general(a.astype(_MXU), b.astype(_MXU), (dims, ((), ())), preferred_element_type=f32)


@jax.custom_vjp
def mm(a, b):
    return _dot(a, b, ((1,), (0,)))


@jax.custom_vjp
def mm_nt(a, b):
    return _dot(a, b, ((1,), (1,)))


@jax.custom_vjp
def mm_tn(a, b):
    return _dot(a, b, ((0,), (0,)))


mm.defvjp(lambda a, b: (mm(a, b), (a, b)), lambda r, g: (mm_nt(g, r[1]), mm_tn(r[0], g)))
mm_nt.defvjp(lambda a, b: (mm_nt(a, b), (a, b)), lambda r, g: (mm(g, r[1]), mm_tn(g, r[0])))
mm_tn.defvjp(lambda a, b: (mm_tn(a, b), (a, b)), lambda r, g: (mm_nt(r[1], g), mm(r[0], g)))


def mmh(a, b):
    return jnp.dot(a, b, precision=HI, preferred_element_type=f32)


def _exact01(x, m01, dims, m_first):
    hi = x.astype(jnp.bfloat16)
    r = x - hi.astype(f32)
    mid = r.astype(jnp.bfloat16)
    lo = (r - mid.astype(f32)).astype(jnp.bfloat16)
    m = m01.astype(jnp.bfloat16)
    dot = lambda p: lax.dot_general(m, p, (dims, ((), ())), preferred_element_type=f32) if m_first else \
        lax.dot_general(p, m, (dims, ((), ())), preferred_element_type=f32)
    return dot(hi) + dot(mid) + dot(lo)


@jax.custom_vjp
def spread01(x, sel):
    return _exact01(x, sel, ((1,), (0,)), False)


spread01.defvjp(lambda x, sel: (spread01(x, sel), sel),
                lambda sel, g: (_exact01(g, sel, ((1,), (1,)), False), jnp.zeros_like(sel)))


@jax.custom_vjp
def cumsum01(tril, x):
    return _exact01(x, tril, ((1,), (0,)), True)


cumsum01.defvjp(lambda tril, x: (cumsum01(tril, x), tril),
                lambda tril, g: (jnp.zeros_like(tril), _exact01(g, tril, ((0,), (0,)), True)))


def _roll(x, shift, axis):
    return pltpu.roll(x, shift, axis)


@functools.partial(jax.custom_vjp, nondiff_argnums=(1,))
def roll_rows(x, s):
    return _roll(x, s, 0) if s else x


roll_rows.defvjp(lambda x, s: (roll_rows(x, s), None),
                 lambda s, _, g: ((_roll(g, g.shape[0] - s, 0) if s else g),))


@jax.custom_vjp
def roll_half(x):
    return _roll(x, x.shape[-1] // 2, 1)


roll_half.defvjp(lambda x: (roll_half(x), None), lambda _, g: (roll_half(g),))


def _iota(shape, axis):
    return lax.broadcasted_iota(jnp.int32, shape, axis)


def silu(x):
    return x * jax.nn.sigmoid(x)


def softplus(x):
    return jnp.maximum(x, 0.0) + jnp.log(1.0 + jnp.exp(-jnp.abs(x)))


def rmsnorm_f(x, w):
    return x * lax.rsqrt(jnp.mean(x * x, axis=-1, keepdims=True) + EPS) * w


def unit_rms(x):
    return x * lax.rsqrt(jnp.mean(x * x, axis=-1, keepdims=True) + EPS)


def _causal(n, strict=False):
    r, c = _iota((n, n), 0), _iota((n, n), 1)
    return (r > c) if strict else (r >= c)


def _tril_ones(n):
    return _causal(n).astype(f32)


def _conv_rows(xe, w):
    acc = w[CONV_WIDTH - 1:CONV_WIDTH, :] * xe
    for j in range(CONV_WIDTH - 1):
        acc = acc + w[j:j + 1, :] * roll_rows(xe, CONV_WIDTH - 1 - j)
    return acc[HALO:, :]


_RET_LOG_GAMMA = [float(np.log(np.float32(1.0) - np.float32(2.0) ** np.float32(-5.0 - h))) for h in range(RET_HEADS)]


def ret_chunk(q, k, v, gate, cos, sin, state):
    c = q.shape[0]
    idx = _iota((c, 1), 0).astype(f32)
    diff = (_iota((c, c), 0) - _iota((c, c), 1)).astype(f32)
    causal = _causal(c)
    hs = range(RET_HEADS)
    cols = [slice(h * RET_D, (h + 1) * RET_D) for h in hs]
    lg = _RET_LOG_GAMMA
    qh = [(q[:, s] * cos + roll_half(q[:, s]) * sin) * (RET_D ** -0.5) for s in cols]
    kh = [k[:, s] * cos + roll_half(k[:, s]) * sin for s in cols]
    vh = [v[:, s] for s in cols]
    sh = [state[s, :] for s in cols]
    scores = [mm_nt(qh[h], kh[h]) * jnp.exp(jnp.where(causal, lg[h] * diff, -jnp.inf)) for h in hs]
    inter = [mm(qh[h] * jnp.exp(lg[h] * (idx + 1.0)), sh[h]) for h in hs]
    y = [mm(scores[h], vh[h]) + inter[h] for h in hs]
    states = [sh[h] * math.exp(lg[h] * c) + mm_tn(kh[h] * jnp.exp(lg[h] * (c - 1.0 - idx)), vh[h]) for h in hs]
    outs = [unit_rms(y[h]) * silu(gate[:, cols[h]]) for h in hs]
    return jnp.concatenate(outs, axis=1), jnp.concatenate(states, axis=0)


def _head_select(n_heads, width):
    r, c = _iota((LANES, n_heads * width), 0), _iota((LANES, n_heads * width), 1)
    return (c // width == r).astype(f32)


def ssd_chunk(z, xe, dtr, state, conv_w, conv_b, dt_bias, a_log, d_skip, norm_w):
    c = z.shape[0]
    xbc = silu(_conv_rows(xe, conv_w) + conv_b)
    xs, bm, cm = xbc[:, :SSD_INNER], xbc[:, SSD_INNER:SSD_INNER + 256], xbc[:, SSD_INNER + 256:]
    sel = _head_select(SSD_HEADS, SSD_P)
    dt = softplus(dtr + dt_bias)
    la = dt * (-jnp.exp(a_log))
    la_cum = cumsum01(_tril_ones(c), la)
    la_cum_t = la_cum.T
    last = jnp.sum(la, axis=0, keepdims=True)
    xd = xs * spread01(dt, sel)
    la_x = spread01(la_cum, sel)
    last_x = spread01(last, sel)
    to_end = jnp.exp(last_x - la_x)
    from_start = jnp.exp(la_x)
    causal = _causal(c)
    left = (_iota((1, LANES), 1) < SSD_P).astype(f32)
    upper = _iota((LANES, 1), 0) < SSD_P
    pairs, heads = range(SSD_HEADS // 2), range(SSD_HEADS)
    bc = [bm[:, g * SSD_N:(g + 1) * SSD_N] for g in range(SSD_GROUPS)]
    cc = [cm[:, g * SSD_N:(g + 1) * SSD_N] for g in range(SSD_GROUPS)]
    cb = [mm_nt(cc[g], bc[g]) for g in range(SSD_GROUPS)]
    cols = [slice(p * LANES, (p + 1) * LANES) for p in pairs]
    xd_p = [xd[:, s] for s in cols]
    sp = [state[s, :] for s in cols]
    lmat = [jnp.exp(jnp.where(causal, la_cum[:, h:h + 1] - la_cum_t[h:h + 1, :], -jnp.inf)) for h in heads]
    off = [mm_nt(cc[p // 2], sp[p]) * from_start[:, cols[p]] for p in pairs]
    diag = [mm(cb[h // 4] * lmat[h], xd_p[h // 2] * (left if h % 2 == 0 else 1.0 - left)) for h in heads]
    cd = [jnp.where(upper, jnp.exp(last[:, 2 * p:2 * p + 1]), jnp.exp(last[:, 2 * p + 1:2 * p + 2])) for p in pairs]
    states = [sp[p] * cd[p] + mm_tn(xd_p[p] * to_end[:, cols[p]], bc[p // 2]) for p in pairs]
    ys = [off[p] + diag[2 * p] + diag[2 * p + 1] for p in pairs]
    y = jnp.concatenate(ys, axis=1) + spread01(d_skip, sel) * xs
    yg = y * silu(z)
    half = SSD_INNER // SSD_GROUPS
    out = jnp.concatenate([unit_rms(yg[:, i * half:(i + 1) * half]) for i in range(SSD_GROUPS)], axis=1) * norm_w
    return out, jnp.concatenate(states, axis=0)


def mm3(a, b):
    return jnp.dot(a, b, precision=lax.Precision.HIGH, preferred_element_type=f32)


@jax.custom_vjp
def _unit_lower_inverses(lowers):
    n = lowers[0].shape[0]
    eye = (_iota((n, n), 0) == _iota((n, n), 1)).astype(f32)
    a = [-l for l in lowers]
    p = [eye + x for x in a]
    k = 2
    while k < n:
        a = [mm3(x, x) for x in a]
        p = [y + mm3(y, x) for y, x in zip(p, a)]
        k *= 2
    return p


def _unit_lower_inverses_bwd(t_inv, g):
    dims_tn, dims_nt = (((0,), (0,)), ((), ())), (((1,), (1,)), ((), ()))
    x = [lax.dot_general(t, gi, dims_tn, precision=lax.Precision.HIGH, preferred_element_type=f32) for t, gi in zip(t_inv, g)]
    return ([-lax.dot_general(xi, t, dims_nt, precision=lax.Precision.HIGH, preferred_element_type=f32) for xi, t in zip(x, t_inv)],)


def _unit_lower_inverses_fwd(lowers):
    t_inv = _unit_lower_inverses(lowers)
    return t_inv, t_inv


_unit_lower_inverses.defvjp(_unit_lower_inverses_fwd, _unit_lower_inverses_bwd)


@jax.custom_vjp
def _known_inverses(lowers, t_inv):
    return t_inv


_known_inverses.defvjp(lambda lowers, t_inv: (t_inv, t_inv),
                       lambda t_inv, g: (_unit_lower_inverses_bwd(t_inv, g)[0], [jnp.zeros_like(t) for t in t_inv]))


GDN_GCOL = 6


def gdn_chunk(xe, z, ba, state, conv_w, a_log, dt_bias, norm_w, kept_inverses=None, keep_inverses=False):
    c = z.shape[0]
    qkv = silu(_conv_rows(xe, conv_w))
    beta_all = jax.nn.sigmoid(ba)
    g_all = -jnp.exp(a_log) * softplus(ba + dt_bias)
    gc = cumsum01(_tril_ones(c), g_all)
    gc_t = gc.T
    last = jnp.sum(g_all, axis=0, keepdims=True)
    causal, strict = _causal(c), _causal(c, strict=True)
    hs = range(GDN_HEADS)
    cols = [slice(h * GDN_D, (h + 1) * GDN_D) for h in hs]
    qh = [qkv[:, h * GDN_D:(h + 1) * GDN_D] for h in hs]
    kh = [qkv[:, GDN_W + h * GDN_D:GDN_W + (h + 1) * GDN_D] for h in hs]
    vh = [qkv[:, 2 * GDN_W + h * GDN_D:2 * GDN_W + (h + 1) * GDN_D] for h in hs]
    qh = [t * lax.rsqrt(jnp.sum(t * t, axis=-1, keepdims=True) + EPS) * (GDN_D ** -0.5) for t in qh]
    kh = [t * lax.rsqrt(jnp.sum(t * t, axis=-1, keepdims=True) + EPS) for t in kh]
    beta = [beta_all[:, h:h + 1] for h in hs]
    col = [gc[:, GDN_GCOL + h:GDN_GCOL + h + 1] for h in hs]
    row = [gc_t[GDN_GCOL + h:GDN_GCOL + h + 1, :] for h in hs]
    lst = [last[:, GDN_GCOL + h:GDN_GCOL + h + 1] for h in hs]
    decay = [jnp.exp(jnp.where(causal, col[h] - row[h], -jnp.inf)) for h in hs]
    e_col = [jnp.exp(t) for t in col]
    kb = [kh[h] * beta[h] for h in hs]
    vb = [vh[h] * beta[h] for h in hs]
    kk = [mm_nt(kb[h], kh[h]) for h in hs]
    qk = [mm_nt(qh[h], kh[h]) for h in hs]
    lowers = [jnp.where(strict, kk[h] * decay[h], 0.0) for h in hs]
    t_inv = _unit_lower_inverses(lowers) if kept_inverses is None else _known_inverses(lowers, list(kept_inverses))
    u = [mm(t_inv[h], vb[h]) for h in hs]
    w = [mm(t_inv[h], kb[h] * e_col[h]) for h in hs]
    attn = [jnp.where(causal, qk[h] * decay[h], 0.0) for h in hs]
    sh = [state[s, :] for s in cols]
    ws = [mm(w[h], sh[h]) for h in hs]
    qs = [mm(qh[h] * e_col[h], sh[h]) for h in hs]
    v_new = [u[h] - ws[h] for h in hs]
    o = [qs[h] + mm(attn[h], v_new[h]) for h in hs]
    states = [sh[h] * jnp.exp(lst[h]) + mm_tn(kh[h] * jnp.exp(lst[h] - col[h]), v_new[h]) for h in hs]
    outs = [unit_rms(o[h]) * norm_w * silu(z[:, cols[h]]) for h in hs]
    results = (jnp.concatenate(outs, axis=1), jnp.concatenate(states, axis=0))
    return results + (list(t_inv),) if keep_inverses else results


def _s5_group_mask():
    r, c = _iota((S5_CH, S5_W), 0), _iota((S5_CH, S5_W), 1)
    return (r // S5_GROUP == c // S5_STATE).astype(f32)


def s5_prep(a_re, a_im, log_step, b_re, b_im, c_re, c_im):
    r, c = _iota((LANES, S5_W), 0), _iota((LANES, S5_W), 1)
    step = jnp.exp(mmh(log_step, (c // S5_STATE == r).astype(f32)))
    zr, zi = a_re * step, a_im * step
    e = jnp.exp(zr)
    lr, li = e * jnp.cos(zi), e * jnp.sin(zi)
    den = a_re * a_re + a_im * a_im
    xr, xi = lr - 1.0, li
    cr, ci = (xr * a_re + xi * a_im) / den, (xi * a_re - xr * a_im) / den
    bbr, bbi = cr * b_re - ci * b_im, cr * b_im + ci * b_re
    mask = _s5_group_mask()
    tile = lambda t: jnp.tile(t, (S5_GROUPS, 1)) * mask
    return lr, li, tile(bbr), tile(bbi), tile(c_re), tile(c_im)


def s5_out_fn(h_re, h_im, u, cblk_re, cblk_im, d_skip, w_glu, b_glu):
    y = mm_nt(h_re, cblk_re) - mm_nt(h_im, cblk_im) + d_skip * u
    y = jax.nn.gelu(y)
    return y * jax.nn.sigmoid(mm(y, w_glu) + b_glu)


def _params(sem, **kw):
    return pltpu.CompilerParams(dimension_semantics=sem, vmem_limit_bytes=VMEM_LIMIT, **kw)


def _const_spec(shape):
    return pl.BlockSpec(shape, lambda i: (0,) * len(shape))


def _resident_spec(shape):
    return pl.BlockSpec(shape, lambda i: (0,) * len(shape), pipeline_mode=pl.Buffered(1))


def _row_spec(rows, cols, col_block=0):
    return pl.BlockSpec((rows, cols), lambda i: (i, col_block))


class Seq:
    def __init__(self, array, width, col_block, kind="tile"):
        self.array, self.width, self.col_block, self.kind = array, width, col_block, kind


CHUNKS_PER_STEP = 4


def mixer_fwd(name, fn, seqs, params, out_width, state_rows, seq_len, per_step=CHUNKS_PER_STEP, exchange=None, kept_shape=None):
    nc = seq_len // CHUNK
    rows, steps = per_step * CHUNK, nc // per_step
    n_refs = sum(2 if s.kind == "halo" else 1 for s in seqs)
    n_par = len(params)
    n_own = 3 if kept_shape else 2
    car = Carried(exchange)

    def body(*refs):
        seq_refs, par_refs = refs[:n_refs], refs[n_refs:n_refs + n_par]
        k0 = n_refs + n_par
        ex_ins, own = refs[k0:k0 + car.n], refs[k0 + car.n:k0 + car.n + n_own]
        out_ref, st_ref = own[:2]
        k0 += car.n + n_own
        ex_outs, state, ex_sems = refs[k0:k0 + car.n], refs[k0 + car.n], refs[k0 + car.n + 1:]
        c = pl.program_id(0)
        car.start_at(c == 0, ex_ins, ex_outs, ex_sems)

        @pl.when(c == 0)
        def _():
            state[...] = jnp.zeros_like(state)

        par_vals = [p[...] for p in par_refs]
        s_cur = state[...]
        for kk in range(per_step):
            lo = kk * CHUNK
            vals, k = [], 0
            for s in seqs:
                if s.kind == "halo":
                    prev = jnp.where(c > 0, seq_refs[k][...], 0.0) if kk == 0 else seq_refs[k + 1][lo - HALO:lo, :]
                    vals.append(jnp.concatenate([prev, seq_refs[k + 1][lo:lo + CHUNK, :]], axis=0))
                    k += 2
                else:
                    vals.append(seq_refs[k][lo:lo + CHUNK, :])
                    k += 1
            st_ref[kk] = s_cur
            res = fn(*vals, s_cur, *par_vals)
            out_ref[lo:lo + CHUNK, :] = res[0]
            s_cur = res[1]
            if kept_shape:
                for h, t in enumerate(res[2]):
                    own[2][kk, h] = t
        state[...] = s_cur
        car.wait_at(c == steps - 1, ex_ins, ex_outs, ex_sems)

    in_specs, operands = [], []
    for s in seqs:
        if s.kind == "halo":
            rb, w, cb = rows // HALO, s.width, s.col_block
            in_specs.append(pl.BlockSpec((HALO, w), lambda i, rb=rb, cb=cb: (jnp.maximum(i * rb - 1, 0), cb)))
            operands.append(s.array)
        in_specs.append(pl.BlockSpec((rows, s.width), lambda i, cb=s.col_block: (i, cb)))
        operands.append(s.array)
    for p in params:
        in_specs.append(_const_spec(p.shape))
        operands.append(p)
    outs = pl.pallas_call(
        body, grid=(steps,), in_specs=in_specs + car.in_specs,
        out_specs=[pl.BlockSpec((rows, out_width), lambda i: (i, 0)),
                   pl.BlockSpec((per_step, state_rows, LANES), lambda i: (i, 0, 0))]
        + ([pl.BlockSpec((per_step,) + tuple(kept_shape), lambda i: (i, 0, 0, 0))] if kept_shape else []) + car.out_specs,
        out_shape=[jax.ShapeDtypeStruct((seq_len, out_width), f32),
                   jax.ShapeDtypeStruct((nc, state_rows, LANES), f32)]
        + ([jax.ShapeDtypeStruct((nc,) + tuple(kept_shape), f32)] if kept_shape else []) + car.out_shape,
        scratch_shapes=[pltpu.VMEM((state_rows, LANES), f32)] + car.scratch,
        compiler_params=_params(("arbitrary",)), name=name)(*operands, *car.operands)
    return tuple(outs[:n_own]) + (outs[n_own:],)


def mixer_bwd(name, fn, seqs, params, states, d_out, seq_len, per_step=CHUNKS_PER_STEP, exchange=None, kept=None):
    nc = seq_len // CHUNK
    rows, steps = per_step * CHUNK, nc // per_step
    state_rows = states.shape[1]
    diff = [s for s in seqs if s.kind != "const"]
    halos = [s for s in diff if s.kind == "halo"]
    n_refs = sum(2 if s.kind == "halo" else 1 for s in seqs)
    n_par = len(params)
    n_kept = 0 if kept is None else 1
    car = Carried(exchange)

    def body(*refs):
        seq_refs, par_refs = refs[:n_refs], refs[n_refs:n_refs + n_par]
        st_ref, dout_ref = refs[n_refs + n_par:n_refs + n_par + 2]
        k0 = n_refs + n_par + 2
        kept_ref = refs[k0] if n_kept else None
        k0 += n_kept
        ex_ins = refs[k0:k0 + car.n]
        k0 += car.n
        dseq_refs, dpar_refs = refs[k0:k0 + len(diff)], refs[k0 + len(diff):k0 + len(diff) + n_par]
        k0 += len(diff) + n_par
        ex_outs = refs[k0:k0 + car.n]
        scratch = refs[k0 + car.n:]
        d_state, carries, ex_sems = scratch[0], scratch[1:1 + len(halos)], scratch[1 + len(halos):]
        i = pl.program_id(0)
        step = steps - 1 - i
        car.start_at(i == 0, ex_ins, ex_outs, ex_sems)

        @pl.when(i == 0)
        def _():
            d_state[...] = jnp.zeros_like(d_state)
            for r in list(carries) + list(dpar_refs):
                r[...] = jnp.zeros_like(r)

        par_vals = [p[...] for p in par_refs]
        d_s = d_state[...]
        d_par = [None] * n_par
        halo_ct = [r[...] for r in carries]
        for kk in range(per_step - 1, -1, -1):
            lo = kk * CHUNK
            dvals, consts, k = [], [], 0
            for s in seqs:
                if s.kind == "halo":
                    prev = jnp.where(step > 0, seq_refs[k][...], 0.0) if kk == 0 else seq_refs[k + 1][lo - HALO:lo, :]
                    dvals.append(jnp.concatenate([prev, seq_refs[k + 1][lo:lo + CHUNK, :]], axis=0))
                    k += 2
                elif s.kind == "tile":
                    dvals.append(seq_refs[k][lo:lo + CHUNK, :])
                    k += 1
                else:
                    consts.append(seq_refs[k][lo:lo + CHUNK, :])
                    k += 1
            nd = len(dvals)

            extra = [[kept_ref[kk, h] for h in range(kept.shape[1])]] if n_kept else []

            def call(*a, consts=consts, nd=nd, extra=extra):
                it_d, it_c = iter(a[:nd]), iter(consts)
                vals = [next(it_c) if s.kind == "const" else next(it_d) for s in seqs]
                return fn(*vals, *a[nd:], *extra)

            _, vjp = jax.vjp(call, *dvals, st_ref[kk], *par_vals)
            cts = vjp((dout_ref[lo:lo + CHUNK, :], d_s))
            hk = 0
            for j, s in enumerate(diff):
                if s.kind == "halo":
                    dseq_refs[j][lo:lo + CHUNK, :] = cts[j][HALO:, :]
                    dseq_refs[j][lo + CHUNK - HALO:lo + CHUNK, :] += halo_ct[hk]
                    halo_ct[hk] = cts[j][:HALO, :]
                    hk += 1
                else:
                    dseq_refs[j][lo:lo + CHUNK, :] = cts[j]
            d_s = cts[nd]
            for j in range(n_par):
                d_par[j] = cts[nd + 1 + j] if d_par[j] is None else d_par[j] + cts[nd + 1 + j]
        d_state[...] = d_s
        for r, v in zip(carries, halo_ct):
            r[...] = v
        for j in range(n_par):
            dpar_refs[j][...] += d_par[j]
        car.wait_at(i == steps - 1, ex_ins, ex_outs, ex_sems)

    in_specs, operands = [], []
    for s in seqs:
        if s.kind == "halo":
            rb, cb = rows // HALO, s.col_block
            in_specs.append(pl.BlockSpec((HALO, s.width), lambda i, rb=rb, cb=cb: (jnp.maximum((steps - 1 - i) * rb - 1, 0), cb)))
            operands.append(s.array)
        in_specs.append(pl.BlockSpec((rows, s.width), lambda i, cb=s.col_block: (steps - 1 - i, cb)))
        operands.append(s.array)
    for p in params:
        in_specs.append(_const_spec(p.shape))
        operands.append(p)
    in_specs.append(pl.BlockSpec((per_step, state_rows, LANES), lambda i: (steps - 1 - i, 0, 0)))
    in_specs.append(pl.BlockSpec((rows, d_out.shape[1]), lambda i: (steps - 1 - i, 0)))
    operands += [states, d_out]
    if n_kept:
        in_specs.append(pl.BlockSpec((per_step,) + kept.shape[1:], lambda i: (steps - 1 - i, 0, 0, 0)))
        operands.append(kept)
    outs = pl.pallas_call(
        body, grid=(steps,), in_specs=in_specs + car.in_specs,
        out_specs=[pl.BlockSpec((rows, s.width), lambda i: (steps - 1 - i, 0)) for s in diff]
        + [_const_spec(p.shape) for p in params] + car.out_specs,
        out_shape=[jax.ShapeDtypeStruct((seq_len, s.width), f32) for s in diff]
        + [jax.ShapeDtypeStruct(p.shape, f32) for p in params] + car.out_shape,
        scratch_shapes=[pltpu.VMEM((state_rows, LANES), f32)] + [pltpu.VMEM((HALO, s.width), f32) for s in halos]
        + car.scratch,
        compiler_params=_params(("arbitrary",)), name=name)(*operands, *car.operands)
    nd = len(diff)
    return outs[:nd], outs[nd:nd + n_par], outs[nd + n_par:]


TOK = 512


def inproj_fwd(name, x, nw, w):
    seq_len, n = x.shape[0], w.shape[1]

    def body(x_ref, nw_ref, w_ref, o_ref):
        o_ref[...] = mm(rmsnorm_f(x_ref[...], nw_ref[...]), w_ref[...])

    return pl.pallas_call(
        body, grid=(seq_len // TOK,),
        in_specs=[_row_spec(TOK, D_MODEL), _const_spec(nw.shape), _resident_spec(w.shape)],
        out_specs=_row_spec(TOK, n), out_shape=jax.ShapeDtypeStruct((seq_len, n), f32),
        compiler_params=_params(("arbitrary",)), name=name)(x, nw, w)


def inproj_bwd(name, x, nw, w, pieces, d_res):
    seq_len, n = x.shape[0], w.shape[1]
    widths = [p.shape[1] for p in pieces]
    assert sum(widths) == n
    k = len(pieces)

    def body(*refs):
        x_ref, nw_ref, w_ref = refs[:3]
        p_refs, dres_ref = refs[3:3 + k], refs[3 + k]
        dx_ref, dw_ref, dnw_ref = refs[4 + k:]

        @pl.when(pl.program_id(0) == 0)
        def _():
            dw_ref[...] = jnp.zeros_like(dw_ref)
            dnw_ref[...] = jnp.zeros_like(dnw_ref)

        h, vjp = jax.vjp(rmsnorm_f, x_ref[...], nw_ref[...])
        dh, off = jnp.zeros_like(h), 0
        for p_ref, wd in zip(p_refs, widths):
            g = p_ref[...]
            dh = dh + mm_nt(g, w_ref[:, off:off + wd])
            dw_ref[:, off:off + wd] += mm_tn(h, g)
            off += wd
        dx, dnw = vjp(dh)
        dx_ref[...] = dres_ref[...] + dx
        dnw_ref[...] += dnw

    return pl.pallas_call(
        body, grid=(seq_len // TOK,),
        in_specs=[_row_spec(TOK, D_MODEL), _const_spec(nw.shape), _resident_spec(w.shape)]
        + [_row_spec(TOK, wd) for wd in widths] + [_row_spec(TOK, D_MODEL)],
        out_specs=[_row_spec(TOK, D_MODEL), _resident_spec((D_MODEL, n)), _const_spec(nw.shape)],
        out_shape=[jax.ShapeDtypeStruct((seq_len, D_MODEL), f32), jax.ShapeDtypeStruct((D_MODEL, n), f32),
                   jax.ShapeDtypeStruct(nw.shape, f32)],
        compiler_params=_params(("arbitrary",)), name=name)(x, nw, w, *pieces, d_res)


def outproj_fwd(name, x, a, b, w):
    seq_len, wa, wb = x.shape[0], a.shape[1], b.shape[1]

    def body(x_ref, a_ref, b_ref, w_ref, o_ref):
        o_ref[...] = x_ref[...] + mm(a_ref[...], w_ref[:wa, :]) + mm(b_ref[...], w_ref[wa:, :])

    return pl.pallas_call(
        body, grid=(seq_len // TOK,),
        in_specs=[_row_spec(TOK, D_MODEL), _row_spec(TOK, wa), _row_spec(TOK, wb), _resident_spec(w.shape)],
        out_specs=_row_spec(TOK, D_MODEL), out_shape=jax.ShapeDtypeStruct((seq_len, D_MODEL), f32),
        compiler_params=_params(("arbitrary",)), name=name)(x, a, b, w)


def outproj_bwd(name, dy, a, b, w):
    seq_len, wa, wb = dy.shape[0], a.shape[1], b.shape[1]

    def body(dy_ref, a_ref, b_ref, w_ref, da_ref, db_ref, dw_ref):
        @pl.when(pl.program_id(0) == 0)
        def _():
            dw_ref[...] = jnp.zeros_like(dw_ref)

        g = dy_ref[...]
        da_ref[...] = mm_nt(g, w_ref[:wa, :])
        db_ref[...] = mm_nt(g, w_ref[wa:, :])
        dw_ref[:wa, :] += mm_tn(a_ref[...], g)
        dw_ref[wa:, :] += mm_tn(b_ref[...], g)

    return pl.pallas_call(
        body, grid=(seq_len // TOK,),
        in_specs=[_row_spec(TOK, D_MODEL), _row_spec(TOK, wa), _row_spec(TOK, wb), _resident_spec(w.shape)],
        out_specs=[_row_spec(TOK, wa), _row_spec(TOK, wb), _resident_spec(w.shape)],
        out_shape=[jax.ShapeDtypeStruct((seq_len, wa), f32), jax.ShapeDtypeStruct((seq_len, wb), f32),
                   jax.ShapeDtypeStruct(w.shape, f32)],
        compiler_params=_params(("arbitrary",)), name=name)(dy, a, b, w)


FF_BLOCK = D_FF // N_DEV


def mlp_fwd(name, x, nw, w_up, w_down, exchange=None):
    seq_len = x.shape[0]
    nt = seq_len // TOK
    car = Carried(exchange)

    def body(*refs):
        x_ref, nw_ref, up_ref, down_ref = refs[:4]
        ex_ins, (o_ref, relu_ref) = refs[4:4 + car.n], refs[4 + car.n:6 + car.n]
        ex_outs, ex_sems = refs[6 + car.n:6 + 2 * car.n], refs[6 + 2 * car.n:]
        i = pl.program_id(0)
        car.start_at(i == 0, ex_ins, ex_outs, ex_sems)
        xv = x_ref[...]
        h = rmsnorm_f(xv, nw_ref[...])
        acc = xv
        for d in range(N_DEV):
            r = jnp.maximum(mm(h, up_ref[d]), 0.0)
            relu_ref[d] = r.astype(_MXU)
            acc = acc + mm(r * r, down_ref[d])
        o_ref[...] = acc
        car.wait_at(i == nt - 1, ex_ins, ex_outs, ex_sems)

    outs = pl.pallas_call(
        body, grid=(nt,),
        in_specs=[_row_spec(TOK, D_MODEL), _const_spec(nw.shape), _resident_spec(w_up.shape), _resident_spec(w_down.shape)]
        + car.in_specs,
        out_specs=[_row_spec(TOK, D_MODEL), pl.BlockSpec((N_DEV, TOK, FF_BLOCK), lambda i: (0, i, 0))] + car.out_specs,
        out_shape=[jax.ShapeDtypeStruct((seq_len, D_MODEL), f32),
                   jax.ShapeDtypeStruct((N_DEV, seq_len, FF_BLOCK), _MXU)] + car.out_shape, scratch_shapes=car.scratch,
        compiler_params=_params(("arbitrary",)), name=name)(x, nw, w_up, w_down, *car.operands)
    return outs[0], outs[1], outs[2:]


MLP_SPLIT = 2


def mlp_bwd(name, x, nw, w_up, w_down, relu, dy, exchange=None):
    seq_len = x.shape[0]
    nt = seq_len // TOK
    per = N_DEV // MLP_SPLIT
    car = Carried(exchange)

    def body(*refs):
        x_ref, nw_ref, up_ref, down_ref, relu_ref, dy_ref = refs[:6]
        ex_ins, (dh_ref, dup_ref, ddown_ref) = refs[6:6 + car.n], refs[6 + car.n:9 + car.n]
        ex_outs, ex_sems = refs[9 + car.n:9 + 2 * car.n], refs[9 + 2 * car.n:]
        j, i = pl.program_id(0), pl.program_id(1)
        car.start_at(jnp.logical_and(j == 0, i == 0), ex_ins, ex_outs, ex_sems)

        @pl.when(i == 0)
        def _():
            dup_ref[...] = jnp.zeros_like(dup_ref)
            ddown_ref[...] = jnp.zeros_like(ddown_ref)

        h = rmsnorm_f(x_ref[...], nw_ref[...])
        g = dy_ref[...]
        dh = jnp.zeros_like(h)
        for d in range(per):
            r = relu_ref[d].astype(f32)
            da = mm_nt(g, down_ref[d]) * (2.0 * r)
            ddown_ref[d] += mm_tn(r * r, g)
            dup_ref[d] += mm_tn(h, da)
            dh = dh + mm_nt(da, up_ref[d])
        dh_ref[...] = dh
        car.wait_at(jnp.logical_and(j == MLP_SPLIT - 1, i == nt - 1), ex_ins, ex_outs, ex_sems)

    outs = pl.pallas_call(
        body, grid=(MLP_SPLIT, nt),
        in_specs=[pl.BlockSpec((TOK, D_MODEL), lambda j, i: (i, 0)), pl.BlockSpec(nw.shape, lambda j, i: (0, 0)),
                  pl.BlockSpec((per, D_MODEL, FF_BLOCK), lambda j, i: (j, 0, 0), pipeline_mode=pl.Buffered(1)),
                  pl.BlockSpec((per, FF_BLOCK, D_MODEL), lambda j, i: (j, 0, 0), pipeline_mode=pl.Buffered(1)),
                  pl.BlockSpec((per, TOK, FF_BLOCK), lambda j, i: (j, i, 0)),
                  pl.BlockSpec((TOK, D_MODEL), lambda j, i: (i, 0))] + car.in_specs,
        out_specs=[pl.BlockSpec((None, TOK, D_MODEL), lambda j, i: (j, i, 0)),
                   pl.BlockSpec((per, D_MODEL, FF_BLOCK), lambda j, i: (j, 0, 0), pipeline_mode=pl.Buffered(1)),
                   pl.BlockSpec((per, FF_BLOCK, D_MODEL), lambda j, i: (j, 0, 0), pipeline_mode=pl.Buffered(1))]
        + car.out_specs,
        out_shape=[jax.ShapeDtypeStruct((MLP_SPLIT, seq_len, D_MODEL), f32),
                   jax.ShapeDtypeStruct(w_up.shape, f32), jax.ShapeDtypeStruct(w_down.shape, f32)] + car.out_shape,
        scratch_shapes=car.scratch,
        compiler_params=_params(("arbitrary", "arbitrary")), name=name)(x, nw, w_up, w_down, relu, dy, *car.operands)
    dh_parts, d_up, d_down, ex_results = outs[0], outs[1], outs[2], outs[3:]

    def norm_body(x_ref, nw_ref, dh_ref, dy_ref, dx_ref, dnw_ref):
        @pl.when(pl.program_id(0) == 0)
        def _():
            dnw_ref[...] = jnp.zeros_like(dnw_ref)

        _, vjp = jax.vjp(rmsnorm_f, x_ref[...], nw_ref[...])
        dh = dh_ref[0]
        for j in range(1, MLP_SPLIT):
            dh = dh + dh_ref[j]
        dx, dnw = vjp(dh)
        dx_ref[...] = dy_ref[...] + dx
        dnw_ref[...] += dnw

    dx, dnw = pl.pallas_call(
        norm_body, grid=(nt,),
        in_specs=[_row_spec(TOK, D_MODEL), _const_spec(nw.shape),
                  pl.BlockSpec((MLP_SPLIT, TOK, D_MODEL), lambda i: (0, i, 0)), _row_spec(TOK, D_MODEL)],
        out_specs=[_row_spec(TOK, D_MODEL), _const_spec(nw.shape)],
        out_shape=[jax.ShapeDtypeStruct((seq_len, D_MODEL), f32), jax.ShapeDtypeStruct(nw.shape, f32)],
        compiler_params=_params(("arbitrary",)), name=name + "_norm")(x, nw, dh_parts, dy)
    return dx, d_up, d_down, dnw, ex_results


def final_loss(name, x, nw, target):
    seq_len = x.shape[0]

    def body(x_ref, nw_ref, t_ref, loss_ref, dx_ref, dnw_ref):
        @pl.when(pl.program_id(0) == 0)
        def _():
            loss_ref[...] = jnp.zeros_like(loss_ref)
            dnw_ref[...] = jnp.zeros_like(dnw_ref)

        y, vjp = jax.vjp(rmsnorm_f, x_ref[...], nw_ref[...])
        err = y - t_ref[...]
        loss_ref[...] += 0.5 * jnp.sum(jnp.mean(err * err, axis=-1, keepdims=True), axis=0, keepdims=True)
        dx, dnw = vjp(err * (1.0 / D_MODEL))
        dx_ref[...] = dx
        dnw_ref[...] += dnw

    return pl.pallas_call(
        body, grid=(seq_len // TOK,),
        in_specs=[_row_spec(TOK, D_MODEL), _const_spec(nw.shape), _row_spec(TOK, D_MODEL)],
        out_specs=[_const_spec((8, LANES)), _row_spec(TOK, D_MODEL), _const_spec(nw.shape)],
        out_shape=[jax.ShapeDtypeStruct((8, LANES), f32), jax.ShapeDtypeStruct((seq_len, D_MODEL), f32),
                   jax.ShapeDtypeStruct(nw.shape, f32)],
        compiler_params=_params(("arbitrary",)), name=name)(x, nw, target)


def _whole(a):
    return pl.BlockSpec(a.shape, lambda: (0,) * len(a.shape))


def s5_prep_fwd(name, raw):
    def body(*refs):
        outs = s5_prep(*[r[...] for r in refs[:7]])
        for o_ref, o in zip(refs[7:], outs):
            o_ref[...] = o

    shapes = [(1, S5_W)] * 2 + [(S5_CH, S5_W)] * 4
    return pl.pallas_call(
        body, in_specs=[_whole(a) for a in raw], out_specs=[pl.BlockSpec(s, lambda s=s: (0,) * len(s)) for s in shapes],
        out_shape=[jax.ShapeDtypeStruct(s, f32) for s in shapes],
        compiler_params=pltpu.CompilerParams(vmem_limit_bytes=VMEM_LIMIT), name=name)(*raw)


def s5_prep_bwd(name, raw, cts):
    def body(*refs):
        _, vjp = jax.vjp(s5_prep, *[r[...] for r in refs[:7]])
        grads = vjp(tuple(r[...] for r in refs[7:13]))
        for o_ref, g in zip(refs[13:], grads):
            o_ref[...] = g

    return pl.pallas_call(
        body, in_specs=[_whole(a) for a in list(raw) + list(cts)], out_specs=[_whole(a) for a in raw],
        out_shape=[jax.ShapeDtypeStruct(a.shape, f32) for a in raw],
        compiler_params=pltpu.CompilerParams(vmem_limit_bytes=VMEM_LIMIT), name=name)(*raw, *cts)


SCAN_SEG = 8
SCAN_LEN = TOK // SCAN_SEG


def _segment_major():
    r, t = _iota((TOK, TOK), 0), _iota((TOK, TOK), 1)
    return (t == (r % SCAN_SEG) * SCAN_LEN + r // SCAN_SEG).astype(f32)


def _store_powers(lr, li, pr_ref, pi_ref):
    qr, qi = lr, li
    for j in range(SCAN_LEN):
        pr_ref[j:j + 1, :] = qr
        pi_ref[j:j + 1, :] = qi
        qr, qi = lr * qr - li * qi, lr * qi + li * qr


def _tile_scan(xr_ref, xi_ref, lr, li, pr_ref, pi_ref, cr_ref, ci_ref, carry_re, carry_im, reverse):
    sign = -1.0 if reverse else 1.0
    ar = jnp.broadcast_to(lr, (SCAN_SEG, lr.shape[1]))
    ai = jnp.broadcast_to(sign * li, (SCAN_SEG, li.shape[1]))
    rows = lambda j: slice(j * SCAN_SEG, (j + 1) * SCAN_SEG)
    hr = hi = jnp.zeros_like(ar)
    for j in (range(SCAN_LEN - 1, -1, -1) if reverse else range(SCAN_LEN)):
        hr, hi = ar * hr - ai * hi + xr_ref[rows(j), :], ar * hi + ai * hr + xi_ref[rows(j), :]
        xr_ref[rows(j), :] = hr
        xi_ref[rows(j), :] = hi
    wr, wi = pr_ref[SCAN_LEN - 1:SCAN_LEN, :], sign * pi_ref[SCAN_LEN - 1:SCAN_LEN, :]
    er, ei = carry_re[0:1, :], carry_im[0:1, :]
    for s in (range(SCAN_SEG - 1, -1, -1) if reverse else range(SCAN_SEG)):
        cr_ref[s:s + 1, :] = er
        ci_ref[s:s + 1, :] = ei
        er, ei = hr[s:s + 1, :] + wr * er - wi * ei, hi[s:s + 1, :] + wr * ei + wi * er
    carry_re[0:1, :] = er
    carry_im[0:1, :] = ei
    cr, ci = cr_ref[...], ci_ref[...]
    for j in range(SCAN_LEN):
        k = SCAN_LEN - 1 - j if reverse else j
        qr, qi = pr_ref[k:k + 1, :], sign * pi_ref[k:k + 1, :]
        xr_ref[rows(j), :] += qr * cr - qi * ci
        xi_ref[rows(j), :] += qr * ci + qi * cr


_SCAN_SCRATCH = [pltpu.VMEM((HALO, S5_W), f32), pltpu.VMEM((HALO, S5_W), f32),
                 pltpu.VMEM((SCAN_LEN, S5_W), f32), pltpu.VMEM((SCAN_LEN, S5_W), f32),
                 pltpu.VMEM((SCAN_SEG, S5_W), f32), pltpu.VMEM((SCAN_SEG, S5_W), f32)]


def s5_fwd(name, proj, u_block, lam_re, lam_im, bblk_re, bblk_im, cblk_re, cblk_im, d_skip, w_glu, b_glu):
    seq_len = proj.shape[0]

    def body(u_ref, lr_ref, li_ref, br_ref, bi_ref, cr_ref, ci_ref, d_ref, wg_ref, bg_ref,
             o_ref, hr_ref, hi_ref, carry_re, carry_im, pw_re, pw_im, cb_re, cb_im):
        lr, li = lr_ref[...], li_ref[...]

        @pl.when(pl.program_id(0) == 0)
        def _():
            carry_re[...] = jnp.zeros_like(carry_re)
            carry_im[...] = jnp.zeros_like(carry_im)
            _store_powers(lr, li, pw_re, pw_im)

        perm = _segment_major()
        u = _exact01(u_ref[...], perm, ((1,), (0,)), True)
        hr_ref[...] = mm(u, br_ref[...])
        hi_ref[...] = mm(u, bi_ref[...])
        _tile_scan(hr_ref, hi_ref, lr, li, pw_re, pw_im, cb_re, cb_im, carry_re, carry_im, reverse=False)
        out = s5_out_fn(hr_ref[...], hi_ref[...], u, cr_ref[...], ci_ref[...], d_ref[...],
                        wg_ref[...].astype(f32), bg_ref[...])
        o_ref[...] = _exact01(out, perm, ((0,), (0,)), True)

    consts = [lam_re, lam_im, bblk_re, bblk_im, cblk_re, cblk_im, d_skip, w_glu, b_glu]
    return pl.pallas_call(
        body, grid=(seq_len // TOK,),
        in_specs=[_row_spec(TOK, S5_CH, u_block)] + [_const_spec(a.shape) for a in consts],
        out_specs=[_row_spec(TOK, S5_CH), _row_spec(TOK, S5_W), _row_spec(TOK, S5_W)],
        out_shape=[jax.ShapeDtypeStruct((seq_len, S5_CH), f32), jax.ShapeDtypeStruct((seq_len, S5_W), f32),
                   jax.ShapeDtypeStruct((seq_len, S5_W), f32)],
        scratch_shapes=_SCAN_SCRATCH,
        compiler_params=_params(("arbitrary",)), name=name)(proj, *consts)


def s5_bwd(name, proj, u_block, h_re, h_im, d_out, lam_re, lam_im, bblk_re, bblk_im, cblk_re, cblk_im, d_skip, w_glu, b_glu):
    seq_len = proj.shape[0]
    nt = seq_len // TOK
    consts = [lam_re, lam_im, bblk_re, bblk_im, cblk_re, cblk_im, d_skip, w_glu, b_glu]

    def body(u_ref, hr_ref, hi_ref, pr_ref, pi_ref, dout_ref, lr_ref, li_ref, br_ref, bi_ref, cr_ref, ci_ref, d_ref, wg_ref, bg_ref,
             du_ref, dlr_ref, dli_ref, dbr_ref, dbi_ref, dcr_ref, dci_ref, dd_ref, dwg_ref, dbg_ref,
             gr_ref, gi_ref, carry_re, carry_im, pw_re, pw_im, cb_re, cb_im):
        i = pl.program_id(0)
        tile = nt - 1 - i
        lr, li = lr_ref[...], li_ref[...]

        @pl.when(i == 0)
        def _():
            for r in (carry_re, carry_im, dlr_ref, dli_ref, dbr_ref, dbi_ref, dcr_ref, dci_ref, dd_ref, dwg_ref, dbg_ref):
                r[...] = jnp.zeros_like(r)
            _store_powers(lr, li, pw_re, pw_im)

        perm = _segment_major()
        u = _exact01(u_ref[...], perm, ((1,), (0,)), True)
        d_out_p = _exact01(dout_ref[...], perm, ((1,), (0,)), True)
        h_r, h_i = hr_ref[...], hi_ref[...]
        _, vjp = jax.vjp(s5_out_fn, h_r, h_i, u, cr_ref[...], ci_ref[...], d_ref[...], wg_ref[...].astype(f32), bg_ref[...])
        ghr, ghi, du, dcr, dci, dd, dwg, dbg = vjp(d_out_p)
        gr_ref[...] = ghr
        gi_ref[...] = ghi
        _tile_scan(gr_ref, gi_ref, lr, li, pw_re, pw_im, cb_re, cb_im, carry_re, carry_im, reverse=True)
        g_r, g_i = gr_ref[...], gi_ref[...]
        keep = jnp.where(tile > 0, 1.0, 0.0)
        top = _iota((SCAN_SEG, 1), 0) == 0

        def earlier(h, before_ref):
            head = jnp.where(top, before_ref[HALO - 1:HALO, :] * keep, _roll(h[TOK - SCAN_SEG:, :], 1, 0))
            return jnp.concatenate([head, h[:TOK - SCAN_SEG, :]], axis=0)

        p_r, p_i = earlier(h_r, pr_ref), earlier(h_i, pi_ref)
        dlr_ref[...] += jnp.sum(g_r * p_r + g_i * p_i, axis=0, keepdims=True)
        dli_ref[...] += jnp.sum(g_i * p_r - g_r * p_i, axis=0, keepdims=True)
        du_p = du + mm_nt(g_r, br_ref[...]) + mm_nt(g_i, bi_ref[...])
        du_ref[...] = _exact01(du_p, perm, ((0,), (0,)), True)
        dbr_ref[...] += mm_tn(u, g_r)
        dbi_ref[...] += mm_tn(u, g_i)
        dcr_ref[...] += dcr
        dci_ref[...] += dci
        dd_ref[...] += dd
        dwg_ref[...] += dwg
        dbg_ref[...] += dbg

    rev = lambda cols, cb=0: pl.BlockSpec((TOK, cols), lambda i, cb=cb: (nt - 1 - i, cb))
    prev = pl.BlockSpec((HALO, S5_W), lambda i: (jnp.maximum((nt - 1 - i) * (TOK // HALO) - 1, 0), 0))
    outs = pl.pallas_call(
        body, grid=(nt,),
        in_specs=[rev(S5_CH, u_block), rev(S5_W), rev(S5_W), prev, prev, rev(S5_CH)] + [_const_spec(a.shape) for a in consts],
        out_specs=[rev(S5_CH)] + [_const_spec(a.shape) for a in consts],
        out_shape=[jax.ShapeDtypeStruct((seq_len, S5_CH), f32)] + [jax.ShapeDtypeStruct(a.shape, f32) for a in consts],
        scratch_shapes=[pltpu.VMEM((TOK, S5_W), f32), pltpu.VMEM((TOK, S5_W), f32)] + _SCAN_SCRATCH,
        compiler_params=_params(("arbitrary",)), name=name)(proj, h_re, h_im, h_re, h_im, d_out, *consts)
    return outs[0], outs[1:]


ANY = pl.BlockSpec(memory_space=pl.ANY)


def _mesh_position():
    x, y, c = lax.axis_index("x"), lax.axis_index("y"), lax.axis_index("c")
    return x, y, c, 4 * x + 2 * y + c


def _peer(x, y, c, r):
    px = 1 - x if r & 4 else x
    py = 1 - y if r & 2 else y
    pc = 1 - c if r & 1 else c
    return (px, py, pc), 4 * px + 2 * py + pc


class Exchange:
    def __init__(self, arrays, gather):
        self.arrays, self.gather, self.n = list(arrays), gather, len(arrays)
        self.in_specs = [ANY] * self.n
        self.out_specs = [ANY] * self.n
        shapes = [((N_DEV,) + a.shape) if gather else a.shape for a in self.arrays]
        self.out_shape = [jax.ShapeDtypeStruct(s, a.dtype) for s, a in zip(shapes, self.arrays)]
        self.scratch = [pltpu.SemaphoreType.DMA((self.n, N_DEV - 1)), pltpu.SemaphoreType.DMA((self.n, N_DEV - 1)),
                        pltpu.SemaphoreType.DMA((self.n,))]

    def _copies(self, ins, outs, sems, landed):
        send_sems, recv_sems, local_sems = sems
        x, y, c, me = _mesh_position()
        local, remote = [], []
        for i in range(self.n):
            mine = ins[i] if self.gather else ins[i].at[me]
            local.append(pltpu.make_async_copy(mine, outs[i].at[me], local_sems.at[i]))
            for r in range(1, N_DEV):
                peer, peer_idx = _peer(x, y, c, r)
                remote.append(pltpu.make_async_remote_copy(
                    src_ref=ins[i] if self.gather else ins[i].at[peer_idx],
                    dst_ref=outs[i].at[peer_idx if landed else me],
                    send_sem=send_sems.at[i, r - 1], recv_sem=recv_sems.at[i, r - 1],
                    device_id=peer, device_id_type=pl.DeviceIdType.MESH))
        return local, remote

    def start(self, ins, outs, sems):
        local, remote = self._copies(ins, outs, sems, landed=False)
        for cp in local + remote:
            cp.start()

    def wait(self, ins, outs, sems):
        local, remote = self._copies(ins, outs, sems, landed=True)
        for cp in remote:
            cp.wait_recv()
            cp.wait_send()
        for cp in local:
            cp.wait()

    def run(self, name):
        n = self.n

        def body(*refs):
            ins, outs, sems = refs[:n], refs[n:2 * n], refs[2 * n:]
            self.start(ins, outs, sems)
            self.wait(ins, outs, sems)

        return pl.pallas_call(body, in_specs=self.in_specs, out_specs=self.out_specs, out_shape=self.out_shape,
                              scratch_shapes=self.scratch, name=name)(*self.arrays)


class Carried:
    def __init__(self, exchange):
        self.ex = exchange
        self.n = exchange.n if exchange else 0
        self.in_specs = exchange.in_specs if exchange else []
        self.out_specs = exchange.out_specs if exchange else []
        self.out_shape = exchange.out_shape if exchange else []
        self.scratch = exchange.scratch if exchange else []
        self.operands = exchange.arrays if exchange else []

    def start_at(self, first, ins, outs, sems):
        if self.ex is not None:
            @pl.when(first)
            def _():
                self.ex.start(ins, outs, sems)

    def wait_at(self, last, ins, outs, sems):
        if self.ex is not None:
            @pl.when(last)
            def _():
                self.ex.wait(ins, outs, sems)


def adamw(name, parts, w, m, v):
    rows, cols = w.shape
    tr = rows
    for cand in (512, 256, 128, 64, 32, 16, 8):
        if rows * cols * 4 > (1 << 20) and rows % cand == 0 and cand * cols * 4 <= (1 << 20):
            tr = cand
            break

    def body(p_ref, w_ref, m_ref, v_ref, g_ref, d_ref, nm_ref, nv_ref):
        g = p_ref[0].astype(f32)
        for s in range(1, N_DEV):
            g = g + p_ref[s].astype(f32)
        nm = ADAM_B1 * m_ref[...] + (1.0 - ADAM_B1) * g
        nv = ADAM_B2 * v_ref[...] + (1.0 - ADAM_B2) * (g * g)
        m_hat = nm / (1.0 - ADAM_B1 ** ADAM_STEP)
        v_hat = nv / (1.0 - ADAM_B2 ** ADAM_STEP)
        g_ref[...] = g
        d_ref[...] = -ADAM_LR * (m_hat / (jnp.sqrt(v_hat) + ADAM_EPS) + ADAM_WD * w_ref[...])
        nm_ref[...] = nm
        nv_ref[...] = nv

    blk = pl.BlockSpec((tr, cols), lambda i: (i, 0))
    return pl.pallas_call(
        body, grid=(rows // tr,),
        in_specs=[pl.BlockSpec((N_DEV, tr, cols), lambda i: (0, i, 0)), blk, blk, blk],
        out_specs=[blk] * 4, out_shape=[jax.ShapeDtypeStruct((rows, cols), f32)] * 4,
        compiler_params=_params(("arbitrary",)), name=name)(parts, w, m, v)


WEIGHTS = ['l0_norm_mix', 'l0_w_in', 'ssd_conv_w', 'ssd_conv_b', 'ssd_dt_bias', 'ssd_A_log', 'ssd_D', 'ssd_norm_w',
           'l0_w_out', 'l0_norm_mlp', 'l0_w_up', 'l0_w_down', 'l1_norm_mix', 'l1_w_in', 'gdn_conv_w', 'gdn_A_log',
           'gdn_dt_bias', 'gdn_norm_w', 's5_A_re', 's5_A_im', 's5_log_step', 's5_B_re', 's5_B_im', 's5_C_re', 's5_C_im',
           's5_D', 's5_w_glu', 's5_b_glu', 'l1_w_out', 'l1_norm_mlp', 'l1_w_up', 'l1_w_down', 'final_norm']
SHARDED = ['l0_w_in', 'l0_w_out', 'l0_w_up', 'l0_w_down', 'l1_w_in', 's5_w_glu', 'l1_w_out', 'l1_w_up', 'l1_w_down',
           'ssd_conv_w', 'gdn_conv_w']
F32_GATHER = ('ssd_conv_w', 'gdn_conv_w')
REPLICATED = [n for n in WEIGHTS if n not in SHARDED]
INPUTS = ['x'] + WEIGHTS + ['loss_target'] + ['m_' + n for n in WEIGHTS] + ['v_' + n for n in WEIGHTS]


def _row(v):
    return v.reshape(1, -1)


def _pad_lanes(v, offset=0):
    return jnp.pad(v, (offset, LANES - offset - v.shape[0])).reshape(1, LANES)


def _cols_to_blocks(g):
    return g.reshape(g.shape[0], N_DEV, -1).transpose(1, 0, 2)


def _blocks_to_cols(g):
    return g.transpose(1, 0, 2).reshape(g.shape[1], -1)


def _pack(arrays):
    parts, slots, at = [], [], 0
    for a in arrays:
        n = a.size
        rows = -(-n // (8 * LANES)) * 8
        parts.append(jnp.pad(a.reshape(-1), (0, rows * LANES - n)).reshape(rows, LANES))
        slots.append((at, rows, n, a.shape))
        at += rows
    return jnp.concatenate(parts, axis=0), slots


def _unpack(buf, slots):
    return [buf[at:at + rows].reshape(-1)[:n].reshape(shape) for at, rows, n, shape in slots]


def kernel(*args):
    a = dict(zip(INPUTS, args, strict=True))
    seq_len = a['x'].shape[1]
    x0 = a['x'].reshape(seq_len, D_MODEL)
    target = a['loss_target'].reshape(seq_len, D_MODEL)

    shard = {n: a[n] if n in F32_GATHER else a[n].astype(_MXU) for n in SHARDED}
    first = ['l0_w_in', 'ssd_conv_w', 'gdn_conv_w']
    g = dict(zip(first, Exchange([shard[n] for n in first], gather=True).run("gather_first")))
    w_nat = _blocks_to_cols(g['l0_w_in'])
    win0 = jnp.concatenate([w_nat[:, :2048], w_nat[:, 2560:3584], w_nat[:, 2048:2560], w_nat[:, 3584:3592],
                            jnp.zeros((D_MODEL, IN0_PAD - IN0_W), _MXU)], axis=1)
    ssd_cw, gdn_cw = _blocks_to_cols(g['ssd_conv_w']), _blocks_to_cols(g['gdn_conv_w'])

    half = RET_D // 2
    inv = ROPE_THETA ** (-jnp.arange(half, dtype=f32) / half)
    ang = jnp.arange(seq_len, dtype=f32)[:, None] * inv[None, :]
    cos, sin = jnp.cos(ang), jnp.sin(ang)
    cos, sin = jnp.concatenate([cos, cos], axis=1), jnp.concatenate([-sin, sin], axis=1)
    ssd_params = [ssd_cw, _row(a['ssd_conv_b']), _pad_lanes(a['ssd_dt_bias']), _pad_lanes(a['ssd_A_log']),
                  _pad_lanes(a['ssd_D']), _row(a['ssd_norm_w'])]
    gdn_params = [gdn_cw, _pad_lanes(a['gdn_A_log'], GDN_GCOL), _pad_lanes(a['gdn_dt_bias'], GDN_GCOL), _row(a['gdn_norm_w'])]
    s5_raw = [a['s5_A_re'].reshape(1, S5_W), a['s5_A_im'].reshape(1, S5_W), _pad_lanes(a['s5_log_step']),
              a['s5_B_re'].transpose(2, 0, 1).reshape(S5_GROUP, S5_W), a['s5_B_im'].transpose(2, 0, 1).reshape(S5_GROUP, S5_W),
              a['s5_C_re'].transpose(1, 0, 2).reshape(S5_GROUP, S5_W), a['s5_C_im'].transpose(1, 0, 2).reshape(S5_GROUP, S5_W)]
    s5_d, s5_bg = _row(a['s5_D']), _row(a['s5_b_glu'])
    nw = {n: _row(a[n]) for n in ('l0_norm_mix', 'l0_norm_mlp', 'l1_norm_mix', 'l1_norm_mlp', 'final_norm')}

    proj0 = inproj_fwd("l0_in", x0, nw['l0_norm_mix'], win0)
    ret_seqs = [Seq(proj0, 512, 0), Seq(proj0, 512, 1), Seq(proj0, 512, 2), Seq(proj0, 512, 3),
                Seq(cos, LANES, 0, "const"), Seq(sin, LANES, 0, "const")]
    later = ['l0_w_out', 'l0_w_up']
    ret_out, ret_st, got = mixer_fwd("ret_fwd", ret_chunk, ret_seqs, [], 512, RET_HEADS * RET_D, seq_len,
                                     exchange=Exchange([shard[n] for n in later], gather=True))
    g.update(zip(later, got))
    wout0 = g['l0_w_out'].reshape(D_MODEL, D_MODEL)
    ssd_seqs = [Seq(proj0, 512, 6), Seq(proj0, 1024, 2, "halo"), Seq(proj0, LANES, 28)]
    later = ['l0_w_down', 's5_w_glu']
    ssd_out, ssd_st, got = mixer_fwd("ssd_fwd", ssd_chunk, ssd_seqs, ssd_params, SSD_INNER, SSD_INNER, seq_len,
                                     exchange=Exchange([shard[n] for n in later], gather=True))
    g.update(zip(later, got))
    wglu = g['s5_w_glu'].reshape(S5_CH, S5_CH)
    x1 = outproj_fwd("l0_out", x0, ret_out, ssd_out, wout0)
    later = ['l1_w_in', 'l1_w_out', 'l1_w_up', 'l1_w_down']
    x2, relu0, got = mlp_fwd("l0_mlp", x1, nw['l0_norm_mlp'], g['l0_w_up'], g['l0_w_down'],
                      exchange=Exchange([shard[n] for n in later], gather=True))
    g.update(zip(later, got))
    w_nat = g['l1_w_in'].reshape(D_MODEL, IN1_W)
    win1 = jnp.concatenate([w_nat[:, :3072], w_nat[:, 3084:3340], w_nat[:, 3072:3084],
                            jnp.zeros((D_MODEL, IN1_PAD - IN1_W), _MXU)], axis=1)
    wout1 = g['l1_w_out'].reshape(D_MODEL, D_MODEL)
    proj1 = inproj_fwd("l1_in", x2, nw['l1_norm_mix'], win1)
    gdn_seqs = [Seq(proj1, 3 * GDN_W, 0, "halo"), Seq(proj1, GDN_W, 3), Seq(proj1, LANES, 26)]
    gdn_out, gdn_st, gdn_inv, _ = mixer_fwd("gdn_fwd", functools.partial(gdn_chunk, keep_inverses=True), gdn_seqs, gdn_params,
                                            GDN_W, GDN_W, seq_len, kept_shape=(GDN_HEADS, CHUNK, CHUNK))
    prep = s5_prep_fwd("s5_prep", s5_raw)
    s5_out, h_re, h_im = s5_fwd("s5_fwd", proj1, 12, *prep, s5_d, wglu, s5_bg)
    x3 = outproj_fwd("l1_out", x2, gdn_out, s5_out, wout1)
    x4, relu1, _ = mlp_fwd("l1_mlp", x3, nw['l1_norm_mlp'], g['l1_w_up'], g['l1_w_down'])
    loss_blk, dx4, d_final = final_loss("final_loss", x4, nw['final_norm'], target)

    parts = {}
    dx3, d_up1, d_down1, d_nmlp1, _ = mlp_bwd("l1_mlp_bwd", x3, nw['l1_norm_mlp'], g['l1_w_up'], g['l1_w_down'], relu1, dx4)
    d_gdn, d_s5, d_wout1 = outproj_bwd("l1_out_bwd", dx3, gdn_out, s5_out, wout1)
    d_u, s5_g = s5_bwd("s5_bwd", proj1, 12, h_re, h_im, d_s5, *prep, s5_d, wglu, s5_bg)
    s5_raw_g = s5_prep_bwd("s5_prep_bwd", s5_raw, s5_g[:6])
    ready = {'l1_w_up': d_up1, 'l1_w_down': d_down1, 'l1_w_out': d_wout1.reshape(N_DEV, -1, D_MODEL),
             's5_w_glu': s5_g[7].reshape(N_DEV, -1, S5_CH)}
    (d_qkv, d_z1, d_ba), gdn_pg, got = mixer_bwd("gdn_bwd", gdn_chunk, gdn_seqs, gdn_params, gdn_st, d_gdn, seq_len,
                                                 exchange=Exchange(list(ready.values()), gather=False), kept=gdn_inv)
    parts.update(zip(ready, got))
    dx2, d_win1, d_nmix1 = inproj_bwd("l1_in_bwd", x2, nw['l1_norm_mix'], win1, [d_qkv, d_z1, d_u, d_ba], dx3)
    d_win1 = jnp.concatenate([d_win1[:, :3072], d_win1[:, 3328:3340], d_win1[:, 3072:3328]], axis=1)
    ready = {'l1_w_in': d_win1.reshape(N_DEV, -1, IN1_W), 'gdn_conv_w': _cols_to_blocks(gdn_pg[0])}
    dx1, d_up0, d_down0, d_nmlp0, got = mlp_bwd("l0_mlp_bwd", x1, nw['l0_norm_mlp'], g['l0_w_up'], g['l0_w_down'], relu0, dx2,
                                                exchange=Exchange(list(ready.values()), gather=False))
    parts.update(zip(ready, got))
    d_ret, d_ssd, d_wout0 = outproj_bwd("l0_out_bwd", dx1, ret_out, ssd_out, wout0)
    ready = {'l0_w_up': d_up0}
    d_qkvg, _, got = mixer_bwd("ret_bwd", ret_chunk, ret_seqs, [], ret_st, d_ret, seq_len,
                               exchange=Exchange(list(ready.values()), gather=False))
    parts.update(zip(ready, got))
    ready = {'l0_w_down': d_down0, 'l0_w_out': d_wout0.reshape(N_DEV, -1, D_MODEL)}
    (d_z0, d_xbc, d_dt), ssd_pg, got = mixer_bwd("ssd_bwd", ssd_chunk, ssd_seqs, ssd_params, ssd_st, d_ssd, seq_len,
                                                 exchange=Exchange(list(ready.values()), gather=False))
    parts.update(zip(ready, got))
    dx0, d_win0, d_nmix0 = inproj_bwd("l0_in_bwd", x0, nw['l0_norm_mix'], win0, list(d_qkvg) + [d_xbc, d_z0, d_dt], dx1)

    d_win0 = jnp.concatenate([d_win0[:, :2048], d_win0[:, 3072:3584], d_win0[:, 2048:3072], d_win0[:, 3584:3592]], axis=1)
    ready = {'l0_w_in': _cols_to_blocks(d_win0).astype(jnp.bfloat16), 'ssd_conv_w': _cols_to_blocks(ssd_pg[0])}
    from_b = lambda t: t.reshape(S5_GROUP, S5_GROUPS, S5_STATE).transpose(1, 2, 0)
    from_c = lambda t: t.reshape(S5_GROUP, S5_GROUPS, S5_STATE).transpose(1, 0, 2)
    replicated_g = {
        'l0_norm_mix': d_nmix0, 'ssd_conv_b': ssd_pg[1], 'ssd_dt_bias': ssd_pg[2][0, :SSD_HEADS], 'ssd_A_log': ssd_pg[3][0, :SSD_HEADS],
        'ssd_D': ssd_pg[4][0, :SSD_HEADS], 'ssd_norm_w': ssd_pg[5], 'l0_norm_mlp': d_nmlp0, 'l1_norm_mix': d_nmix1,
        'gdn_A_log': gdn_pg[1][0, GDN_GCOL:GDN_GCOL + GDN_HEADS], 'gdn_dt_bias': gdn_pg[2][0, GDN_GCOL:GDN_GCOL + GDN_HEADS],
        'gdn_norm_w': gdn_pg[3], 's5_A_re': s5_raw_g[0], 's5_A_im': s5_raw_g[1], 's5_log_step': s5_raw_g[2][0, :S5_GROUPS],
        's5_B_re': from_b(s5_raw_g[3]), 's5_B_im': from_b(s5_raw_g[4]), 's5_C_re': from_c(s5_raw_g[5]), 's5_C_im': from_c(s5_raw_g[6]),
        's5_D': s5_g[6], 's5_b_glu': s5_g[8], 'l1_norm_mlp': d_nmlp1, 'final_norm': d_final}
    replicated_g = {n: replicated_g[n].reshape(a[n].shape) for n in REPLICATED}

    parts.update(zip(ready, Exchange(list(ready.values()), gather=False).run("scatter_last")))
    packed_g, slots = _pack([replicated_g[n] for n in REPLICATED])
    (packed_parts,) = Exchange([packed_g], gather=True).run("gather_small_grads")
    results = {}
    for n in SHARDED:
        results[n] = adamw("adamw_" + n, parts[n], a[n], a['m_' + n], a['v_' + n])
    packed = [_pack([a[pre + n] for n in REPLICATED])[0] for pre in ('', 'm_', 'v_')]
    small = [_unpack(t, slots) for t in adamw("adamw_small", packed_parts, *packed)]
    for i, n in enumerate(REPLICATED):
        results[n] = tuple(small[k][i] for k in range(4))

    loss = lax.psum(loss_blk[0, 0], ("x", "y", "c"))
    grad_x = dx0.reshape(a['x'].shape)
    return (loss, grad_x, *[results[n][0] for n in WEIGHTS], *[results[n][1] for n in WEIGHTS],
            *[results[n][2] for n in WEIGHTS], *[results[n][3] for n in WEIGHTS])
```

```python
import functools
import math

import numpy as np
import jax
import jax.numpy as jnp
from jax import lax
from jax.experimental import pallas as pl
from jax.experimental.pallas import tpu as pltpu

f32 = jnp.float32
_MXU = jnp.bfloat16
HI = lax.Precision.HIGHEST

D_MODEL = 1024
CHUNK = 64
EPS = 1e-6
N_DEV = 8
LANES = 128
HALO = 8
CONV_WIDTH = 4

RET_HEADS, RET_D = 4, 128
SSD_HEADS, SSD_P, SSD_N, SSD_GROUPS = 8, 64, 128, 2
SSD_INNER = SSD_HEADS * SSD_P
GDN_HEADS, GDN_D = 6, 128
GDN_W = GDN_HEADS * GDN_D
S5_CH, S5_GROUP, S5_GROUPS, S5_STATE = 256, 16, 16, 64
S5_W = S5_GROUPS * S5_STATE
D_FF = 4096
ROPE_THETA = 10000.0

IN0_W = 3592
IN0_PAD = 3712
IN1_W = 3340
IN1_PAD = 3456

ADAM_LR, ADAM_B1, ADAM_B2, ADAM_EPS, ADAM_WD, ADAM_STEP = 0.001, 0.9, 0.999, 1e-08, 0.01, 10

VMEM_LIMIT = 56 * 1024 * 1024


def _dot(a, b, dims):
    return lax.dot_general(a.astype(_MXU), b.astype(_MXU), (dims, ((), ())), preferred_element_type=f32)


@jax.custom_vjp
def mm(a, b):
    return _dot(a, b, ((1,), (0,)))


@jax.custom_vjp
def mm_nt(a, b):
    return _dot(a, b, ((1,), (1,)))


@jax.custom_vjp
def mm_tn(a, b):
    return _dot(a, b, ((0,), (0,)))


mm.defvjp(lambda a, b: (mm(a, b), (a, b)), lambda r, g: (mm_nt(g, r[1]), mm_tn(r[0], g)))
mm_nt.defvjp(lambda a, b: (mm_nt(a, b), (a, b)), lambda r, g: (mm(g, r[1]), mm_tn(g, r[0])))
mm_tn.defvjp(lambda a, b: (mm_tn(a, b), (a, b)), lambda r, g: (mm_nt(r[1], g), mm(r[0], g)))


def mmh(a, b):
    return jnp.dot(a, b, precision=HI, preferred_element_type=f32)


def _exact01(x, m01, dims, m_first):
    hi = x.astype(jnp.bfloat16)
    r = x - hi.astype(f32)
    mid = r.astype(jnp.bfloat16)
    lo = (r - mid.astype(f32)).astype(jnp.bfloat16)
    m = m01.astype(jnp.bfloat16)
    dot = lambda p: lax.dot_general(m, p, (dims, ((), ())), preferred_element_type=f32) if m_first else \
        lax.dot_general(p, m, (dims, ((), ())), preferred_element_type=f32)
    return dot(hi) + dot(mid) + dot(lo)


@jax.custom_vjp
def spread01(x, sel):
    return _exact01(x, sel, ((1,), (0,)), False)


spread01.defvjp(lambda x, sel: (spread01(x, sel), sel),
                lambda sel, g: (_exact01(g, sel, ((1,), (1,)), False), jnp.zeros_like(sel)))


@jax.custom_vjp
def cumsum01(tril, x):
    return _exact01(x, tril, ((1,), (0,)), True)


cumsum01.defvjp(lambda tril, x: (cumsum01(tril, x), tril),
                lambda tril, g: (jnp.zeros_like(tril), _exact01(g, tril, ((0,), (0,)), True)))


def _roll(x, shift, axis):
    return pltpu.roll(x, shift, axis)


@functools.partial(jax.custom_vjp, nondiff_argnums=(1,))
def roll_rows(x, s):
    return _roll(x, s, 0) if s else x


roll_rows.defvjp(lambda x, s: (roll_rows(x, s), None),
                 lambda s, _, g: ((_roll(g, g.shape[0] - s, 0) if s else g),))


@jax.custom_vjp
def roll_half(x):
    return _roll(x, x.shape[-1] // 2, 1)


roll_half.defvjp(lambda x: (roll_half(x), None), lambda _, g: (roll_half(g),))


def _iota(shape, axis):
    return lax.broadcasted_iota(jnp.int32, shape, axis)


def silu(x):
    return x * jax.nn.sigmoid(x)


def softplus(x):
    return jnp.maximum(x, 0.0) + jnp.log(1.0 + jnp.exp(-jnp.abs(x)))


def rmsnorm_f(x, w):
    return x * lax.rsqrt(jnp.mean(x * x, axis=-1, keepdims=True) + EPS) * w


def unit_rms(x):
    return x * lax.rsqrt(jnp.mean(x * x, axis=-1, keepdims=True) + EPS)


def _causal(n, strict=False):
    r, c = _iota((n, n), 0), _iota((n, n), 1)
    return (r > c) if strict else (r >= c)


def _tril_ones(n):
    return _causal(n).astype(f32)


def _conv_rows(xe, w):
    acc = w[CONV_WIDTH - 1:CONV_WIDTH, :] * xe
    for j in range(CONV_WIDTH - 1):
        acc = acc + w[j:j + 1, :] * roll_rows(xe, CONV_WIDTH - 1 - j)
    return acc[HALO:, :]


_RET_LOG_GAMMA = [float(np.log(np.float32(1.0) - np.float32(2.0) ** np.float32(-5.0 - h))) for h in range(RET_HEADS)]


def ret_chunk(q, k, v, gate, cos, sin, state):
    c = q.shape[0]
    idx = _iota((c, 1), 0).astype(f32)
    diff = (_iota((c, c), 0) - _iota((c, c), 1)).astype(f32)
    causal = _causal(c)
    hs = range(RET_HEADS)
    cols = [slice(h * RET_D, (h + 1) * RET_D) for h in hs]
    lg = _RET_LOG_GAMMA
    qh = [(q[:, s] * cos + roll_half(q[:, s]) * sin) * (RET_D ** -0.5) for s in cols]
    kh = [k[:, s] * cos + roll_half(k[:, s]) * sin for s in cols]
    vh = [v[:, s] for s in cols]
    sh = [state[s, :] for s in cols]
    scores = [mm_nt(qh[h], kh[h]) * jnp.exp(jnp.where(causal, lg[h] * diff, -jnp.inf)) for h in hs]
    inter = [mm(qh[h] * jnp.exp(lg[h] * (idx + 1.0)), sh[h]) for h in hs]
    y = [mm(scores[h], vh[h]) + inter[h] for h in hs]
    states = [sh[h] * math.exp(lg[h] * c) + mm_tn(kh[h] * jnp.exp(lg[h] * (c - 1.0 - idx)), vh[h]) for h in hs]
    outs = [unit_rms(y[h]) * silu(gate[:, cols[h]]) for h in hs]
    return jnp.concatenate(outs, axis=1), jnp.concatenate(states, axis=0)


def _head_select(n_heads, width):
    r, c = _iota((LANES, n_heads * width), 0), _iota((LANES, n_heads * width), 1)
    return (c // width == r).astype(f32)


def ssd_chunk(z, xe, dtr, state, conv_w, conv_b, dt_bias, a_log, d_skip, norm_w):
    c = z.shape[0]
    xbc = silu(_conv_rows(xe, conv_w) + conv_b)
    xs, bm, cm = xbc[:, :SSD_INNER], xbc[:, SSD_INNER:SSD_INNER + 256], xbc[:, SSD_INNER + 256:]
    sel = _head_select(SSD_HEADS, SSD_P)
    dt = softplus(dtr + dt_bias)
    la = dt * (-jnp.exp(a_log))
    la_cum = cumsum01(_tril_ones(c), la)
    la_cum_t = la_cum.T
    last = jnp.sum(la, axis=0, keepdims=True)
    xd = xs * spread01(dt, sel)
    la_x = spread01(la_cum, sel)
    last_x = spread01(last, sel)
    to_end = jnp.exp(last_x - la_x)
    from_start = jnp.exp(la_x)
    causal = _causal(c)
    left = (_iota((1, LANES), 1) < SSD_P).astype(f32)
    upper = _iota((LANES, 1), 0) < SSD_P
    pairs, heads = range(SSD_HEADS // 2), range(SSD_HEADS)
    bc = [bm[:, g * SSD_N:(g + 1) * SSD_N] for g in range(SSD_GROUPS)]
    cc = [cm[:, g * SSD_N:(g + 1) * SSD_N] for g in range(SSD_GROUPS)]
    cb = [mm_nt(cc[g], bc[g]) for g in range(SSD_GROUPS)]
    cols = [slice(p * LANES, (p + 1) * LANES) for p in pairs]
    xd_p = [xd[:, s] for s in cols]
    sp = [state[s, :] for s in cols]
    lmat = [jnp.exp(jnp.where(causal, la_cum[:, h:h + 1] - la_cum_t[h:h + 1, :], -jnp.inf)) for h in heads]
    off = [mm_nt(cc[p // 2], sp[p]) * from_start[:, cols[p]] for p in pairs]
    diag = [mm(cb[h // 4] * lmat[h], xd_p[h // 2] * (left if h % 2 == 0 else 1.0 - left)) for h in heads]
    cd = [jnp.where(upper, jnp.exp(last[:, 2 * p:2 * p + 1]), jnp.exp(last[:, 2 * p + 1:2 * p + 2])) for p in pairs]
    states = [sp[p] * cd[p] + mm_tn(xd_p[p] * to_end[:, cols[p]], bc[p // 2]) for p in pairs]
    ys = [off[p] + diag[2 * p] + diag[2 * p + 1] for p in pairs]
    y = jnp.concatenate(ys, axis=1) + spread01(d_skip, sel) * xs
    yg = y * silu(z)
    half = SSD_INNER // SSD_GROUPS
    out = jnp.concatenate([unit_rms(yg[:, i * half:(i + 1) * half]) for i in range(SSD_GROUPS)], axis=1) * norm_w
    return out, jnp.concatenate(states, axis=0)


def mm3(a, b):
    return jnp.dot(a, b, precision=lax.Precision.HIGH, preferred_element_type=f32)


@jax.custom_vjp
def _unit_lower_inverses(lowers):
    n = lowers[0].shape[0]
    eye = (_iota((n, n), 0) == _iota((n, n), 1)).astype(f32)
    a = [-l for l in lowers]
    p = [eye + x for x in a]
    k = 2
    while k < n:
        a = [mm3(x, x) for x in a]
        p = [y + mm3(y, x) for y, x in zip(p, a)]
        k *= 2
    return p


def _unit_lower_inverses_bwd(t_inv, g):
    dims_tn, dims_nt = (((0,), (0,)), ((), ())), (((1,), (1,)), ((), ()))
    x = [lax.dot_general(t, gi, dims_tn, precision=lax.Precision.HIGH, preferred_element_type=f32) for t, gi in zip(t_inv, g)]
    return ([-lax.dot_general(xi, t, dims_nt, precision=lax.Precision.HIGH, preferred_element_type=f32) for xi, t in zip(x, t_inv)],)


def _unit_lower_inverses_fwd(lowers):
    t_inv = _unit_lower_inverses(lowers)
    return t_inv, t_inv


_unit_lower_inverses.defvjp(_unit_lower_inverses_fwd, _unit_lower_inverses_bwd)


@jax.custom_vjp
def _known_inverses(lowers, t_inv):
    return t_inv


_known_inverses.defvjp(lambda lowers, t_inv: (t_inv, t_inv),
                       lambda t_inv, g: (_unit_lower_inverses_bwd(t_inv, g)[0], [jnp.zeros_like(t) for t in t_inv]))


GDN_GCOL = 6


def gdn_chunk(xe, z, ba, state, conv_w, a_log, dt_bias, norm_w, kept_inverses=None, keep_inverses=False):
    c = z.shape[0]
    qkv = silu(_conv_rows(xe, conv_w))
    beta_all = jax.nn.sigmoid(ba)
    g_all = -jnp.exp(a_log) * softplus(ba + dt_bias)
    gc = cumsum01(_tril_ones(c), g_all)
    gc_t = gc.T
    last = jnp.sum(g_all, axis=0, keepdims=True)
    causal, strict = _causal(c), _causal(c, strict=True)
    hs = range(GDN_HEADS)
    cols = [slice(h * GDN_D, (h + 1) * GDN_D) for h in hs]
    qh = [qkv[:, h * GDN_D:(h + 1) * GDN_D] for h in hs]
    kh = [qkv[:, GDN_W + h * GDN_D:GDN_W + (h + 1) * GDN_D] for h in hs]
    vh = [qkv[:, 2 * GDN_W + h * GDN_D:2 * GDN_W + (h + 1) * GDN_D] for h in hs]
    qh = [t * lax.rsqrt(jnp.sum(t * t, axis=-1, keepdims=True) + EPS) * (GDN_D ** -0.5) for t in qh]
    kh = [t * lax.rsqrt(jnp.sum(t * t, axis=-1, keepdims=True) + EPS) for t in kh]
    beta = [beta_all[:, h:h + 1] for h in hs]
    col = [gc[:, GDN_GCOL + h:GDN_GCOL + h + 1] for h in hs]
    row = [gc_t[GDN_GCOL + h:GDN_GCOL + h + 1, :] for h in hs]
    lst = [last[:, GDN_GCOL + h:GDN_GCOL + h + 1] for h in hs]
    decay = [jnp.exp(jnp.where(causal, col[h] - row[h], -jnp.inf)) for h in hs]
    e_col = [jnp.exp(t) for t in col]
    kb = [kh[h] * beta[h] for h in hs]
    vb = [vh[h] * beta[h] for h in hs]
    kk = [mm_nt(kb[h], kh[h]) for h in hs]
    qk = [mm_nt(qh[h], kh[h]) for h in hs]
    lowers = [jnp.where(strict, kk[h] * decay[h], 0.0) for h in hs]
    t_inv = _unit_lower_inverses(lowers) if kept_inverses is None else _known_inverses(lowers, list(kept_inverses))
    u = [mm(t_inv[h], vb[h]) for h in hs]
    w = [mm(t_inv[h], kb[h] * e_col[h]) for h in hs]
    attn = [jnp.where(causal, qk[h] * decay[h], 0.0) for h in hs]
    sh = [state[s, :] for s in cols]
    ws = [mm(w[h], sh[h]) for h in hs]
    qs = [mm(qh[h] * e_col[h], sh[h]) for h in hs]
    v_new = [u[h] - ws[h] for h in hs]
    o = [qs[h] + mm(attn[h], v_new[h]) for h in hs]
    states = [sh[h] * jnp.exp(lst[h]) + mm_tn(kh[h] * jnp.exp(lst[h] - col[h]), v_new[h]) for h in hs]
    outs = [unit_rms(o[h]) * norm_w * silu(z[:, cols[h]]) for h in hs]
    results = (jnp.concatenate(outs, axis=1), jnp.concatenate(states, axis=0))
    return results + (list(t_inv),) if keep_inverses else results


def _s5_group_mask():
    r, c = _iota((S5_CH, S5_W), 0), _iota((S5_CH, S5_W), 1)
    return (r // S5_GROUP == c // S5_STATE).astype(f32)


def s5_prep(a_re, a_im, log_step, b_re, b_im, c_re, c_im):
    r, c = _iota((LANES, S5_W), 0), _iota((LANES, S5_W), 1)
    step = jnp.exp(mmh(log_step, (c // S5_STATE == r).astype(f32)))
    zr, zi = a_re * step, a_im * step
    e = jnp.exp(zr)
    lr, li = e * jnp.cos(zi), e * jnp.sin(zi)
    den = a_re * a_re + a_im * a_im
    xr, xi = lr - 1.0, li
    cr, ci = (xr * a_re + xi * a_im) / den, (xi * a_re - xr * a_im) / den
    bbr, bbi = cr * b_re - ci * b_im, cr * b_im + ci * b_re
    mask = _s5_group_mask()
    tile = lambda t: jnp.tile(t, (S5_GROUPS, 1)) * mask
    return lr, li, tile(bbr), tile(bbi), tile(c_re), tile(c_im)


def s5_out_fn(h_re, h_im, u, cblk_re, cblk_im, d_skip, w_glu, b_glu):
    y = mm_nt(h_re, cblk_re) - mm_nt(h_im, cblk_im) + d_skip * u
    y = jax.nn.gelu(y)
    return y * jax.nn.sigmoid(mm(y, w_glu) + b_glu)


def _params(sem, **kw):
    return pltpu.CompilerParams(dimension_semantics=sem, vmem_limit_bytes=VMEM_LIMIT, **kw)


def _const_spec(shape):
    return pl.BlockSpec(shape, lambda i: (0,) * len(shape))


def _resident_spec(shape):
    return pl.BlockSpec(shape, lambda i: (0,) * len(shape), pipeline_mode=pl.Buffered(1))


def _row_spec(rows, cols, col_block=0):
    return pl.BlockSpec((rows, cols), lambda i: (i, col_block))


class Seq:
    def __init__(self, array, width, col_block, kind="tile"):
        self.array, self.width, self.col_block, self.kind = array, width, col_block, kind


CHUNKS_PER_STEP = 4


def mixer_fwd(name, fn, seqs, params, out_width, state_rows, seq_len, per_step=CHUNKS_PER_STEP, exchange=None, kept_shape=None):
    nc = seq_len // CHUNK
    rows, steps = per_step * CHUNK, nc // per_step
    n_refs = sum(2 if s.kind == "halo" else 1 for s in seqs)
    n_par = len(params)
    n_own = 3 if kept_shape else 2
    car = Carried(exchange)

    def body(*refs):
        seq_refs, par_refs = refs[:n_refs], refs[n_refs:n_refs + n_par]
        k0 = n_refs + n_par
        ex_ins, own = refs[k0:k0 + car.n], refs[k0 + car.n:k0 + car.n + n_own]
        out_ref, st_ref = own[:2]
        k0 += car.n + n_own
        ex_outs, state, ex_sems = refs[k0:k0 + car.n], refs[k0 + car.n], refs[k0 + car.n + 1:]
        c = pl.program_id(0)
        car.start_at(c == 0, ex_ins, ex_outs, ex_sems)

        @pl.when(c == 0)
        def _():
            state[...] = jnp.zeros_like(state)

        par_vals = [p[...] for p in par_refs]
        s_cur = state[...]
        for kk in range(per_step):
            lo = kk * CHUNK
            vals, k = [], 0
            for s in seqs:
                if s.kind == "halo":
                    prev = jnp.where(c > 0, seq_refs[k][...], 0.0) if kk == 0 else seq_refs[k + 1][lo - HALO:lo, :]
                    vals.append(jnp.concatenate([prev, seq_refs[k + 1][lo:lo + CHUNK, :]], axis=0))
                    k += 2
                else:
                    vals.append(seq_refs[k][lo:lo + CHUNK, :])
                    k += 1
            st_ref[kk] = s_cur
            res = fn(*vals, s_cur, *par_vals)
            out_ref[lo:lo + CHUNK, :] = res[0]
            s_cur = res[1]
            if kept_shape:
                for h, t in enumerate(res[2]):
                    own[2][kk, h] = t
        state[...] = s_cur
        car.wait_at(c == steps - 1, ex_ins, ex_outs, ex_sems)

    in_specs, operands = [], []
    for s in seqs:
        if s.kind == "halo":
            rb, w, cb = rows // HALO, s.width, s.col_block
            in_specs.append(pl.BlockSpec((HALO, w), lambda i, rb=rb, cb=cb: (jnp.maximum(i * rb - 1, 0), cb)))
            operands.append(s.array)
        in_specs.append(pl.BlockSpec((rows, s.width), lambda i, cb=s.col_block: (i, cb)))
        operands.append(s.array)
    for p in params:
        in_specs.append(_const_spec(p.shape))
        operands.append(p)
    outs = pl.pallas_call(
        body, grid=(steps,), in_specs=in_specs + car.in_specs,
        out_specs=[pl.BlockSpec((rows, out_width), lambda i: (i, 0)),
                   pl.BlockSpec((per_step, state_rows, LANES), lambda i: (i, 0, 0))]
        + ([pl.BlockSpec((per_step,) + tuple(kept_shape), lambda i: (i, 0, 0, 0))] if kept_shape else []) + car.out_specs,
        out_shape=[jax.ShapeDtypeStruct((seq_len, out_width), f32),
                   jax.ShapeDtypeStruct((nc, state_rows, LANES), f32)]
        + ([jax.ShapeDtypeStruct((nc,) + tuple(kept_shape), f32)] if kept_shape else []) + car.out_shape,
        scratch_shapes=[pltpu.VMEM((state_rows, LANES), f32)] + car.scratch,
        compiler_params=_params(("arbitrary",)), name=name)(*operands, *car.operands)
    return tuple(outs[:n_own]) + (outs[n_own:],)


def mixer_bwd(name, fn, seqs, params, states, d_out, seq_len, per_step=CHUNKS_PER_STEP, exchange=None, kept=None):
    nc = seq_len // CHUNK
    rows, steps = per_step * CHUNK, nc // per_step
    state_rows = states.shape[1]
    diff = [s for s in seqs if s.kind != "const"]
    halos = [s for s in diff if s.kind == "halo"]
    n_refs = sum(2 if s.kind == "halo" else 1 for s in seqs)
    n_par = len(params)
    n_kept = 0 if kept is None else 1
    car = Carried(exchange)

    def body(*refs):
        seq_refs, par_refs = refs[:n_refs], refs[n_refs:n_refs + n_par]
        st_ref, dout_ref = refs[n_refs + n_par:n_refs + n_par + 2]
        k0 = n_refs + n_par + 2
        kept_ref = refs[k0] if n_kept else None
        k0 += n_kept
        ex_ins = refs[k0:k0 + car.n]
        k0 += car.n
        dseq_refs, dpar_refs = refs[k0:k0 + len(diff)], refs[k0 + len(diff):k0 + len(diff) + n_par]
        k0 += len(diff) + n_par
        ex_outs = refs[k0:k0 + car.n]
        scratch = refs[k0 + car.n:]
        d_state, carries, ex_sems = scratch[0], scratch[1:1 + len(halos)], scratch[1 + len(halos):]
        i = pl.program_id(0)
        step = steps - 1 - i
        car.start_at(i == 0, ex_ins, ex_outs, ex_sems)

        @pl.when(i == 0)
        def _():
            d_state[...] = jnp.zeros_like(d_state)
            for r in list(carries) + list(dpar_refs):
                r[...] = jnp.zeros_like(r)

        par_vals = [p[...] for p in par_refs]
        d_s = d_state[...]
        d_par = [None] * n_par
        halo_ct = [r[...] for r in carries]
        for kk in range(per_step - 1, -1, -1):
            lo = kk * CHUNK
            dvals, consts, k = [], [], 0
            for s in seqs:
                if s.kind == "halo":
                    prev = jnp.where(step > 0, seq_refs[k][...], 0.0) if kk == 0 else seq_refs[k + 1][lo - HALO:lo, :]
                    dvals.append(jnp.concatenate([prev, seq_refs[k + 1][lo:lo + CHUNK, :]], axis=0))
                    k += 2
                elif s.kind == "tile":
                    dvals.append(seq_refs[k][lo:lo + CHUNK, :])
                    k += 1
                else:
                    consts.append(seq_refs[k][lo:lo + CHUNK, :])
                    k += 1
            nd = len(dvals)

            extra = [[kept_ref[kk, h] for h in range(kept.shape[1])]] if n_kept else []

            def call(*a, consts=consts, nd=nd, extra=extra):
                it_d, it_c = iter(a[:nd]), iter(consts)
                vals = [next(it_c) if s.kind == "const" else next(it_d) for s in seqs]
                return fn(*vals, *a[nd:], *extra)

            _, vjp = jax.vjp(call, *dvals, st_ref[kk], *par_vals)
            cts = vjp((dout_ref[lo:lo + CHUNK, :], d_s))
            hk = 0
            for j, s in enumerate(diff):
                if s.kind == "halo":
                    dseq_refs[j][lo:lo + CHUNK, :] = cts[j][HALO:, :]
                    dseq_refs[j][lo + CHUNK - HALO:lo + CHUNK, :] += halo_ct[hk]
                    halo_ct[hk] = cts[j][:HALO, :]
                    hk += 1
                else:
                    dseq_refs[j][lo:lo + CHUNK, :] = cts[j]
            d_s = cts[nd]
            for j in range(n_par):
                d_par[j] = cts[nd + 1 + j] if d_par[j] is None else d_par[j] + cts[nd + 1 + j]
        d_state[...] = d_s
        for r, v in zip(carries, halo_ct):
            r[...] = v
        for j in range(n_par):
            dpar_refs[j][...] += d_par[j]
        car.wait_at(i == steps - 1, ex_ins, ex_outs, ex_sems)

    in_specs, operands = [], []
    for s in seqs:
        if s.kind == "halo":
            rb, cb = rows // HALO, s.col_block
            in_specs.append(pl.BlockSpec((HALO, s.width), lambda i, rb=rb, cb=cb: (jnp.maximum((steps - 1 - i) * rb - 1, 0), cb)))
            operands.append(s.array)
        in_specs.append(pl.BlockSpec((rows, s.width), lambda i, cb=s.col_block: (steps - 1 - i, cb)))
        operands.append(s.array)
    for p in params:
        in_specs.append(_const_spec(p.shape))
        operands.append(p)
    in_specs.append(pl.BlockSpec((per_step, state_rows, LANES), lambda i: (steps - 1 - i, 0, 0)))
    in_specs.append(pl.BlockSpec((rows, d_out.shape[1]), lambda i: (steps - 1 - i, 0)))
    operands += [states, d_out]
    if n_kept:
        in_specs.append(pl.BlockSpec((per_step,) + kept.shape[1:], lambda i: (steps - 1 - i, 0, 0, 0)))
        operands.append(kept)
    outs = pl.pallas_call(
        body, grid=(steps,), in_specs=in_specs + car.in_specs,
        out_specs=[pl.BlockSpec((rows, s.width), lambda i: (steps - 1 - i, 0)) for s in diff]
        + [_const_spec(p.shape) for p in params] + car.out_specs,
        out_shape=[jax.ShapeDtypeStruct((seq_len, s.width), f32) for s in diff]
        + [jax.ShapeDtypeStruct(p.shape, f32) for p in params] + car.out_shape,
        scratch_shapes=[pltpu.VMEM((state_rows, LANES), f32)] + [pltpu.VMEM((HALO, s.width), f32) for s in halos]
        + car.scratch,
        compiler_params=_params(("arbitrary",)), name=name)(*operands, *car.operands)
    nd = len(diff)
    return outs[:nd], outs[nd:nd + n_par], outs[nd + n_par:]


TOK = 512


def inproj_fwd(name, x, nw, w):
    seq_len, n = x.shape[0], w.shape[1]

    def body(x_ref, nw_ref, w_ref, o_ref):
        o_ref[...] = mm(rmsnorm_f(x_ref[...], nw_ref[...]), w_ref[...])

    return pl.pallas_call(
        body, grid=(seq_len // TOK,),
        in_specs=[_row_spec(TOK, D_MODEL), _const_spec(nw.shape), _resident_spec(w.shape)],
        out_specs=_row_spec(TOK, n), out_shape=jax.ShapeDtypeStruct((seq_len, n), f32),
        compiler_params=_params(("arbitrary",)), name=name)(x, nw, w)


def inproj_bwd(name, x, nw, w, pieces, d_res):
    seq_len, n = x.shape[0], w.shape[1]
    widths = [p.shape[1] for p in pieces]
    assert sum(widths) == n
    k = len(pieces)

    def body(*refs):
        x_ref, nw_ref, w_ref = refs[:3]
        p_refs, dres_ref = refs[3:3 + k], refs[3 + k]
        dx_ref, dw_ref, dnw_ref = refs[4 + k:]

        @pl.when(pl.program_id(0) == 0)
        def _():
            dw_ref[...] = jnp.zeros_like(dw_ref)
            dnw_ref[...] = jnp.zeros_like(dnw_ref)

        h, vjp = jax.vjp(rmsnorm_f, x_ref[...], nw_ref[...])
        dh, off = jnp.zeros_like(h), 0
        for p_ref, wd in zip(p_refs, widths):
            g = p_ref[...]
            dh = dh + mm_nt(g, w_ref[:, off:off + wd])
            dw_ref[:, off:off + wd] += mm_tn(h, g)
            off += wd
        dx, dnw = vjp(dh)
        dx_ref[...] = dres_ref[...] + dx
        dnw_ref[...] += dnw

    return pl.pallas_call(
        body, grid=(seq_len // TOK,),
        in_specs=[_row_spec(TOK, D_MODEL), _const_spec(nw.shape), _resident_spec(w.shape)]
        + [_row_spec(TOK, wd) for wd in widths] + [_row_spec(TOK, D_MODEL)],
        out_specs=[_row_spec(TOK, D_MODEL), _resident_spec((D_MODEL, n)), _const_spec(nw.shape)],
        out_shape=[jax.ShapeDtypeStruct((seq_len, D_MODEL), f32), jax.ShapeDtypeStruct((D_MODEL, n), f32),
                   jax.ShapeDtypeStruct(nw.shape, f32)],
        compiler_params=_params(("arbitrary",)), name=name)(x, nw, w, *pieces, d_res)


def outproj_fwd(name, x, a, b, w):
    seq_len, wa, wb = x.shape[0], a.shape[1], b.shape[1]

    def body(x_ref, a_ref, b_ref, w_ref, o_ref):
        o_ref[...] = x_ref[...] + mm(a_ref[...], w_ref[:wa, :]) + mm(b_ref[...], w_ref[wa:, :])

    return pl.pallas_call(
        body, grid=(seq_len // TOK,),
        in_specs=[_row_spec(TOK, D_MODEL), _row_spec(TOK, wa), _row_spec(TOK, wb), _resident_spec(w.shape)],
        out_specs=_row_spec(TOK, D_MODEL), out_shape=jax.ShapeDtypeStruct((seq_len, D_MODEL), f32),
        compiler_params=_params(("arbitrary",)), name=name)(x, a, b, w)


def outproj_bwd(name, dy, a, b, w):
    seq_len, wa, wb = dy.shape[0], a.shape[1], b.shape[1]

    def body(dy_ref, a_ref, b_ref, w_ref, da_ref, db_ref, dw_ref):
        @pl.when(pl.program_id(0) == 0)
        def _():
            dw_ref[...] = jnp.zeros_like(dw_ref)

        g = dy_ref[...]
        da_ref[...] = mm_nt(g, w_ref[:wa, :])
        db_ref[...] = mm_nt(g, w_ref[wa:, :])
        dw_ref[:wa, :] += mm_tn(a_ref[...], g)
        dw_ref[wa:, :] += mm_tn(b_ref[...], g)

    return pl.pallas_call(
        body, grid=(seq_len // TOK,),
        in_specs=[_row_spec(TOK, D_MODEL), _row_spec(TOK, wa), _row_spec(TOK, wb), _resident_spec(w.shape)],
        out_specs=[_row_spec(TOK, wa), _row_spec(TOK, wb), _resident_spec(w.shape)],
        out_shape=[jax.ShapeDtypeStruct((seq_len, wa), f32), jax.ShapeDtypeStruct((seq_len, wb), f32),
                   jax.ShapeDtypeStruct(w.shape, f32)],
        compiler_params=_params(("arbitrary",)), name=name)(dy, a, b, w)


FF_BLOCK = D_FF // N_DEV


def mlp_fwd(name, x, nw, w_up, w_down, exchange=None):
    seq_len = x.shape[0]
    nt = seq_len // TOK
    car = Carried(exchange)

    def body(*refs):
        x_ref, nw_ref, up_ref, down_ref = refs[:4]
        ex_ins, (o_ref, relu_ref) = refs[4:4 + car.n], refs[4 + car.n:6 + car.n]
        ex_outs, ex_sems = refs[6 + car.n:6 + 2 * car.n], refs[6 + 2 * car.n:]
        i = pl.program_id(0)
        car.start_at(i == 0, ex_ins, ex_outs, ex_sems)
        xv = x_ref[...]
        h = rmsnorm_f(xv, nw_ref[...])
        acc = xv
        for d in range(N_DEV):
            r = jnp.maximum(mm(h, up_ref[d]), 0.0)
            relu_ref[d] = r.astype(_MXU)
            acc = acc + mm(r * r, down_ref[d])
        o_ref[...] = acc
        car.wait_at(i == nt - 1, ex_ins, ex_outs, ex_sems)

    outs = pl.pallas_call(
        body, grid=(nt,),
        in_specs=[_row_spec(TOK, D_MODEL), _const_spec(nw.shape), _resident_spec(w_up.shape), _resident_spec(w_down.shape)]
        + car.in_specs,
        out_specs=[_row_spec(TOK, D_MODEL), pl.BlockSpec((N_DEV, TOK, FF_BLOCK), lambda i: (0, i, 0))] + car.out_specs,
        out_shape=[jax.ShapeDtypeStruct((seq_len, D_MODEL), f32),
                   jax.ShapeDtypeStruct((N_DEV, seq_len, FF_BLOCK), _MXU)] + car.out_shape, scratch_shapes=car.scratch,
        compiler_params=_params(("arbitrary",)), name=name)(x, nw, w_up, w_down, *car.operands)
    return outs[0], outs[1], outs[2:]


MLP_SPLIT = 2


def mlp_bwd(name, x, nw, w_up, w_down, relu, dy, exchange=None):
    seq_len = x.shape[0]
    nt = seq_len // TOK
    per = N_DEV // MLP_SPLIT
    dh = d_up = d_down = ex_results = None
    for j in range(MLP_SPLIT):
        last, has_prev = j == MLP_SPLIT - 1, j > 0
        car = Carried(exchange if j == 0 else None)
        n_in = 6 + (3 if has_prev else 0)
        n_out = 4 if last else 3

        def body(*refs, last=last, has_prev=has_prev, car=car, n_in=n_in, n_out=n_out):
            x_ref, nw_ref, up_ref, down_ref, relu_ref, dy_ref = refs[:6]
            dh_prev_ref = refs[6] if has_prev else None
            ex_ins, own = refs[n_in:n_in + car.n], refs[n_in + car.n:n_in + car.n + n_out]
            ex_outs, ex_sems = refs[n_in + car.n + n_out:n_in + 2 * car.n + n_out], refs[n_in + 2 * car.n + n_out:]
            dup_ref, ddown_ref = own[-2:]
            i = pl.program_id(0)
            car.start_at(i == 0, ex_ins, ex_outs, ex_sems)

            @pl.when(i == 0)
            def _():
                dup_ref[...] = jnp.zeros_like(dup_ref)
                ddown_ref[...] = jnp.zeros_like(ddown_ref)
                if last:
                    own[1][...] = jnp.zeros_like(own[1])

            h, vjp = jax.vjp(rmsnorm_f, x_ref[...], nw_ref[...])
            g = dy_ref[...]
            dh_acc = dh_prev_ref[...] if has_prev else jnp.zeros_like(h)
            for d in range(per):
                r = relu_ref[d].astype(f32)
                da = mm_nt(g, down_ref[d]) * (2.0 * r)
                ddown_ref[d] += mm_tn(r * r, g)
                dup_ref[d] += mm_tn(h, da)
                dh_acc = dh_acc + mm_nt(da, up_ref[d])
            if last:
                dx, dnw = vjp(dh_acc)
                own[0][...] = g + dx
                own[1][...] += dnw
            else:
                own[0][...] = dh_acc
            car.wait_at(i == nt - 1, ex_ins, ex_outs, ex_sems)

        weights = lambda shape: pl.BlockSpec(shape, lambda i, j=j: (j, 0, 0), pipeline_mode=pl.Buffered(1))
        in_specs = [_row_spec(TOK, D_MODEL), _const_spec(nw.shape), weights((per, D_MODEL, FF_BLOCK)),
                    weights((per, FF_BLOCK, D_MODEL)), pl.BlockSpec((per, TOK, FF_BLOCK), lambda i, j=j: (j, i, 0)),
                    _row_spec(TOK, D_MODEL)]
        operands = [x, nw, w_up, w_down, relu, dy]
        if has_prev:
            in_specs += [_row_spec(TOK, D_MODEL), ANY, ANY]
            operands += [dh, d_up, d_down]
        first_outs = ([_row_spec(TOK, D_MODEL), _const_spec(nw.shape)] if last else [_row_spec(TOK, D_MODEL)])
        first_shapes = [jax.ShapeDtypeStruct((seq_len, D_MODEL), f32)] + ([jax.ShapeDtypeStruct(nw.shape, f32)] if last else [])
        outs = pl.pallas_call(
            body, grid=(nt,), in_specs=in_specs + car.in_specs,
            out_specs=first_outs + [weights((per, D_MODEL, FF_BLOCK)), weights((per, FF_BLOCK, D_MODEL))] + car.out_specs,
            out_shape=first_shapes + [jax.ShapeDtypeStruct(w_up.shape, f32), jax.ShapeDtypeStruct(w_down.shape, f32)]
            + car.out_shape,
            scratch_shapes=car.scratch,
            input_output_aliases=({7: n_out - 2, 8: n_out - 1} if has_prev else {}),
            compiler_params=_params(("arbitrary",)), name=name if j == 0 else f"{name}_{j}")(*operands, *car.operands)
        dh, d_up, d_down = outs[0], outs[n_out - 2], outs[n_out - 1]
        if j == 0:
            ex_results = outs[n_out:]
        if last:
            dx, dnw = outs[0], outs[1]
    return dx, d_up, d_down, dnw, ex_results


def final_loss(name, x, nw, target):
    seq_len = x.shape[0]

    def body(x_ref, nw_ref, t_ref, loss_ref, dx_ref, dnw_ref):
        @pl.when(pl.program_id(0) == 0)
        def _():
            loss_ref[...] = jnp.zeros_like(loss_ref)
            dnw_ref[...] = jnp.zeros_like(dnw_ref)

        y, vjp = jax.vjp(rmsnorm_f, x_ref[...], nw_ref[...])
        err = y - t_ref[...]
        loss_ref[...] += 0.5 * jnp.sum(jnp.mean(err * err, axis=-1, keepdims=True), axis=0, keepdims=True)
        dx, dnw = vjp(err * (1.0 / D_MODEL))
        dx_ref[...] = dx
        dnw_ref[...] += dnw

    return pl.pallas_call(
        body, grid=(seq_len // TOK,),
        in_specs=[_row_spec(TOK, D_MODEL), _const_spec(nw.shape), _row_spec(TOK, D_MODEL)],
        out_specs=[_const_spec((8, LANES)), _row_spec(TOK, D_MODEL), _const_spec(nw.shape)],
        out_shape=[jax.ShapeDtypeStruct((8, LANES), f32), jax.ShapeDtypeStruct((seq_len, D_MODEL), f32),
                   jax.ShapeDtypeStruct(nw.shape, f32)],
        compiler_params=_params(("arbitrary",)), name=name)(x, nw, target)


def _whole(a):
    return pl.BlockSpec(a.shape, lambda: (0,) * len(a.shape))


def s5_prep_fwd(name, raw):
    def body(*refs):
        outs = s5_prep(*[r[...] for r in refs[:7]])
        for o_ref, o in zip(refs[7:], outs):
            o_ref[...] = o

    shapes = [(1, S5_W)] * 2 + [(S5_CH, S5_W)] * 4
    return pl.pallas_call(
        body, in_specs=[_whole(a) for a in raw], out_specs=[pl.BlockSpec(s, lambda s=s: (0,) * len(s)) for s in shapes],
        out_shape=[jax.ShapeDtypeStruct(s, f32) for s in shapes],
        compiler_params=pltpu.CompilerParams(vmem_limit_bytes=VMEM_LIMIT), name=name)(*raw)


def s5_prep_bwd(name, raw, cts):
    def body(*refs):
        _, vjp = jax.vjp(s5_prep, *[r[...] for r in refs[:7]])
        grads = vjp(tuple(r[...] for r in refs[7:13]))
        for o_ref, g in zip(refs[13:], grads):
            o_ref[...] = g

    return pl.pallas_call(
        body, in_specs=[_whole(a) for a in list(raw) + list(cts)], out_specs=[_whole(a) for a in raw],
        out_shape=[jax.ShapeDtypeStruct(a.shape, f32) for a in raw],
        compiler_params=pltpu.CompilerParams(vmem_limit_bytes=VMEM_LIMIT), name=name)(*raw, *cts)


SCAN_SEG = 8
SCAN_LEN = TOK // SCAN_SEG


def _segment_major():
    r, t = _iota((TOK, TOK), 0), _iota((TOK, TOK), 1)
    return (t == (r % SCAN_SEG) * SCAN_LEN + r // SCAN_SEG).astype(f32)


def _store_powers(lr, li, pr_ref, pi_ref):
    qr, qi = lr, li
    for j in range(SCAN_LEN):
        pr_ref[j:j + 1, :] = qr
        pi_ref[j:j + 1, :] = qi
        qr, qi = lr * qr - li * qi, lr * qi + li * qr


def _tile_scan(xr_ref, xi_ref, lr, li, pr_ref, pi_ref, cr_ref, ci_ref, carry_re, carry_im, reverse):
    sign = -1.0 if reverse else 1.0
    ar = jnp.broadcast_to(lr, (SCAN_SEG, lr.shape[1]))
    ai = jnp.broadcast_to(sign * li, (SCAN_SEG, li.shape[1]))
    rows = lambda j: slice(j * SCAN_SEG, (j + 1) * SCAN_SEG)
    hr = hi = jnp.zeros_like(ar)
    for j in (range(SCAN_LEN - 1, -1, -1) if reverse else range(SCAN_LEN)):
        hr, hi = ar * hr - ai * hi + xr_ref[rows(j), :], ar * hi + ai * hr + xi_ref[rows(j), :]
        xr_ref[rows(j), :] = hr
        xi_ref[rows(j), :] = hi
    wr, wi = pr_ref[SCAN_LEN - 1:SCAN_LEN, :], sign * pi_ref[SCAN_LEN - 1:SCAN_LEN, :]
    er, ei = carry_re[0:1, :], carry_im[0:1, :]
    for s in (range(SCAN_SEG - 1, -1, -1) if reverse else range(SCAN_SEG)):
        cr_ref[s:s + 1, :] = er
        ci_ref[s:s + 1, :] = ei
        er, ei = hr[s:s + 1, :] + wr * er - wi * ei, hi[s:s + 1, :] + wr * ei + wi * er
    carry_re[0:1, :] = er
    carry_im[0:1, :] = ei
    cr, ci = cr_ref[...], ci_ref[...]
    for j in range(SCAN_LEN):
        k = SCAN_LEN - 1 - j if reverse else j
        qr, qi = pr_ref[k:k + 1, :], sign * pi_ref[k:k + 1, :]
        xr_ref[rows(j), :] += qr * cr - qi * ci
        xi_ref[rows(j), :] += qr * ci + qi * cr


_SCAN_SCRATCH = [pltpu.VMEM((HALO, S5_W), f32), pltpu.VMEM((HALO, S5_W), f32),
                 pltpu.VMEM((SCAN_LEN, S5_W), f32), pltpu.VMEM((SCAN_LEN, S5_W), f32),
                 pltpu.VMEM((SCAN_SEG, S5_W), f32), pltpu.VMEM((SCAN_SEG, S5_W), f32)]


def s5_fwd(name, proj, u_block, lam_re, lam_im, bblk_re, bblk_im, cblk_re, cblk_im, d_skip, w_glu, b_glu):
    seq_len = proj.shape[0]

    def body(u_ref, lr_ref, li_ref, br_ref, bi_ref, cr_ref, ci_ref, d_ref, wg_ref, bg_ref,
             o_ref, hr_ref, hi_ref, carry_re, carry_im, pw_re, pw_im, cb_re, cb_im):
        lr, li = lr_ref[...], li_ref[...]

        @pl.when(pl.program_id(0) == 0)
        def _():
            carry_re[...] = jnp.zeros_like(carry_re)
            carry_im[...] = jnp.zeros_like(carry_im)
            _store_powers(lr, li, pw_re, pw_im)

        perm = _segment_major()
        u = _exact01(u_ref[...], perm, ((1,), (0,)), True)
        hr_ref[...] = mm(u, br_ref[...])
        hi_ref[...] = mm(u, bi_ref[...])
        _tile_scan(hr_ref, hi_ref, lr, li, pw_re, pw_im, cb_re, cb_im, carry_re, carry_im, reverse=False)
        out = s5_out_fn(hr_ref[...], hi_ref[...], u, cr_ref[...], ci_ref[...], d_ref[...],
                        wg_ref[...].astype(f32), bg_ref[...])
        o_ref[...] = _exact01(out, perm, ((0,), (0,)), True)

    consts = [lam_re, lam_im, bblk_re, bblk_im, cblk_re, cblk_im, d_skip, w_glu, b_glu]
    return pl.pallas_call(
        body, grid=(seq_len // TOK,),
        in_specs=[_row_spec(TOK, S5_CH, u_block)] + [_const_spec(a.shape) for a in consts],
        out_specs=[_row_spec(TOK, S5_CH), _row_spec(TOK, S5_W), _row_spec(TOK, S5_W)],
        out_shape=[jax.ShapeDtypeStruct((seq_len, S5_CH), f32), jax.ShapeDtypeStruct((seq_len, S5_W), f32),
                   jax.ShapeDtypeStruct((seq_len, S5_W), f32)],
        scratch_shapes=_SCAN_SCRATCH,
        compiler_params=_params(("arbitrary",)), name=name)(proj, *consts)


def s5_bwd(name, proj, u_block, h_re, h_im, d_out, lam_re, lam_im, bblk_re, bblk_im, cblk_re, cblk_im, d_skip, w_glu, b_glu):
    seq_len = proj.shape[0]
    nt = seq_len // TOK
    consts = [lam_re, lam_im, bblk_re, bblk_im, cblk_re, cblk_im, d_skip, w_glu, b_glu]

    def body(u_ref, hr_ref, hi_ref, pr_ref, pi_ref, dout_ref, lr_ref, li_ref, br_ref, bi_ref, cr_ref, ci_ref, d_ref, wg_ref, bg_ref,
             du_ref, dlr_ref, dli_ref, dbr_ref, dbi_ref, dcr_ref, dci_ref, dd_ref, dwg_ref, dbg_ref,
             gr_ref, gi_ref, carry_re, carry_im, pw_re, pw_im, cb_re, cb_im):
        i = pl.program_id(0)
        tile = nt - 1 - i
        lr, li = lr_ref[...], li_ref[...]

        @pl.when(i == 0)
        def _():
            for r in (carry_re, carry_im, dlr_ref, dli_ref, dbr_ref, dbi_ref, dcr_ref, dci_ref, dd_ref, dwg_ref, dbg_ref):
                r[...] = jnp.zeros_like(r)
            _store_powers(lr, li, pw_re, pw_im)

        perm = _segment_major()
        u = _exact01(u_ref[...], perm, ((1,), (0,)), True)
        d_out_p = _exact01(dout_ref[...], perm, ((1,), (0,)), True)
        h_r, h_i = hr_ref[...], hi_ref[...]
        _, vjp = jax.vjp(s5_out_fn, h_r, h_i, u, cr_ref[...], ci_ref[...], d_ref[...], wg_ref[...].astype(f32), bg_ref[...])
        ghr, ghi, du, dcr, dci, dd, dwg, dbg = vjp(d_out_p)
        gr_ref[...] = ghr
        gi_ref[...] = ghi
        _tile_scan(gr_ref, gi_ref, lr, li, pw_re, pw_im, cb_re, cb_im, carry_re, carry_im, reverse=True)
        g_r, g_i = gr_ref[...], gi_ref[...]
        keep = jnp.where(tile > 0, 1.0, 0.0)
        top = _iota((SCAN_SEG, 1), 0) == 0

        def earlier(h, before_ref):
            head = jnp.where(top, before_ref[HALO - 1:HALO, :] * keep, _roll(h[TOK - SCAN_SEG:, :], 1, 0))
            return jnp.concatenate([head, h[:TOK - SCAN_SEG, :]], axis=0)

        p_r, p_i = earlier(h_r, pr_ref), earlier(h_i, pi_ref)
        dlr_ref[...] += jnp.sum(g_r * p_r + g_i * p_i, axis=0, keepdims=True)
        dli_ref[...] += jnp.sum(g_i * p_r - g_r * p_i, axis=0, keepdims=True)
        du_p = du + mm_nt(g_r, br_ref[...]) + mm_nt(g_i, bi_ref[...])
        du_ref[...] = _exact01(du_p, perm, ((0,), (0,)), True)
        dbr_ref[...] += mm_tn(u, g_r)
        dbi_ref[...] += mm_tn(u, g_i)
        dcr_ref[...] += dcr
        dci_ref[...] += dci
        dd_ref[...] += dd
        dwg_ref[...] += dwg
        dbg_ref[...] += dbg

    rev = lambda cols, cb=0: pl.BlockSpec((TOK, cols), lambda i, cb=cb: (nt - 1 - i, cb))
    prev = pl.BlockSpec((HALO, S5_W), lambda i: (jnp.maximum((nt - 1 - i) * (TOK // HALO) - 1, 0), 0))
    outs = pl.pallas_call(
        body, grid=(nt,),
        in_specs=[rev(S5_CH, u_block), rev(S5_W), rev(S5_W), prev, prev, rev(S5_CH)] + [_const_spec(a.shape) for a in consts],
        out_specs=[rev(S5_CH)] + [_const_spec(a.shape) for a in consts],
        out_shape=[jax.ShapeDtypeStruct((seq_len, S5_CH), f32)] + [jax.ShapeDtypeStruct(a.shape, f32) for a in consts],
        scratch_shapes=[pltpu.VMEM((TOK, S5_W), f32), pltpu.VMEM((TOK, S5_W), f32)] + _SCAN_SCRATCH,
        compiler_params=_params(("arbitrary",)), name=name)(proj, h_re, h_im, h_re, h_im, d_out, *consts)
    return outs[0], outs[1:]


ANY = pl.BlockSpec(memory_space=pl.ANY)


def _mesh_position():
    x, y, c = lax.axis_index("x"), lax.axis_index("y"), lax.axis_index("c")
    return x, y, c, 4 * x + 2 * y + c


def _peer(x, y, c, r):
    px = 1 - x if r & 4 else x
    py = 1 - y if r & 2 else y
    pc = 1 - c if r & 1 else c
    return (px, py, pc), 4 * px + 2 * py + pc


class Exchange:
    def __init__(self, arrays, gather):
        self.arrays, self.gather, self.n = list(arrays), gather, len(arrays)
        self.in_specs = [ANY] * self.n
        self.out_specs = [ANY] * self.n
        shapes = [((N_DEV,) + a.shape) if gather else a.shape for a in self.arrays]
        self.out_shape = [jax.ShapeDtypeStruct(s, a.dtype) for s, a in zip(shapes, self.arrays)]
        self.scratch = [pltpu.SemaphoreType.DMA((self.n, N_DEV - 1)), pltpu.SemaphoreType.DMA((self.n, N_DEV - 1)),
                        pltpu.SemaphoreType.DMA((self.n,))]

    def _copies(self, ins, outs, sems, landed):
        send_sems, recv_sems, local_sems = sems
        x, y, c, me = _mesh_position()
        local, remote = [], []
        for i in range(self.n):
            mine = ins[i] if self.gather else ins[i].at[me]
            local.append(pltpu.make_async_copy(mine, outs[i].at[me], local_sems.at[i]))
            for r in range(1, N_DEV):
                peer, peer_idx = _peer(x, y, c, r)
                remote.append(pltpu.make_async_remote_copy(
                    src_ref=ins[i] if self.gather else ins[i].at[peer_idx],
                    dst_ref=outs[i].at[peer_idx if landed else me],
                    send_sem=send_sems.at[i, r - 1], recv_sem=recv_sems.at[i, r - 1],
                    device_id=peer, device_id_type=pl.DeviceIdType.MESH))
        return local, remote

    def start(self, ins, outs, sems):
        local, remote = self._copies(ins, outs, sems, landed=False)
        for cp in local + remote:
            cp.start()

    def wait(self, ins, outs, sems):
        local, remote = self._copies(ins, outs, sems, landed=True)
        for cp in remote:
            cp.wait_recv()
            cp.wait_send()
        for cp in local:
            cp.wait()

    def run(self, name):
        n = self.n

        def body(*refs):
            ins, outs, sems = refs[:n], refs[n:2 * n], refs[2 * n:]
            self.start(ins, outs, sems)
            self.wait(ins, outs, sems)

        return pl.pallas_call(body, in_specs=self.in_specs, out_specs=self.out_specs, out_shape=self.out_shape,
                              scratch_shapes=self.scratch, name=name)(*self.arrays)


class Carried:
    def __init__(self, exchange):
        self.ex = exchange
        self.n = exchange.n if exchange else 0
        self.in_specs = exchange.in_specs if exchange else []
        self.out_specs = exchange.out_specs if exchange else []
        self.out_shape = exchange.out_shape if exchange else []
        self.scratch = exchange.scratch if exchange else []
        self.operands = exchange.arrays if exchange else []

    def start_at(self, first, ins, outs, sems):
        if self.ex is not None:
            @pl.when(first)
            def _():
                self.ex.start(ins, outs, sems)

    def wait_at(self, last, ins, outs, sems):
        if self.ex is not None:
            @pl.when(last)
            def _():
                self.ex.wait(ins, outs, sems)


def adamw(name, parts, w, m, v):
    rows, cols = w.shape
    tr = rows
    for cand in (512, 256, 128, 64, 32, 16, 8):
        if rows * cols * 4 > (1 << 20) and rows % cand == 0 and cand * cols * 4 <= (1 << 20):
            tr = cand
            break

    def body(p_ref, w_ref, m_ref, v_ref, g_ref, d_ref, nm_ref, nv_ref):
        g = p_ref[0].astype(f32)
        for s in range(1, N_DEV):
            g = g + p_ref[s].astype(f32)
        nm = ADAM_B1 * m_ref[...] + (1.0 - ADAM_B1) * g
        nv = ADAM_B2 * v_ref[...] + (1.0 - ADAM_B2) * (g * g)
        m_hat = nm / (1.0 - ADAM_B1 ** ADAM_STEP)
        v_hat = nv / (1.0 - ADAM_B2 ** ADAM_STEP)
        g_ref[...] = g
        d_ref[...] = -ADAM_LR * (m_hat / (jnp.sqrt(v_hat) + ADAM_EPS) + ADAM_WD * w_ref[...])
        nm_ref[...] = nm
        nv_ref[...] = nv

    blk = pl.BlockSpec((tr, cols), lambda i: (i, 0))
    return pl.pallas_call(
        body, grid=(rows // tr,),
        in_specs=[pl.BlockSpec((N_DEV, tr, cols), lambda i: (0, i, 0)), blk, blk, blk],
        out_specs=[blk] * 4, out_shape=[jax.ShapeDtypeStruct((rows, cols), f32)] * 4,
        compiler_params=_params(("arbitrary",)), name=name)(parts, w, m, v)


WEIGHTS = ['l0_norm_mix', 'l0_w_in', 'ssd_conv_w', 'ssd_conv_b', 'ssd_dt_bias', 'ssd_A_log', 'ssd_D', 'ssd_norm_w',
           'l0_w_out', 'l0_norm_mlp', 'l0_w_up', 'l0_w_down', 'l1_norm_mix', 'l1_w_in', 'gdn_conv_w', 'gdn_A_log',
           'gdn_dt_bias', 'gdn_norm_w', 's5_A_re', 's5_A_im', 's5_log_step', 's5_B_re', 's5_B_im', 's5_C_re', 's5_C_im',
           's5_D', 's5_w_glu', 's5_b_glu', 'l1_w_out', 'l1_norm_mlp', 'l1_w_up', 'l1_w_down', 'final_norm']
SHARDED = ['l0_w_in', 'l0_w_out', 'l0_w_up', 'l0_w_down', 'l1_w_in', 's5_w_glu', 'l1_w_out', 'l1_w_up', 'l1_w_down',
           'ssd_conv_w', 'gdn_conv_w']
F32_GATHER = ('ssd_conv_w', 'gdn_conv_w')
REPLICATED = [n for n in WEIGHTS if n not in SHARDED]
INPUTS = ['x'] + WEIGHTS + ['loss_target'] + ['m_' + n for n in WEIGHTS] + ['v_' + n for n in WEIGHTS]


def _row(v):
    return v.reshape(1, -1)


def _pad_lanes(v, offset=0):
    return jnp.pad(v, (offset, LANES - offset - v.shape[0])).reshape(1, LANES)


def _cols_to_blocks(g):
    return g.reshape(g.shape[0], N_DEV, -1).transpose(1, 0, 2)


def _blocks_to_cols(g):
    return g.transpose(1, 0, 2).reshape(g.shape[1], -1)


def _pack(arrays):
    parts, slots, at = [], [], 0
    for a in arrays:
        n = a.size
        rows = -(-n // (8 * LANES)) * 8
        parts.append(jnp.pad(a.reshape(-1), (0, rows * LANES - n)).reshape(rows, LANES))
        slots.append((at, rows, n, a.shape))
        at += rows
    return jnp.concatenate(parts, axis=0), slots


def _unpack(buf, slots):
    return [buf[at:at + rows].reshape(-1)[:n].reshape(shape) for at, rows, n, shape in slots]


def kernel(*args):
    a = dict(zip(INPUTS, args, strict=True))
    seq_len = a['x'].shape[1]
    x0 = a['x'].reshape(seq_len, D_MODEL)
    target = a['loss_target'].reshape(seq_len, D_MODEL)

    shard = {n: a[n] if n in F32_GATHER else a[n].astype(_MXU) for n in SHARDED}
    first = ['l0_w_in', 'ssd_conv_w', 'gdn_conv_w']
    g = dict(zip(first, Exchange([shard[n] for n in first], gather=True).run("gather_first")))
    w_nat = _blocks_to_cols(g['l0_w_in'])
    win0 = jnp.concatenate([w_nat[:, :2048], w_nat[:, 2560:3584], w_nat[:, 2048:2560], w_nat[:, 3584:3592],
                            jnp.zeros((D_MODEL, IN0_PAD - IN0_W), _MXU)], axis=1)
    ssd_cw, gdn_cw = _blocks_to_cols(g['ssd_conv_w']), _blocks_to_cols(g['gdn_conv_w'])

    half = RET_D // 2
    inv = ROPE_THETA ** (-jnp.arange(half, dtype=f32) / half)
    ang = jnp.arange(seq_len, dtype=f32)[:, None] * inv[None, :]
    cos, sin = jnp.cos(ang), jnp.sin(ang)
    cos, sin = jnp.concatenate([cos, cos], axis=1), jnp.concatenate([-sin, sin], axis=1)
    ssd_params = [ssd_cw, _row(a['ssd_conv_b']), _pad_lanes(a['ssd_dt_bias']), _pad_lanes(a['ssd_A_log']),
                  _pad_lanes(a['ssd_D']), _row(a['ssd_norm_w'])]
    gdn_params = [gdn_cw, _pad_lanes(a['gdn_A_log'], GDN_GCOL), _pad_lanes(a['gdn_dt_bias'], GDN_GCOL), _row(a['gdn_norm_w'])]
    s5_raw = [a['s5_A_re'].reshape(1, S5_W), a['s5_A_im'].reshape(1, S5_W), _pad_lanes(a['s5_log_step']),
              a['s5_B_re'].transpose(2, 0, 1).reshape(S5_GROUP, S5_W), a['s5_B_im'].transpose(2, 0, 1).reshape(S5_GROUP, S5_W),
              a['s5_C_re'].transpose(1, 0, 2).reshape(S5_GROUP, S5_W), a['s5_C_im'].transpose(1, 0, 2).reshape(S5_GROUP, S5_W)]
    s5_d, s5_bg = _row(a['s5_D']), _row(a['s5_b_glu'])
    nw = {n: _row(a[n]) for n in ('l0_norm_mix', 'l0_norm_mlp', 'l1_norm_mix', 'l1_norm_mlp', 'final_norm')}

    proj0 = inproj_fwd("l0_in", x0, nw['l0_norm_mix'], win0)
    ret_seqs = [Seq(proj0, 512, 0), Seq(proj0, 512, 1), Seq(proj0, 512, 2), Seq(proj0, 512, 3),
                Seq(cos, LANES, 0, "const"), Seq(sin, LANES, 0, "const")]
    later = ['l0_w_out', 'l0_w_up']
    ret_out, ret_st, got = mixer_fwd("ret_fwd", ret_chunk, ret_seqs, [], 512, RET_HEADS * RET_D, seq_len,
                                     exchange=Exchange([shard[n] for n in later], gather=True))
    g.update(zip(later, got))
    wout0 = g['l0_w_out'].reshape(D_MODEL, D_MODEL)
    ssd_seqs = [Seq(proj0, 512, 6), Seq(proj0, 1024, 2, "halo"), Seq(proj0, LANES, 28)]
    later = ['l0_w_down', 's5_w_glu']
    ssd_out, ssd_st, got = mixer_fwd("ssd_fwd", ssd_chunk, ssd_seqs, ssd_params, SSD_INNER, SSD_INNER, seq_len,
                                     exchange=Exchange([shard[n] for n in later], gather=True))
    g.update(zip(later, got))
    wglu = g['s5_w_glu'].reshape(S5_CH, S5_CH)
    x1 = outproj_fwd("l0_out", x0, ret_out, ssd_out, wout0)
    later = ['l1_w_in', 'l1_w_out', 'l1_w_up', 'l1_w_down']
    x2, relu0, got = mlp_fwd("l0_mlp", x1, nw['l0_norm_mlp'], g['l0_w_up'], g['l0_w_down'],
                      exchange=Exchange([shard[n] for n in later], gather=True))
    g.update(zip(later, got))
    w_nat = g['l1_w_in'].reshape(D_MODEL, IN1_W)
    win1 = jnp.concatenate([w_nat[:, :3072], w_nat[:, 3084:3340], w_nat[:, 3072:3084],
                            jnp.zeros((D_MODEL, IN1_PAD - IN1_W), _MXU)], axis=1)
    wout1 = g['l1_w_out'].reshape(D_MODEL, D_MODEL)
    proj1 = inproj_fwd("l1_in", x2, nw['l1_norm_mix'], win1)
    gdn_seqs = [Seq(proj1, 3 * GDN_W, 0, "halo"), Seq(proj1, GDN_W, 3), Seq(proj1, LANES, 26)]
    gdn_out, gdn_st, gdn_inv, _ = mixer_fwd("gdn_fwd", functools.partial(gdn_chunk, keep_inverses=True), gdn_seqs, gdn_params,
                                            GDN_W, GDN_W, seq_len, kept_shape=(GDN_HEADS, CHUNK, CHUNK))
    prep = s5_prep_fwd("s5_prep", s5_raw)
    s5_out, h_re, h_im = s5_fwd("s5_fwd", proj1, 12, *prep, s5_d, wglu, s5_bg)
    x3 = outproj_fwd("l1_out", x2, gdn_out, s5_out, wout1)
    x4, relu1, _ = mlp_fwd("l1_mlp", x3, nw['l1_norm_mlp'], g['l1_w_up'], g['l1_w_down'])
    loss_blk, dx4, d_final = final_loss("final_loss", x4, nw['final_norm'], target)

    parts = {}
    dx3, d_up1, d_down1, d_nmlp1, _ = mlp_bwd("l1_mlp_bwd", x3, nw['l1_norm_mlp'], g['l1_w_up'], g['l1_w_down'], relu1, dx4)
    d_gdn, d_s5, d_wout1 = outproj_bwd("l1_out_bwd", dx3, gdn_out, s5_out, wout1)
    d_u, s5_g = s5_bwd("s5_bwd", proj1, 12, h_re, h_im, d_s5, *prep, s5_d, wglu, s5_bg)
    s5_raw_g = s5_prep_bwd("s5_prep_bwd", s5_raw, s5_g[:6])
    ready = {'l1_w_up': d_up1, 'l1_w_down': d_down1, 'l1_w_out': d_wout1.reshape(N_DEV, -1, D_MODEL),
             's5_w_glu': s5_g[7].reshape(N_DEV, -1, S5_CH)}
    (d_qkv, d_z1, d_ba), gdn_pg, got = mixer_bwd("gdn_bwd", gdn_chunk, gdn_seqs, gdn_params, gdn_st, d_gdn, seq_len,
                                                 exchange=Exchange(list(ready.values()), gather=False), kept=gdn_inv)
    parts.update(zip(ready, got))
    dx2, d_win1, d_nmix1 = inproj_bwd("l1_in_bwd", x2, nw['l1_norm_mix'], win1, [d_qkv, d_z1, d_u, d_ba], dx3)
    d_win1 = jnp.concatenate([d_win1[:, :3072], d_win1[:, 3328:3340], d_win1[:, 3072:3328]], axis=1)
    ready = {'l1_w_in': d_win1.reshape(N_DEV, -1, IN1_W), 'gdn_conv_w': _cols_to_blocks(gdn_pg[0])}
    dx1, d_up0, d_down0, d_nmlp0, got = mlp_bwd("l0_mlp_bwd", x1, nw['l0_norm_mlp'], g['l0_w_up'], g['l0_w_down'], relu0, dx2,
                                                exchange=Exchange(list(ready.values()), gather=False))
    parts.update(zip(ready, got))
    d_ret, d_ssd, d_wout0 = outproj_bwd("l0_out_bwd", dx1, ret_out, ssd_out, wout0)
    ready = {'l0_w_up': d_up0}
    d_qkvg, _, got = mixer_bwd("ret_bwd", ret_chunk, ret_seqs, [], ret_st, d_ret, seq_len,
                               exchange=Exchange(list(ready.values()), gather=False))
    parts.update(zip(ready, got))
    ready = {'l0_w_down': d_down0, 'l0_w_out': d_wout0.reshape(N_DEV, -1, D_MODEL)}
    (d_z0, d_xbc, d_dt), ssd_pg, got = mixer_bwd("ssd_bwd", ssd_chunk, ssd_seqs, ssd_params, ssd_st, d_ssd, seq_len,
                                                 exchange=Exchange(list(ready.values()), gather=False))
    parts.update(zip(ready, got))
    dx0, d_win0, d_nmix0 = inproj_bwd("l0_in_bwd", x0, nw['l0_norm_mix'], win0, list(d_qkvg) + [d_xbc, d_z0, d_dt], dx1)

    d_win0 = jnp.concatenate([d_win0[:, :2048], d_win0[:, 3072:3584], d_win0[:, 2048:3072], d_win0[:, 3584:3592]], axis=1)
    ready = {'l0_w_in': _cols_to_blocks(d_win0).astype(jnp.bfloat16), 'ssd_conv_w': _cols_to_blocks(ssd_pg[0])}
    from_b = lambda t: t.reshape(S5_GROUP, S5_GROUPS, S5_STATE).transpose(1, 2, 0)
    from_c = lambda t: t.reshape(S5_GROUP, S5_GROUPS, S5_STATE).transpose(1, 0, 2)
    replicated_g = {
        'l0_norm_mix': d_nmix0, 'ssd_conv_b': ssd_pg[1], 'ssd_dt_bias': ssd_pg[2][0, :SSD_HEADS], 'ssd_A_log': ssd_pg[3][0, :SSD_HEADS],
        'ssd_D': ssd_pg[4][0, :SSD_HEADS], 'ssd_norm_w': ssd_pg[5], 'l0_norm_mlp': d_nmlp0, 'l1_norm_mix': d_nmix1,
        'gdn_A_log': gdn_pg[1][0, GDN_GCOL:GDN_GCOL + GDN_HEADS], 'gdn_dt_bias': gdn_pg[2][0, GDN_GCOL:GDN_GCOL + GDN_HEADS],
        'gdn_norm_w': gdn_pg[3], 's5_A_re': s5_raw_g[0], 's5_A_im': s5_raw_g[1], 's5_log_step': s5_raw_g[2][0, :S5_GROUPS],
        's5_B_re': from_b(s5_raw_g[3]), 's5_B_im': from_b(s5_raw_g[4]), 's5_C_re': from_c(s5_raw_g[5]), 's5_C_im': from_c(s5_raw_g[6]),
        's5_D': s5_g[6], 's5_b_glu': s5_g[8], 'l1_norm_mlp': d_nmlp1, 'final_norm': d_final}
    replicated_g = {n: replicated_g[n].reshape(a[n].shape) for n in REPLICATED}

    parts.update(zip(ready, Exchange(list(ready.values()), gather=False).run("scatter_last")))
    packed_g, slots = _pack([replicated_g[n] for n in REPLICATED])
    (packed_parts,) = Exchange([packed_g], gather=True).run("gather_small_grads")
    results = {}
    for n in SHARDED:
        results[n] = adamw("adamw_" + n, parts[n], a[n], a['m_' + n], a['v_' + n])
    packed = [_pack([a[pre + n] for n in REPLICATED])[0] for pre in ('', 'm_', 'v_')]
    small = [_unpack(t, slots) for t in adamw("adamw_small", packed_parts, *packed)]
    for i, n in enumerate(REPLICATED):
        results[n] = tuple(small[k][i] for k in range(4))

    loss = lax.psum(loss_blk[0, 0], ("x", "y", "c"))
    grad_x = dx0.reshape(a['x'].shape)
    return (loss, grad_x, *[results[n][0] for n in WEIGHTS], *[results[n][1] for n in WEIGHTS],
            *[results[n][2] for n in WEIGHTS], *[results[n][3] for n in WEIGHTS])
```

```python
import functools
import math

import numpy as np
import jax
import jax.numpy as jnp
from jax import lax
from jax.experimental import pallas as pl
from jax.experimental.pallas import tpu as pltpu

f32 = jnp.float32
_MXU = jnp.bfloat16
HI = lax.Precision.HIGHEST

D_MODEL = 1024
CHUNK = 64
EPS = 1e-6
N_DEV = 8
LANES = 128
HALO = 8
CONV_WIDTH = 4

RET_HEADS, RET_D = 4, 128
SSD_HEADS, SSD_P, SSD_N, SSD_GROUPS = 8, 64, 128, 2
SSD_INNER = SSD_HEADS * SSD_P
GDN_HEADS, GDN_D = 6, 128
GDN_W = GDN_HEADS * GDN_D
S5_CH, S5_GROUP, S5_GROUPS, S5_STATE = 256, 16, 16, 64
S5_W = S5_GROUPS * S5_STATE
D_FF = 4096
ROPE_THETA = 10000.0

IN0_W = 3592
IN0_PAD = 3712
IN1_W = 3340
IN1_PAD = 3456

ADAM_LR, ADAM_B1, ADAM_B2, ADAM_EPS, ADAM_WD, ADAM_STEP = 0.001, 0.9, 0.999, 1e-08, 0.01, 10

VMEM_LIMIT = 56 * 1024 * 1024


def _dot(a, b, dims):
    return lax.dot_general(a.astype(_MXU), b.astype(_MXU), (dims, ((), ())), preferred_element_type=f32)


@jax.custom_vjp
def mm(a, b):
    return _dot(a, b, ((1,), (0,)))


@jax.custom_vjp
def mm_nt(a, b):
    return _dot(a, b, ((1,), (1,)))


@jax.custom_vjp
def mm_tn(a, b):
    return _dot(a, b, ((0,), (0,)))


mm.defvjp(lambda a, b: (mm(a, b), (a, b)), lambda r, g: (mm_nt(g, r[1]), mm_tn(r[0], g)))
mm_nt.defvjp(lambda a, b: (mm_nt(a, b), (a, b)), lambda r, g: (mm(g, r[1]), mm_tn(g, r[0])))
mm_tn.defvjp(lambda a, b: (mm_tn(a, b), (a, b)), lambda r, g: (mm_nt(r[1], g), mm(r[0], g)))


def mmh(a, b):
    return jnp.dot(a, b, precision=HI, preferred_element_type=f32)


def _exact01(x, m01, dims, m_first):
    hi = x.astype(jnp.bfloat16)
    r = x - hi.astype(f32)
    mid = r.astype(jnp.bfloat16)
    lo = (r - mid.astype(f32)).astype(jnp.bfloat16)
    m = m01.astype(jnp.bfloat16)
    dot = lambda p: lax.dot_general(m, p, (dims, ((), ())), preferred_element_type=f32) if m_first else \
        lax.dot_general(p, m, (dims, ((), ())), preferred_element_type=f32)
    return dot(hi) + dot(mid) + dot(lo)


@jax.custom_vjp
def spread01(x, sel):
    return _exact01(x, sel, ((1,), (0,)), False)


spread01.defvjp(lambda x, sel: (spread01(x, sel), sel),
                lambda sel, g: (_exact01(g, sel, ((1,), (1,)), False), jnp.zeros_like(sel)))


@jax.custom_vjp
def cumsum01(tril, x):
    return _exact01(x, tril, ((1,), (0,)), True)


cumsum01.defvjp(lambda tril, x: (cumsum01(tril, x), tril),
                lambda tril, g: (jnp.zeros_like(tril), _exact01(g, tril, ((0,), (0,)), True)))


def _roll(x, shift, axis):
    return pltpu.roll(x, shift, axis)


@functools.partial(jax.custom_vjp, nondiff_argnums=(1,))
def roll_rows(x, s):
    return _roll(x, s, 0) if s else x


roll_rows.defvjp(lambda x, s: (roll_rows(x, s), None),
                 lambda s, _, g: ((_roll(g, g.shape[0] - s, 0) if s else g),))


@jax.custom_vjp
def roll_half(x):
    return _roll(x, x.shape[-1] // 2, 1)


roll_half.defvjp(lambda x: (roll_half(x), None), lambda _, g: (roll_half(g),))


def _iota(shape, axis):
    return lax.broadcasted_iota(jnp.int32, shape, axis)


def silu(x):
    return x * jax.nn.sigmoid(x)


def softplus(x):
    return jnp.maximum(x, 0.0) + jnp.log(1.0 + jnp.exp(-jnp.abs(x)))


def rmsnorm_f(x, w):
    return x * lax.rsqrt(jnp.mean(x * x, axis=-1, keepdims=True) + EPS) * w


def unit_rms(x):
    return x * lax.rsqrt(jnp.mean(x * x, axis=-1, keepdims=True) + EPS)


def _causal(n, strict=False):
    r, c = _iota((n, n), 0), _iota((n, n), 1)
    return (r > c) if strict else (r >= c)


def _tril_ones(n):
    return _causal(n).astype(f32)


def _conv_rows(xe, w):
    acc = w[CONV_WIDTH - 1:CONV_WIDTH, :] * xe
    for j in range(CONV_WIDTH - 1):
        acc = acc + w[j:j + 1, :] * roll_rows(xe, CONV_WIDTH - 1 - j)
    return acc[HALO:, :]


_RET_LOG_GAMMA = [float(np.log(np.float32(1.0) - np.float32(2.0) ** np.float32(-5.0 - h))) for h in range(RET_HEADS)]


def ret_chunk(q, k, v, gate, cos, sin, state):
    c = q.shape[0]
    idx = _iota((c, 1), 0).astype(f32)
    diff = (_iota((c, c), 0) - _iota((c, c), 1)).astype(f32)
    causal = _causal(c)
    hs = range(RET_HEADS)
    cols = [slice(h * RET_D, (h + 1) * RET_D) for h in hs]
    lg = _RET_LOG_GAMMA
    qh = [(q[:, s] * cos + roll_half(q[:, s]) * sin) * (RET_D ** -0.5) for s in cols]
    kh = [k[:, s] * cos + roll_half(k[:, s]) * sin for s in cols]
    vh = [v[:, s] for s in cols]
    sh = [state[s, :] for s in cols]
    scores = [mm_nt(qh[h], kh[h]) * jnp.exp(jnp.where(causal, lg[h] * diff, -jnp.inf)) for h in hs]
    inter = [mm(qh[h] * jnp.exp(lg[h] * (idx + 1.0)), sh[h]) for h in hs]
    y = [mm(scores[h], vh[h]) + inter[h] for h in hs]
    states = [sh[h] * math.exp(lg[h] * c) + mm_tn(kh[h] * jnp.exp(lg[h] * (c - 1.0 - idx)), vh[h]) for h in hs]
    outs = [unit_rms(y[h]) * silu(gate[:, cols[h]]) for h in hs]
    return jnp.concatenate(outs, axis=1), jnp.concatenate(states, axis=0)


def _head_select(n_heads, width):
    r, c = _iota((LANES, n_heads * width), 0), _iota((LANES, n_heads * width), 1)
    return (c // width == r).astype(f32)


def ssd_chunk(z, xe, dtr, state, conv_w, conv_b, dt_bias, a_log, d_skip, norm_w):
    c = z.shape[0]
    xbc = silu(_conv_rows(xe, conv_w) + conv_b)
    xs, bm, cm = xbc[:, :SSD_INNER], xbc[:, SSD_INNER:SSD_INNER + 256], xbc[:, SSD_INNER + 256:]
    sel = _head_select(SSD_HEADS, SSD_P)
    dt = softplus(dtr + dt_bias)
    la = dt * (-jnp.exp(a_log))
    la_cum = cumsum01(_tril_ones(c), la)
    la_cum_t = la_cum.T
    last = jnp.sum(la, axis=0, keepdims=True)
    xd = xs * spread01(dt, sel)
    la_x = spread01(la_cum, sel)
    last_x = spread01(last, sel)
    to_end = jnp.exp(last_x - la_x)
    from_start = jnp.exp(la_x)
    causal = _causal(c)
    left = (_iota((1, LANES), 1) < SSD_P).astype(f32)
    upper = _iota((LANES, 1), 0) < SSD_P
    pairs, heads = range(SSD_HEADS // 2), range(SSD_HEADS)
    bc = [bm[:, g * SSD_N:(g + 1) * SSD_N] for g in range(SSD_GROUPS)]
    cc = [cm[:, g * SSD_N:(g + 1) * SSD_N] for g in range(SSD_GROUPS)]
    cb = [mm_nt(cc[g], bc[g]) for g in range(SSD_GROUPS)]
    cols = [slice(p * LANES, (p + 1) * LANES) for p in pairs]
    xd_p = [xd[:, s] for s in cols]
    sp = [state[s, :] for s in cols]
    lmat = [jnp.exp(jnp.where(causal, la_cum[:, h:h + 1] - la_cum_t[h:h + 1, :], -jnp.inf)) for h in heads]
    off = [mm_nt(cc[p // 2], sp[p]) * from_start[:, cols[p]] for p in pairs]
    diag = [mm(cb[h // 4] * lmat[h], xd_p[h // 2] * (left if h % 2 == 0 else 1.0 - left)) for h in heads]
    cd = [jnp.where(upper, jnp.exp(last[:, 2 * p:2 * p + 1]), jnp.exp(last[:, 2 * p + 1:2 * p + 2])) for p in pairs]
    states = [sp[p] * cd[p] + mm_tn(xd_p[p] * to_end[:, cols[p]], bc[p // 2]) for p in pairs]
    ys = [off[p] + diag[2 * p] + diag[2 * p + 1] for p in pairs]
    y = jnp.concatenate(ys, axis=1) + spread01(d_skip, sel) * xs
    yg = y * silu(z)
    half = SSD_INNER // SSD_GROUPS
    out = jnp.concatenate([unit_rms(yg[:, i * half:(i + 1) * half]) for i in range(SSD_GROUPS)], axis=1) * norm_w
    return out, jnp.concatenate(states, axis=0)


def mm3(a, b):
    return jnp.dot(a, b, precision=lax.Precision.HIGH, preferred_element_type=f32)


@jax.custom_vjp
def _unit_lower_inverses(lowers):
    n = lowers[0].shape[0]
    eye = (_iota((n, n), 0) == _iota((n, n), 1)).astype(f32)
    a = [-l for l in lowers]
    p = [eye + x for x in a]
    k = 2
    while k < n:
        a = [mm3(x, x) for x in a]
        p = [y + mm3(y, x) for y, x in zip(p, a)]
        k *= 2
    return p


def _unit_lower_inverses_bwd(t_inv, g):
    dims_tn, dims_nt = (((0,), (0,)), ((), ())), (((1,), (1,)), ((), ()))
    x = [lax.dot_general(t, gi, dims_tn, precision=lax.Precision.HIGH, preferred_element_type=f32) for t, gi in zip(t_inv, g)]
    return ([-lax.dot_general(xi, t, dims_nt, precision=lax.Precision.HIGH, preferred_element_type=f32) for xi, t in zip(x, t_inv)],)


def _unit_lower_inverses_fwd(lowers):
    t_inv = _unit_lower_inverses(lowers)
    return t_inv, t_inv


_unit_lower_inverses.defvjp(_unit_lower_inverses_fwd, _unit_lower_inverses_bwd)


@jax.custom_vjp
def _known_inverses(lowers, t_inv):
    return t_inv


_known_inverses.defvjp(lambda lowers, t_inv: (t_inv, t_inv),
                       lambda t_inv, g: (_unit_lower_inverses_bwd(t_inv, g)[0], [jnp.zeros_like(t) for t in t_inv]))


GDN_GCOL = 6


def gdn_chunk(xe, z, ba, state, conv_w, a_log, dt_bias, norm_w, kept_inverses=None, keep_inverses=False):
    c = z.shape[0]
    qkv = silu(_conv_rows(xe, conv_w))
    beta_all = jax.nn.sigmoid(ba)
    g_all = -jnp.exp(a_log) * softplus(ba + dt_bias)
    gc = cumsum01(_tril_ones(c), g_all)
    gc_t = gc.T
    last = jnp.sum(g_all, axis=0, keepdims=True)
    causal, strict = _causal(c), _causal(c, strict=True)
    hs = range(GDN_HEADS)
    cols = [slice(h * GDN_D, (h + 1) * GDN_D) for h in hs]
    qh = [qkv[:, h * GDN_D:(h + 1) * GDN_D] for h in hs]
    kh = [qkv[:, GDN_W + h * GDN_D:GDN_W + (h + 1) * GDN_D] for h in hs]
    vh = [qkv[:, 2 * GDN_W + h * GDN_D:2 * GDN_W + (h + 1) * GDN_D] for h in hs]
    qh = [t * lax.rsqrt(jnp.sum(t * t, axis=-1, keepdims=True) + EPS) * (GDN_D ** -0.5) for t in qh]
    kh = [t * lax.rsqrt(jnp.sum(t * t, axis=-1, keepdims=True) + EPS) for t in kh]
    beta = [beta_all[:, h:h + 1] for h in hs]
    col = [gc[:, GDN_GCOL + h:GDN_GCOL + h + 1] for h in hs]
    row = [gc_t[GDN_GCOL + h:GDN_GCOL + h + 1, :] for h in hs]
    lst = [last[:, GDN_GCOL + h:GDN_GCOL + h + 1] for h in hs]
    decay = [jnp.exp(jnp.where(causal, col[h] - row[h], -jnp.inf)) for h in hs]
    e_col = [jnp.exp(t) for t in col]
    kb = [kh[h] * beta[h] for h in hs]
    vb = [vh[h] * beta[h] for h in hs]
    kk = [mm_nt(kb[h], kh[h]) for h in hs]
    qk = [mm_nt(qh[h], kh[h]) for h in hs]
    lowers = [jnp.where(strict, kk[h] * decay[h], 0.0) for h in hs]
    t_inv = _unit_lower_inverses(lowers) if kept_inverses is None else _known_inverses(lowers, list(kept_inverses))
    u = [mm(t_inv[h], vb[h]) for h in hs]
    w = [mm(t_inv[h], kb[h] * e_col[h]) for h in hs]
    attn = [jnp.where(causal, qk[h] * decay[h], 0.0) for h in hs]
    sh = [state[s, :] for s in cols]
    ws = [mm(w[h], sh[h]) for h in hs]
    qs = [mm(qh[h] * e_col[h], sh[h]) for h in hs]
    v_new = [u[h] - ws[h] for h in hs]
    o = [qs[h] + mm(attn[h], v_new[h]) for h in hs]
    states = [sh[h] * jnp.exp(lst[h]) + mm_tn(kh[h] * jnp.exp(lst[h] - col[h]), v_new[h]) for h in hs]
    outs = [unit_rms(o[h]) * norm_w * silu(z[:, cols[h]]) for h in hs]
    results = (jnp.concatenate(outs, axis=1), jnp.concatenate(states, axis=0))
    return results + (list(t_inv),) if keep_inverses else results


def _s5_group_mask():
    r, c = _iota((S5_CH, S5_W), 0), _iota((S5_CH, S5_W), 1)
    return (r // S5_GROUP == c // S5_STATE).astype(f32)


def s5_prep(a_re, a_im, log_step, b_re, b_im, c_re, c_im):
    r, c = _iota((LANES, S5_W), 0), _iota((LANES, S5_W), 1)
    step = jnp.exp(mmh(log_step, (c // S5_STATE == r).astype(f32)))
    zr, zi = a_re * step, a_im * step
    e = jnp.exp(zr)
    lr, li = e * jnp.cos(zi), e * jnp.sin(zi)
    den = a_re * a_re + a_im * a_im
    xr, xi = lr - 1.0, li
    cr, ci = (xr * a_re + xi * a_im) / den, (xi * a_re - xr * a_im) / den
    bbr, bbi = cr * b_re - ci * b_im, cr * b_im + ci * b_re
    mask = _s5_group_mask()
    tile = lambda t: jnp.tile(t, (S5_GROUPS, 1)) * mask
    return lr, li, tile(bbr), tile(bbi), tile(c_re), tile(c_im)


def s5_out_fn(h_re, h_im, u, cblk_re, cblk_im, d_skip, w_glu, b_glu):
    y = mm_nt(h_re, cblk_re) - mm_nt(h_im, cblk_im) + d_skip * u
    y = jax.nn.gelu(y)
    return y * jax.nn.sigmoid(mm(y, w_glu) + b_glu)


def _params(sem, **kw):
    return pltpu.CompilerParams(dimension_semantics=sem, vmem_limit_bytes=VMEM_LIMIT, **kw)


def _const_spec(shape):
    return pl.BlockSpec(shape, lambda i: (0,) * len(shape))


def _resident_spec(shape):
    return pl.BlockSpec(shape, lambda i: (0,) * len(shape), pipeline_mode=pl.Buffered(1))


def _row_spec(rows, cols, col_block=0):
    return pl.BlockSpec((rows, cols), lambda i: (i, col_block))


class Seq:
    def __init__(self, array, width, col_block, kind="tile"):
        self.array, self.width, self.col_block, self.kind = array, width, col_block, kind


CHUNKS_PER_STEP = 4
LIGHT_CHUNKS_PER_STEP = 8


def mixer_fwd(name, fn, seqs, params, out_width, state_rows, seq_len, per_step=CHUNKS_PER_STEP, exchange=None, kept_shape=None):
    nc = seq_len // CHUNK
    rows, steps = per_step * CHUNK, nc // per_step
    n_refs = sum(2 if s.kind == "halo" else 1 for s in seqs)
    n_par = len(params)
    n_own = 3 if kept_shape else 2
    car = Carried(exchange)

    def body(*refs):
        seq_refs, par_refs = refs[:n_refs], refs[n_refs:n_refs + n_par]
        k0 = n_refs + n_par
        ex_ins, own = refs[k0:k0 + car.n], refs[k0 + car.n:k0 + car.n + n_own]
        out_ref, st_ref = own[:2]
        k0 += car.n + n_own
        ex_outs, state, ex_sems = refs[k0:k0 + car.n], refs[k0 + car.n], refs[k0 + car.n + 1:]
        c = pl.program_id(0)
        car.start_at(c == 0, ex_ins, ex_outs, ex_sems)

        @pl.when(c == 0)
        def _():
            state[...] = jnp.zeros_like(state)

        par_vals = [p[...] for p in par_refs]
        s_cur = state[...]
        for kk in range(per_step):
            lo = kk * CHUNK
            vals, k = [], 0
            for s in seqs:
                if s.kind == "halo":
                    prev = jnp.where(c > 0, seq_refs[k][...], 0.0) if kk == 0 else seq_refs[k + 1][lo - HALO:lo, :]
                    vals.append(jnp.concatenate([prev, seq_refs[k + 1][lo:lo + CHUNK, :]], axis=0))
                    k += 2
                else:
                    vals.append(seq_refs[k][lo:lo + CHUNK, :])
                    k += 1
            st_ref[kk] = s_cur
            res = fn(*vals, s_cur, *par_vals)
            out_ref[lo:lo + CHUNK, :] = res[0]
            s_cur = res[1]
            if kept_shape:
                for h, t in enumerate(res[2]):
                    own[2][kk, h] = t
        state[...] = s_cur
        car.wait_at(c == steps - 1, ex_ins, ex_outs, ex_sems)

    in_specs, operands = [], []
    for s in seqs:
        if s.kind == "halo":
            rb, w, cb = rows // HALO, s.width, s.col_block
            in_specs.append(pl.BlockSpec((HALO, w), lambda i, rb=rb, cb=cb: (jnp.maximum(i * rb - 1, 0), cb)))
            operands.append(s.array)
        in_specs.append(pl.BlockSpec((rows, s.width), lambda i, cb=s.col_block: (i, cb)))
        operands.append(s.array)
    for p in params:
        in_specs.append(_const_spec(p.shape))
        operands.append(p)
    outs = pl.pallas_call(
        body, grid=(steps,), in_specs=in_specs + car.in_specs,
        out_specs=[pl.BlockSpec((rows, out_width), lambda i: (i, 0)),
                   pl.BlockSpec((per_step, state_rows, LANES), lambda i: (i, 0, 0))]
        + ([pl.BlockSpec((per_step,) + tuple(kept_shape), lambda i: (i, 0, 0, 0))] if kept_shape else []) + car.out_specs,
        out_shape=[jax.ShapeDtypeStruct((seq_len, out_width), f32),
                   jax.ShapeDtypeStruct((nc, state_rows, LANES), f32)]
        + ([jax.ShapeDtypeStruct((nc,) + tuple(kept_shape), f32)] if kept_shape else []) + car.out_shape,
        scratch_shapes=[pltpu.VMEM((state_rows, LANES), f32)] + car.scratch,
        compiler_params=_params(("arbitrary",)), name=name)(*operands, *car.operands)
    return tuple(outs[:n_own]) + (outs[n_own:],)


def mixer_bwd(name, fn, seqs, params, states, d_out, seq_len, per_step=CHUNKS_PER_STEP, exchange=None, kept=None):
    nc = seq_len // CHUNK
    rows, steps = per_step * CHUNK, nc // per_step
    state_rows = states.shape[1]
    diff = [s for s in seqs if s.kind != "const"]
    halos = [s for s in diff if s.kind == "halo"]
    n_refs = sum(2 if s.kind == "halo" else 1 for s in seqs)
    n_par = len(params)
    n_kept = 0 if kept is None else 1
    car = Carried(exchange)

    def body(*refs):
        seq_refs, par_refs = refs[:n_refs], refs[n_refs:n_refs + n_par]
        st_ref, dout_ref = refs[n_refs + n_par:n_refs + n_par + 2]
        k0 = n_refs + n_par + 2
        kept_ref = refs[k0] if n_kept else None
        k0 += n_kept
        ex_ins = refs[k0:k0 + car.n]
        k0 += car.n
        dseq_refs, dpar_refs = refs[k0:k0 + len(diff)], refs[k0 + len(diff):k0 + len(diff) + n_par]
        k0 += len(diff) + n_par
        ex_outs = refs[k0:k0 + car.n]
        scratch = refs[k0 + car.n:]
        d_state, carries, ex_sems = scratch[0], scratch[1:1 + len(halos)], scratch[1 + len(halos):]
        i = pl.program_id(0)
        step = steps - 1 - i
        car.start_at(i == 0, ex_ins, ex_outs, ex_sems)

        @pl.when(i == 0)
        def _():
            d_state[...] = jnp.zeros_like(d_state)
            for r in list(carries) + list(dpar_refs):
                r[...] = jnp.zeros_like(r)

        par_vals = [p[...] for p in par_refs]
        d_s = d_state[...]
        d_par = [None] * n_par
        halo_ct = [r[...] for r in carries]
        for kk in range(per_step - 1, -1, -1):
            lo = kk * CHUNK
            dvals, consts, k = [], [], 0
            for s in seqs:
                if s.kind == "halo":
                    prev = jnp.where(step > 0, seq_refs[k][...], 0.0) if kk == 0 else seq_refs[k + 1][lo - HALO:lo, :]
                    dvals.append(jnp.concatenate([prev, seq_refs[k + 1][lo:lo + CHUNK, :]], axis=0))
                    k += 2
                elif s.kind == "tile":
                    dvals.append(seq_refs[k][lo:lo + CHUNK, :])
                    k += 1
                else:
                    consts.append(seq_refs[k][lo:lo + CHUNK, :])
                    k += 1
            nd = len(dvals)

            extra = [[kept_ref[kk, h] for h in range(kept.shape[1])]] if n_kept else []

            def call(*a, consts=consts, nd=nd, extra=extra):
                it_d, it_c = iter(a[:nd]), iter(consts)
                vals = [next(it_c) if s.kind == "const" else next(it_d) for s in seqs]
                return fn(*vals, *a[nd:], *extra)

            _, vjp = jax.vjp(call, *dvals, st_ref[kk], *par_vals)
            cts = vjp((dout_ref[lo:lo + CHUNK, :], d_s))
            hk = 0
            for j, s in enumerate(diff):
                if s.kind == "halo":
                    dseq_refs[j][lo:lo + CHUNK, :] = cts[j][HALO:, :]
                    dseq_refs[j][lo + CHUNK - HALO:lo + CHUNK, :] += halo_ct[hk]
                    halo_ct[hk] = cts[j][:HALO, :]
                    hk += 1
                else:
                    dseq_refs[j][lo:lo + CHUNK, :] = cts[j]
            d_s = cts[nd]
            for j in range(n_par):
                d_par[j] = cts[nd + 1 + j] if d_par[j] is None else d_par[j] + cts[nd + 1 + j]
        d_state[...] = d_s
        for r, v in zip(carries, halo_ct):
            r[...] = v
        for j in range(n_par):
            dpar_refs[j][...] += d_par[j]
        car.wait_at(i == steps - 1, ex_ins, ex_outs, ex_sems)

    in_specs, operands = [], []
    for s in seqs:
        if s.kind == "halo":
            rb, cb = rows // HALO, s.col_block
            in_specs.append(pl.BlockSpec((HALO, s.width), lambda i, rb=rb, cb=cb: (jnp.maximum((steps - 1 - i) * rb - 1, 0), cb)))
            operands.append(s.array)
        in_specs.append(pl.BlockSpec((rows, s.width), lambda i, cb=s.col_block: (steps - 1 - i, cb)))
        operands.append(s.array)
    for p in params:
        in_specs.append(_const_spec(p.shape))
        operands.append(p)
    in_specs.append(pl.BlockSpec((per_step, state_rows, LANES), lambda i: (steps - 1 - i, 0, 0)))
    in_specs.append(pl.BlockSpec((rows, d_out.shape[1]), lambda i: (steps - 1 - i, 0)))
    operands += [states, d_out]
    if n_kept:
        in_specs.append(pl.BlockSpec((per_step,) + kept.shape[1:], lambda i: (steps - 1 - i, 0, 0, 0)))
        operands.append(kept)
    outs = pl.pallas_call(
        body, grid=(steps,), in_specs=in_specs + car.in_specs,
        out_specs=[pl.BlockSpec((rows, s.width), lambda i: (steps - 1 - i, 0)) for s in diff]
        + [_const_spec(p.shape) for p in params] + car.out_specs,
        out_shape=[jax.ShapeDtypeStruct((seq_len, s.width), f32) for s in diff]
        + [jax.ShapeDtypeStruct(p.shape, f32) for p in params] + car.out_shape,
        scratch_shapes=[pltpu.VMEM((state_rows, LANES), f32)] + [pltpu.VMEM((HALO, s.width), f32) for s in halos]
        + car.scratch,
        compiler_params=_params(("arbitrary",)), name=name)(*operands, *car.operands)
    nd = len(diff)
    return outs[:nd], outs[nd:nd + n_par], outs[nd + n_par:]


TOK = 512


def inproj_fwd(name, x, nw, w):
    seq_len, n = x.shape[0], w.shape[1]

    def body(x_ref, nw_ref, w_ref, o_ref):
        o_ref[...] = mm(rmsnorm_f(x_ref[...], nw_ref[...]), w_ref[...])

    return pl.pallas_call(
        body, grid=(seq_len // TOK,),
        in_specs=[_row_spec(TOK, D_MODEL), _const_spec(nw.shape), _resident_spec(w.shape)],
        out_specs=_row_spec(TOK, n), out_shape=jax.ShapeDtypeStruct((seq_len, n), f32),
        compiler_params=_params(("arbitrary",)), name=name)(x, nw, w)


def inproj_bwd(name, x, nw, w, pieces, d_res):
    seq_len, n = x.shape[0], w.shape[1]
    widths = [p.shape[1] for p in pieces]
    assert sum(widths) == n
    k = len(pieces)

    def body(*refs):
        x_ref, nw_ref, w_ref = refs[:3]
        p_refs, dres_ref = refs[3:3 + k], refs[3 + k]
        dx_ref, dw_ref, dnw_ref = refs[4 + k:]

        @pl.when(pl.program_id(0) == 0)
        def _():
            dw_ref[...] = jnp.zeros_like(dw_ref)
            dnw_ref[...] = jnp.zeros_like(dnw_ref)

        h, vjp = jax.vjp(rmsnorm_f, x_ref[...], nw_ref[...])
        dh, off = jnp.zeros_like(h), 0
        for p_ref, wd in zip(p_refs, widths):
            g = p_ref[...]
            dh = dh + mm_nt(g, w_ref[:, off:off + wd])
            dw_ref[:, off:off + wd] += mm_tn(h, g)
            off += wd
        dx, dnw = vjp(dh)
        dx_ref[...] = dres_ref[...] + dx
        dnw_ref[...] += dnw

    return pl.pallas_call(
        body, grid=(seq_len // TOK,),
        in_specs=[_row_spec(TOK, D_MODEL), _const_spec(nw.shape), _resident_spec(w.shape)]
        + [_row_spec(TOK, wd) for wd in widths] + [_row_spec(TOK, D_MODEL)],
        out_specs=[_row_spec(TOK, D_MODEL), _resident_spec((D_MODEL, n)), _const_spec(nw.shape)],
        out_shape=[jax.ShapeDtypeStruct((seq_len, D_MODEL), f32), jax.ShapeDtypeStruct((D_MODEL, n), f32),
                   jax.ShapeDtypeStruct(nw.shape, f32)],
        compiler_params=_params(("arbitrary",)), name=name)(x, nw, w, *pieces, d_res)


def outproj_fwd(name, x, a, b, w):
    seq_len, wa, wb = x.shape[0], a.shape[1], b.shape[1]

    def body(x_ref, a_ref, b_ref, w_ref, o_ref):
        o_ref[...] = x_ref[...] + mm(a_ref[...], w_ref[:wa, :]) + mm(b_ref[...], w_ref[wa:, :])

    return pl.pallas_call(
        body, grid=(seq_len // TOK,),
        in_specs=[_row_spec(TOK, D_MODEL), _row_spec(TOK, wa), _row_spec(TOK, wb), _resident_spec(w.shape)],
        out_specs=_row_spec(TOK, D_MODEL), out_shape=jax.ShapeDtypeStruct((seq_len, D_MODEL), f32),
        compiler_params=_params(("arbitrary",)), name=name)(x, a, b, w)


def outproj_bwd(name, dy, a, b, w):
    seq_len, wa, wb = dy.shape[0], a.shape[1], b.shape[1]

    def body(dy_ref, a_ref, b_ref, w_ref, da_ref, db_ref, dw_ref):
        @pl.when(pl.program_id(0) == 0)
        def _():
            dw_ref[...] = jnp.zeros_like(dw_ref)

        g = dy_ref[...]
        da_ref[...] = mm_nt(g, w_ref[:wa, :])
        db_ref[...] = mm_nt(g, w_ref[wa:, :])
        dw_ref[:wa, :] += mm_tn(a_ref[...], g)
        dw_ref[wa:, :] += mm_tn(b_ref[...], g)

    return pl.pallas_call(
        body, grid=(seq_len // TOK,),
        in_specs=[_row_spec(TOK, D_MODEL), _row_spec(TOK, wa), _row_spec(TOK, wb), _resident_spec(w.shape)],
        out_specs=[_row_spec(TOK, wa), _row_spec(TOK, wb), _resident_spec(w.shape)],
        out_shape=[jax.ShapeDtypeStruct((seq_len, wa), f32), jax.ShapeDtypeStruct((seq_len, wb), f32),
                   jax.ShapeDtypeStruct(w.shape, f32)],
        compiler_params=_params(("arbitrary",)), name=name)(dy, a, b, w)


FF_BLOCK = D_FF // N_DEV


def mlp_fwd(name, x, nw, w_up, w_down, exchange=None):
    seq_len = x.shape[0]
    nt = seq_len // TOK
    car = Carried(exchange)

    def body(*refs):
        x_ref, nw_ref, up_ref, down_ref = refs[:4]
        ex_ins, (o_ref, relu_ref) = refs[4:4 + car.n], refs[4 + car.n:6 + car.n]
        ex_outs, ex_sems = refs[6 + car.n:6 + 2 * car.n], refs[6 + 2 * car.n:]
        i = pl.program_id(0)
        car.start_at(i == 0, ex_ins, ex_outs, ex_sems)
        xv = x_ref[...]
        h = rmsnorm_f(xv, nw_ref[...])
        acc = xv
        for d in range(N_DEV):
            r = jnp.maximum(mm(h, up_ref[d]), 0.0)
            relu_ref[d] = r.astype(_MXU)
            acc = acc + mm(r * r, down_ref[d])
        o_ref[...] = acc
        car.wait_at(i == nt - 1, ex_ins, ex_outs, ex_sems)

    outs = pl.pallas_call(
        body, grid=(nt,),
        in_specs=[_row_spec(TOK, D_MODEL), _const_spec(nw.shape), _resident_spec(w_up.shape), _resident_spec(w_down.shape)]
        + car.in_specs,
        out_specs=[_row_spec(TOK, D_MODEL), pl.BlockSpec((N_DEV, TOK, FF_BLOCK), lambda i: (0, i, 0))] + car.out_specs,
        out_shape=[jax.ShapeDtypeStruct((seq_len, D_MODEL), f32),
                   jax.ShapeDtypeStruct((N_DEV, seq_len, FF_BLOCK), _MXU)] + car.out_shape, scratch_shapes=car.scratch,
        compiler_params=_params(("arbitrary",)), name=name)(x, nw, w_up, w_down, *car.operands)
    return outs[0], outs[1], outs[2:]


MLP_SPLIT = 2


def mlp_bwd(name, x, nw, w_up, w_down, relu, dy, exchange=None):
    seq_len = x.shape[0]
    nt = seq_len // TOK
    per = N_DEV // MLP_SPLIT
    dh = d_up = d_down = ex_results = None
    for j in range(MLP_SPLIT):
        last, has_prev = j == MLP_SPLIT - 1, j > 0
        car = Carried(exchange if j == 0 else None)
        n_in = 6 + (3 if has_prev else 0)
        n_out = 4 if last else 3

        def body(*refs, last=last, has_prev=has_prev, car=car, n_in=n_in, n_out=n_out):
            x_ref, nw_ref, up_ref, down_ref, relu_ref, dy_ref = refs[:6]
            dh_prev_ref = refs[6] if has_prev else None
            ex_ins, own = refs[n_in:n_in + car.n], refs[n_in + car.n:n_in + car.n + n_out]
            ex_outs, ex_sems = refs[n_in + car.n + n_out:n_in + 2 * car.n + n_out], refs[n_in + 2 * car.n + n_out:]
            dup_ref, ddown_ref = own[-2:]
            i = pl.program_id(0)
            car.start_at(i == 0, ex_ins, ex_outs, ex_sems)

            @pl.when(i == 0)
            def _():
                dup_ref[...] = jnp.zeros_like(dup_ref)
                ddown_ref[...] = jnp.zeros_like(ddown_ref)
                if last:
                    own[1][...] = jnp.zeros_like(own[1])

            h, vjp = jax.vjp(rmsnorm_f, x_ref[...], nw_ref[...])
            g = dy_ref[...]
            dh_acc = dh_prev_ref[...] if has_prev else jnp.zeros_like(h)
            for d in range(per):
                r = relu_ref[d].astype(f32)
                da = mm_nt(g, down_ref[d]) * (2.0 * r)
                ddown_ref[d] += mm_tn(r * r, g)
                dup_ref[d] += mm_tn(h, da)
                dh_acc = dh_acc + mm_nt(da, up_ref[d])
            if last:
                dx, dnw = vjp(dh_acc)
                own[0][...] = g + dx
                own[1][...] += dnw
            else:
                own[0][...] = dh_acc
            car.wait_at(i == nt - 1, ex_ins, ex_outs, ex_sems)

        weights = lambda shape: pl.BlockSpec(shape, lambda i, j=j: (j, 0, 0), pipeline_mode=pl.Buffered(1))
        in_specs = [_row_spec(TOK, D_MODEL), _const_spec(nw.shape), weights((per, D_MODEL, FF_BLOCK)),
                    weights((per, FF_BLOCK, D_MODEL)), pl.BlockSpec((per, TOK, FF_BLOCK), lambda i, j=j: (j, i, 0)),
                    _row_spec(TOK, D_MODEL)]
        operands = [x, nw, w_up, w_down, relu, dy]
        if has_prev:
            in_specs += [_row_spec(TOK, D_MODEL), ANY, ANY]
            operands += [dh, d_up, d_down]
        first_outs = ([_row_spec(TOK, D_MODEL), _const_spec(nw.shape)] if last else [_row_spec(TOK, D_MODEL)])
        first_shapes = [jax.ShapeDtypeStruct((seq_len, D_MODEL), f32)] + ([jax.ShapeDtypeStruct(nw.shape, f32)] if last else [])
        outs = pl.pallas_call(
            body, grid=(nt,), in_specs=in_specs + car.in_specs,
            out_specs=first_outs + [weights((per, D_MODEL, FF_BLOCK)), weights((per, FF_BLOCK, D_MODEL))] + car.out_specs,
            out_shape=first_shapes + [jax.ShapeDtypeStruct(w_up.shape, f32), jax.ShapeDtypeStruct(w_down.shape, f32)]
            + car.out_shape,
            scratch_shapes=car.scratch,
            input_output_aliases=({7: n_out - 2, 8: n_out - 1} if has_prev else {}),
            compiler_params=_params(("arbitrary",)), name=name if j == 0 else f"{name}_{j}")(*operands, *car.operands)
        dh, d_up, d_down = outs[0], outs[n_out - 2], outs[n_out - 1]
        if j == 0:
            ex_results = outs[n_out:]
        if last:
            dx, dnw = outs[0], outs[1]
    return dx, d_up, d_down, dnw, ex_results


def final_loss(name, x, nw, target):
    seq_len = x.shape[0]

    def body(x_ref, nw_ref, t_ref, loss_ref, dx_ref, dnw_ref):
        @pl.when(pl.program_id(0) == 0)
        def _():
            loss_ref[...] = jnp.zeros_like(loss_ref)
            dnw_ref[...] = jnp.zeros_like(dnw_ref)

        y, vjp = jax.vjp(rmsnorm_f, x_ref[...], nw_ref[...])
        err = y - t_ref[...]
        loss_ref[...] += 0.5 * jnp.sum(jnp.mean(err * err, axis=-1, keepdims=True), axis=0, keepdims=True)
        dx, dnw = vjp(err * (1.0 / D_MODEL))
        dx_ref[...] = dx
        dnw_ref[...] += dnw

    return pl.pallas_call(
        body, grid=(seq_len // TOK,),
        in_specs=[_row_spec(TOK, D_MODEL), _const_spec(nw.shape), _row_spec(TOK, D_MODEL)],
        out_specs=[_const_spec((8, LANES)), _row_spec(TOK, D_MODEL), _const_spec(nw.shape)],
        out_shape=[jax.ShapeDtypeStruct((8, LANES), f32), jax.ShapeDtypeStruct((seq_len, D_MODEL), f32),
                   jax.ShapeDtypeStruct(nw.shape, f32)],
        compiler_params=_params(("arbitrary",)), name=name)(x, nw, target)


def _whole(a):
    return pl.BlockSpec(a.shape, lambda: (0,) * len(a.shape))


def s5_prep_fwd(name, raw):
    def body(*refs):
        outs = s5_prep(*[r[...] for r in refs[:7]])
        for o_ref, o in zip(refs[7:], outs):
            o_ref[...] = o

    shapes = [(1, S5_W)] * 2 + [(S5_CH, S5_W)] * 4
    return pl.pallas_call(
        body, in_specs=[_whole(a) for a in raw], out_specs=[pl.BlockSpec(s, lambda s=s: (0,) * len(s)) for s in shapes],
        out_shape=[jax.ShapeDtypeStruct(s, f32) for s in shapes],
        compiler_params=pltpu.CompilerParams(vmem_limit_bytes=VMEM_LIMIT), name=name)(*raw)


def s5_prep_bwd(name, raw, cts):
    def body(*refs):
        _, vjp = jax.vjp(s5_prep, *[r[...] for r in refs[:7]])
        grads = vjp(tuple(r[...] for r in refs[7:13]))
        for o_ref, g in zip(refs[13:], grads):
            o_ref[...] = g

    return pl.pallas_call(
        body, in_specs=[_whole(a) for a in list(raw) + list(cts)], out_specs=[_whole(a) for a in raw],
        out_shape=[jax.ShapeDtypeStruct(a.shape, f32) for a in raw],
        compiler_params=pltpu.CompilerParams(vmem_limit_bytes=VMEM_LIMIT), name=name)(*raw, *cts)


SCAN_SEG = 8
SCAN_LEN = TOK // SCAN_SEG


def _segment_major():
    r, t = _iota((TOK, TOK), 0), _iota((TOK, TOK), 1)
    return (t == (r % SCAN_SEG) * SCAN_LEN + r // SCAN_SEG).astype(f32)


def _store_powers(lr, li, pr_ref, pi_ref):
    qr, qi = lr, li
    for j in range(SCAN_LEN):
        pr_ref[j:j + 1, :] = qr
        pi_ref[j:j + 1, :] = qi
        qr, qi = lr * qr - li * qi, lr * qi + li * qr


def _tile_scan(xr_ref, xi_ref, lr, li, pr_ref, pi_ref, cr_ref, ci_ref, carry_re, carry_im, reverse):
    sign = -1.0 if reverse else 1.0
    ar = jnp.broadcast_to(lr, (SCAN_SEG, lr.shape[1]))
    ai = jnp.broadcast_to(sign * li, (SCAN_SEG, li.shape[1]))
    rows = lambda j: slice(j * SCAN_SEG, (j + 1) * SCAN_SEG)
    hr = hi = jnp.zeros_like(ar)
    for j in (range(SCAN_LEN - 1, -1, -1) if reverse else range(SCAN_LEN)):
        hr, hi = ar * hr - ai * hi + xr_ref[rows(j), :], ar * hi + ai * hr + xi_ref[rows(j), :]
        xr_ref[rows(j), :] = hr
        xi_ref[rows(j), :] = hi
    wr, wi = pr_ref[SCAN_LEN - 1:SCAN_LEN, :], sign * pi_ref[SCAN_LEN - 1:SCAN_LEN, :]
    er, ei = carry_re[0:1, :], carry_im[0:1, :]
    for s in (range(SCAN_SEG - 1, -1, -1) if reverse else range(SCAN_SEG)):
        cr_ref[s:s + 1, :] = er
        ci_ref[s:s + 1, :] = ei
        er, ei = hr[s:s + 1, :] + wr * er - wi * ei, hi[s:s + 1, :] + wr * ei + wi * er
    carry_re[0:1, :] = er
    carry_im[0:1, :] = ei
    cr, ci = cr_ref[...], ci_ref[...]
    for j in range(SCAN_LEN):
        k = SCAN_LEN - 1 - j if reverse else j
        qr, qi = pr_ref[k:k + 1, :], sign * pi_ref[k:k + 1, :]
        xr_ref[rows(j), :] += qr * cr - qi * ci
        xi_ref[rows(j), :] += qr * ci + qi * cr


_SCAN_SCRATCH = [pltpu.VMEM((HALO, S5_W), f32), pltpu.VMEM((HALO, S5_W), f32),
                 pltpu.VMEM((SCAN_LEN, S5_W), f32), pltpu.VMEM((SCAN_LEN, S5_W), f32),
                 pltpu.VMEM((SCAN_SEG, S5_W), f32), pltpu.VMEM((SCAN_SEG, S5_W), f32)]


def s5_fwd(name, proj, u_block, lam_re, lam_im, bblk_re, bblk_im, cblk_re, cblk_im, d_skip, w_glu, b_glu):
    seq_len = proj.shape[0]

    def body(u_ref, lr_ref, li_ref, br_ref, bi_ref, cr_ref, ci_ref, d_ref, wg_ref, bg_ref,
             o_ref, hr_ref, hi_ref, carry_re, carry_im, pw_re, pw_im, cb_re, cb_im):
        lr, li = lr_ref[...], li_ref[...]

        @pl.when(pl.program_id(0) == 0)
        def _():
            carry_re[...] = jnp.zeros_like(carry_re)
            carry_im[...] = jnp.zeros_like(carry_im)
            _store_powers(lr, li, pw_re, pw_im)

        perm = _segment_major()
        u = _exact01(u_ref[...], perm, ((1,), (0,)), True)
        hr_ref[...] = mm(u, br_ref[...])
        hi_ref[...] = mm(u, bi_ref[...])
        _tile_scan(hr_ref, hi_ref, lr, li, pw_re, pw_im, cb_re, cb_im, carry_re, carry_im, reverse=False)
        out = s5_out_fn(hr_ref[...], hi_ref[...], u, cr_ref[...], ci_ref[...], d_ref[...],
                        wg_ref[...].astype(f32), bg_ref[...])
        o_ref[...] = _exact01(out, perm, ((0,), (0,)), True)

    consts = [lam_re, lam_im, bblk_re, bblk_im, cblk_re, cblk_im, d_skip, w_glu, b_glu]
    return pl.pallas_call(
        body, grid=(seq_len // TOK,),
        in_specs=[_row_spec(TOK, S5_CH, u_block)] + [_const_spec(a.shape) for a in consts],
        out_specs=[_row_spec(TOK, S5_CH), _row_spec(TOK, S5_W), _row_spec(TOK, S5_W)],
        out_shape=[jax.ShapeDtypeStruct((seq_len, S5_CH), f32), jax.ShapeDtypeStruct((seq_len, S5_W), f32),
                   jax.ShapeDtypeStruct((seq_len, S5_W), f32)],
        scratch_shapes=_SCAN_SCRATCH,
        compiler_params=_params(("arbitrary",)), name=name)(proj, *consts)


def s5_bwd(name, proj, u_block, h_re, h_im, d_out, lam_re, lam_im, bblk_re, bblk_im, cblk_re, cblk_im, d_skip, w_glu, b_glu):
    seq_len = proj.shape[0]
    nt = seq_len // TOK
    consts = [lam_re, lam_im, bblk_re, bblk_im, cblk_re, cblk_im, d_skip, w_glu, b_glu]

    def body(u_ref, hr_ref, hi_ref, pr_ref, pi_ref, dout_ref, lr_ref, li_ref, br_ref, bi_ref, cr_ref, ci_ref, d_ref, wg_ref, bg_ref,
             du_ref, dlr_ref, dli_ref, dbr_ref, dbi_ref, dcr_ref, dci_ref, dd_ref, dwg_ref, dbg_ref,
             gr_ref, gi_ref, carry_re, carry_im, pw_re, pw_im, cb_re, cb_im):
        i = pl.program_id(0)
        tile = nt - 1 - i
        lr, li = lr_ref[...], li_ref[...]

        @pl.when(i == 0)
        def _():
            for r in (carry_re, carry_im, dlr_ref, dli_ref, dbr_ref, dbi_ref, dcr_ref, dci_ref, dd_ref, dwg_ref, dbg_ref):
                r[...] = jnp.zeros_like(r)
            _store_powers(lr, li, pw_re, pw_im)

        perm = _segment_major()
        u = _exact01(u_ref[...], perm, ((1,), (0,)), True)
        d_out_p = _exact01(dout_ref[...], perm, ((1,), (0,)), True)
        h_r, h_i = hr_ref[...], hi_ref[...]
        _, vjp = jax.vjp(s5_out_fn, h_r, h_i, u, cr_ref[...], ci_ref[...], d_ref[...], wg_ref[...].astype(f32), bg_ref[...])
        ghr, ghi, du, dcr, dci, dd, dwg, dbg = vjp(d_out_p)
        gr_ref[...] = ghr
        gi_ref[...] = ghi
        _tile_scan(gr_ref, gi_ref, lr, li, pw_re, pw_im, cb_re, cb_im, carry_re, carry_im, reverse=True)
        g_r, g_i = gr_ref[...], gi_ref[...]
        keep = jnp.where(tile > 0, 1.0, 0.0)
        top = _iota((SCAN_SEG, 1), 0) == 0

        def earlier(h, before_ref):
            head = jnp.where(top, before_ref[HALO - 1:HALO, :] * keep, _roll(h[TOK - SCAN_SEG:, :], 1, 0))
            return jnp.concatenate([head, h[:TOK - SCAN_SEG, :]], axis=0)

        p_r, p_i = earlier(h_r, pr_ref), earlier(h_i, pi_ref)
        dlr_ref[...] += jnp.sum(g_r * p_r + g_i * p_i, axis=0, keepdims=True)
        dli_ref[...] += jnp.sum(g_i * p_r - g_r * p_i, axis=0, keepdims=True)
        du_p = du + mm_nt(g_r, br_ref[...]) + mm_nt(g_i, bi_ref[...])
        du_ref[...] = _exact01(du_p, perm, ((0,), (0,)), True)
        dbr_ref[...] += mm_tn(u, g_r)
        dbi_ref[...] += mm_tn(u, g_i)
        dcr_ref[...] += dcr
        dci_ref[...] += dci
        dd_ref[...] += dd
        dwg_ref[...] += dwg
        dbg_ref[...] += dbg

    rev = lambda cols, cb=0: pl.BlockSpec((TOK, cols), lambda i, cb=cb: (nt - 1 - i, cb))
    prev = pl.BlockSpec((HALO, S5_W), lambda i: (jnp.maximum((nt - 1 - i) * (TOK // HALO) - 1, 0), 0))
    outs = pl.pallas_call(
        body, grid=(nt,),
        in_specs=[rev(S5_CH, u_block), rev(S5_W), rev(S5_W), prev, prev, rev(S5_CH)] + [_const_spec(a.shape) for a in consts],
        out_specs=[rev(S5_CH)] + [_const_spec(a.shape) for a in consts],
        out_shape=[jax.ShapeDtypeStruct((seq_len, S5_CH), f32)] + [jax.ShapeDtypeStruct(a.shape, f32) for a in consts],
        scratch_shapes=[pltpu.VMEM((TOK, S5_W), f32), pltpu.VMEM((TOK, S5_W), f32)] + _SCAN_SCRATCH,
        compiler_params=_params(("arbitrary",)), name=name)(proj, h_re, h_im, h_re, h_im, d_out, *consts)
    return outs[0], outs[1:]


ANY = pl.BlockSpec(memory_space=pl.ANY)


def _mesh_position():
    x, y, c = lax.axis_index("x"), lax.axis_index("y"), lax.axis_index("c")
    return x, y, c, 4 * x + 2 * y + c


def _peer(x, y, c, r):
    px = 1 - x if r & 4 else x
    py = 1 - y if r & 2 else y
    pc = 1 - c if r & 1 else c
    return (px, py, pc), 4 * px + 2 * py + pc


class Exchange:
    def __init__(self, arrays, gather):
        self.arrays, self.gather, self.n = list(arrays), gather, len(arrays)
        self.in_specs = [ANY] * self.n
        self.out_specs = [ANY] * self.n
        shapes = [((N_DEV,) + a.shape) if gather else a.shape for a in self.arrays]
        self.out_shape = [jax.ShapeDtypeStruct(s, a.dtype) for s, a in zip(shapes, self.arrays)]
        self.scratch = [pltpu.SemaphoreType.DMA((self.n, N_DEV - 1)), pltpu.SemaphoreType.DMA((self.n, N_DEV - 1)),
                        pltpu.SemaphoreType.DMA((self.n,))]

    def _copies(self, ins, outs, sems, landed):
        send_sems, recv_sems, local_sems = sems
        x, y, c, me = _mesh_position()
        local, remote = [], []
        for i in range(self.n):
            mine = ins[i] if self.gather else ins[i].at[me]
            local.append(pltpu.make_async_copy(mine, outs[i].at[me], local_sems.at[i]))
            for r in range(1, N_DEV):
                peer, peer_idx = _peer(x, y, c, r)
                remote.append(pltpu.make_async_remote_copy(
                    src_ref=ins[i] if self.gather else ins[i].at[peer_idx],
                    dst_ref=outs[i].at[peer_idx if landed else me],
                    send_sem=send_sems.at[i, r - 1], recv_sem=recv_sems.at[i, r - 1],
                    device_id=peer, device_id_type=pl.DeviceIdType.MESH))
        return local, remote

    def start(self, ins, outs, sems):
        local, remote = self._copies(ins, outs, sems, landed=False)
        for cp in local + remote:
            cp.start()

    def wait(self, ins, outs, sems):
        local, remote = self._copies(ins, outs, sems, landed=True)
        for cp in remote:
            cp.wait_recv()
            cp.wait_send()
        for cp in local:
            cp.wait()

    def run(self, name):
        n = self.n

        def body(*refs):
            ins, outs, sems = refs[:n], refs[n:2 * n], refs[2 * n:]
            self.start(ins, outs, sems)
            self.wait(ins, outs, sems)

        return pl.pallas_call(body, in_specs=self.in_specs, out_specs=self.out_specs, out_shape=self.out_shape,
                              scratch_shapes=self.scratch, name=name)(*self.arrays)


class Carried:
    def __init__(self, exchange):
        self.ex = exchange
        self.n = exchange.n if exchange else 0
        self.in_specs = exchange.in_specs if exchange else []
        self.out_specs = exchange.out_specs if exchange else []
        self.out_shape = exchange.out_shape if exchange else []
        self.scratch = exchange.scratch if exchange else []
        self.operands = exchange.arrays if exchange else []

    def start_at(self, first, ins, outs, sems):
        if self.ex is not None:
            @pl.when(first)
            def _():
                self.ex.start(ins, outs, sems)

    def wait_at(self, last, ins, outs, sems):
        if self.ex is not None:
            @pl.when(last)
            def _():
                self.ex.wait(ins, outs, sems)


def adamw(name, parts, w, m, v):
    rows, cols = w.shape
    tr = rows
    for cand in (512, 256, 128, 64, 32, 16, 8):
        if rows * cols * 4 > (1 << 20) and rows % cand == 0 and cand * cols * 4 <= (1 << 20):
            tr = cand
            break

    def body(p_ref, w_ref, m_ref, v_ref, g_ref, d_ref, nm_ref, nv_ref):
        g = p_ref[0].astype(f32)
        for s in range(1, N_DEV):
            g = g + p_ref[s].astype(f32)
        nm = ADAM_B1 * m_ref[...] + (1.0 - ADAM_B1) * g
        nv = ADAM_B2 * v_ref[...] + (1.0 - ADAM_B2) * (g * g)
        m_hat = nm / (1.0 - ADAM_B1 ** ADAM_STEP)
        v_hat = nv / (1.0 - ADAM_B2 ** ADAM_STEP)
        g_ref[...] = g
        d_ref[...] = -ADAM_LR * (m_hat / (jnp.sqrt(v_hat) + ADAM_EPS) + ADAM_WD * w_ref[...])
        nm_ref[...] = nm
        nv_ref[...] = nv

    blk = pl.BlockSpec((tr, cols), lambda i: (i, 0))
    return pl.pallas_call(
        body, grid=(rows // tr,),
        in_specs=[pl.BlockSpec((N_DEV, tr, cols), lambda i: (0, i, 0)), blk, blk, blk],
        out_specs=[blk] * 4, out_shape=[jax.ShapeDtypeStruct((rows, cols), f32)] * 4,
        compiler_params=_params(("arbitrary",)), name=name)(parts, w, m, v)


WEIGHTS = ['l0_norm_mix', 'l0_w_in', 'ssd_conv_w', 'ssd_conv_b', 'ssd_dt_bias', 'ssd_A_log', 'ssd_D', 'ssd_norm_w',
           'l0_w_out', 'l0_norm_mlp', 'l0_w_up', 'l0_w_down', 'l1_norm_mix', 'l1_w_in', 'gdn_conv_w', 'gdn_A_log',
           'gdn_dt_bias', 'gdn_norm_w', 's5_A_re', 's5_A_im', 's5_log_step', 's5_B_re', 's5_B_im', 's5_C_re', 's5_C_im',
           's5_D', 's5_w_glu', 's5_b_glu', 'l1_w_out', 'l1_norm_mlp', 'l1_w_up', 'l1_w_down', 'final_norm']
SHARDED = ['l0_w_in', 'l0_w_out', 'l0_w_up', 'l0_w_down', 'l1_w_in', 's5_w_glu', 'l1_w_out', 'l1_w_up', 'l1_w_down',
           'ssd_conv_w', 'gdn_conv_w']
F32_GATHER = ('ssd_conv_w', 'gdn_conv_w')
REPLICATED = [n for n in WEIGHTS if n not in SHARDED]
INPUTS = ['x'] + WEIGHTS + ['loss_target'] + ['m_' + n for n in WEIGHTS] + ['v_' + n for n in WEIGHTS]


def _row(v):
    return v.reshape(1, -1)


def _pad_lanes(v, offset=0):
    return jnp.pad(v, (offset, LANES - offset - v.shape[0])).reshape(1, LANES)


def _cols_to_blocks(g):
    return g.reshape(g.shape[0], N_DEV, -1).transpose(1, 0, 2)


def _blocks_to_cols(g):
    return g.transpose(1, 0, 2).reshape(g.shape[1], -1)


def _pack(arrays):
    parts, slots, at = [], [], 0
    for a in arrays:
        n = a.size
        rows = -(-n // (8 * LANES)) * 8
        parts.append(jnp.pad(a.reshape(-1), (0, rows * LANES - n)).reshape(rows, LANES))
        slots.append((at, rows, n, a.shape))
        at += rows
    return jnp.concatenate(parts, axis=0), slots


def _unpack(buf, slots):
    return [buf[at:at + rows].reshape(-1)[:n].reshape(shape) for at, rows, n, shape in slots]


def kernel(*args):
    a = dict(zip(INPUTS, args, strict=True))
    seq_len = a['x'].shape[1]
    x0 = a['x'].reshape(seq_len, D_MODEL)
    target = a['loss_target'].reshape(seq_len, D_MODEL)

    shard = {n: a[n] if n in F32_GATHER else a[n].astype(_MXU) for n in SHARDED}
    first = ['l0_w_in', 'ssd_conv_w', 'gdn_conv_w']
    g = dict(zip(first, Exchange([shard[n] for n in first], gather=True).run("gather_first")))
    w_nat = _blocks_to_cols(g['l0_w_in'])
    win0 = jnp.concatenate([w_nat[:, :2048], w_nat[:, 2560:3584], w_nat[:, 2048:2560], w_nat[:, 3584:3592],
                            jnp.zeros((D_MODEL, IN0_PAD - IN0_W), _MXU)], axis=1)
    ssd_cw, gdn_cw = _blocks_to_cols(g['ssd_conv_w']), _blocks_to_cols(g['gdn_conv_w'])

    half = RET_D // 2
    inv = ROPE_THETA ** (-jnp.arange(half, dtype=f32) / half)
    ang = jnp.arange(seq_len, dtype=f32)[:, None] * inv[None, :]
    cos, sin = jnp.cos(ang), jnp.sin(ang)
    cos, sin = jnp.concatenate([cos, cos], axis=1), jnp.concatenate([-sin, sin], axis=1)
    ssd_params = [ssd_cw, _row(a['ssd_conv_b']), _pad_lanes(a['ssd_dt_bias']), _pad_lanes(a['ssd_A_log']),
                  _pad_lanes(a['ssd_D']), _row(a['ssd_norm_w'])]
    gdn_params = [gdn_cw, _pad_lanes(a['gdn_A_log'], GDN_GCOL), _pad_lanes(a['gdn_dt_bias'], GDN_GCOL), _row(a['gdn_norm_w'])]
    s5_raw = [a['s5_A_re'].reshape(1, S5_W), a['s5_A_im'].reshape(1, S5_W), _pad_lanes(a['s5_log_step']),
              a['s5_B_re'].transpose(2, 0, 1).reshape(S5_GROUP, S5_W), a['s5_B_im'].transpose(2, 0, 1).reshape(S5_GROUP, S5_W),
              a['s5_C_re'].transpose(1, 0, 2).reshape(S5_GROUP, S5_W), a['s5_C_im'].transpose(1, 0, 2).reshape(S5_GROUP, S5_W)]
    s5_d, s5_bg = _row(a['s5_D']), _row(a['s5_b_glu'])
    nw = {n: _row(a[n]) for n in ('l0_norm_mix', 'l0_norm_mlp', 'l1_norm_mix', 'l1_norm_mlp', 'final_norm')}

    proj0 = inproj_fwd("l0_in", x0, nw['l0_norm_mix'], win0)
    ret_seqs = [Seq(proj0, 512, 0), Seq(proj0, 512, 1), Seq(proj0, 512, 2), Seq(proj0, 512, 3),
                Seq(cos, LANES, 0, "const"), Seq(sin, LANES, 0, "const")]
    later = ['l0_w_out', 'l0_w_up']
    ret_out, ret_st, got = mixer_fwd("ret_fwd", ret_chunk, ret_seqs, [], 512, RET_HEADS * RET_D, seq_len,
                                     per_step=LIGHT_CHUNKS_PER_STEP, exchange=Exchange([shard[n] for n in later], gather=True))
    g.update(zip(later, got))
    wout0 = g['l0_w_out'].reshape(D_MODEL, D_MODEL)
    ssd_seqs = [Seq(proj0, 512, 6), Seq(proj0, 1024, 2, "halo"), Seq(proj0, LANES, 28)]
    later = ['l0_w_down', 's5_w_glu']
    ssd_out, ssd_st, got = mixer_fwd("ssd_fwd", ssd_chunk, ssd_seqs, ssd_params, SSD_INNER, SSD_INNER, seq_len,
                                     per_step=LIGHT_CHUNKS_PER_STEP, exchange=Exchange([shard[n] for n in later], gather=True))
    g.update(zip(later, got))
    wglu = g['s5_w_glu'].reshape(S5_CH, S5_CH)
    x1 = outproj_fwd("l0_out", x0, ret_out, ssd_out, wout0)
    later = ['l1_w_in', 'l1_w_out', 'l1_w_up', 'l1_w_down']
    x2, relu0, got = mlp_fwd("l0_mlp", x1, nw['l0_norm_mlp'], g['l0_w_up'], g['l0_w_down'],
                      exchange=Exchange([shard[n] for n in later], gather=True))
    g.update(zip(later, got))
    w_nat = g['l1_w_in'].reshape(D_MODEL, IN1_W)
    win1 = jnp.concatenate([w_nat[:, :3072], w_nat[:, 3084:3340], w_nat[:, 3072:3084],
                            jnp.zeros((D_MODEL, IN1_PAD - IN1_W), _MXU)], axis=1)
    wout1 = g['l1_w_out'].reshape(D_MODEL, D_MODEL)
    proj1 = inproj_fwd("l1_in", x2, nw['l1_norm_mix'], win1)
    gdn_seqs = [Seq(proj1, 3 * GDN_W, 0, "halo"), Seq(proj1, GDN_W, 3), Seq(proj1, LANES, 26)]
    gdn_out, gdn_st, gdn_inv, _ = mixer_fwd("gdn_fwd", functools.partial(gdn_chunk, keep_inverses=True), gdn_seqs, gdn_params,
                                            GDN_W, GDN_W, seq_len, kept_shape=(GDN_HEADS, CHUNK, CHUNK))
    prep = s5_prep_fwd("s5_prep", s5_raw)
    s5_out, h_re, h_im = s5_fwd("s5_fwd", proj1, 12, *prep, s5_d, wglu, s5_bg)
    x3 = outproj_fwd("l1_out", x2, gdn_out, s5_out, wout1)
    x4, relu1, _ = mlp_fwd("l1_mlp", x3, nw['l1_norm_mlp'], g['l1_w_up'], g['l1_w_down'])
    loss_blk, dx4, d_final = final_loss("final_loss", x4, nw['final_norm'], target)

    parts = {}
    dx3, d_up1, d_down1, d_nmlp1, _ = mlp_bwd("l1_mlp_bwd", x3, nw['l1_norm_mlp'], g['l1_w_up'], g['l1_w_down'], relu1, dx4)
    d_gdn, d_s5, d_wout1 = outproj_bwd("l1_out_bwd", dx3, gdn_out, s5_out, wout1)
    d_u, s5_g = s5_bwd("s5_bwd", proj1, 12, h_re, h_im, d_s5, *prep, s5_d, wglu, s5_bg)
    s5_raw_g = s5_prep_bwd("s5_prep_bwd", s5_raw, s5_g[:6])
    ready = {'l1_w_up': d_up1, 'l1_w_down': d_down1, 'l1_w_out': d_wout1.reshape(N_DEV, -1, D_MODEL),
             's5_w_glu': s5_g[7].reshape(N_DEV, -1, S5_CH)}
    (d_qkv, d_z1, d_ba), gdn_pg, got = mixer_bwd("gdn_bwd", gdn_chunk, gdn_seqs, gdn_params, gdn_st, d_gdn, seq_len,
                                                 exchange=Exchange(list(ready.values()), gather=False), kept=gdn_inv)
    parts.update(zip(ready, got))
    dx2, d_win1, d_nmix1 = inproj_bwd("l1_in_bwd", x2, nw['l1_norm_mix'], win1, [d_qkv, d_z1, d_u, d_ba], dx3)
    d_win1 = jnp.concatenate([d_win1[:, :3072], d_win1[:, 3328:3340], d_win1[:, 3072:3328]], axis=1)
    ready = {'l1_w_in': d_win1.reshape(N_DEV, -1, IN1_W), 'gdn_conv_w': _cols_to_blocks(gdn_pg[0])}
    dx1, d_up0, d_down0, d_nmlp0, got = mlp_bwd("l0_mlp_bwd", x1, nw['l0_norm_mlp'], g['l0_w_up'], g['l0_w_down'], relu0, dx2,
                                                exchange=Exchange(list(ready.values()), gather=False))
    parts.update(zip(ready, got))
    d_ret, d_ssd, d_wout0 = outproj_bwd("l0_out_bwd", dx1, ret_out, ssd_out, wout0)
    ready = {'l0_w_up': d_up0}
    d_qkvg, _, got = mixer_bwd("ret_bwd", ret_chunk, ret_seqs, [], ret_st, d_ret, seq_len,
                               per_step=LIGHT_CHUNKS_PER_STEP, exchange=Exchange(list(ready.values()), gather=False))
    parts.update(zip(ready, got))
    ready = {'l0_w_down': d_down0, 'l0_w_out': d_wout0.reshape(N_DEV, -1, D_MODEL)}
    (d_z0, d_xbc, d_dt), ssd_pg, got = mixer_bwd("ssd_bwd", ssd_chunk, ssd_seqs, ssd_params, ssd_st, d_ssd, seq_len,
                                                 per_step=LIGHT_CHUNKS_PER_STEP,
                                                 exchange=Exchange(list(ready.values()), gather=False))
    parts.update(zip(ready, got))
    dx0, d_win0, d_nmix0 = inproj_bwd("l0_in_bwd", x0, nw['l0_norm_mix'], win0, list(d_qkvg) + [d_xbc, d_z0, d_dt], dx1)

    d_win0 = jnp.concatenate([d_win0[:, :2048], d_win0[:, 3072:3584], d_win0[:, 2048:3072], d_win0[:, 3584:3592]], axis=1)
    ready = {'l0_w_in': _cols_to_blocks(d_win0).astype(jnp.bfloat16), 'ssd_conv_w': _cols_to_blocks(ssd_pg[0])}
    from_b = lambda t: t.reshape(S5_GROUP, S5_GROUPS, S5_STATE).transpose(1, 2, 0)
    from_c = lambda t: t.reshape(S5_GROUP, S5_GROUPS, S5_STATE).transpose(1, 0, 2)
    replicated_g = {
        'l0_norm_mix': d_nmix0, 'ssd_conv_b': ssd_pg[1], 'ssd_dt_bias': ssd_pg[2][0, :SSD_HEADS], 'ssd_A_log': ssd_pg[3][0, :SSD_HEADS],
        'ssd_D': ssd_pg[4][0, :SSD_HEADS], 'ssd_norm_w': ssd_pg[5], 'l0_norm_mlp': d_nmlp0, 'l1_norm_mix': d_nmix1,
        'gdn_A_log': gdn_pg[1][0, GDN_GCOL:GDN_GCOL + GDN_HEADS], 'gdn_dt_bias': gdn_pg[2][0, GDN_GCOL:GDN_GCOL + GDN_HEADS],
        'gdn_norm_w': gdn_pg[3], 's5_A_re': s5_raw_g[0], 's5_A_im': s5_raw_g[1], 's5_log_step': s5_raw_g[2][0, :S5_GROUPS],
        's5_B_re': from_b(s5_raw_g[3]), 's5_B_im': from_b(s5_raw_g[4]), 's5_C_re': from_c(s5_raw_g[5]), 's5_C_im': from_c(s5_raw_g[6]),
        's5_D': s5_g[6], 's5_b_glu': s5_g[8], 'l1_norm_mlp': d_nmlp1, 'final_norm': d_final}
    replicated_g = {n: replicated_g[n].reshape(a[n].shape) for n in REPLICATED}

    parts.update(zip(ready, Exchange(list(ready.values()), gather=False).run("scatter_last")))
    packed_g, slots = _pack([replicated_g[n] for n in REPLICATED])
    (packed_parts,) = Exchange([packed_g], gather=True).run("gather_small_grads")
    results = {}
    for n in SHARDED:
        results[n] = adamw("adamw_" + n, parts[n], a[n], a['m_' + n], a['v_' + n])
    packed = [_pack([a[pre + n] for n in REPLICATED])[0] for pre in ('', 'm_', 'v_')]
    small = [_unpack(t, slots) for t in adamw("adamw_small", packed_parts, *packed)]
    for i, n in enumerate(REPLICATED):
        results[n] = tuple(small[k][i] for k in range(4))

    loss = lax.psum(loss_blk[0, 0], ("x", "y", "c"))
    grad_x = dx0.reshape(a['x'].shape)
    return (loss, grad_x, *[results[n][0] for n in WEIGHTS], *[results[n][1] for n in WEIGHTS],
            *[results[n][2] for n in WEIGHTS], *[results[n][3] for n in WEIGHTS])
```

```python
import functools
import math

import numpy as np
import jax
import jax.numpy as jnp
from jax import lax
from jax.experimental import pallas as pl
from jax.experimental.pallas import tpu as pltpu

f32 = jnp.float32
_MXU = jnp.bfloat16
HI = lax.Precision.HIGHEST

D_MODEL = 1024
CHUNK = 64
EPS = 1e-6
N_DEV = 8
LANES = 128
HALO = 8
CONV_WIDTH = 4

RET_HEADS, RET_D = 4, 128
SSD_HEADS, SSD_P, SSD_N, SSD_GROUPS = 8, 64, 128, 2
SSD_INNER = SSD_HEADS * SSD_P
GDN_HEADS, GDN_D = 6, 128
GDN_W = GDN_HEADS * GDN_D
S5_CH, S5_GROUP, S5_GROUPS, S5_STATE = 256, 16, 16, 64
S5_W = S5_GROUPS * S5_STATE
D_FF = 4096
ROPE_THETA = 10000.0

IN0_W = 3592
IN0_PAD = 3712
IN1_W = 3340
IN1_PAD = 3456

ADAM_LR, ADAM_B1, ADAM_B2, ADAM_EPS, ADAM_WD, ADAM_STEP = 0.001, 0.9, 0.999, 1e-08, 0.01, 10

VMEM_LIMIT = 56 * 1024 * 1024


def _dot(a, b, dims):
    return lax.dot_general(a.astype(_MXU), b.astype(_MXU), (dims, ((), ())), preferred_element_type=f32)


@jax.custom_vjp
def mm(a, b):
    return _dot(a, b, ((1,), (0,)))


@jax.custom_vjp
def mm_nt(a, b):
    return _dot(a, b, ((1,), (1,)))


@jax.custom_vjp
def mm_tn(a, b):
    return _dot(a, b, ((0,), (0,)))


mm.defvjp(lambda a, b: (mm(a, b), (a, b)), lambda r, g: (mm_nt(g, r[1]), mm_tn(r[0], g)))
mm_nt.defvjp(lambda a, b: (mm_nt(a, b), (a, b)), lambda r, g: (mm(g, r[1]), mm_tn(g, r[0])))
mm_tn.defvjp(lambda a, b: (mm_tn(a, b), (a, b)), lambda r, g: (mm_nt(r[1], g), mm(r[0], g)))


def mmh(a, b):
    return jnp.dot(a, b, precision=HI, preferred_element_type=f32)


def _exact01(x, m01, dims, m_first):
    hi = x.astype(jnp.bfloat16)
    r = x - hi.astype(f32)
    mid = r.astype(jnp.bfloat16)
    lo = (r - mid.astype(f32)).astype(jnp.bfloat16)
    m = m01.astype(jnp.bfloat16)
    dot = lambda p: lax.dot_general(m, p, (dims, ((), ())), preferred_element_type=f32) if m_first else \
        lax.dot_general(p, m, (dims, ((), ())), preferred_element_type=f32)
    return dot(hi) + dot(mid) + dot(lo)


@jax.custom_vjp
def spread01(x, sel):
    return _exact01(x, sel, ((1,), (0,)), False)


spread01.defvjp(lambda x, sel: (spread01(x, sel), sel),
                lambda sel, g: (_exact01(g, sel, ((1,), (1,)), False), jnp.zeros_like(sel)))


@jax.custom_vjp
def cumsum01(tril, x):
    return _exact01(x, tril, ((1,), (0,)), True)


cumsum01.defvjp(lambda tril, x: (cumsum01(tril, x), tril),
                lambda tril, g: (jnp.zeros_like(tril), _exact01(g, tril, ((0,), (0,)), True)))


def _roll(x, shift, axis):
    return pltpu.roll(x, shift, axis)


@functools.partial(jax.custom_vjp, nondiff_argnums=(1,))
def roll_rows(x, s):
    return _roll(x, s, 0) if s else x


roll_rows.defvjp(lambda x, s: (roll_rows(x, s), None),
                 lambda s, _, g: ((_roll(g, g.shape[0] - s, 0) if s else g),))


@jax.custom_vjp
def roll_half(x):
    return _roll(x, x.shape[-1] // 2, 1)


roll_half.defvjp(lambda x: (roll_half(x), None), lambda _, g: (roll_half(g),))


def _iota(shape, axis):
    return lax.broadcasted_iota(jnp.int32, shape, axis)


def silu(x):
    return x * jax.nn.sigmoid(x)


def softplus(x):
    return jnp.maximum(x, 0.0) + jnp.log(1.0 + jnp.exp(-jnp.abs(x)))


def rmsnorm_f(x, w):
    return x * lax.rsqrt(jnp.mean(x * x, axis=-1, keepdims=True) + EPS) * w


def unit_rms(x):
    return x * lax.rsqrt(jnp.mean(x * x, axis=-1, keepdims=True) + EPS)


def _causal(n, strict=False):
    r, c = _iota((n, n), 0), _iota((n, n), 1)
    return (r > c) if strict else (r >= c)


def _tril_ones(n):
    return _causal(n).astype(f32)


def _conv_rows(xe, w):
    acc = w[CONV_WIDTH - 1:CONV_WIDTH, :] * xe
    for j in range(CONV_WIDTH - 1):
        acc = acc + w[j:j + 1, :] * roll_rows(xe, CONV_WIDTH - 1 - j)
    return acc[HALO:, :]


_RET_LOG_GAMMA = [float(np.log(np.float32(1.0) - np.float32(2.0) ** np.float32(-5.0 - h))) for h in range(RET_HEADS)]


def ret_chunk(q, k, v, gate, cos, sin, state):
    c = q.shape[0]
    idx = _iota((c, 1), 0).astype(f32)
    diff = (_iota((c, c), 0) - _iota((c, c), 1)).astype(f32)
    causal = _causal(c)
    hs = range(RET_HEADS)
    cols = [slice(h * RET_D, (h + 1) * RET_D) for h in hs]
    lg = _RET_LOG_GAMMA
    qh = [(q[:, s] * cos + roll_half(q[:, s]) * sin) * (RET_D ** -0.5) for s in cols]
    kh = [k[:, s] * cos + roll_half(k[:, s]) * sin for s in cols]
    vh = [v[:, s] for s in cols]
    sh = [state[s, :] for s in cols]
    scores = [mm_nt(qh[h], kh[h]) * jnp.exp(jnp.where(causal, lg[h] * diff, -jnp.inf)) for h in hs]
    inter = [mm(qh[h] * jnp.exp(lg[h] * (idx + 1.0)), sh[h]) for h in hs]
    y = [mm(scores[h], vh[h]) + inter[h] for h in hs]
    states = [sh[h] * math.exp(lg[h] * c) + mm_tn(kh[h] * jnp.exp(lg[h] * (c - 1.0 - idx)), vh[h]) for h in hs]
    outs = [unit_rms(y[h]) * silu(gate[:, cols[h]]) for h in hs]
    return jnp.concatenate(outs, axis=1), jnp.concatenate(states, axis=0)


def _head_select(n_heads, width):
    r, c = _iota((LANES, n_heads * width), 0), _iota((LANES, n_heads * width), 1)
    return (c // width == r).astype(f32)


def ssd_chunk(z, xe, dtr, state, conv_w, conv_b, dt_bias, a_log, d_skip, norm_w):
    c = z.shape[0]
    xbc = silu(_conv_rows(xe, conv_w) + conv_b)
    xs, bm, cm = xbc[:, :SSD_INNER], xbc[:, SSD_INNER:SSD_INNER + 256], xbc[:, SSD_INNER + 256:]
    sel = _head_select(SSD_HEADS, SSD_P)
    dt = softplus(dtr + dt_bias)
    la = dt * (-jnp.exp(a_log))
    la_cum = cumsum01(_tril_ones(c), la)
    la_cum_t = la_cum.T
    last = jnp.sum(la, axis=0, keepdims=True)
    xd = xs * spread01(dt, sel)
    la_x = spread01(la_cum, sel)
    last_x = spread01(last, sel)
    to_end = jnp.exp(last_x - la_x)
    from_start = jnp.exp(la_x)
    causal = _causal(c)
    left = (_iota((1, LANES), 1) < SSD_P).astype(f32)
    upper = _iota((LANES, 1), 0) < SSD_P
    pairs, heads = range(SSD_HEADS // 2), range(SSD_HEADS)
    bc = [bm[:, g * SSD_N:(g + 1) * SSD_N] for g in range(SSD_GROUPS)]
    cc = [cm[:, g * SSD_N:(g + 1) * SSD_N] for g in range(SSD_GROUPS)]
    cb = [mm_nt(cc[g], bc[g]) for g in range(SSD_GROUPS)]
    cols = [slice(p * LANES, (p + 1) * LANES) for p in pairs]
    xd_p = [xd[:, s] for s in cols]
    sp = [state[s, :] for s in cols]
    lmat = [jnp.exp(jnp.where(causal, la_cum[:, h:h + 1] - la_cum_t[h:h + 1, :], -jnp.inf)) for h in heads]
    off = [mm_nt(cc[p // 2], sp[p]) * from_start[:, cols[p]] for p in pairs]
    diag = [mm(cb[h // 4] * lmat[h], xd_p[h // 2] * (left if h % 2 == 0 else 1.0 - left)) for h in heads]
    cd = [jnp.where(upper, jnp.exp(last[:, 2 * p:2 * p + 1]), jnp.exp(last[:, 2 * p + 1:2 * p + 2])) for p in pairs]
    states = [sp[p] * cd[p] + mm_tn(xd_p[p] * to_end[:, cols[p]], bc[p // 2]) for p in pairs]
    ys = [off[p] + diag[2 * p] + diag[2 * p + 1] for p in pairs]
    y = jnp.concatenate(ys, axis=1) + spread01(d_skip, sel) * xs
    yg = y * silu(z)
    half = SSD_INNER // SSD_GROUPS
    out = jnp.concatenate([unit_rms(yg[:, i * half:(i + 1) * half]) for i in range(SSD_GROUPS)], axis=1) * norm_w
    return out, jnp.concatenate(states, axis=0)


def mm3(a, b):
    return jnp.dot(a, b, precision=lax.Precision.HIGH, preferred_element_type=f32)


@jax.custom_vjp
def _unit_lower_inverses(lowers):
    n = lowers[0].shape[0]
    eye = (_iota((n, n), 0) == _iota((n, n), 1)).astype(f32)
    a = [-l for l in lowers]
    p = [eye + x for x in a]
    k = 2
    while k < n:
        a = [mm3(x, x) for x in a]
        p = [y + mm3(y, x) for y, x in zip(p, a)]
        k *= 2
    return p


def _unit_lower_inverses_bwd(t_inv, g):
    dims_tn, dims_nt = (((0,), (0,)), ((), ())), (((1,), (1,)), ((), ()))
    x = [lax.dot_general(t, gi, dims_tn, precision=lax.Precision.HIGH, preferred_element_type=f32) for t, gi in zip(t_inv, g)]
    return ([-lax.dot_general(xi, t, dims_nt, precision=lax.Precision.HIGH, preferred_element_type=f32) for xi, t in zip(x, t_inv)],)


def _unit_lower_inverses_fwd(lowers):
    t_inv = _unit_lower_inverses(lowers)
    return t_inv, t_inv


_unit_lower_inverses.defvjp(_unit_lower_inverses_fwd, _unit_lower_inverses_bwd)


@jax.custom_vjp
def _known_inverses(lowers, t_inv):
    return t_inv


_known_inverses.defvjp(lambda lowers, t_inv: (t_inv, t_inv),
                       lambda t_inv, g: (_unit_lower_inverses_bwd(t_inv, g)[0], [jnp.zeros_like(t) for t in t_inv]))


GDN_GCOL = 6


def gdn_chunk(xe, z, ba, state, conv_w, a_log, dt_bias, norm_w, kept_inverses=None, keep_inverses=False):
    c = z.shape[0]
    qkv = silu(_conv_rows(xe, conv_w))
    beta_all = jax.nn.sigmoid(ba)
    g_all = -jnp.exp(a_log) * softplus(ba + dt_bias)
    gc = cumsum01(_tril_ones(c), g_all)
    gc_t = gc.T
    last = jnp.sum(g_all, axis=0, keepdims=True)
    causal, strict = _causal(c), _causal(c, strict=True)
    hs = range(GDN_HEADS)
    cols = [slice(h * GDN_D, (h + 1) * GDN_D) for h in hs]
    qh = [qkv[:, h * GDN_D:(h + 1) * GDN_D] for h in hs]
    kh = [qkv[:, GDN_W + h * GDN_D:GDN_W + (h + 1) * GDN_D] for h in hs]
    vh = [qkv[:, 2 * GDN_W + h * GDN_D:2 * GDN_W + (h + 1) * GDN_D] for h in hs]
    qh = [t * lax.rsqrt(jnp.sum(t * t, axis=-1, keepdims=True) + EPS) * (GDN_D ** -0.5) for t in qh]
    kh = [t * lax.rsqrt(jnp.sum(t * t, axis=-1, keepdims=True) + EPS) for t in kh]
    beta = [beta_all[:, h:h + 1] for h in hs]
    col = [gc[:, GDN_GCOL + h:GDN_GCOL + h + 1] for h in hs]
    row = [gc_t[GDN_GCOL + h:GDN_GCOL + h + 1, :] for h in hs]
    lst = [last[:, GDN_GCOL + h:GDN_GCOL + h + 1] for h in hs]
    decay = [jnp.exp(jnp.where(causal, col[h] - row[h], -jnp.inf)) for h in hs]
    e_col = [jnp.exp(t) for t in col]
    kb = [kh[h] * beta[h] for h in hs]
    vb = [vh[h] * beta[h] for h in hs]
    kk = [mm_nt(kb[h], kh[h]) for h in hs]
    qk = [mm_nt(qh[h], kh[h]) for h in hs]
    lowers = [jnp.where(strict, kk[h] * decay[h], 0.0) for h in hs]
    t_inv = _unit_lower_inverses(lowers) if kept_inverses is None else _known_inverses(lowers, list(kept_inverses))
    u = [mm(t_inv[h], vb[h]) for h in hs]
    w = [mm(t_inv[h], kb[h] * e_col[h]) for h in hs]
    attn = [jnp.where(causal, qk[h] * decay[h], 0.0) for h in hs]
    sh = [state[s, :] for s in cols]
    ws = [mm(w[h], sh[h]) for h in hs]
    qs = [mm(qh[h] * e_col[h], sh[h]) for h in hs]
    v_new = [u[h] - ws[h] for h in hs]
    o = [qs[h] + mm(attn[h], v_new[h]) for h in hs]
    states = [sh[h] * jnp.exp(lst[h]) + mm_tn(kh[h] * jnp.exp(lst[h] - col[h]), v_new[h]) for h in hs]
    outs = [unit_rms(o[h]) * norm_w * silu(z[:, cols[h]]) for h in hs]
    results = (jnp.concatenate(outs, axis=1), jnp.concatenate(states, axis=0))
    return results + (list(t_inv),) if keep_inverses else results


def _s5_group_mask():
    r, c = _iota((S5_CH, S5_W), 0), _iota((S5_CH, S5_W), 1)
    return (r // S5_GROUP == c // S5_STATE).astype(f32)


def s5_prep(a_re, a_im, log_step, b_re, b_im, c_re, c_im):
    r, c = _iota((LANES, S5_W), 0), _iota((LANES, S5_W), 1)
    step = jnp.exp(mmh(log_step, (c // S5_STATE == r).astype(f32)))
    zr, zi = a_re * step, a_im * step
    e = jnp.exp(zr)
    lr, li = e * jnp.cos(zi), e * jnp.sin(zi)
    den = a_re * a_re + a_im * a_im
    xr, xi = lr - 1.0, li
    cr, ci = (xr * a_re + xi * a_im) / den, (xi * a_re - xr * a_im) / den
    bbr, bbi = cr * b_re - ci * b_im, cr * b_im + ci * b_re
    mask = _s5_group_mask()
    tile = lambda t: jnp.tile(t, (S5_GROUPS, 1)) * mask
    return lr, li, tile(bbr), tile(bbi), tile(c_re), tile(c_im)


def s5_out_fn(h_re, h_im, u, cblk_re, cblk_im, d_skip, w_glu, b_glu):
    y = mm_nt(h_re, cblk_re) - mm_nt(h_im, cblk_im) + d_skip * u
    y = jax.nn.gelu(y)
    return y * jax.nn.sigmoid(mm(y, w_glu) + b_glu)


def _params(sem, **kw):
    return pltpu.CompilerParams(dimension_semantics=sem, vmem_limit_bytes=VMEM_LIMIT, **kw)


def _const_spec(shape):
    return pl.BlockSpec(shape, lambda i: (0,) * len(shape))


def _resident_spec(shape):
    return pl.BlockSpec(shape, lambda i: (0,) * len(shape), pipeline_mode=pl.Buffered(1))


def _row_spec(rows, cols, col_block=0):
    return pl.BlockSpec((rows, cols), lambda i: (i, col_block))


class Seq:
    def __init__(self, array, width, col_block, kind="tile"):
        self.array, self.width, self.col_block, self.kind = array, width, col_block, kind


CHUNKS_PER_STEP = 4
LIGHT_CHUNKS_PER_STEP = 8


def mixer_fwd(name, fn, seqs, params, out_width, state_rows, seq_len, per_step=CHUNKS_PER_STEP, exchange=None, kept_shape=None):
    nc = seq_len // CHUNK
    rows, steps = per_step * CHUNK, nc // per_step
    n_refs = sum(2 if s.kind == "halo" else 1 for s in seqs)
    n_par = len(params)
    n_own = 3 if kept_shape else 2
    car = Carried(exchange)

    def body(*refs):
        seq_refs, par_refs = refs[:n_refs], refs[n_refs:n_refs + n_par]
        k0 = n_refs + n_par
        ex_ins, own = refs[k0:k0 + car.n], refs[k0 + car.n:k0 + car.n + n_own]
        out_ref, st_ref = own[:2]
        k0 += car.n + n_own
        ex_outs, state, ex_sems = refs[k0:k0 + car.n], refs[k0 + car.n], refs[k0 + car.n + 1:]
        c = pl.program_id(0)
        car.start_at(c == 0, ex_ins, ex_outs, ex_sems)

        @pl.when(c == 0)
        def _():
            state[...] = jnp.zeros_like(state)

        par_vals = [p[...] for p in par_refs]
        s_cur = state[...]
        for kk in range(per_step):
            lo = kk * CHUNK
            vals, k = [], 0
            for s in seqs:
                if s.kind == "halo":
                    prev = jnp.where(c > 0, seq_refs[k][...], 0.0) if kk == 0 else seq_refs[k + 1][lo - HALO:lo, :]
                    vals.append(jnp.concatenate([prev, seq_refs[k + 1][lo:lo + CHUNK, :]], axis=0))
                    k += 2
                else:
                    vals.append(seq_refs[k][lo:lo + CHUNK, :])
                    k += 1
            st_ref[kk] = s_cur
            res = fn(*vals, s_cur, *par_vals)
            out_ref[lo:lo + CHUNK, :] = res[0]
            s_cur = res[1]
            if kept_shape:
                for h, t in enumerate(res[2]):
                    own[2][kk, h] = t
        state[...] = s_cur
        car.wait_at(c == steps - 1, ex_ins, ex_outs, ex_sems)

    in_specs, operands = [], []
    for s in seqs:
        if s.kind == "halo":
            rb, w, cb = rows // HALO, s.width, s.col_block
            in_specs.append(pl.BlockSpec((HALO, w), lambda i, rb=rb, cb=cb: (jnp.maximum(i * rb - 1, 0), cb)))
            operands.append(s.array)
        in_specs.append(pl.BlockSpec((rows, s.width), lambda i, cb=s.col_block: (i, cb)))
        operands.append(s.array)
    for p in params:
        in_specs.append(_const_spec(p.shape))
        operands.append(p)
    outs = pl.pallas_call(
        body, grid=(steps,), in_specs=in_specs + car.in_specs,
        out_specs=[pl.BlockSpec((rows, out_width), lambda i: (i, 0)),
                   pl.BlockSpec((per_step, state_rows, LANES), lambda i: (i, 0, 0))]
        + ([pl.BlockSpec((per_step,) + tuple(kept_shape), lambda i: (i, 0, 0, 0))] if kept_shape else []) + car.out_specs,
        out_shape=[jax.ShapeDtypeStruct((seq_len, out_width), f32),
                   jax.ShapeDtypeStruct((nc, state_rows, LANES), f32)]
        + ([jax.ShapeDtypeStruct((nc,) + tuple(kept_shape), f32)] if kept_shape else []) + car.out_shape,
        scratch_shapes=[pltpu.VMEM((state_rows, LANES), f32)] + car.scratch,
        compiler_params=_params(("arbitrary",)), name=name)(*operands, *car.operands)
    return tuple(outs[:n_own]) + (outs[n_own:],)


def mixer_bwd(name, fn, seqs, params, states, d_out, seq_len, per_step=CHUNKS_PER_STEP, exchange=None, kept=None):
    nc = seq_len // CHUNK
    rows, steps = per_step * CHUNK, nc // per_step
    state_rows = states.shape[1]
    diff = [s for s in seqs if s.kind != "const"]
    halos = [s for s in diff if s.kind == "halo"]
    n_refs = sum(2 if s.kind == "halo" else 1 for s in seqs)
    n_par = len(params)
    n_kept = 0 if kept is None else 1
    car = Carried(exchange)

    def body(*refs):
        seq_refs, par_refs = refs[:n_refs], refs[n_refs:n_refs + n_par]
        st_ref, dout_ref = refs[n_refs + n_par:n_refs + n_par + 2]
        k0 = n_refs + n_par + 2
        kept_ref = refs[k0] if n_kept else None
        k0 += n_kept
        ex_ins = refs[k0:k0 + car.n]
        k0 += car.n
        dseq_refs, dpar_refs = refs[k0:k0 + len(diff)], refs[k0 + len(diff):k0 + len(diff) + n_par]
        k0 += len(diff) + n_par
        ex_outs = refs[k0:k0 + car.n]
        scratch = refs[k0 + car.n:]
        d_state, carries, ex_sems = scratch[0], scratch[1:1 + len(halos)], scratch[1 + len(halos):]
        i = pl.program_id(0)
        step = steps - 1 - i
        car.start_at(i == 0, ex_ins, ex_outs, ex_sems)

        @pl.when(i == 0)
        def _():
            d_state[...] = jnp.zeros_like(d_state)
            for r in list(carries) + list(dpar_refs):
                r[...] = jnp.zeros_like(r)

        par_vals = [p[...] for p in par_refs]
        d_s = d_state[...]
        d_par = [None] * n_par
        halo_ct = [r[...] for r in carries]
        for kk in range(per_step - 1, -1, -1):
            lo = kk * CHUNK
            dvals, consts, k = [], [], 0
            for s in seqs:
                if s.kind == "halo":
                    prev = jnp.where(step > 0, seq_refs[k][...], 0.0) if kk == 0 else seq_refs[k + 1][lo - HALO:lo, :]
                    dvals.append(jnp.concatenate([prev, seq_refs[k + 1][lo:lo + CHUNK, :]], axis=0))
                    k += 2
                elif s.kind == "tile":
                    dvals.append(seq_refs[k][lo:lo + CHUNK, :])
                    k += 1
                else:
                    consts.append(seq_refs[k][lo:lo + CHUNK, :])
                    k += 1
            nd = len(dvals)

            extra = [[kept_ref[kk, h] for h in range(kept.shape[1])]] if n_kept else []

            def call(*a, consts=consts, nd=nd, extra=extra):
                it_d, it_c = iter(a[:nd]), iter(consts)
                vals = [next(it_c) if s.kind == "const" else next(it_d) for s in seqs]
                return fn(*vals, *a[nd:], *extra)

            _, vjp = jax.vjp(call, *dvals, st_ref[kk], *par_vals)
            cts = vjp((dout_ref[lo:lo + CHUNK, :], d_s))
            hk = 0
            for j, s in enumerate(diff):
                if s.kind == "halo":
                    dseq_refs[j][lo:lo + CHUNK, :] = cts[j][HALO:, :]
                    dseq_refs[j][lo + CHUNK - HALO:lo + CHUNK, :] += halo_ct[hk]
                    halo_ct[hk] = cts[j][:HALO, :]
                    hk += 1
                else:
                    dseq_refs[j][lo:lo + CHUNK, :] = cts[j]
            d_s = cts[nd]
            for j in range(n_par):
                d_par[j] = cts[nd + 1 + j] if d_par[j] is None else d_par[j] + cts[nd + 1 + j]
        d_state[...] = d_s
        for r, v in zip(carries, halo_ct):
            r[...] = v
        for j in range(n_par):
            dpar_refs[j][...] += d_par[j]
        car.wait_at(i == steps - 1, ex_ins, ex_outs, ex_sems)

    in_specs, operands = [], []
    for s in seqs:
        if s.kind == "halo":
            rb, cb = rows // HALO, s.col_block
            in_specs.append(pl.BlockSpec((HALO, s.width), lambda i, rb=rb, cb=cb: (jnp.maximum((steps - 1 - i) * rb - 1, 0), cb)))
            operands.append(s.array)
        in_specs.append(pl.BlockSpec((rows, s.width), lambda i, cb=s.col_block: (steps - 1 - i, cb)))
        operands.append(s.array)
    for p in params:
        in_specs.append(_const_spec(p.shape))
        operands.append(p)
    in_specs.append(pl.BlockSpec((per_step, state_rows, LANES), lambda i: (steps - 1 - i, 0, 0)))
    in_specs.append(pl.BlockSpec((rows, d_out.shape[1]), lambda i: (steps - 1 - i, 0)))
    operands += [states, d_out]
    if n_kept:
        in_specs.append(pl.BlockSpec((per_step,) + kept.shape[1:], lambda i: (steps - 1 - i, 0, 0, 0)))
        operands.append(kept)
    outs = pl.pallas_call(
        body, grid=(steps,), in_specs=in_specs + car.in_specs,
        out_specs=[pl.BlockSpec((rows, s.width), lambda i: (steps - 1 - i, 0)) for s in diff]
        + [_const_spec(p.shape) for p in params] + car.out_specs,
        out_shape=[jax.ShapeDtypeStruct((seq_len, s.width), f32) for s in diff]
        + [jax.ShapeDtypeStruct(p.shape, f32) for p in params] + car.out_shape,
        scratch_shapes=[pltpu.VMEM((state_rows, LANES), f32)] + [pltpu.VMEM((HALO, s.width), f32) for s in halos]
        + car.scratch,
        compiler_params=_params(("arbitrary",)), name=name)(*operands, *car.operands)
    nd = len(diff)
    return outs[:nd], outs[nd:nd + n_par], outs[nd + n_par:]


TOK = 512


def inproj_fwd(name, x, nw, w):
    seq_len, n = x.shape[0], w.shape[1]

    def body(x_ref, nw_ref, w_ref, o_ref):
        o_ref[...] = mm(rmsnorm_f(x_ref[...], nw_ref[...]), w_ref[...])

    return pl.pallas_call(
        body, grid=(seq_len // TOK,),
        in_specs=[_row_spec(TOK, D_MODEL), _const_spec(nw.shape), _resident_spec(w.shape)],
        out_specs=_row_spec(TOK, n), out_shape=jax.ShapeDtypeStruct((seq_len, n), f32),
        compiler_params=_params(("arbitrary",)), name=name)(x, nw, w)


def inproj_bwd(name, x, nw, w, pieces, d_res):
    seq_len, n = x.shape[0], w.shape[1]
    widths = [p.shape[1] for p in pieces]
    assert sum(widths) == n
    k = len(pieces)

    def body(*refs):
        x_ref, nw_ref, w_ref = refs[:3]
        p_refs, dres_ref = refs[3:3 + k], refs[3 + k]
        dx_ref, dw_ref, dnw_ref = refs[4 + k:]

        @pl.when(pl.program_id(0) == 0)
        def _():
            dw_ref[...] = jnp.zeros_like(dw_ref)
            dnw_ref[...] = jnp.zeros_like(dnw_ref)

        h, vjp = jax.vjp(rmsnorm_f, x_ref[...], nw_ref[...])
        dh, off = jnp.zeros_like(h), 0
        for p_ref, wd in zip(p_refs, widths):
            g = p_ref[...]
            dh = dh + mm_nt(g, w_ref[:, off:off + wd])
            dw_ref[:, off:off + wd] += mm_tn(h, g)
            off += wd
        dx, dnw = vjp(dh)
        dx_ref[...] = dres_ref[...] + dx
        dnw_ref[...] += dnw

    return pl.pallas_call(
        body, grid=(seq_len // TOK,),
        in_specs=[_row_spec(TOK, D_MODEL), _const_spec(nw.shape), _resident_spec(w.shape)]
        + [_row_spec(TOK, wd) for wd in widths] + [_row_spec(TOK, D_MODEL)],
        out_specs=[_row_spec(TOK, D_MODEL), _resident_spec((D_MODEL, n)), _const_spec(nw.shape)],
        out_shape=[jax.ShapeDtypeStruct((seq_len, D_MODEL), f32), jax.ShapeDtypeStruct((D_MODEL, n), f32),
                   jax.ShapeDtypeStruct(nw.shape, f32)],
        compiler_params=_params(("arbitrary",)), name=name)(x, nw, w, *pieces, d_res)


def outproj_fwd(name, x, a, b, w):
    seq_len, wa, wb = x.shape[0], a.shape[1], b.shape[1]

    def body(x_ref, a_ref, b_ref, w_ref, o_ref):
        o_ref[...] = x_ref[...] + mm(a_ref[...], w_ref[:wa, :]) + mm(b_ref[...], w_ref[wa:, :])

    return pl.pallas_call(
        body, grid=(seq_len // TOK,),
        in_specs=[_row_spec(TOK, D_MODEL), _row_spec(TOK, wa), _row_spec(TOK, wb), _resident_spec(w.shape)],
        out_specs=_row_spec(TOK, D_MODEL), out_shape=jax.ShapeDtypeStruct((seq_len, D_MODEL), f32),
        compiler_params=_params(("arbitrary",)), name=name)(x, a, b, w)


def outproj_bwd(name, dy, a, b, w):
    seq_len, wa, wb = dy.shape[0], a.shape[1], b.shape[1]

    def body(dy_ref, a_ref, b_ref, w_ref, da_ref, db_ref, dw_ref):
        @pl.when(pl.program_id(0) == 0)
        def _():
            dw_ref[...] = jnp.zeros_like(dw_ref)

        g = dy_ref[...]
        da_ref[...] = mm_nt(g, w_ref[:wa, :])
        db_ref[...] = mm_nt(g, w_ref[wa:, :])
        dw_ref[:wa, :] += mm_tn(a_ref[...], g)
        dw_ref[wa:, :] += mm_tn(b_ref[...], g)

    return pl.pallas_call(
        body, grid=(seq_len // TOK,),
        in_specs=[_row_spec(TOK, D_MODEL), _row_spec(TOK, wa), _row_spec(TOK, wb), _resident_spec(w.shape)],
        out_specs=[_row_spec(TOK, wa), _row_spec(TOK, wb), _resident_spec(w.shape)],
        out_shape=[jax.ShapeDtypeStruct((seq_len, wa), f32), jax.ShapeDtypeStruct((seq_len, wb), f32),
                   jax.ShapeDtypeStruct(w.shape, f32)],
        compiler_params=_params(("arbitrary",)), name=name)(dy, a, b, w)


FF_BLOCK = D_FF // N_DEV


def mlp_fwd(name, x, nw, w_up, w_down, exchange=None):
    seq_len = x.shape[0]
    nt = seq_len // TOK
    car = Carried(exchange)

    def body(*refs):
        x_ref, nw_ref, up_ref, down_ref = refs[:4]
        ex_ins, (o_ref, relu_ref) = refs[4:4 + car.n], refs[4 + car.n:6 + car.n]
        ex_outs, ex_sems = refs[6 + car.n:6 + 2 * car.n], refs[6 + 2 * car.n:]
        i = pl.program_id(0)
        car.start_at(i == 0, ex_ins, ex_outs, ex_sems)
        xv = x_ref[...]
        h = rmsnorm_f(xv, nw_ref[...])
        acc = xv
        for d in range(N_DEV):
            r = jnp.maximum(mm(h, up_ref[d]), 0.0)
            relu_ref[d] = r.astype(_MXU)
            acc = acc + mm(r * r, down_ref[d])
        o_ref[...] = acc
        car.wait_at(i == nt - 1, ex_ins, ex_outs, ex_sems)

    outs = pl.pallas_call(
        body, grid=(nt,),
        in_specs=[_row_spec(TOK, D_MODEL), _const_spec(nw.shape), _resident_spec(w_up.shape), _resident_spec(w_down.shape)]
        + car.in_specs,
        out_specs=[_row_spec(TOK, D_MODEL), pl.BlockSpec((N_DEV, TOK, FF_BLOCK), lambda i: (0, i, 0))] + car.out_specs,
        out_shape=[jax.ShapeDtypeStruct((seq_len, D_MODEL), f32),
                   jax.ShapeDtypeStruct((N_DEV, seq_len, FF_BLOCK), _MXU)] + car.out_shape, scratch_shapes=car.scratch,
        compiler_params=_params(("arbitrary",)), name=name)(x, nw, w_up, w_down, *car.operands)
    return outs[0], outs[1], outs[2:]


MLP_SPLIT = 2


def mlp_bwd(name, x, nw, w_up, w_down, relu, dy, exchange=None):
    seq_len = x.shape[0]
    nt = seq_len // TOK
    per = N_DEV // MLP_SPLIT
    dh = d_up = d_down = ex_results = None
    for j in range(MLP_SPLIT):
        last, has_prev = j == MLP_SPLIT - 1, j > 0
        car = Carried(exchange if j == 0 else None)
        n_in = 6 + (3 if has_prev else 0)
        n_out = 4 if last else 3

        def body(*refs, last=last, has_prev=has_prev, car=car, n_in=n_in, n_out=n_out):
            x_ref, nw_ref, up_ref, down_ref, relu_ref, dy_ref = refs[:6]
            dh_prev_ref = refs[6] if has_prev else None
            ex_ins, own = refs[n_in:n_in + car.n], refs[n_in + car.n:n_in + car.n + n_out]
            ex_outs, ex_sems = refs[n_in + car.n + n_out:n_in + 2 * car.n + n_out], refs[n_in + 2 * car.n + n_out:]
            dup_ref, ddown_ref = own[-2:]
            i = pl.program_id(0)
            car.start_at(i == 0, ex_ins, ex_outs, ex_sems)

            @pl.when(i == 0)
            def _():
                dup_ref[...] = jnp.zeros_like(dup_ref)
                ddown_ref[...] = jnp.zeros_like(ddown_ref)
                if last:
                    own[1][...] = jnp.zeros_like(own[1])

            h, vjp = jax.vjp(rmsnorm_f, x_ref[...], nw_ref[...])
            g = dy_ref[...]
            dh_acc = dh_prev_ref[...] if has_prev else jnp.zeros_like(h)
            for d in range(per):
                r = relu_ref[d].astype(f32)
                da = mm_nt(g, down_ref[d]) * (2.0 * r)
                ddown_ref[d] += mm_tn(r * r, g)
                dup_ref[d] += mm_tn(h, da)
                dh_acc = dh_acc + mm_nt(da, up_ref[d])
            if last:
                dx, dnw = vjp(dh_acc)
                own[0][...] = g + dx
                own[1][...] += dnw
            else:
                own[0][...] = dh_acc
            car.wait_at(i == nt - 1, ex_ins, ex_outs, ex_sems)

        weights = lambda shape: pl.BlockSpec(shape, lambda i, j=j: (j, 0, 0), pipeline_mode=pl.Buffered(1))
        in_specs = [_row_spec(TOK, D_MODEL), _const_spec(nw.shape), weights((per, D_MODEL, FF_BLOCK)),
                    weights((per, FF_BLOCK, D_MODEL)), pl.BlockSpec((per, TOK, FF_BLOCK), lambda i, j=j: (j, i, 0)),
                    _row_spec(TOK, D_MODEL)]
        operands = [x, nw, w_up, w_down, relu, dy]
        if has_prev:
            in_specs += [_row_spec(TOK, D_MODEL), ANY, ANY]
            operands += [dh, d_up, d_down]
        first_outs = ([_row_spec(TOK, D_MODEL), _const_spec(nw.shape)] if last else [_row_spec(TOK, D_MODEL)])
        first_shapes = [jax.ShapeDtypeStruct((seq_len, D_MODEL), f32)] + ([jax.ShapeDtypeStruct(nw.shape, f32)] if last else [])
        outs = pl.pallas_call(
            body, grid=(nt,), in_specs=in_specs + car.in_specs,
            out_specs=first_outs + [weights((per, D_MODEL, FF_BLOCK)), weights((per, FF_BLOCK, D_MODEL))] + car.out_specs,
            out_shape=first_shapes + [jax.ShapeDtypeStruct(w_up.shape, f32), jax.ShapeDtypeStruct(w_down.shape, f32)]
            + car.out_shape,
            scratch_shapes=car.scratch,
            input_output_aliases=({7: n_out - 2, 8: n_out - 1} if has_prev else {}),
            compiler_params=_params(("arbitrary",)), name=name if j == 0 else f"{name}_{j}")(*operands, *car.operands)
        dh, d_up, d_down = outs[0], outs[n_out - 2], outs[n_out - 1]
        if j == 0:
            ex_results = outs[n_out:]
        if last:
            dx, dnw = outs[0], outs[1]
    return dx, d_up, d_down, dnw, ex_results


def final_loss(name, x, nw, target):
    seq_len = x.shape[0]

    def body(x_ref, nw_ref, t_ref, loss_ref, dx_ref, dnw_ref):
        @pl.when(pl.program_id(0) == 0)
        def _():
            loss_ref[...] = jnp.zeros_like(loss_ref)
            dnw_ref[...] = jnp.zeros_like(dnw_ref)

        y, vjp = jax.vjp(rmsnorm_f, x_ref[...], nw_ref[...])
        err = y - t_ref[...]
        loss_ref[...] += 0.5 * jnp.sum(jnp.mean(err * err, axis=-1, keepdims=True), axis=0, keepdims=True)
        dx, dnw = vjp(err * (1.0 / D_MODEL))
        dx_ref[...] = dx
        dnw_ref[...] += dnw

    return pl.pallas_call(
        body, grid=(seq_len // TOK,),
        in_specs=[_row_spec(TOK, D_MODEL), _const_spec(nw.shape), _row_spec(TOK, D_MODEL)],
        out_specs=[_const_spec((8, LANES)), _row_spec(TOK, D_MODEL), _const_spec(nw.shape)],
        out_shape=[jax.ShapeDtypeStruct((8, LANES), f32), jax.ShapeDtypeStruct((seq_len, D_MODEL), f32),
                   jax.ShapeDtypeStruct(nw.shape, f32)],
        compiler_params=_params(("arbitrary",)), name=name)(x, nw, target)


def _whole(a):
    return pl.BlockSpec(a.shape, lambda: (0,) * len(a.shape))


def s5_prep_fwd(name, raw):
    def body(*refs):
        outs = s5_prep(*[r[...] for r in refs[:7]])
        for o_ref, o in zip(refs[7:], outs):
            o_ref[...] = o

    shapes = [(1, S5_W)] * 2 + [(S5_CH, S5_W)] * 4
    return pl.pallas_call(
        body, in_specs=[_whole(a) for a in raw], out_specs=[pl.BlockSpec(s, lambda s=s: (0,) * len(s)) for s in shapes],
        out_shape=[jax.ShapeDtypeStruct(s, f32) for s in shapes],
        compiler_params=pltpu.CompilerParams(vmem_limit_bytes=VMEM_LIMIT), name=name)(*raw)


def s5_prep_bwd(name, raw, cts):
    def body(*refs):
        _, vjp = jax.vjp(s5_prep, *[r[...] for r in refs[:7]])
        grads = vjp(tuple(r[...] for r in refs[7:13]))
        for o_ref, g in zip(refs[13:], grads):
            o_ref[...] = g

    return pl.pallas_call(
        body, in_specs=[_whole(a) for a in list(raw) + list(cts)], out_specs=[_whole(a) for a in raw],
        out_shape=[jax.ShapeDtypeStruct(a.shape, f32) for a in raw],
        compiler_params=pltpu.CompilerParams(vmem_limit_bytes=VMEM_LIMIT), name=name)(*raw, *cts)


SCAN_SEG = 8
SCAN_LEN = TOK // SCAN_SEG


def _segment_major():
    r, t = _iota((TOK, TOK), 0), _iota((TOK, TOK), 1)
    return (t == (r % SCAN_SEG) * SCAN_LEN + r // SCAN_SEG).astype(f32)


def _store_powers(lr, li, pr_ref, pi_ref):
    qr, qi = lr, li
    for j in range(SCAN_LEN):
        pr_ref[j:j + 1, :] = qr
        pi_ref[j:j + 1, :] = qi
        qr, qi = lr * qr - li * qi, lr * qi + li * qr


def _tile_scan(xr_ref, xi_ref, lr, li, pr_ref, pi_ref, cr_ref, ci_ref, carry_re, carry_im, reverse):
    sign = -1.0 if reverse else 1.0
    ar = jnp.broadcast_to(lr, (SCAN_SEG, lr.shape[1]))
    ai = jnp.broadcast_to(sign * li, (SCAN_SEG, li.shape[1]))
    rows = lambda j: slice(j * SCAN_SEG, (j + 1) * SCAN_SEG)
    hr = hi = jnp.zeros_like(ar)
    for j in (range(SCAN_LEN - 1, -1, -1) if reverse else range(SCAN_LEN)):
        hr, hi = ar * hr - ai * hi + xr_ref[rows(j), :], ar * hi + ai * hr + xi_ref[rows(j), :]
        xr_ref[rows(j), :] = hr
        xi_ref[rows(j), :] = hi
    wr, wi = pr_ref[SCAN_LEN - 1:SCAN_LEN, :], sign * pi_ref[SCAN_LEN - 1:SCAN_LEN, :]
    er, ei = carry_re[0:1, :], carry_im[0:1, :]
    for s in (range(SCAN_SEG - 1, -1, -1) if reverse else range(SCAN_SEG)):
        cr_ref[s:s + 1, :] = er
        ci_ref[s:s + 1, :] = ei
        er, ei = hr[s:s + 1, :] + wr * er - wi * ei, hi[s:s + 1, :] + wr * ei + wi * er
    carry_re[0:1, :] = er
    carry_im[0:1, :] = ei
    cr, ci = cr_ref[...], ci_ref[...]
    for j in range(SCAN_LEN):
        k = SCAN_LEN - 1 - j if reverse else j
        qr, qi = pr_ref[k:k + 1, :], sign * pi_ref[k:k + 1, :]
        xr_ref[rows(j), :] += qr * cr - qi * ci
        xi_ref[rows(j), :] += qr * ci + qi * cr


_SCAN_SCRATCH = [pltpu.VMEM((HALO, S5_W), f32), pltpu.VMEM((HALO, S5_W), f32),
                 pltpu.VMEM((SCAN_LEN, S5_W), f32), pltpu.VMEM((SCAN_LEN, S5_W), f32),
                 pltpu.VMEM((SCAN_SEG, S5_W), f32), pltpu.VMEM((SCAN_SEG, S5_W), f32)]


def s5_fwd(name, proj, u_block, lam_re, lam_im, bblk_re, bblk_im, cblk_re, cblk_im, d_skip, w_glu, b_glu):
    seq_len = proj.shape[0]

    def body(u_ref, lr_ref, li_ref, br_ref, bi_ref, cr_ref, ci_ref, d_ref, wg_ref, bg_ref,
             o_ref, hr_ref, hi_ref, carry_re, carry_im, pw_re, pw_im, cb_re, cb_im):
        lr, li = lr_ref[...], li_ref[...]

        @pl.when(pl.program_id(0) == 0)
        def _():
            carry_re[...] = jnp.zeros_like(carry_re)
            carry_im[...] = jnp.zeros_like(carry_im)
            _store_powers(lr, li, pw_re, pw_im)

        perm = _segment_major()
        u = _exact01(u_ref[...], perm, ((1,), (0,)), True)
        hr_ref[...] = mm(u, br_ref[...])
        hi_ref[...] = mm(u, bi_ref[...])
        _tile_scan(hr_ref, hi_ref, lr, li, pw_re, pw_im, cb_re, cb_im, carry_re, carry_im, reverse=False)
        out = s5_out_fn(hr_ref[...], hi_ref[...], u, cr_ref[...], ci_ref[...], d_ref[...],
                        wg_ref[...].astype(f32), bg_ref[...])
        o_ref[...] = _exact01(out, perm, ((0,), (0,)), True)

    consts = [lam_re, lam_im, bblk_re, bblk_im, cblk_re, cblk_im, d_skip, w_glu, b_glu]
    return pl.pallas_call(
        body, grid=(seq_len // TOK,),
        in_specs=[_row_spec(TOK, S5_CH, u_block)] + [_const_spec(a.shape) for a in consts],
        out_specs=[_row_spec(TOK, S5_CH), _row_spec(TOK, S5_W), _row_spec(TOK, S5_W)],
        out_shape=[jax.ShapeDtypeStruct((seq_len, S5_CH), f32), jax.ShapeDtypeStruct((seq_len, S5_W), f32),
                   jax.ShapeDtypeStruct((seq_len, S5_W), f32)],
        scratch_shapes=_SCAN_SCRATCH,
        compiler_params=_params(("arbitrary",)), name=name)(proj, *consts)


def s5_bwd(name, proj, u_block, h_re, h_im, d_out, lam_re, lam_im, bblk_re, bblk_im, cblk_re, cblk_im, d_skip, w_glu, b_glu):
    seq_len = proj.shape[0]
    nt = seq_len // TOK
    consts = [lam_re, lam_im, bblk_re, bblk_im, cblk_re, cblk_im, d_skip, w_glu, b_glu]

    def body(u_ref, hr_ref, hi_ref, pr_ref, pi_ref, dout_ref, lr_ref, li_ref, br_ref, bi_ref, cr_ref, ci_ref, d_ref, wg_ref, bg_ref,
             du_ref, dlr_ref, dli_ref, dbr_ref, dbi_ref, dcr_ref, dci_ref, dd_ref, dwg_ref, dbg_ref,
             gr_ref, gi_ref, carry_re, carry_im, pw_re, pw_im, cb_re, cb_im):
        i = pl.program_id(0)
        tile = nt - 1 - i
        lr, li = lr_ref[...], li_ref[...]

        @pl.when(i == 0)
        def _():
            for r in (carry_re, carry_im, dlr_ref, dli_ref, dbr_ref, dbi_ref, dcr_ref, dci_ref, dd_ref, dwg_ref, dbg_ref):
                r[...] = jnp.zeros_like(r)
            _store_powers(lr, li, pw_re, pw_im)

        perm = _segment_major()
        u = _exact01(u_ref[...], perm, ((1,), (0,)), True)
        d_out_p = _exact01(dout_ref[...], perm, ((1,), (0,)), True)
        h_r, h_i = hr_ref[...], hi_ref[...]
        _, vjp = jax.vjp(s5_out_fn, h_r, h_i, u, cr_ref[...], ci_ref[...], d_ref[...], wg_ref[...].astype(f32), bg_ref[...])
        ghr, ghi, du, dcr, dci, dd, dwg, dbg = vjp(d_out_p)
        gr_ref[...] = ghr
        gi_ref[...] = ghi
        _tile_scan(gr_ref, gi_ref, lr, li, pw_re, pw_im, cb_re, cb_im, carry_re, carry_im, reverse=True)
        g_r, g_i = gr_ref[...], gi_ref[...]
        keep = jnp.where(tile > 0, 1.0, 0.0)
        top = _iota((SCAN_SEG, 1), 0) == 0

        def earlier(h, before_ref):
            head = jnp.where(top, before_ref[HALO - 1:HALO, :] * keep, _roll(h[TOK - SCAN_SEG:, :], 1, 0))
            return jnp.concatenate([head, h[:TOK - SCAN_SEG, :]], axis=0)

        p_r, p_i = earlier(h_r, pr_ref), earlier(h_i, pi_ref)
        dlr_ref[...] += jnp.sum(g_r * p_r + g_i * p_i, axis=0, keepdims=True)
        dli_ref[...] += jnp.sum(g_i * p_r - g_r * p_i, axis=0, keepdims=True)
        du_p = du + mm_nt(g_r, br_ref[...]) + mm_nt(g_i, bi_ref[...])
        du_ref[...] = _exact01(du_p, perm, ((0,), (0,)), True)
        dbr_ref[...] += mm_tn(u, g_r)
        dbi_ref[...] += mm_tn(u, g_i)
        dcr_ref[...] += dcr
        dci_ref[...] += dci
        dd_ref[...] += dd
        dwg_ref[...] += dwg
        dbg_ref[...] += dbg

    rev = lambda cols, cb=0: pl.BlockSpec((TOK, cols), lambda i, cb=cb: (nt - 1 - i, cb))
    prev = pl.BlockSpec((HALO, S5_W), lambda i: (jnp.maximum((nt - 1 - i) * (TOK // HALO) - 1, 0), 0))
    outs = pl.pallas_call(
        body, grid=(nt,),
        in_specs=[rev(S5_CH, u_block), rev(S5_W), rev(S5_W), prev, prev, rev(S5_CH)] + [_const_spec(a.shape) for a in consts],
        out_specs=[rev(S5_CH)] + [_const_spec(a.shape) for a in consts],
        out_shape=[jax.ShapeDtypeStruct((seq_len, S5_CH), f32)] + [jax.ShapeDtypeStruct(a.shape, f32) for a in consts],
        scratch_shapes=[pltpu.VMEM((TOK, S5_W), f32), pltpu.VMEM((TOK, S5_W), f32)] + _SCAN_SCRATCH,
        compiler_params=_params(("arbitrary",)), name=name)(proj, h_re, h_im, h_re, h_im, d_out, *consts)
    return outs[0], outs[1:]


ANY = pl.BlockSpec(memory_space=pl.ANY)


def _mesh_position():
    x, y, c = lax.axis_index("x"), lax.axis_index("y"), lax.axis_index("c")
    return x, y, c, 4 * x + 2 * y + c


def _peer(x, y, c, r):
    px = 1 - x if r & 4 else x
    py = 1 - y if r & 2 else y
    pc = 1 - c if r & 1 else c
    return (px, py, pc), 4 * px + 2 * py + pc


class Exchange:
    def __init__(self, arrays, gather):
        self.arrays, self.gather, self.n = list(arrays), gather, len(arrays)
        self.in_specs = [ANY] * self.n
        self.out_specs = [ANY] * self.n
        shapes = [((N_DEV,) + a.shape) if gather else a.shape for a in self.arrays]
        self.out_shape = [jax.ShapeDtypeStruct(s, a.dtype) for s, a in zip(shapes, self.arrays)]
        self.scratch = [pltpu.SemaphoreType.DMA((self.n, N_DEV - 1)), pltpu.SemaphoreType.DMA((self.n, N_DEV - 1)),
                        pltpu.SemaphoreType.DMA((self.n,))]

    def _copies(self, ins, outs, sems, landed):
        send_sems, recv_sems, local_sems = sems
        x, y, c, me = _mesh_position()
        local, remote = [], []
        for i in range(self.n):
            mine = ins[i] if self.gather else ins[i].at[me]
            local.append(pltpu.make_async_copy(mine, outs[i].at[me], local_sems.at[i]))
            for r in range(1, N_DEV):
                peer, peer_idx = _peer(x, y, c, r)
                remote.append(pltpu.make_async_remote_copy(
                    src_ref=ins[i] if self.gather else ins[i].at[peer_idx],
                    dst_ref=outs[i].at[peer_idx if landed else me],
                    send_sem=send_sems.at[i, r - 1], recv_sem=recv_sems.at[i, r - 1],
                    device_id=peer, device_id_type=pl.DeviceIdType.MESH))
        return local, remote

    def start(self, ins, outs, sems):
        local, remote = self._copies(ins, outs, sems, landed=False)
        for cp in local + remote:
            cp.start()

    def wait(self, ins, outs, sems):
        local, remote = self._copies(ins, outs, sems, landed=True)
        for cp in remote:
            cp.wait_recv()
            cp.wait_send()
        for cp in local:
            cp.wait()

    def run(self, name):
        n = self.n

        def body(*refs):
            ins, outs, sems = refs[:n], refs[n:2 * n], refs[2 * n:]
            self.start(ins, outs, sems)
            self.wait(ins, outs, sems)

        return pl.pallas_call(body, in_specs=self.in_specs, out_specs=self.out_specs, out_shape=self.out_shape,
                              scratch_shapes=self.scratch, name=name)(*self.arrays)


class Carried:
    def __init__(self, exchange):
        self.ex = exchange
        self.n = exchange.n if exchange else 0
        self.in_specs = exchange.in_specs if exchange else []
        self.out_specs = exchange.out_specs if exchange else []
        self.out_shape = exchange.out_shape if exchange else []
        self.scratch = exchange.scratch if exchange else []
        self.operands = exchange.arrays if exchange else []

    def start_at(self, first, ins, outs, sems):
        if self.ex is not None:
            @pl.when(first)
            def _():
                self.ex.start(ins, outs, sems)

    def wait_at(self, last, ins, outs, sems):
        if self.ex is not None:
            @pl.when(last)
            def _():
                self.ex.wait(ins, outs, sems)


def adamw(name, parts, w, m, v):
    rows, cols = w.shape
    tr = rows
    for cand in (512, 256, 128, 64, 32, 16, 8):
        if rows * cols * 4 > (1 << 20) and rows % cand == 0 and cand * cols * 4 <= (1 << 20):
            tr = cand
            break

    def body(p_ref, w_ref, m_ref, v_ref, g_ref, d_ref, nm_ref, nv_ref):
        g = p_ref[0].astype(f32)
        for s in range(1, N_DEV):
            g = g + p_ref[s].astype(f32)
        nm = ADAM_B1 * m_ref[...] + (1.0 - ADAM_B1) * g
        nv = ADAM_B2 * v_ref[...] + (1.0 - ADAM_B2) * (g * g)
        m_hat = nm / (1.0 - ADAM_B1 ** ADAM_STEP)
        v_hat = nv / (1.0 - ADAM_B2 ** ADAM_STEP)
        g_ref[...] = g
        d_ref[...] = -ADAM_LR * (m_hat / (jnp.sqrt(v_hat) + ADAM_EPS) + ADAM_WD * w_ref[...])
        nm_ref[...] = nm
        nv_ref[...] = nv

    blk = pl.BlockSpec((tr, cols), lambda i: (i, 0))
    return pl.pallas_call(
        body, grid=(rows // tr,),
        in_specs=[pl.BlockSpec((N_DEV, tr, cols), lambda i: (0, i, 0)), blk, blk, blk],
        out_specs=[blk] * 4, out_shape=[jax.ShapeDtypeStruct((rows, cols), f32)] * 4,
        compiler_params=_params(("arbitrary",)), name=name)(parts, w, m, v)


WEIGHTS = ['l0_norm_mix', 'l0_w_in', 'ssd_conv_w', 'ssd_conv_b', 'ssd_dt_bias', 'ssd_A_log', 'ssd_D', 'ssd_norm_w',
           'l0_w_out', 'l0_norm_mlp', 'l0_w_up', 'l0_w_down', 'l1_norm_mix', 'l1_w_in', 'gdn_conv_w', 'gdn_A_log',
           'gdn_dt_bias', 'gdn_norm_w', 's5_A_re', 's5_A_im', 's5_log_step', 's5_B_re', 's5_B_im', 's5_C_re', 's5_C_im',
           's5_D', 's5_w_glu', 's5_b_glu', 'l1_w_out', 'l1_norm_mlp', 'l1_w_up', 'l1_w_down', 'final_norm']
SHARDED = ['l0_w_in', 'l0_w_out', 'l0_w_up', 'l0_w_down', 'l1_w_in', 's5_w_glu', 'l1_w_out', 'l1_w_up', 'l1_w_down',
           'ssd_conv_w', 'gdn_conv_w']
F32_GATHER = ('ssd_conv_w', 'gdn_conv_w')
REPLICATED = [n for n in WEIGHTS if n not in SHARDED]
INPUTS = ['x'] + WEIGHTS + ['loss_target'] + ['m_' + n for n in WEIGHTS] + ['v_' + n for n in WEIGHTS]


def _row(v):
    return v.reshape(1, -1)


def _pad_lanes(v, offset=0):
    return jnp.pad(v, (offset, LANES - offset - v.shape[0])).reshape(1, LANES)


def _cols_to_blocks(g):
    return g.reshape(g.shape[0], N_DEV, -1).transpose(1, 0, 2)


def _blocks_to_cols(g):
    return g.transpose(1, 0, 2).reshape(g.shape[1], -1)


def _pack(arrays):
    parts, slots, at = [], [], 0
    for a in arrays:
        n = a.size
        rows = -(-n // (8 * LANES)) * 8
        parts.append(jnp.pad(a.reshape(-1), (0, rows * LANES - n)).reshape(rows, LANES))
        slots.append((at, rows, n, a.shape))
        at += rows
    return jnp.concatenate(parts, axis=0), slots


def _unpack(buf, slots):
    return [buf[at:at + rows].reshape(-1)[:n].reshape(shape) for at, rows, n, shape in slots]


def kernel(*args):
    a = dict(zip(INPUTS, args, strict=True))
    seq_len = a['x'].shape[1]
    x0 = a['x'].reshape(seq_len, D_MODEL)
    target = a['loss_target'].reshape(seq_len, D_MODEL)

    shard = {n: a[n] if n in F32_GATHER else a[n].astype(_MXU) for n in SHARDED}
    first = ['l0_w_in', 'ssd_conv_w', 'gdn_conv_w']
    g = dict(zip(first, Exchange([shard[n] for n in first], gather=True).run("gather_first")))
    w_nat = _blocks_to_cols(g['l0_w_in'])
    win0 = jnp.concatenate([w_nat[:, :2048], w_nat[:, 2560:3584], w_nat[:, 2048:2560], w_nat[:, 3584:3592],
                            jnp.zeros((D_MODEL, IN0_PAD - IN0_W), _MXU)], axis=1)
    ssd_cw, gdn_cw = _blocks_to_cols(g['ssd_conv_w']), _blocks_to_cols(g['gdn_conv_w'])

    half = RET_D // 2
    inv = ROPE_THETA ** (-jnp.arange(half, dtype=f32) / half)
    ang = jnp.arange(seq_len, dtype=f32)[:, None] * inv[None, :]
    cos, sin = jnp.cos(ang), jnp.sin(ang)
    cos, sin = jnp.concatenate([cos, cos], axis=1), jnp.concatenate([-sin, sin], axis=1)
    ssd_params = [ssd_cw, _row(a['ssd_conv_b']), _pad_lanes(a['ssd_dt_bias']), _pad_lanes(a['ssd_A_log']),
                  _pad_lanes(a['ssd_D']), _row(a['ssd_norm_w'])]
    gdn_params = [gdn_cw, _pad_lanes(a['gdn_A_log'], GDN_GCOL), _pad_lanes(a['gdn_dt_bias'], GDN_GCOL), _row(a['gdn_norm_w'])]
    s5_raw = [a['s5_A_re'].reshape(1, S5_W), a['s5_A_im'].reshape(1, S5_W), _pad_lanes(a['s5_log_step']),
              a['s5_B_re'].transpose(2, 0, 1).reshape(S5_GROUP, S5_W), a['s5_B_im'].transpose(2, 0, 1).reshape(S5_GROUP, S5_W),
              a['s5_C_re'].transpose(1, 0, 2).reshape(S5_GROUP, S5_W), a['s5_C_im'].transpose(1, 0, 2).reshape(S5_GROUP, S5_W)]
    s5_d, s5_bg = _row(a['s5_D']), _row(a['s5_b_glu'])
    nw = {n: _row(a[n]) for n in ('l0_norm_mix', 'l0_norm_mlp', 'l1_norm_mix', 'l1_norm_mlp', 'final_norm')}

    proj0 = inproj_fwd("l0_in", x0, nw['l0_norm_mix'], win0)
    ret_seqs = [Seq(proj0, 512, 0), Seq(proj0, 512, 1), Seq(proj0, 512, 2), Seq(proj0, 512, 3),
                Seq(cos, LANES, 0, "const"), Seq(sin, LANES, 0, "const")]
    later = ['l0_w_up']
    ret_out, ret_st, got = mixer_fwd("ret_fwd", ret_chunk, ret_seqs, [], 512, RET_HEADS * RET_D, seq_len,
                                     per_step=LIGHT_CHUNKS_PER_STEP, exchange=Exchange([shard[n] for n in later], gather=True))
    g.update(zip(later, got))
    ssd_seqs = [Seq(proj0, 512, 6), Seq(proj0, 1024, 2, "halo"), Seq(proj0, LANES, 28)]
    later = ['l0_w_out', 'l0_w_down', 's5_w_glu']
    ssd_out, ssd_st, got = mixer_fwd("ssd_fwd", ssd_chunk, ssd_seqs, ssd_params, SSD_INNER, SSD_INNER, seq_len,
                                     per_step=LIGHT_CHUNKS_PER_STEP, exchange=Exchange([shard[n] for n in later], gather=True))
    g.update(zip(later, got))
    wout0 = g['l0_w_out'].reshape(D_MODEL, D_MODEL)
    wglu = g['s5_w_glu'].reshape(S5_CH, S5_CH)
    x1 = outproj_fwd("l0_out", x0, ret_out, ssd_out, wout0)
    later = ['l1_w_in', 'l1_w_out', 'l1_w_up', 'l1_w_down']
    x2, relu0, got = mlp_fwd("l0_mlp", x1, nw['l0_norm_mlp'], g['l0_w_up'], g['l0_w_down'],
                      exchange=Exchange([shard[n] for n in later], gather=True))
    g.update(zip(later, got))
    w_nat = g['l1_w_in'].reshape(D_MODEL, IN1_W)
    win1 = jnp.concatenate([w_nat[:, :3072], w_nat[:, 3084:3340], w_nat[:, 3072:3084],
                            jnp.zeros((D_MODEL, IN1_PAD - IN1_W), _MXU)], axis=1)
    wout1 = g['l1_w_out'].reshape(D_MODEL, D_MODEL)
    proj1 = inproj_fwd("l1_in", x2, nw['l1_norm_mix'], win1)
    gdn_seqs = [Seq(proj1, 3 * GDN_W, 0, "halo"), Seq(proj1, GDN_W, 3), Seq(proj1, LANES, 26)]
    gdn_out, gdn_st, gdn_inv, _ = mixer_fwd("gdn_fwd", functools.partial(gdn_chunk, keep_inverses=True), gdn_seqs, gdn_params,
                                            GDN_W, GDN_W, seq_len, kept_shape=(GDN_HEADS, CHUNK, CHUNK))
    prep = s5_prep_fwd("s5_prep", s5_raw)
    s5_out, h_re, h_im = s5_fwd("s5_fwd", proj1, 12, *prep, s5_d, wglu, s5_bg)
    x3 = outproj_fwd("l1_out", x2, gdn_out, s5_out, wout1)
    x4, relu1, _ = mlp_fwd("l1_mlp", x3, nw['l1_norm_mlp'], g['l1_w_up'], g['l1_w_down'])
    loss_blk, dx4, d_final = final_loss("final_loss", x4, nw['final_norm'], target)

    parts = {}
    dx3, d_up1, d_down1, d_nmlp1, _ = mlp_bwd("l1_mlp_bwd", x3, nw['l1_norm_mlp'], g['l1_w_up'], g['l1_w_down'], relu1, dx4)
    d_gdn, d_s5, d_wout1 = outproj_bwd("l1_out_bwd", dx3, gdn_out, s5_out, wout1)
    d_u, s5_g = s5_bwd("s5_bwd", proj1, 12, h_re, h_im, d_s5, *prep, s5_d, wglu, s5_bg)
    s5_raw_g = s5_prep_bwd("s5_prep_bwd", s5_raw, s5_g[:6])
    ready = {'l1_w_up': d_up1, 'l1_w_down': d_down1, 'l1_w_out': d_wout1.reshape(N_DEV, -1, D_MODEL),
             's5_w_glu': s5_g[7].reshape(N_DEV, -1, S5_CH)}
    (d_qkv, d_z1, d_ba), gdn_pg, got = mixer_bwd("gdn_bwd", gdn_chunk, gdn_seqs, gdn_params, gdn_st, d_gdn, seq_len,
                                                 exchange=Exchange(list(ready.values()), gather=False), kept=gdn_inv)
    parts.update(zip(ready, got))
    dx2, d_win1, d_nmix1 = inproj_bwd("l1_in_bwd", x2, nw['l1_norm_mix'], win1, [d_qkv, d_z1, d_u, d_ba], dx3)
    d_win1 = jnp.concatenate([d_win1[:, :3072], d_win1[:, 3328:3340], d_win1[:, 3072:3328]], axis=1)
    ready = {'l1_w_in': d_win1.reshape(N_DEV, -1, IN1_W), 'gdn_conv_w': _cols_to_blocks(gdn_pg[0])}
    dx1, d_up0, d_down0, d_nmlp0, got = mlp_bwd("l0_mlp_bwd", x1, nw['l0_norm_mlp'], g['l0_w_up'], g['l0_w_down'], relu0, dx2,
                                                exchange=Exchange(list(ready.values()), gather=False))
    parts.update(zip(ready, got))
    d_ret, d_ssd, d_wout0 = outproj_bwd("l0_out_bwd", dx1, ret_out, ssd_out, wout0)
    ready = {'l0_w_up': d_up0}
    d_qkvg, _, got = mixer_bwd("ret_bwd", ret_chunk, ret_seqs, [], ret_st, d_ret, seq_len,
                               per_step=LIGHT_CHUNKS_PER_STEP, exchange=Exchange(list(ready.values()), gather=False))
    parts.update(zip(ready, got))
    ready = {'l0_w_down': d_down0, 'l0_w_out': d_wout0.reshape(N_DEV, -1, D_MODEL)}
    (d_z0, d_xbc, d_dt), ssd_pg, got = mixer_bwd("ssd_bwd", ssd_chunk, ssd_seqs, ssd_params, ssd_st, d_ssd, seq_len,
                                                 per_step=LIGHT_CHUNKS_PER_STEP,
                                                 exchange=Exchange(list(ready.values()), gather=False))
    parts.update(zip(ready, got))
    dx0, d_win0, d_nmix0 = inproj_bwd("l0_in_bwd", x0, nw['l0_norm_mix'], win0, list(d_qkvg) + [d_xbc, d_z0, d_dt], dx1)

    d_win0 = jnp.concatenate([d_win0[:, :2048], d_win0[:, 3072:3584], d_win0[:, 2048:3072], d_win0[:, 3584:3592]], axis=1)
    ready = {'l0_w_in': _cols_to_blocks(d_win0).astype(jnp.bfloat16), 'ssd_conv_w': _cols_to_blocks(ssd_pg[0])}
    from_b = lambda t: t.reshape(S5_GROUP, S5_GROUPS, S5_STATE).transpose(1, 2, 0)
    from_c = lambda t: t.reshape(S5_GROUP, S5_GROUPS, S5_STATE).transpose(1, 0, 2)
    replicated_g = {
        'l0_norm_mix': d_nmix0, 'ssd_conv_b': ssd_pg[1], 'ssd_dt_bias': ssd_pg[2][0, :SSD_HEADS], 'ssd_A_log': ssd_pg[3][0, :SSD_HEADS],
        'ssd_D': ssd_pg[4][0, :SSD_HEADS], 'ssd_norm_w': ssd_pg[5], 'l0_norm_mlp': d_nmlp0, 'l1_norm_mix': d_nmix1,
        'gdn_A_log': gdn_pg[1][0, GDN_GCOL:GDN_GCOL + GDN_HEADS], 'gdn_dt_bias': gdn_pg[2][0, GDN_GCOL:GDN_GCOL + GDN_HEADS],
        'gdn_norm_w': gdn_pg[3], 's5_A_re': s5_raw_g[0], 's5_A_im': s5_raw_g[1], 's5_log_step': s5_raw_g[2][0, :S5_GROUPS],
        's5_B_re': from_b(s5_raw_g[3]), 's5_B_im': from_b(s5_raw_g[4]), 's5_C_re': from_c(s5_raw_g[5]), 's5_C_im': from_c(s5_raw_g[6]),
        's5_D': s5_g[6], 's5_b_glu': s5_g[8], 'l1_norm_mlp': d_nmlp1, 'final_norm': d_final}
    replicated_g = {n: replicated_g[n].reshape(a[n].shape) for n in REPLICATED}

    parts.update(zip(ready, Exchange(list(ready.values()), gather=False).run("scatter_last")))
    packed_g, slots = _pack([replicated_g[n] for n in REPLICATED])
    (packed_parts,) = Exchange([packed_g], gather=True).run("gather_small_grads")
    results = {}
    for n in SHARDED:
        results[n] = adamw("adamw_" + n, parts[n], a[n], a['m_' + n], a['v_' + n])
    packed = [_pack([a[pre + n] for n in REPLICATED])[0] for pre in ('', 'm_', 'v_')]
    small = [_unpack(t, slots) for t in adamw("adamw_small", packed_parts, *packed)]
    for i, n in enumerate(REPLICATED):
        results[n] = tuple(small[k][i] for k in range(4))

    loss = lax.psum(loss_blk[0, 0], ("x", "y", "c"))
    grad_x = dx0.reshape(a['x'].shape)
    return (loss, grad_x, *[results[n][0] for n in WEIGHTS], *[results[n][1] for n in WEIGHTS],
            *[results[n][2] for n in WEIGHTS], *[results[n][3] for n in WEIGHTS])
```

```python
import functools
import math

import numpy as np
import jax
import jax.numpy as jnp
from jax import lax
from jax.experimental import pallas as pl
from jax.experimental.pallas import tpu as pltpu

f32 = jnp.float32
_MXU = jnp.bfloat16
HI = lax.Precision.HIGHEST

D_MODEL = 1024
CHUNK = 64
EPS = 1e-6
N_DEV = 8
LANES = 128
HALO = 8
CONV_WIDTH = 4

RET_HEADS, RET_D = 4, 128
SSD_HEADS, SSD_P, SSD_N, SSD_GROUPS = 8, 64, 128, 2
SSD_INNER = SSD_HEADS * SSD_P
GDN_HEADS, GDN_D = 6, 128
GDN_W = GDN_HEADS * GDN_D
S5_CH, S5_GROUP, S5_GROUPS, S5_STATE = 256, 16, 16, 64
S5_W = S5_GROUPS * S5_STATE
D_FF = 4096
ROPE_THETA = 10000.0

IN0_W = 3592
IN0_PAD = 3712
IN1_W = 3340
IN1_PAD = 3456

ADAM_LR, ADAM_B1, ADAM_B2, ADAM_EPS, ADAM_WD, ADAM_STEP = 0.001, 0.9, 0.999, 1e-08, 0.01, 10

VMEM_LIMIT = 56 * 1024 * 1024


def _dot(a, b, dims):
    return lax.dot_general(a.astype(_MXU), b.astype(_MXU), (dims, ((), ())), preferred_element_type=f32)


@jax.custom_vjp
def mm(a, b):
    return _dot(a, b, ((1,), (0,)))


@jax.custom_vjp
def mm_nt(a, b):
    return _dot(a, b, ((1,), (1,)))


@jax.custom_vjp
def mm_tn(a, b):
    return _dot(a, b, ((0,), (0,)))


mm.defvjp(lambda a, b: (mm(a, b), (a, b)), lambda r, g: (mm_nt(g, r[1]), mm_tn(r[0], g)))
mm_nt.defvjp(lambda a, b: (mm_nt(a, b), (a, b)), lambda r, g: (mm(g, r[1]), mm_tn(g, r[0])))
mm_tn.defvjp(lambda a, b: (mm_tn(a, b), (a, b)), lambda r, g: (mm_nt(r[1], g), mm(r[0], g)))


def mmh(a, b):
    return jnp.dot(a, b, precision=HI, preferred_element_type=f32)


def _exact01(x, m01, dims, m_first):
    hi = x.astype(jnp.bfloat16)
    r = x - hi.astype(f32)
    mid = r.astype(jnp.bfloat16)
    lo = (r - mid.astype(f32)).astype(jnp.bfloat16)
    m = m01.astype(jnp.bfloat16)
    dot = lambda p: lax.dot_general(m, p, (dims, ((), ())), preferred_element_type=f32) if m_first else \
        lax.dot_general(p, m, (dims, ((), ())), preferred_element_type=f32)
    return dot(hi) + dot(mid) + dot(lo)


@jax.custom_vjp
def spread01(x, sel):
    return _exact01(x, sel, ((1,), (0,)), False)


spread01.defvjp(lambda x, sel: (spread01(x, sel), sel),
                lambda sel, g: (_exact01(g, sel, ((1,), (1,)), False), jnp.zeros_like(sel)))


@jax.custom_vjp
def cumsum01(tril, x):
    return _exact01(x, tril, ((1,), (0,)), True)


cumsum01.defvjp(lambda tril, x: (cumsum01(tril, x), tril),
                lambda tril, g: (jnp.zeros_like(tril), _exact01(g, tril, ((0,), (0,)), True)))


def _roll(x, shift, axis):
    return pltpu.roll(x, shift, axis)


@functools.partial(jax.custom_vjp, nondiff_argnums=(1,))
def roll_rows(x, s):
    return _roll(x, s, 0) if s else x


roll_rows.defvjp(lambda x, s: (roll_rows(x, s), None),
                 lambda s, _, g: ((_roll(g, g.shape[0] - s, 0) if s else g),))


@jax.custom_vjp
def roll_half(x):
    return _roll(x, x.shape[-1] // 2, 1)


roll_half.defvjp(lambda x: (roll_half(x), None), lambda _, g: (roll_half(g),))


def _iota(shape, axis):
    return lax.broadcasted_iota(jnp.int32, shape, axis)


def silu(x):
    return x * jax.nn.sigmoid(x)


def softplus(x):
    return jnp.maximum(x, 0.0) + jnp.log(1.0 + jnp.exp(-jnp.abs(x)))


def rmsnorm_f(x, w):
    return x * lax.rsqrt(jnp.mean(x * x, axis=-1, keepdims=True) + EPS) * w


def unit_rms(x):
    return x * lax.rsqrt(jnp.mean(x * x, axis=-1, keepdims=True) + EPS)


def _causal(n, strict=False):
    r, c = _iota((n, n), 0), _iota((n, n), 1)
    return (r > c) if strict else (r >= c)


def _tril_ones(n):
    return _causal(n).astype(f32)


def _conv_rows(xe, w):
    acc = w[CONV_WIDTH - 1:CONV_WIDTH, :] * xe
    for j in range(CONV_WIDTH - 1):
        acc = acc + w[j:j + 1, :] * roll_rows(xe, CONV_WIDTH - 1 - j)
    return acc[HALO:, :]


_RET_LOG_GAMMA = [float(np.log(np.float32(1.0) - np.float32(2.0) ** np.float32(-5.0 - h))) for h in range(RET_HEADS)]


def ret_chunk(q, k, v, gate, cos, sin, state):
    c = q.shape[0]
    idx = _iota((c, 1), 0).astype(f32)
    diff = (_iota((c, c), 0) - _iota((c, c), 1)).astype(f32)
    causal = _causal(c)
    hs = range(RET_HEADS)
    cols = [slice(h * RET_D, (h + 1) * RET_D) for h in hs]
    lg = _RET_LOG_GAMMA
    qh = [(q[:, s] * cos + roll_half(q[:, s]) * sin) * (RET_D ** -0.5) for s in cols]
    kh = [k[:, s] * cos + roll_half(k[:, s]) * sin for s in cols]
    vh = [v[:, s] for s in cols]
    sh = [state[s, :] for s in cols]
    scores = [mm_nt(qh[h], kh[h]) * jnp.exp(jnp.where(causal, lg[h] * diff, -jnp.inf)) for h in hs]
    inter = [mm(qh[h] * jnp.exp(lg[h] * (idx + 1.0)), sh[h]) for h in hs]
    y = [mm(scores[h], vh[h]) + inter[h] for h in hs]
    states = [sh[h] * math.exp(lg[h] * c) + mm_tn(kh[h] * jnp.exp(lg[h] * (c - 1.0 - idx)), vh[h]) for h in hs]
    outs = [unit_rms(y[h]) * silu(gate[:, cols[h]]) for h in hs]
    return jnp.concatenate(outs, axis=1), jnp.concatenate(states, axis=0)


def _head_select(n_heads, width):
    r, c = _iota((LANES, n_heads * width), 0), _iota((LANES, n_heads * width), 1)
    return (c // width == r).astype(f32)


def ssd_chunk(z, xe, dtr, state, conv_w, conv_b, dt_bias, a_log, d_skip, norm_w):
    c = z.shape[0]
    xbc = silu(_conv_rows(xe, conv_w) + conv_b)
    xs, bm, cm = xbc[:, :SSD_INNER], xbc[:, SSD_INNER:SSD_INNER + 256], xbc[:, SSD_INNER + 256:]
    sel = _head_select(SSD_HEADS, SSD_P)
    dt = softplus(dtr + dt_bias)
    la = dt * (-jnp.exp(a_log))
    la_cum = cumsum01(_tril_ones(c), la)
    la_cum_t = la_cum.T
    last = jnp.sum(la, axis=0, keepdims=True)
    xd = xs * spread01(dt, sel)
    la_x = spread01(la_cum, sel)
    last_x = spread01(last, sel)
    to_end = jnp.exp(last_x - la_x)
    from_start = jnp.exp(la_x)
    causal = _causal(c)
    left = (_iota((1, LANES), 1) < SSD_P).astype(f32)
    upper = _iota((LANES, 1), 0) < SSD_P
    pairs, heads = range(SSD_HEADS // 2), range(SSD_HEADS)
    bc = [bm[:, g * SSD_N:(g + 1) * SSD_N] for g in range(SSD_GROUPS)]
    cc = [cm[:, g * SSD_N:(g + 1) * SSD_N] for g in range(SSD_GROUPS)]
    cb = [mm_nt(cc[g], bc[g]) for g in range(SSD_GROUPS)]
    cols = [slice(p * LANES, (p + 1) * LANES) for p in pairs]
    xd_p = [xd[:, s] for s in cols]
    sp = [state[s, :] for s in cols]
    lmat = [jnp.exp(jnp.where(causal, la_cum[:, h:h + 1] - la_cum_t[h:h + 1, :], -jnp.inf)) for h in heads]
    off = [mm_nt(cc[p // 2], sp[p]) * from_start[:, cols[p]] for p in pairs]
    diag = [mm(cb[h // 4] * lmat[h], xd_p[h // 2] * (left if h % 2 == 0 else 1.0 - left)) for h in heads]
    cd = [jnp.where(upper, jnp.exp(last[:, 2 * p:2 * p + 1]), jnp.exp(last[:, 2 * p + 1:2 * p + 2])) for p in pairs]
    states = [sp[p] * cd[p] + mm_tn(xd_p[p] * to_end[:, cols[p]], bc[p // 2]) for p in pairs]
    ys = [off[p] + diag[2 * p] + diag[2 * p + 1] for p in pairs]
    y = jnp.concatenate(ys, axis=1) + spread01(d_skip, sel) * xs
    yg = y * silu(z)
    half = SSD_INNER // SSD_GROUPS
    out = jnp.concatenate([unit_rms(yg[:, i * half:(i + 1) * half]) for i in range(SSD_GROUPS)], axis=1) * norm_w
    return out, jnp.concatenate(states, axis=0)


def mm3(a, b):
    return jnp.dot(a, b, precision=lax.Precision.HIGH, preferred_element_type=f32)


@jax.custom_vjp
def _unit_lower_inverses(lowers):
    n = lowers[0].shape[0]
    eye = (_iota((n, n), 0) == _iota((n, n), 1)).astype(f32)
    a = [-l for l in lowers]
    p = [eye + x for x in a]
    k = 2
    while k < n:
        a = [mm3(x, x) for x in a]
        p = [y + mm3(y, x) for y, x in zip(p, a)]
        k *= 2
    return p


def _unit_lower_inverses_bwd(t_inv, g):
    dims_tn, dims_nt = (((0,), (0,)), ((), ())), (((1,), (1,)), ((), ()))
    x = [lax.dot_general(t, gi, dims_tn, precision=lax.Precision.HIGH, preferred_element_type=f32) for t, gi in zip(t_inv, g)]
    return ([-lax.dot_general(xi, t, dims_nt, precision=lax.Precision.HIGH, preferred_element_type=f32) for xi, t in zip(x, t_inv)],)


def _unit_lower_inverses_fwd(lowers):
    t_inv = _unit_lower_inverses(lowers)
    return t_inv, t_inv


_unit_lower_inverses.defvjp(_unit_lower_inverses_fwd, _unit_lower_inverses_bwd)


@jax.custom_vjp
def _known_inverses(lowers, t_inv):
    return t_inv


_known_inverses.defvjp(lambda lowers, t_inv: (t_inv, t_inv),
                       lambda t_inv, g: (_unit_lower_inverses_bwd(t_inv, g)[0], [jnp.zeros_like(t) for t in t_inv]))


GDN_GCOL = 6


def gdn_chunk(xe, z, ba, state, conv_w, a_log, dt_bias, norm_w, kept_inverses=None, keep_inverses=False):
    c = z.shape[0]
    qkv = silu(_conv_rows(xe, conv_w))
    beta_all = jax.nn.sigmoid(ba)
    g_all = -jnp.exp(a_log) * softplus(ba + dt_bias)
    gc = cumsum01(_tril_ones(c), g_all)
    gc_t = gc.T
    last = jnp.sum(g_all, axis=0, keepdims=True)
    causal, strict = _causal(c), _causal(c, strict=True)
    hs = range(GDN_HEADS)
    cols = [slice(h * GDN_D, (h + 1) * GDN_D) for h in hs]
    qh = [qkv[:, h * GDN_D:(h + 1) * GDN_D] for h in hs]
    kh = [qkv[:, GDN_W + h * GDN_D:GDN_W + (h + 1) * GDN_D] for h in hs]
    vh = [qkv[:, 2 * GDN_W + h * GDN_D:2 * GDN_W + (h + 1) * GDN_D] for h in hs]
    qh = [t * lax.rsqrt(jnp.sum(t * t, axis=-1, keepdims=True) + EPS) * (GDN_D ** -0.5) for t in qh]
    kh = [t * lax.rsqrt(jnp.sum(t * t, axis=-1, keepdims=True) + EPS) for t in kh]
    beta = [beta_all[:, h:h + 1] for h in hs]
    col = [gc[:, GDN_GCOL + h:GDN_GCOL + h + 1] for h in hs]
    row = [gc_t[GDN_GCOL + h:GDN_GCOL + h + 1, :] for h in hs]
    lst = [last[:, GDN_GCOL + h:GDN_GCOL + h + 1] for h in hs]
    decay = [jnp.exp(jnp.where(causal, col[h] - row[h], -jnp.inf)) for h in hs]
    e_col = [jnp.exp(t) for t in col]
    kb = [kh[h] * beta[h] for h in hs]
    vb = [vh[h] * beta[h] for h in hs]
    kk = [mm_nt(kb[h], kh[h]) for h in hs]
    qk = [mm_nt(qh[h], kh[h]) for h in hs]
    lowers = [jnp.where(strict, kk[h] * decay[h], 0.0) for h in hs]
    t_inv = _unit_lower_inverses(lowers) if kept_inverses is None else _known_inverses(lowers, list(kept_inverses))
    u = [mm(t_inv[h], vb[h]) for h in hs]
    w = [mm(t_inv[h], kb[h] * e_col[h]) for h in hs]
    attn = [jnp.where(causal, qk[h] * decay[h], 0.0) for h in hs]
    sh = [state[s, :] for s in cols]
    ws = [mm(w[h], sh[h]) for h in hs]
    qs = [mm(qh[h] * e_col[h], sh[h]) for h in hs]
    v_new = [u[h] - ws[h] for h in hs]
    o = [qs[h] + mm(attn[h], v_new[h]) for h in hs]
    states = [sh[h] * jnp.exp(lst[h]) + mm_tn(kh[h] * jnp.exp(lst[h] - col[h]), v_new[h]) for h in hs]
    outs = [unit_rms(o[h]) * norm_w * silu(z[:, cols[h]]) for h in hs]
    results = (jnp.concatenate(outs, axis=1), jnp.concatenate(states, axis=0))
    return results + (list(t_inv),) if keep_inverses else results


def _s5_group_mask():
    r, c = _iota((S5_CH, S5_W), 0), _iota((S5_CH, S5_W), 1)
    return (r // S5_GROUP == c // S5_STATE).astype(f32)


def s5_prep(a_re, a_im, log_step, b_re, b_im, c_re, c_im):
    r, c = _iota((LANES, S5_W), 0), _iota((LANES, S5_W), 1)
    step = jnp.exp(mmh(log_step, (c // S5_STATE == r).astype(f32)))
    zr, zi = a_re * step, a_im * step
    e = jnp.exp(zr)
    lr, li = e * jnp.cos(zi), e * jnp.sin(zi)
    den = a_re * a_re + a_im * a_im
    xr, xi = lr - 1.0, li
    cr, ci = (xr * a_re + xi * a_im) / den, (xi * a_re - xr * a_im) / den
    bbr, bbi = cr * b_re - ci * b_im, cr * b_im + ci * b_re
    mask = _s5_group_mask()
    tile = lambda t: jnp.tile(t, (S5_GROUPS, 1)) * mask
    return lr, li, tile(bbr), tile(bbi), tile(c_re), tile(c_im)


def s5_out_fn(h_re, h_im, u, cblk_re, cblk_im, d_skip, w_glu, b_glu):
    y = mm_nt(h_re, cblk_re) - mm_nt(h_im, cblk_im) + d_skip * u
    y = jax.nn.gelu(y)
    return y * jax.nn.sigmoid(mm(y, w_glu) + b_glu)


def _params(sem, **kw):
    return pltpu.CompilerParams(dimension_semantics=sem, vmem_limit_bytes=VMEM_LIMIT, **kw)


def _const_spec(shape):
    return pl.BlockSpec(shape, lambda i: (0,) * len(shape))


def _resident_spec(shape):
    return pl.BlockSpec(shape, lambda i: (0,) * len(shape), pipeline_mode=pl.Buffered(1))


def _row_spec(rows, cols, col_block=0):
    return pl.BlockSpec((rows, cols), lambda i: (i, col_block))


class Seq:
    def __init__(self, array, width, col_block, kind="tile"):
        self.array, self.width, self.col_block, self.kind = array, width, col_block, kind


CHUNKS_PER_STEP = 4
LIGHT_CHUNKS_PER_STEP = 8


def mixer_fwd(name, fn, seqs, params, out_width, state_rows, seq_len, per_step=CHUNKS_PER_STEP, exchange=None, kept_shape=None):
    nc = seq_len // CHUNK
    rows, steps = per_step * CHUNK, nc // per_step
    n_refs = sum(2 if s.kind == "halo" else 1 for s in seqs)
    n_par = len(params)
    n_own = 3 if kept_shape else 2
    car = Carried(exchange)

    def body(*refs):
        seq_refs, par_refs = refs[:n_refs], refs[n_refs:n_refs + n_par]
        k0 = n_refs + n_par
        ex_ins, own = refs[k0:k0 + car.n], refs[k0 + car.n:k0 + car.n + n_own]
        out_ref, st_ref = own[:2]
        k0 += car.n + n_own
        ex_outs, state, ex_sems = refs[k0:k0 + car.n], refs[k0 + car.n], refs[k0 + car.n + 1:]
        c = pl.program_id(0)
        car.start_at(c == 0, ex_ins, ex_outs, ex_sems)

        @pl.when(c == 0)
        def _():
            state[...] = jnp.zeros_like(state)

        s_cur = state[...]
        for kk in range(per_step):
            lo = kk * CHUNK
            vals, k = [], 0
            for s in seqs:
                if s.kind == "halo":
                    prev = jnp.where(c > 0, seq_refs[k][...], 0.0) if kk == 0 else seq_refs[k + 1][lo - HALO:lo, :]
                    vals.append(jnp.concatenate([prev, seq_refs[k + 1][lo:lo + CHUNK, :]], axis=0))
                    k += 2
                else:
                    vals.append(seq_refs[k][lo:lo + CHUNK, :])
                    k += 1
            st_ref[kk] = s_cur
            res = fn(*vals, s_cur, *[p[...] for p in par_refs])
            out_ref[lo:lo + CHUNK, :] = res[0]
            s_cur = res[1]
            if kept_shape:
                for h, t in enumerate(res[2]):
                    own[2][kk, h] = t
        state[...] = s_cur
        car.wait_at(c == steps - 1, ex_ins, ex_outs, ex_sems)

    in_specs, operands = [], []
    for s in seqs:
        if s.kind == "halo":
            rb, w, cb = rows // HALO, s.width, s.col_block
            in_specs.append(pl.BlockSpec((HALO, w), lambda i, rb=rb, cb=cb: (jnp.maximum(i * rb - 1, 0), cb)))
            operands.append(s.array)
        in_specs.append(pl.BlockSpec((rows, s.width), lambda i, cb=s.col_block: (i, cb)))
        operands.append(s.array)
    for p in params:
        in_specs.append(_const_spec(p.shape))
        operands.append(p)
    outs = pl.pallas_call(
        body, grid=(steps,), in_specs=in_specs + car.in_specs,
        out_specs=[pl.BlockSpec((rows, out_width), lambda i: (i, 0)),
                   pl.BlockSpec((per_step, state_rows, LANES), lambda i: (i, 0, 0))]
        + ([pl.BlockSpec((per_step,) + tuple(kept_shape), lambda i: (i, 0, 0, 0))] if kept_shape else []) + car.out_specs,
        out_shape=[jax.ShapeDtypeStruct((seq_len, out_width), f32),
                   jax.ShapeDtypeStruct((nc, state_rows, LANES), f32)]
        + ([jax.ShapeDtypeStruct((nc,) + tuple(kept_shape), f32)] if kept_shape else []) + car.out_shape,
        scratch_shapes=[pltpu.VMEM((state_rows, LANES), f32)] + car.scratch,
        compiler_params=_params(("arbitrary",)), name=name)(*operands, *car.operands)
    return tuple(outs[:n_own]) + (outs[n_own:],)


def mixer_bwd(name, fn, seqs, params, states, d_out, seq_len, per_step=CHUNKS_PER_STEP, exchange=None, kept=None):
    nc = seq_len // CHUNK
    rows, steps = per_step * CHUNK, nc // per_step
    state_rows = states.shape[1]
    diff = [s for s in seqs if s.kind != "const"]
    halos = [s for s in diff if s.kind == "halo"]
    n_refs = sum(2 if s.kind == "halo" else 1 for s in seqs)
    n_par = len(params)
    n_kept = 0 if kept is None else 1
    car = Carried(exchange)

    def body(*refs):
        seq_refs, par_refs = refs[:n_refs], refs[n_refs:n_refs + n_par]
        st_ref, dout_ref = refs[n_refs + n_par:n_refs + n_par + 2]
        k0 = n_refs + n_par + 2
        kept_ref = refs[k0] if n_kept else None
        k0 += n_kept
        ex_ins = refs[k0:k0 + car.n]
        k0 += car.n
        dseq_refs, dpar_refs = refs[k0:k0 + len(diff)], refs[k0 + len(diff):k0 + len(diff) + n_par]
        k0 += len(diff) + n_par
        ex_outs = refs[k0:k0 + car.n]
        scratch = refs[k0 + car.n:]
        d_state, carries, ex_sems = scratch[0], scratch[1:1 + len(halos)], scratch[1 + len(halos):]
        i = pl.program_id(0)
        step = steps - 1 - i
        car.start_at(i == 0, ex_ins, ex_outs, ex_sems)

        @pl.when(i == 0)
        def _():
            d_state[...] = jnp.zeros_like(d_state)
            for r in list(carries) + list(dpar_refs):
                r[...] = jnp.zeros_like(r)

        d_s = d_state[...]
        halo_ct = [r[...] for r in carries]
        for kk in range(per_step - 1, -1, -1):
            lo = kk * CHUNK
            dvals, consts, k = [], [], 0
            for s in seqs:
                if s.kind == "halo":
                    prev = jnp.where(step > 0, seq_refs[k][...], 0.0) if kk == 0 else seq_refs[k + 1][lo - HALO:lo, :]
                    dvals.append(jnp.concatenate([prev, seq_refs[k + 1][lo:lo + CHUNK, :]], axis=0))
                    k += 2
                elif s.kind == "tile":
                    dvals.append(seq_refs[k][lo:lo + CHUNK, :])
                    k += 1
                else:
                    consts.append(seq_refs[k][lo:lo + CHUNK, :])
                    k += 1
            nd = len(dvals)

            extra = [[kept_ref[kk, h] for h in range(kept.shape[1])]] if n_kept else []

            def call(*a, consts=consts, nd=nd, extra=extra):
                it_d, it_c = iter(a[:nd]), iter(consts)
                vals = [next(it_c) if s.kind == "const" else next(it_d) for s in seqs]
                return fn(*vals, *a[nd:], *extra)

            _, vjp = jax.vjp(call, *dvals, st_ref[kk], *[p[...] for p in par_refs])
            cts = vjp((dout_ref[lo:lo + CHUNK, :], d_s))
            hk = 0
            for j, s in enumerate(diff):
                if s.kind == "halo":
                    dseq_refs[j][lo:lo + CHUNK, :] = cts[j][HALO:, :]
                    dseq_refs[j][lo + CHUNK - HALO:lo + CHUNK, :] += halo_ct[hk]
                    halo_ct[hk] = cts[j][:HALO, :]
                    hk += 1
                else:
                    dseq_refs[j][lo:lo + CHUNK, :] = cts[j]
            d_s = cts[nd]
            for j in range(n_par):
                dpar_refs[j][...] += cts[nd + 1 + j]
        d_state[...] = d_s
        for r, v in zip(carries, halo_ct):
            r[...] = v
        car.wait_at(i == steps - 1, ex_ins, ex_outs, ex_sems)

    in_specs, operands = [], []
    for s in seqs:
        if s.kind == "halo":
            rb, cb = rows // HALO, s.col_block
            in_specs.append(pl.BlockSpec((HALO, s.width), lambda i, rb=rb, cb=cb: (jnp.maximum((steps - 1 - i) * rb - 1, 0), cb)))
            operands.append(s.array)
        in_specs.append(pl.BlockSpec((rows, s.width), lambda i, cb=s.col_block: (steps - 1 - i, cb)))
        operands.append(s.array)
    for p in params:
        in_specs.append(_const_spec(p.shape))
        operands.append(p)
    in_specs.append(pl.BlockSpec((per_step, state_rows, LANES), lambda i: (steps - 1 - i, 0, 0)))
    in_specs.append(pl.BlockSpec((rows, d_out.shape[1]), lambda i: (steps - 1 - i, 0)))
    operands += [states, d_out]
    if n_kept:
        in_specs.append(pl.BlockSpec((per_step,) + kept.shape[1:], lambda i: (steps - 1 - i, 0, 0, 0)))
        operands.append(kept)
    outs = pl.pallas_call(
        body, grid=(steps,), in_specs=in_specs + car.in_specs,
        out_specs=[pl.BlockSpec((rows, s.width), lambda i: (steps - 1 - i, 0)) for s in diff]
        + [_const_spec(p.shape) for p in params] + car.out_specs,
        out_shape=[jax.ShapeDtypeStruct((seq_len, s.width), f32) for s in diff]
        + [jax.ShapeDtypeStruct(p.shape, f32) for p in params] + car.out_shape,
        scratch_shapes=[pltpu.VMEM((state_rows, LANES), f32)] + [pltpu.VMEM((HALO, s.width), f32) for s in halos]
        + car.scratch,
        compiler_params=_params(("arbitrary",)), name=name)(*operands, *car.operands)
    nd = len(diff)
    return outs[:nd], outs[nd:nd + n_par], outs[nd + n_par:]


TOK = 512


def inproj_fwd(name, x, nw, w):
    seq_len, n = x.shape[0], w.shape[1]

    def body(x_ref, nw_ref, w_ref, o_ref):
        o_ref[...] = mm(rmsnorm_f(x_ref[...], nw_ref[...]), w_ref[...])

    return pl.pallas_call(
        body, grid=(seq_len // TOK,),
        in_specs=[_row_spec(TOK, D_MODEL), _const_spec(nw.shape), _resident_spec(w.shape)],
        out_specs=_row_spec(TOK, n), out_shape=jax.ShapeDtypeStruct((seq_len, n), f32),
        compiler_params=_params(("arbitrary",)), name=name)(x, nw, w)


def inproj_bwd(name, x, nw, w, pieces, d_res):
    seq_len, n = x.shape[0], w.shape[1]
    widths = [p.shape[1] for p in pieces]
    assert sum(widths) == n
    k = len(pieces)

    def body(*refs):
        x_ref, nw_ref, w_ref = refs[:3]
        p_refs, dres_ref = refs[3:3 + k], refs[3 + k]
        dx_ref, dw_ref, dnw_ref = refs[4 + k:]

        @pl.when(pl.program_id(0) == 0)
        def _():
            dw_ref[...] = jnp.zeros_like(dw_ref)
            dnw_ref[...] = jnp.zeros_like(dnw_ref)

        h, vjp = jax.vjp(rmsnorm_f, x_ref[...], nw_ref[...])
        dh, off = jnp.zeros_like(h), 0
        for p_ref, wd in zip(p_refs, widths):
            g = p_ref[...]
            dh = dh + mm_nt(g, w_ref[:, off:off + wd])
            dw_ref[:, off:off + wd] += mm_tn(h, g)
            off += wd
        dx, dnw = vjp(dh)
        dx_ref[...] = dres_ref[...] + dx
        dnw_ref[...] += dnw

    return pl.pallas_call(
        body, grid=(seq_len // TOK,),
        in_specs=[_row_spec(TOK, D_MODEL), _const_spec(nw.shape), _resident_spec(w.shape)]
        + [_row_spec(TOK, wd) for wd in widths] + [_row_spec(TOK, D_MODEL)],
        out_specs=[_row_spec(TOK, D_MODEL), _resident_spec((D_MODEL, n)), _const_spec(nw.shape)],
        out_shape=[jax.ShapeDtypeStruct((seq_len, D_MODEL), f32), jax.ShapeDtypeStruct((D_MODEL, n), f32),
                   jax.ShapeDtypeStruct(nw.shape, f32)],
        compiler_params=_params(("arbitrary",)), name=name)(x, nw, w, *pieces, d_res)


def outproj_fwd(name, x, a, b, w):
    seq_len, wa, wb = x.shape[0], a.shape[1], b.shape[1]

    def body(x_ref, a_ref, b_ref, w_ref, o_ref):
        o_ref[...] = x_ref[...] + mm(a_ref[...], w_ref[:wa, :]) + mm(b_ref[...], w_ref[wa:, :])

    return pl.pallas_call(
        body, grid=(seq_len // TOK,),
        in_specs=[_row_spec(TOK, D_MODEL), _row_spec(TOK, wa), _row_spec(TOK, wb), _resident_spec(w.shape)],
        out_specs=_row_spec(TOK, D_MODEL), out_shape=jax.ShapeDtypeStruct((seq_len, D_MODEL), f32),
        compiler_params=_params(("arbitrary",)), name=name)(x, a, b, w)


def outproj_bwd(name, dy, a, b, w):
    seq_len, wa, wb = dy.shape[0], a.shape[1], b.shape[1]

    def body(dy_ref, a_ref, b_ref, w_ref, da_ref, db_ref, dw_ref):
        @pl.when(pl.program_id(0) == 0)
        def _():
            dw_ref[...] = jnp.zeros_like(dw_ref)

        g = dy_ref[...]
        da_ref[...] = mm_nt(g, w_ref[:wa, :])
        db_ref[...] = mm_nt(g, w_ref[wa:, :])
        dw_ref[:wa, :] += mm_tn(a_ref[...], g)
        dw_ref[wa:, :] += mm_tn(b_ref[...], g)

    return pl.pallas_call(
        body, grid=(seq_len // TOK,),
        in_specs=[_row_spec(TOK, D_MODEL), _row_spec(TOK, wa), _row_spec(TOK, wb), _resident_spec(w.shape)],
        out_specs=[_row_spec(TOK, wa), _row_spec(TOK, wb), _resident_spec(w.shape)],
        out_shape=[jax.ShapeDtypeStruct((seq_len, wa), f32), jax.ShapeDtypeStruct((seq_len, wb), f32),
                   jax.ShapeDtypeStruct(w.shape, f32)],
        compiler_params=_params(("arbitrary",)), name=name)(dy, a, b, w)


FF_BLOCK = D_FF // N_DEV


def mlp_fwd(name, x, nw, w_up, w_down, exchange=None):
    seq_len = x.shape[0]
    nt = seq_len // TOK
    car = Carried(exchange)

    def body(*refs):
        x_ref, nw_ref, up_ref, down_ref = refs[:4]
        ex_ins, (o_ref, relu_ref) = refs[4:4 + car.n], refs[4 + car.n:6 + car.n]
        ex_outs, ex_sems = refs[6 + car.n:6 + 2 * car.n], refs[6 + 2 * car.n:]
        i = pl.program_id(0)
        car.start_at(i == 0, ex_ins, ex_outs, ex_sems)
        xv = x_ref[...]
        h = rmsnorm_f(xv, nw_ref[...])
        acc = xv
        for d in range(N_DEV):
            r = jnp.maximum(mm(h, up_ref[d]), 0.0)
            relu_ref[d] = r.astype(_MXU)
            acc = acc + mm(r * r, down_ref[d])
        o_ref[...] = acc
        car.wait_at(i == nt - 1, ex_ins, ex_outs, ex_sems)

    outs = pl.pallas_call(
        body, grid=(nt,),
        in_specs=[_row_spec(TOK, D_MODEL), _const_spec(nw.shape), _resident_spec(w_up.shape), _resident_spec(w_down.shape)]
        + car.in_specs,
        out_specs=[_row_spec(TOK, D_MODEL), pl.BlockSpec((N_DEV, TOK, FF_BLOCK), lambda i: (0, i, 0))] + car.out_specs,
        out_shape=[jax.ShapeDtypeStruct((seq_len, D_MODEL), f32),
                   jax.ShapeDtypeStruct((N_DEV, seq_len, FF_BLOCK), _MXU)] + car.out_shape, scratch_shapes=car.scratch,
        compiler_params=_params(("arbitrary",)), name=name)(x, nw, w_up, w_down, *car.operands)
    return outs[0], outs[1], outs[2:]


MLP_SPLIT = 2


def mlp_bwd(name, x, nw, w_up, w_down, relu, dy, exchange=None):
    seq_len = x.shape[0]
    nt = seq_len // TOK
    per = N_DEV // MLP_SPLIT
    dh = d_up = d_down = ex_results = None
    for j in range(MLP_SPLIT):
        last, has_prev = j == MLP_SPLIT - 1, j > 0
        car = Carried(exchange if j == 0 else None)
        n_in = 6 + (3 if has_prev else 0)
        n_out = 4 if last else 3

        def body(*refs, last=last, has_prev=has_prev, car=car, n_in=n_in, n_out=n_out):
            x_ref, nw_ref, up_ref, down_ref, relu_ref, dy_ref = refs[:6]
            dh_prev_ref = refs[6] if has_prev else None
            ex_ins, own = refs[n_in:n_in + car.n], refs[n_in + car.n:n_in + car.n + n_out]
            ex_outs, ex_sems = refs[n_in + car.n + n_out:n_in + 2 * car.n + n_out], refs[n_in + 2 * car.n + n_out:]
            dup_ref, ddown_ref = own[-2:]
            i = pl.program_id(0)
            car.start_at(i == 0, ex_ins, ex_outs, ex_sems)

            @pl.when(i == 0)
            def _():
                dup_ref[...] = jnp.zeros_like(dup_ref)
                ddown_ref[...] = jnp.zeros_like(ddown_ref)
                if last:
                    own[1][...] = jnp.zeros_like(own[1])

            h, vjp = jax.vjp(rmsnorm_f, x_ref[...], nw_ref[...])
            g = dy_ref[...]
            dh_acc = dh_prev_ref[...] if has_prev else jnp.zeros_like(h)
            for d in range(per):
                r = relu_ref[d].astype(f32)
                da = mm_nt(g, down_ref[d]) * (2.0 * r)
                ddown_ref[d] += mm_tn(r * r, g)
                dup_ref[d] += mm_tn(h, da)
                dh_acc = dh_acc + mm_nt(da, up_ref[d])
            if last:
                dx, dnw = vjp(dh_acc)
                own[0][...] = g + dx
                own[1][...] += dnw
            else:
                own[0][...] = dh_acc
            car.wait_at(i == nt - 1, ex_ins, ex_outs, ex_sems)

        weights = lambda shape: pl.BlockSpec(shape, lambda i, j=j: (j, 0, 0), pipeline_mode=pl.Buffered(1))
        in_specs = [_row_spec(TOK, D_MODEL), _const_spec(nw.shape), weights((per, D_MODEL, FF_BLOCK)),
                    weights((per, FF_BLOCK, D_MODEL)), pl.BlockSpec((per, TOK, FF_BLOCK), lambda i, j=j: (j, i, 0)),
                    _row_spec(TOK, D_MODEL)]
        operands = [x, nw, w_up, w_down, relu, dy]
        if has_prev:
            in_specs += [_row_spec(TOK, D_MODEL), ANY, ANY]
            operands += [dh, d_up, d_down]
        first_outs = ([_row_spec(TOK, D_MODEL), _const_spec(nw.shape)] if last else [_row_spec(TOK, D_MODEL)])
        first_shapes = [jax.ShapeDtypeStruct((seq_len, D_MODEL), f32)] + ([jax.ShapeDtypeStruct(nw.shape, f32)] if last else [])
        outs = pl.pallas_call(
            body, grid=(nt,), in_specs=in_specs + car.in_specs,
            out_specs=first_outs + [weights((per, D_MODEL, FF_BLOCK)), weights((per, FF_BLOCK, D_MODEL))] + car.out_specs,
            out_shape=first_shapes + [jax.ShapeDtypeStruct(w_up.shape, f32), jax.ShapeDtypeStruct(w_down.shape, f32)]
            + car.out_shape,
            scratch_shapes=car.scratch,
            input_output_aliases=({7: n_out - 2, 8: n_out - 1} if has_prev else {}),
            compiler_params=_params(("arbitrary",)), name=name if j == 0 else f"{name}_{j}")(*operands, *car.operands)
        dh, d_up, d_down = outs[0], outs[n_out - 2], outs[n_out - 1]
        if j == 0:
            ex_results = outs[n_out:]
        if last:
            dx, dnw = outs[0], outs[1]
    return dx, d_up, d_down, dnw, ex_results


def final_loss(name, x, nw, target):
    seq_len = x.shape[0]

    def body(x_ref, nw_ref, t_ref, loss_ref, dx_ref, dnw_ref):
        @pl.when(pl.program_id(0) == 0)
        def _():
            loss_ref[...] = jnp.zeros_like(loss_ref)
            dnw_ref[...] = jnp.zeros_like(dnw_ref)

        y, vjp = jax.vjp(rmsnorm_f, x_ref[...], nw_ref[...])
        err = y - t_ref[...]
        loss_ref[...] += 0.5 * jnp.sum(jnp.mean(err * err, axis=-1, keepdims=True), axis=0, keepdims=True)
        dx, dnw = vjp(err * (1.0 / D_MODEL))
        dx_ref[...] = dx
        dnw_ref[...] += dnw

    return pl.pallas_call(
        body, grid=(seq_len // TOK,),
        in_specs=[_row_spec(TOK, D_MODEL), _const_spec(nw.shape), _row_spec(TOK, D_MODEL)],
        out_specs=[_const_spec((8, LANES)), _row_spec(TOK, D_MODEL), _const_spec(nw.shape)],
        out_shape=[jax.ShapeDtypeStruct((8, LANES), f32), jax.ShapeDtypeStruct((seq_len, D_MODEL), f32),
                   jax.ShapeDtypeStruct(nw.shape, f32)],
        compiler_params=_params(("arbitrary",)), name=name)(x, nw, target)


def _whole(a):
    return pl.BlockSpec(a.shape, lambda: (0,) * len(a.shape))


def s5_prep_fwd(name, raw):
    def body(*refs):
        outs = s5_prep(*[r[...] for r in refs[:7]])
        for o_ref, o in zip(refs[7:], outs):
            o_ref[...] = o

    shapes = [(1, S5_W)] * 2 + [(S5_CH, S5_W)] * 4
    return pl.pallas_call(
        body, in_specs=[_whole(a) for a in raw], out_specs=[pl.BlockSpec(s, lambda s=s: (0,) * len(s)) for s in shapes],
        out_shape=[jax.ShapeDtypeStruct(s, f32) for s in shapes],
        compiler_params=pltpu.CompilerParams(vmem_limit_bytes=VMEM_LIMIT), name=name)(*raw)


def s5_prep_bwd(name, raw, cts):
    def body(*refs):
        _, vjp = jax.vjp(s5_prep, *[r[...] for r in refs[:7]])
        grads = vjp(tuple(r[...] for r in refs[7:13]))
        for o_ref, g in zip(refs[13:], grads):
            o_ref[...] = g

    return pl.pallas_call(
        body, in_specs=[_whole(a) for a in list(raw) + list(cts)], out_specs=[_whole(a) for a in raw],
        out_shape=[jax.ShapeDtypeStruct(a.shape, f32) for a in raw],
        compiler_params=pltpu.CompilerParams(vmem_limit_bytes=VMEM_LIMIT), name=name)(*raw, *cts)


SCAN_SEG = 8
SCAN_LEN = TOK // SCAN_SEG


def _segment_major():
    r, t = _iota((TOK, TOK), 0), _iota((TOK, TOK), 1)
    return (t == (r % SCAN_SEG) * SCAN_LEN + r // SCAN_SEG).astype(f32)


def _store_powers(lr, li, pr_ref, pi_ref):
    qr, qi = lr, li
    for j in range(SCAN_LEN):
        pr_ref[j:j + 1, :] = qr
        pi_ref[j:j + 1, :] = qi
        qr, qi = lr * qr - li * qi, lr * qi + li * qr


def _tile_scan(xr_ref, xi_ref, lr, li, pr_ref, pi_ref, cr_ref, ci_ref, carry_re, carry_im, reverse):
    sign = -1.0 if reverse else 1.0
    ar = jnp.broadcast_to(lr, (SCAN_SEG, lr.shape[1]))
    ai = jnp.broadcast_to(sign * li, (SCAN_SEG, li.shape[1]))
    rows = lambda j: slice(j * SCAN_SEG, (j + 1) * SCAN_SEG)
    hr = hi = jnp.zeros_like(ar)
    for j in (range(SCAN_LEN - 1, -1, -1) if reverse else range(SCAN_LEN)):
        hr, hi = ar * hr - ai * hi + xr_ref[rows(j), :], ar * hi + ai * hr + xi_ref[rows(j), :]
        xr_ref[rows(j), :] = hr
        xi_ref[rows(j), :] = hi
    wr, wi = pr_ref[SCAN_LEN - 1:SCAN_LEN, :], sign * pi_ref[SCAN_LEN - 1:SCAN_LEN, :]
    er, ei = carry_re[0:1, :], carry_im[0:1, :]
    for s in (range(SCAN_SEG - 1, -1, -1) if reverse else range(SCAN_SEG)):
        cr_ref[s:s + 1, :] = er
        ci_ref[s:s + 1, :] = ei
        er, ei = hr[s:s + 1, :] + wr * er - wi * ei, hi[s:s + 1, :] + wr * ei + wi * er
    carry_re[0:1, :] = er
    carry_im[0:1, :] = ei
    cr, ci = cr_ref[...], ci_ref[...]
    for j in range(SCAN_LEN):
        k = SCAN_LEN - 1 - j if reverse else j
        qr, qi = pr_ref[k:k + 1, :], sign * pi_ref[k:k + 1, :]
        xr_ref[rows(j), :] += qr * cr - qi * ci
        xi_ref[rows(j), :] += qr * ci + qi * cr


_SCAN_SCRATCH = [pltpu.VMEM((HALO, S5_W), f32), pltpu.VMEM((HALO, S5_W), f32),
                 pltpu.VMEM((SCAN_LEN, S5_W), f32), pltpu.VMEM((SCAN_LEN, S5_W), f32),
                 pltpu.VMEM((SCAN_SEG, S5_W), f32), pltpu.VMEM((SCAN_SEG, S5_W), f32)]


def s5_fwd(name, proj, u_block, lam_re, lam_im, bblk_re, bblk_im, cblk_re, cblk_im, d_skip, w_glu, b_glu):
    seq_len = proj.shape[0]

    def body(u_ref, lr_ref, li_ref, br_ref, bi_ref, cr_ref, ci_ref, d_ref, wg_ref, bg_ref,
             o_ref, hr_ref, hi_ref, carry_re, carry_im, pw_re, pw_im, cb_re, cb_im):
        lr, li = lr_ref[...], li_ref[...]

        @pl.when(pl.program_id(0) == 0)
        def _():
            carry_re[...] = jnp.zeros_like(carry_re)
            carry_im[...] = jnp.zeros_like(carry_im)
            _store_powers(lr, li, pw_re, pw_im)

        perm = _segment_major()
        u = _exact01(u_ref[...], perm, ((1,), (0,)), True)
        hr_ref[...] = mm(u, br_ref[...])
        hi_ref[...] = mm(u, bi_ref[...])
        _tile_scan(hr_ref, hi_ref, lr, li, pw_re, pw_im, cb_re, cb_im, carry_re, carry_im, reverse=False)
        out = s5_out_fn(hr_ref[...], hi_ref[...], u, cr_ref[...], ci_ref[...], d_ref[...],
                        wg_ref[...].astype(f32), bg_ref[...])
        o_ref[...] = _exact01(out, perm, ((0,), (0,)), True)

    consts = [lam_re, lam_im, bblk_re, bblk_im, cblk_re, cblk_im, d_skip, w_glu, b_glu]
    return pl.pallas_call(
        body, grid=(seq_len // TOK,),
        in_specs=[_row_spec(TOK, S5_CH, u_block)] + [_const_spec(a.shape) for a in consts],
        out_specs=[_row_spec(TOK, S5_CH), _row_spec(TOK, S5_W), _row_spec(TOK, S5_W)],
        out_shape=[jax.ShapeDtypeStruct((seq_len, S5_CH), f32), jax.ShapeDtypeStruct((seq_len, S5_W), f32),
                   jax.ShapeDtypeStruct((seq_len, S5_W), f32)],
        scratch_shapes=_SCAN_SCRATCH,
        compiler_params=_params(("arbitrary",)), name=name)(proj, *consts)


def s5_bwd(name, proj, u_block, h_re, h_im, d_out, lam_re, lam_im, bblk_re, bblk_im, cblk_re, cblk_im, d_skip, w_glu, b_glu):
    seq_len = proj.shape[0]
    nt = seq_len // TOK
    consts = [lam_re, lam_im, bblk_re, bblk_im, cblk_re, cblk_im, d_skip, w_glu, b_glu]

    def body(u_ref, hr_ref, hi_ref, pr_ref, pi_ref, dout_ref, lr_ref, li_ref, br_ref, bi_ref, cr_ref, ci_ref, d_ref, wg_ref, bg_ref,
             du_ref, dlr_ref, dli_ref, dbr_ref, dbi_ref, dcr_ref, dci_ref, dd_ref, dwg_ref, dbg_ref,
             gr_ref, gi_ref, carry_re, carry_im, pw_re, pw_im, cb_re, cb_im):
        i = pl.program_id(0)
        tile = nt - 1 - i
        lr, li = lr_ref[...], li_ref[...]

        @pl.when(i == 0)
        def _():
            for r in (carry_re, carry_im, dlr_ref, dli_ref, dbr_ref, dbi_ref, dcr_ref, dci_ref, dd_ref, dwg_ref, dbg_ref):
                r[...] = jnp.zeros_like(r)
            _store_powers(lr, li, pw_re, pw_im)

        perm = _segment_major()
        u = _exact01(u_ref[...], perm, ((1,), (0,)), True)
        d_out_p = _exact01(dout_ref[...], perm, ((1,), (0,)), True)
        h_r, h_i = hr_ref[...], hi_ref[...]
        _, vjp = jax.vjp(s5_out_fn, h_r, h_i, u, cr_ref[...], ci_ref[...], d_ref[...], wg_ref[...].astype(f32), bg_ref[...])
        ghr, ghi, du, dcr, dci, dd, dwg, dbg = vjp(d_out_p)
        gr_ref[...] = ghr
        gi_ref[...] = ghi
        _tile_scan(gr_ref, gi_ref, lr, li, pw_re, pw_im, cb_re, cb_im, carry_re, carry_im, reverse=True)
        g_r, g_i = gr_ref[...], gi_ref[...]
        keep = jnp.where(tile > 0, 1.0, 0.0)
        top = _iota((SCAN_SEG, 1), 0) == 0

        def earlier(h, before_ref):
            head = jnp.where(top, before_ref[HALO - 1:HALO, :] * keep, _roll(h[TOK - SCAN_SEG:, :], 1, 0))
            return jnp.concatenate([head, h[:TOK - SCAN_SEG, :]], axis=0)

        p_r, p_i = earlier(h_r, pr_ref), earlier(h_i, pi_ref)
        dlr_ref[...] += jnp.sum(g_r * p_r + g_i * p_i, axis=0, keepdims=True)
        dli_ref[...] += jnp.sum(g_i * p_r - g_r * p_i, axis=0, keepdims=True)
        du_p = du + mm_nt(g_r, br_ref[...]) + mm_nt(g_i, bi_ref[...])
        du_ref[...] = _exact01(du_p, perm, ((0,), (0,)), True)
        dbr_ref[...] += mm_tn(u, g_r)
        dbi_ref[...] += mm_tn(u, g_i)
        dcr_ref[...] += dcr
        dci_ref[...] += dci
        dd_ref[...] += dd
        dwg_ref[...] += dwg
        dbg_ref[...] += dbg

    rev = lambda cols, cb=0: pl.BlockSpec((TOK, cols), lambda i, cb=cb: (nt - 1 - i, cb))
    prev = pl.BlockSpec((HALO, S5_W), lambda i: (jnp.maximum((nt - 1 - i) * (TOK // HALO) - 1, 0), 0))
    outs = pl.pallas_call(
        body, grid=(nt,),
        in_specs=[rev(S5_CH, u_block), rev(S5_W), rev(S5_W), prev, prev, rev(S5_CH)] + [_const_spec(a.shape) for a in consts],
        out_specs=[rev(S5_CH)] + [_const_spec(a.shape) for a in consts],
        out_shape=[jax.ShapeDtypeStruct((seq_len, S5_CH), f32)] + [jax.ShapeDtypeStruct(a.shape, f32) for a in consts],
        scratch_shapes=[pltpu.VMEM((TOK, S5_W), f32), pltpu.VMEM((TOK, S5_W), f32)] + _SCAN_SCRATCH,
        compiler_params=_params(("arbitrary",)), name=name)(proj, h_re, h_im, h_re, h_im, d_out, *consts)
    return outs[0], outs[1:]


ANY = pl.BlockSpec(memory_space=pl.ANY)


def _mesh_position():
    x, y, c = lax.axis_index("x"), lax.axis_index("y"), lax.axis_index("c")
    return x, y, c, 4 * x + 2 * y + c


def _peer(x, y, c, r):
    px = 1 - x if r & 4 else x
    py = 1 - y if r & 2 else y
    pc = 1 - c if r & 1 else c
    return (px, py, pc), 4 * px + 2 * py + pc


class Exchange:
    def __init__(self, arrays, gather):
        self.arrays, self.gather, self.n = list(arrays), gather, len(arrays)
        self.in_specs = [ANY] * self.n
        self.out_specs = [ANY] * self.n
        shapes = [((N_DEV,) + a.shape) if gather else a.shape for a in self.arrays]
        self.out_shape = [jax.ShapeDtypeStruct(s, a.dtype) for s, a in zip(shapes, self.arrays)]
        self.scratch = [pltpu.SemaphoreType.DMA((self.n, N_DEV - 1)), pltpu.SemaphoreType.DMA((self.n, N_DEV - 1)),
                        pltpu.SemaphoreType.DMA((self.n,))]

    def _copies(self, ins, outs, sems, landed):
        send_sems, recv_sems, local_sems = sems
        x, y, c, me = _mesh_position()
        local, remote = [], []
        for i in range(self.n):
            mine = ins[i] if self.gather else ins[i].at[me]
            local.append(pltpu.make_async_copy(mine, outs[i].at[me], local_sems.at[i]))
            for r in range(1, N_DEV):
                peer, peer_idx = _peer(x, y, c, r)
                remote.append(pltpu.make_async_remote_copy(
                    src_ref=ins[i] if self.gather else ins[i].at[peer_idx],
                    dst_ref=outs[i].at[peer_idx if landed else me],
                    send_sem=send_sems.at[i, r - 1], recv_sem=recv_sems.at[i, r - 1],
                    device_id=peer, device_id_type=pl.DeviceIdType.MESH))
        return local, remote

    def start(self, ins, outs, sems):
        local, remote = self._copies(ins, outs, sems, landed=False)
        for cp in local + remote:
            cp.start()

    def wait(self, ins, outs, sems):
        local, remote = self._copies(ins, outs, sems, landed=True)
        for cp in remote:
            cp.wait_recv()
            cp.wait_send()
        for cp in local:
            cp.wait()

    def run(self, name):
        n = self.n

        def body(*refs):
            ins, outs, sems = refs[:n], refs[n:2 * n], refs[2 * n:]
            self.start(ins, outs, sems)
            self.wait(ins, outs, sems)

        return pl.pallas_call(body, in_specs=self.in_specs, out_specs=self.out_specs, out_shape=self.out_shape,
                              scratch_shapes=self.scratch, name=name)(*self.arrays)


class Carried:
    def __init__(self, exchange):
        self.ex = exchange
        self.n = exchange.n if exchange else 0
        self.in_specs = exchange.in_specs if exchange else []
        self.out_specs = exchange.out_specs if exchange else []
        self.out_shape = exchange.out_shape if exchange else []
        self.scratch = exchange.scratch if exchange else []
        self.operands = exchange.arrays if exchange else []

    def start_at(self, first, ins, outs, sems):
        if self.ex is not None:
            @pl.when(first)
            def _():
                self.ex.start(ins, outs, sems)

    def wait_at(self, last, ins, outs, sems):
        if self.ex is not None:
            @pl.when(last)
            def _():
                self.ex.wait(ins, outs, sems)


def adamw(name, parts, w, m, v):
    rows, cols = w.shape
    tr = rows
    for cand in (512, 256, 128, 64, 32, 16, 8):
        if rows * cols * 4 > (1 << 20) and rows % cand == 0 and cand * cols * 4 <= (1 << 20):
            tr = cand
            break

    def body(p_ref, w_ref, m_ref, v_ref, g_ref, d_ref, nm_ref, nv_ref):
        g = p_ref[0].astype(f32)
        for s in range(1, N_DEV):
            g = g + p_ref[s].astype(f32)
        nm = ADAM_B1 * m_ref[...] + (1.0 - ADAM_B1) * g
        nv = ADAM_B2 * v_ref[...] + (1.0 - ADAM_B2) * (g * g)
        m_hat = nm / (1.0 - ADAM_B1 ** ADAM_STEP)
        v_hat = nv / (1.0 - ADAM_B2 ** ADAM_STEP)
        g_ref[...] = g
        d_ref[...] = -ADAM_LR * (m_hat / (jnp.sqrt(v_hat) + ADAM_EPS) + ADAM_WD * w_ref[...])
        nm_ref[...] = nm
        nv_ref[...] = nv

    blk = pl.BlockSpec((tr, cols), lambda i: (i, 0))
    return pl.pallas_call(
        body, grid=(rows // tr,),
        in_specs=[pl.BlockSpec((N_DEV, tr, cols), lambda i: (0, i, 0)), blk, blk, blk],
        out_specs=[blk] * 4, out_shape=[jax.ShapeDtypeStruct((rows, cols), f32)] * 4,
        compiler_params=_params(("arbitrary",)), name=name)(parts, w, m, v)


WEIGHTS = ['l0_norm_mix', 'l0_w_in', 'ssd_conv_w', 'ssd_conv_b', 'ssd_dt_bias', 'ssd_A_log', 'ssd_D', 'ssd_norm_w',
           'l0_w_out', 'l0_norm_mlp', 'l0_w_up', 'l0_w_down', 'l1_norm_mix', 'l1_w_in', 'gdn_conv_w', 'gdn_A_log',
           'gdn_dt_bias', 'gdn_norm_w', 's5_A_re', 's5_A_im', 's5_log_step', 's5_B_re', 's5_B_im', 's5_C_re', 's5_C_im',
           's5_D', 's5_w_glu', 's5_b_glu', 'l1_w_out', 'l1_norm_mlp', 'l1_w_up', 'l1_w_down', 'final_norm']
SHARDED = ['l0_w_in', 'l0_w_out', 'l0_w_up', 'l0_w_down', 'l1_w_in', 's5_w_glu', 'l1_w_out', 'l1_w_up', 'l1_w_down',
           'ssd_conv_w', 'gdn_conv_w']
F32_GATHER = ('ssd_conv_w', 'gdn_conv_w')
REPLICATED = [n for n in WEIGHTS if n not in SHARDED]
INPUTS = ['x'] + WEIGHTS + ['loss_target'] + ['m_' + n for n in WEIGHTS] + ['v_' + n for n in WEIGHTS]


def _row(v):
    return v.reshape(1, -1)


def _pad_lanes(v, offset=0):
    return jnp.pad(v, (offset, LANES - offset - v.shape[0])).reshape(1, LANES)


def _cols_to_blocks(g):
    return g.reshape(g.shape[0], N_DEV, -1).transpose(1, 0, 2)


def _blocks_to_cols(g):
    return g.transpose(1, 0, 2).reshape(g.shape[1], -1)


def _pack(arrays):
    parts, slots, at = [], [], 0
    for a in arrays:
        n = a.size
        rows = -(-n // (8 * LANES)) * 8
        parts.append(jnp.pad(a.reshape(-1), (0, rows * LANES - n)).reshape(rows, LANES))
        slots.append((at, rows, n, a.shape))
        at += rows
    return jnp.concatenate(parts, axis=0), slots


def _unpack(buf, slots):
    return [buf[at:at + rows].reshape(-1)[:n].reshape(shape) for at, rows, n, shape in slots]


def kernel(*args):
    a = dict(zip(INPUTS, args, strict=True))
    seq_len = a['x'].shape[1]
    x0 = a['x'].reshape(seq_len, D_MODEL)
    target = a['loss_target'].reshape(seq_len, D_MODEL)

    shard = {n: a[n] if n in F32_GATHER else a[n].astype(_MXU) for n in SHARDED}
    first = ['l0_w_in', 'ssd_conv_w', 'gdn_conv_w']
    g = dict(zip(first, Exchange([shard[n] for n in first], gather=True).run("gather_first")))
    w_nat = _blocks_to_cols(g['l0_w_in'])
    win0 = jnp.concatenate([w_nat[:, :2048], w_nat[:, 2560:3584], w_nat[:, 2048:2560], w_nat[:, 3584:3592],
                            jnp.zeros((D_MODEL, IN0_PAD - IN0_W), _MXU)], axis=1)
    ssd_cw, gdn_cw = _blocks_to_cols(g['ssd_conv_w']), _blocks_to_cols(g['gdn_conv_w'])

    half = RET_D // 2
    inv = ROPE_THETA ** (-jnp.arange(half, dtype=f32) / half)
    ang = jnp.arange(seq_len, dtype=f32)[:, None] * inv[None, :]
    cos, sin = jnp.cos(ang), jnp.sin(ang)
    cos, sin = jnp.concatenate([cos, cos], axis=1), jnp.concatenate([-sin, sin], axis=1)
    ssd_params = [ssd_cw, _row(a['ssd_conv_b']), _pad_lanes(a['ssd_dt_bias']), _pad_lanes(a['ssd_A_log']),
                  _pad_lanes(a['ssd_D']), _row(a['ssd_norm_w'])]
    gdn_params = [gdn_cw, _pad_lanes(a['gdn_A_log'], GDN_GCOL), _pad_lanes(a['gdn_dt_bias'], GDN_GCOL), _row(a['gdn_norm_w'])]
    s5_raw = [a['s5_A_re'].reshape(1, S5_W), a['s5_A_im'].reshape(1, S5_W), _pad_lanes(a['s5_log_step']),
              a['s5_B_re'].transpose(2, 0, 1).reshape(S5_GROUP, S5_W), a['s5_B_im'].transpose(2, 0, 1).reshape(S5_GROUP, S5_W),
              a['s5_C_re'].transpose(1, 0, 2).reshape(S5_GROUP, S5_W), a['s5_C_im'].transpose(1, 0, 2).reshape(S5_GROUP, S5_W)]
    s5_d, s5_bg = _row(a['s5_D']), _row(a['s5_b_glu'])
    nw = {n: _row(a[n]) for n in ('l0_norm_mix', 'l0_norm_mlp', 'l1_norm_mix', 'l1_norm_mlp', 'final_norm')}

    proj0 = inproj_fwd("l0_in", x0, nw['l0_norm_mix'], win0)
    ret_seqs = [Seq(proj0, 512, 0), Seq(proj0, 512, 1), Seq(proj0, 512, 2), Seq(proj0, 512, 3),
                Seq(cos, LANES, 0, "const"), Seq(sin, LANES, 0, "const")]
    later = ['l0_w_up']
    ret_out, ret_st, got = mixer_fwd("ret_fwd", ret_chunk, ret_seqs, [], 512, RET_HEADS * RET_D, seq_len,
                                     per_step=LIGHT_CHUNKS_PER_STEP, exchange=Exchange([shard[n] for n in later], gather=True))
    g.update(zip(later, got))
    ssd_seqs = [Seq(proj0, 512, 6), Seq(proj0, 1024, 2, "halo"), Seq(proj0, LANES, 28)]
    later = ['l0_w_out', 'l0_w_down', 's5_w_glu']
    ssd_out, ssd_st, got = mixer_fwd("ssd_fwd", ssd_chunk, ssd_seqs, ssd_params, SSD_INNER, SSD_INNER, seq_len,
                                     per_step=LIGHT_CHUNKS_PER_STEP, exchange=Exchange([shard[n] for n in later], gather=True))
    g.update(zip(later, got))
    wout0 = g['l0_w_out'].reshape(D_MODEL, D_MODEL)
    wglu = g['s5_w_glu'].reshape(S5_CH, S5_CH)
    x1 = outproj_fwd("l0_out", x0, ret_out, ssd_out, wout0)
    later = ['l1_w_in', 'l1_w_out', 'l1_w_up', 'l1_w_down']
    x2, relu0, got = mlp_fwd("l0_mlp", x1, nw['l0_norm_mlp'], g['l0_w_up'], g['l0_w_down'],
                      exchange=Exchange([shard[n] for n in later], gather=True))
    g.update(zip(later, got))
    w_nat = g['l1_w_in'].reshape(D_MODEL, IN1_W)
    win1 = jnp.concatenate([w_nat[:, :3072], w_nat[:, 3084:3340], w_nat[:, 3072:3084],
                            jnp.zeros((D_MODEL, IN1_PAD - IN1_W), _MXU)], axis=1)
    wout1 = g['l1_w_out'].reshape(D_MODEL, D_MODEL)
    proj1 = inproj_fwd("l1_in", x2, nw['l1_norm_mix'], win1)
    gdn_seqs = [Seq(proj1, 3 * GDN_W, 0, "halo"), Seq(proj1, GDN_W, 3), Seq(proj1, LANES, 26)]
    gdn_out, gdn_st, gdn_inv, _ = mixer_fwd("gdn_fwd", functools.partial(gdn_chunk, keep_inverses=True), gdn_seqs, gdn_params,
                                            GDN_W, GDN_W, seq_len, kept_shape=(GDN_HEADS, CHUNK, CHUNK))
    prep = s5_prep_fwd("s5_prep", s5_raw)
    s5_out, h_re, h_im = s5_fwd("s5_fwd", proj1, 12, *prep, s5_d, wglu, s5_bg)
    x3 = outproj_fwd("l1_out", x2, gdn_out, s5_out, wout1)
    x4, relu1, _ = mlp_fwd("l1_mlp", x3, nw['l1_norm_mlp'], g['l1_w_up'], g['l1_w_down'])
    loss_blk, dx4, d_final = final_loss("final_loss", x4, nw['final_norm'], target)

    parts = {}
    dx3, d_up1, d_down1, d_nmlp1, _ = mlp_bwd("l1_mlp_bwd", x3, nw['l1_norm_mlp'], g['l1_w_up'], g['l1_w_down'], relu1, dx4)
    d_gdn, d_s5, d_wout1 = outproj_bwd("l1_out_bwd", dx3, gdn_out, s5_out, wout1)
    d_u, s5_g = s5_bwd("s5_bwd", proj1, 12, h_re, h_im, d_s5, *prep, s5_d, wglu, s5_bg)
    s5_raw_g = s5_prep_bwd("s5_prep_bwd", s5_raw, s5_g[:6])
    ready = {'l1_w_up': d_up1, 'l1_w_down': d_down1, 'l1_w_out': d_wout1.reshape(N_DEV, -1, D_MODEL),
             's5_w_glu': s5_g[7].reshape(N_DEV, -1, S5_CH)}
    (d_qkv, d_z1, d_ba), gdn_pg, got = mixer_bwd("gdn_bwd", gdn_chunk, gdn_seqs, gdn_params, gdn_st, d_gdn, seq_len,
                                                 exchange=Exchange(list(ready.values()), gather=False), kept=gdn_inv)
    parts.update(zip(ready, got))
    dx2, d_win1, d_nmix1 = inproj_bwd("l1_in_bwd", x2, nw['l1_norm_mix'], win1, [d_qkv, d_z1, d_u, d_ba], dx3)
    d_win1 = jnp.concatenate([d_win1[:, :3072], d_win1[:, 3328:3340], d_win1[:, 3072:3328]], axis=1)
    ready = {'l1_w_in': d_win1.reshape(N_DEV, -1, IN1_W), 'gdn_conv_w': _cols_to_blocks(gdn_pg[0])}
    dx1, d_up0, d_down0, d_nmlp0, got = mlp_bwd("l0_mlp_bwd", x1, nw['l0_norm_mlp'], g['l0_w_up'], g['l0_w_down'], relu0, dx2,
                                                exchange=Exchange(list(ready.values()), gather=False))
    parts.update(zip(ready, got))
    d_ret, d_ssd, d_wout0 = outproj_bwd("l0_out_bwd", dx1, ret_out, ssd_out, wout0)
    ready = {'l0_w_up': d_up0}
    d_qkvg, _, got = mixer_bwd("ret_bwd", ret_chunk, ret_seqs, [], ret_st, d_ret, seq_len,
                               per_step=LIGHT_CHUNKS_PER_STEP, exchange=Exchange(list(ready.values()), gather=False))
    parts.update(zip(ready, got))
    ready = {'l0_w_down': d_down0, 'l0_w_out': d_wout0.reshape(N_DEV, -1, D_MODEL)}
    (d_z0, d_xbc, d_dt), ssd_pg, got = mixer_bwd("ssd_bwd", ssd_chunk, ssd_seqs, ssd_params, ssd_st, d_ssd, seq_len,
                                                 per_step=LIGHT_CHUNKS_PER_STEP,
                                                 exchange=Exchange(list(ready.values()), gather=False))
    parts.update(zip(ready, got))
    dx0, d_win0, d_nmix0 = inproj_bwd("l0_in_bwd", x0, nw['l0_norm_mix'], win0, list(d_qkvg) + [d_xbc, d_z0, d_dt], dx1)

    d_win0 = jnp.concatenate([d_win0[:, :2048], d_win0[:, 3072:3584], d_win0[:, 2048:3072], d_win0[:, 3584:3592]], axis=1)
    ready = {'l0_w_in': _cols_to_blocks(d_win0).astype(jnp.bfloat16), 'ssd_conv_w': _cols_to_blocks(ssd_pg[0])}
    from_b = lambda t: t.reshape(S5_GROUP, S5_GROUPS, S5_STATE).transpose(1, 2, 0)
    from_c = lambda t: t.reshape(S5_GROUP, S5_GROUPS, S5_STATE).transpose(1, 0, 2)
    replicated_g = {
        'l0_norm_mix': d_nmix0, 'ssd_conv_b': ssd_pg[1], 'ssd_dt_bias': ssd_pg[2][0, :SSD_HEADS], 'ssd_A_log': ssd_pg[3][0, :SSD_HEADS],
        'ssd_D': ssd_pg[4][0, :SSD_HEADS], 'ssd_norm_w': ssd_pg[5], 'l0_norm_mlp': d_nmlp0, 'l1_norm_mix': d_nmix1,
        'gdn_A_log': gdn_pg[1][0, GDN_GCOL:GDN_GCOL + GDN_HEADS], 'gdn_dt_bias': gdn_pg[2][0, GDN_GCOL:GDN_GCOL + GDN_HEADS],
        'gdn_norm_w': gdn_pg[3], 's5_A_re': s5_raw_g[0], 's5_A_im': s5_raw_g[1], 's5_log_step': s5_raw_g[2][0, :S5_GROUPS],
        's5_B_re': from_b(s5_raw_g[3]), 's5_B_im': from_b(s5_raw_g[4]), 's5_C_re': from_c(s5_raw_g[5]), 's5_C_im': from_c(s5_raw_g[6]),
        's5_D': s5_g[6], 's5_b_glu': s5_g[8], 'l1_norm_mlp': d_nmlp1, 'final_norm': d_final}
    replicated_g = {n: replicated_g[n].reshape(a[n].shape) for n in REPLICATED}

    parts.update(zip(ready, Exchange(list(ready.values()), gather=False).run("scatter_last")))
    packed_g, slots = _pack([replicated_g[n] for n in REPLICATED])
    (packed_parts,) = Exchange([packed_g], gather=True).run("gather_small_grads")
    results = {}
    for n in SHARDED:
        results[n] = adamw("adamw_" + n, parts[n], a[n], a['m_' + n], a['v_' + n])
    packed = [_pack([a[pre + n] for n in REPLICATED])[0] for pre in ('', 'm_', 'v_')]
    small = [_unpack(t, slots) for t in adamw("adamw_small", packed_parts, *packed)]
    for i, n in enumerate(REPLICATED):
        results[n] = tuple(small[k][i] for k in range(4))

    loss = lax.psum(loss_blk[0, 0], ("x", "y", "c"))
    grad_x = dx0.reshape(a['x'].shape)
    return (loss, grad_x, *[results[n][0] for n in WEIGHTS], *[results[n][1] for n in WEIGHTS],
            *[results[n][2] for n in WEIGHTS], *[results[n][3] for n in WEIGHTS])
```
